```python
import math
import jax, jax.numpy as jnp
from jax import lax
import numpy as np

D_MODEL = 2048
BATCH = 2
SEQ = 8192
DEPTH = 2

ATTN_HEADS = 8
ATTN_KV_HEADS = 2
ATTN_HEAD_DIM = 128
ATTN_GROUP = ATTN_HEADS // ATTN_KV_HEADS
ATTN_INNER = ATTN_HEADS * ATTN_HEAD_DIM
IDX_HEADS = 8
IDX_HEAD_DIM = 64
TOPK_MAX = 256
Q_BLOCK = 128
ROPE_THETA = 500000.0
ROPE_FRACTION = 4

SSM_HEADS = 16
SSM_HEAD_DIM = 64
SSM_INNER = SSM_HEADS * SSM_HEAD_DIM
SSM_GROUPS = 2
SSM_STATE = 128
SSM_CONV = 4
SSM_CONV_DIM = SSM_INNER + 2 * SSM_GROUPS * SSM_STATE
SSM_CHUNK = 128

HYB_SPLITS = (ATTN_INNER, ATTN_KV_HEADS * ATTN_HEAD_DIM, ATTN_KV_HEADS * ATTN_HEAD_DIM,
              IDX_HEADS * IDX_HEAD_DIM, IDX_HEAD_DIM, IDX_HEADS,
              SSM_INNER, SSM_CONV_DIM, SSM_HEADS)
HYB_IN = sum(HYB_SPLITS)
HYB_MIX = ATTN_INNER + SSM_INNER

RET_HEADS = 8
RET_QK_DIM = 256
RET_V_DIM = 512
RET_QK_TOTAL = RET_HEADS * RET_QK_DIM
RET_V_TOTAL = RET_HEADS * RET_V_DIM
RET_IN = 2 * RET_QK_TOTAL + 2 * RET_V_TOTAL
RET_CHUNK = 128
RET_THETA = 10000.0

MOE_GROUPS = 8
MOE_EXPERTS_PER_GROUP = 8
N_EXPERTS = MOE_GROUPS * MOE_EXPERTS_PER_GROUP
MOE_TOPK = 2
EXPERT_FF = 512
MOE_BLOCK = 128

EPS = 1e-6
F32 = jnp.float32

kernel_name = "hybrid_dsa_ssd_retention_hmoe"


def _split(t, sizes):
    offs = [int(o) for o in np.cumsum(sizes)[:-1]]
    return jnp.split(t, offs, axis=-1)


def rmsnorm(x, g):
    xf = x.astype(F32)
    y = xf * lax.rsqrt(jnp.mean(xf * xf, axis=-1, keepdims=True) + EPS)
    return (y * g.astype(F32)).astype(x.dtype)


def rope(x, pos, rot_dim, theta):
    half = rot_dim // 2
    inv = jnp.power(theta, -(jnp.arange(half, dtype=F32) * 2.0 / rot_dim))
    ang = pos.astype(F32)[..., None] * inv
    cos = jnp.cos(ang)[:, :, None, :]
    sin = jnp.sin(ang)[:, :, None, :]
    xf = x.astype(F32)
    x1 = xf[..., :half]
    x2 = xf[..., half:rot_dim]
    out = jnp.concatenate([x1 * cos - x2 * sin, x2 * cos + x1 * sin, xf[..., rot_dim:]], axis=-1)
    return out.astype(x.dtype)


def dsa_attention(q, k, v, qi, ki, wi, k_sel):
    B, S = q.shape[:2]
    nb = S // Q_BLOCK

    def to_blocks(t):
        return t.reshape((B, nb, Q_BLOCK) + t.shape[2:]).swapaxes(0, 1)

    qb = to_blocks(q.reshape(B, S, ATTN_KV_HEADS, ATTN_GROUP, ATTN_HEAD_DIM))
    qib = to_blocks(qi)
    wib = to_blocks(wi)
    starts = jnp.arange(nb, dtype=jnp.int32) * Q_BLOCK
    key_pos = jnp.arange(S, dtype=jnp.int32)
    scale = ATTN_HEAD_DIM ** -0.5

    def block(args):
        qblk, qiblk, wblk, t0 = args
        t_pos = t0 + jnp.arange(Q_BLOCK, dtype=jnp.int32)
        rel = jax.nn.relu(jnp.einsum('bthd,bsd->btsh', qiblk, ki))
        score = jnp.einsum('btsh,bth->bts', rel, wblk).astype(F32)
        causal = key_pos[None, :] <= t_pos[:, None]
        score = jnp.where(causal[None], score, -jnp.inf)
        _, idx = lax.top_k(score, k_sel)
        valid = idx <= t_pos[None, :, None]
        kg = jax.vmap(lambda kk, ii: kk[ii])(k, idx)
        vg = jax.vmap(lambda vv, ii: vv[ii])(v, idx)
        s = jnp.einsum('btkgd,btskd->btkgs', qblk, kg).astype(F32) * scale
        s = jnp.where(valid[:, :, None, None, :], s, -jnp.inf)
        p = jax.nn.softmax(s, axis=-1).astype(v.dtype)
        return jnp.einsum('btkgs,btskd->btkgd', p, vg)

    out = lax.map(block, (qb, qib, wib, starts))
    return out.swapaxes(0, 1).reshape(B, S, ATTN_INNER)


def causal_conv(x, w, b):
    y = lax.conv_general_dilated(x, w[:, None, :].astype(x.dtype), window_strides=(1,),
                                 padding=[(SSM_CONV - 1, 0)],
                                 dimension_numbers=('NWC', 'WIO', 'NWC'),
                                 feature_group_count=x.shape[-1])
    return y + b.astype(x.dtype)


def ssd_chunked(xs, dt, a, bm, cm):
    B, S, H, P = xs.shape
    G, N = bm.shape[2], bm.shape[3]
    Hg = H // G
    Q = SSM_CHUNK
    nc = S // Q
    x = xs.reshape(B, nc, Q, G, Hg, P)
    dtc = dt.reshape(B, nc, Q, G, Hg)
    bc = bm.reshape(B, nc, Q, G, N)
    cc = cm.reshape(B, nc, Q, G, N)
    acum = jnp.cumsum(dtc * a.reshape(G, Hg), axis=2)
    xdt = x * dtc[..., None]
    causal = jnp.tril(jnp.ones((Q, Q), dtype=bool))
    seg = acum[:, :, :, None] - acum[:, :, None]
    lmat = jnp.exp(jnp.where(causal[:, :, None, None], seg, -jnp.inf))
    cb = jnp.einsum('bcign,bcjgn->bcijg', cc, bc)
    y_diag = jnp.einsum('bcijgh,bcjghp->bcighp', cb[..., None] * lmat, xdt)
    decay_to_end = jnp.exp(acum[:, :, -1:] - acum)
    states = jnp.einsum('bcjgn,bcjghp->bcghpn', bc, xdt * decay_to_end[..., None])
    chunk_decay = jnp.exp(acum[:, :, -1])

    def step(hs, inp):
        st, dec = inp
        return hs * dec[..., None, None] + st, hs

    h0 = jnp.zeros((B, G, Hg, P, N), F32)
    _, prev = lax.scan(step, h0, (states.swapaxes(0, 1), chunk_decay.swapaxes(0, 1)))
    prev = prev.swapaxes(0, 1)
    y_off = jnp.einsum('bcign,bcghpn->bcighp', cc, prev) * jnp.exp(acum)[..., None]
    return (y_diag + y_off).reshape(B, S, H, P)


def mamba2_ssd(z, xbc, dt_raw, conv_w, conv_b, dt_bias, a_log, d_skip, norm_g):
    B, S = z.shape[:2]
    xbc = jax.nn.silu(causal_conv(xbc, conv_w, conv_b))
    xs, bm, cm = _split(xbc, (SSM_INNER, SSM_GROUPS * SSM_STATE, SSM_GROUPS * SSM_STATE))
    xs = xs.reshape(B, S, SSM_HEADS, SSM_HEAD_DIM).astype(F32)
    bm = bm.reshape(B, S, SSM_GROUPS, SSM_STATE).astype(F32)
    cm = cm.reshape(B, S, SSM_GROUPS, SSM_STATE).astype(F32)
    dt = jax.nn.softplus(dt_raw.astype(F32) + dt_bias.astype(F32))
    a = -jnp.exp(a_log.astype(F32))
    y = ssd_chunked(xs, dt, a, bm, cm) + d_skip.astype(F32)[:, None] * xs
    y = y.reshape(B, S, SSM_INNER) * jax.nn.silu(z.astype(F32))
    y = rmsnorm(y.reshape(B, S, SSM_GROUPS, SSM_INNER // SSM_GROUPS),
                norm_g.reshape(SSM_GROUPS, SSM_INNER // SSM_GROUPS))
    return y.reshape(B, S, SSM_INNER).astype(z.dtype)


def hybrid_attn_ssm(h, pos, w_in, q_norm, k_norm, kidx_norm, conv_w, conv_b,
                    dt_bias, a_log, d_skip, ssm_norm_g, w_out, k_sel):
    B, S, _ = h.shape
    q, k, v, qi, ki, wi, z, xbc, dt_raw = _split(h @ w_in, HYB_SPLITS)
    rot = ATTN_HEAD_DIM // ROPE_FRACTION
    rot_i = IDX_HEAD_DIM // ROPE_FRACTION
    q = rope(rmsnorm(q.reshape(B, S, ATTN_HEADS, ATTN_HEAD_DIM), q_norm), pos, rot, ROPE_THETA)
    k = rope(rmsnorm(k.reshape(B, S, ATTN_KV_HEADS, ATTN_HEAD_DIM), k_norm), pos, rot, ROPE_THETA)
    v = v.reshape(B, S, ATTN_KV_HEADS, ATTN_HEAD_DIM)
    qi = rope(qi.reshape(B, S, IDX_HEADS, IDX_HEAD_DIM), pos, rot_i, ROPE_THETA)
    ki = rope(rmsnorm(ki, kidx_norm)[:, :, None, :], pos, rot_i, ROPE_THETA)[:, :, 0]
    wi = wi * (IDX_HEADS ** -0.5 * IDX_HEAD_DIM ** -0.5)
    attn = dsa_attention(q, k, v, qi, ki, wi, k_sel)
    ssm = mamba2_ssd(z, xbc, dt_raw, conv_w, conv_b, dt_bias, a_log, d_skip, ssm_norm_g)
    return jnp.concatenate([attn, ssm], axis=-1) @ w_out


def retention(h, pos, w_in, norm_g, w_out):
    B, S, _ = h.shape
    q, k, v, g = _split(h @ w_in, (RET_QK_TOTAL, RET_QK_TOTAL, RET_V_TOTAL, RET_V_TOTAL))
    q = rope(q.reshape(B, S, RET_HEADS, RET_QK_DIM), pos, RET_QK_DIM, RET_THETA).astype(F32)
    k = rope(k.reshape(B, S, RET_HEADS, RET_QK_DIM), pos, RET_QK_DIM, RET_THETA).astype(F32) * RET_QK_DIM ** -0.5
    v = v.reshape(B, S, RET_HEADS, RET_V_DIM).astype(F32)
    Q = RET_CHUNK
    nc = S // Q
    log_gamma = jnp.log(1.0 - jnp.power(2.0, -5.0 - jnp.arange(RET_HEADS, dtype=F32)))
    i = jnp.arange(Q, dtype=F32)
    diff = i[:, None] - i[None, :]
    dmat = jnp.where(diff[None] >= 0, jnp.exp(jnp.maximum(diff, 0.0)[None] * log_gamma[:, None, None]), 0.0)
    q_decay = jnp.exp((i + 1.0)[:, None] * log_gamma[None])
    k_decay = jnp.exp((Q - 1.0 - i)[:, None] * log_gamma[None])
    chunk_decay = jnp.exp(Q * log_gamma)

    def chunks(t):
        return t.reshape((B, nc, Q) + t.shape[2:]).swapaxes(0, 1)

    def step(state, inp):
        qc, kc, vc = inp
        s = jnp.einsum('bihd,bjhd->bhij', qc, kc) * dmat
        inner = jnp.einsum('bhij,bjhv->bihv', s, vc)
        cross = jnp.einsum('bihd,bhdv->bihv', qc, state) * q_decay[None, :, :, None]
        state = state * chunk_decay[None, :, None, None] + jnp.einsum(
            'bjhd,bjhv->bhdv', kc * k_decay[None, :, :, None], vc)
        return state, inner + cross

    s0 = jnp.zeros((B, RET_HEADS, RET_QK_DIM, RET_V_DIM), F32)
    _, o = lax.scan(step, s0, (chunks(q), chunks(k), chunks(v)))
    o = o.swapaxes(0, 1).reshape(B, S, RET_HEADS, RET_V_DIM)
    o = rmsnorm(o, norm_g.reshape(RET_HEADS, RET_V_DIM)).reshape(B, S, RET_V_TOTAL)
    o = (jax.nn.silu(g.astype(F32)) * o).astype(h.dtype)
    return o @ w_out


def hier_moe(h, w_rg, w_re, w_gate, w_up, w_down):
    N, D = h.shape
    E = N_EXPERTS
    g_logits = (h @ w_rg).astype(F32)
    grp = jnp.argmax(g_logits, axis=-1)
    p_grp = jnp.take_along_axis(jax.nn.softmax(g_logits, axis=-1), grp[:, None], axis=1)
    e_logits = (h @ w_re).astype(F32).reshape(N, MOE_GROUPS, MOE_EXPERTS_PER_GROUP)
    e_in = jnp.take_along_axis(e_logits, grp[:, None, None], axis=1)[:, 0]
    top_v, top_i = lax.top_k(e_in, MOE_TOPK)
    gate = (p_grp * jax.nn.softmax(top_v, axis=-1)).reshape(-1)
    eid = (grp[:, None] * MOE_EXPERTS_PER_GROUP + top_i).reshape(-1).astype(jnp.int32)
    tok = jnp.repeat(jnp.arange(N, dtype=jnp.int32), MOE_TOPK)
    A = N * MOE_TOPK
    order = jnp.argsort(eid)
    eid_s = eid[order]
    tok_s = tok[order]
    gate_s = gate[order]
    counts = jnp.zeros((E,), jnp.int32).at[eid].add(1)
    start = jnp.cumsum(counts) - counts
    padded = (counts + MOE_BLOCK - 1) // MOE_BLOCK * MOE_BLOCK
    pad_end = jnp.cumsum(padded)
    pad_start = pad_end - padded
    dest = pad_start[eid_s] + jnp.arange(A, dtype=jnp.int32) - start[eid_s]
    P = (-(-A // MOE_BLOCK) + E) * MOE_BLOCK
    nblk = P // MOE_BLOCK
    slot_tok = jnp.zeros((P,), jnp.int32).at[dest].set(tok_s)
    slot_gate = jnp.zeros((P,), F32).at[dest].set(gate_s)
    blk_expert = jnp.minimum(
        jnp.searchsorted(pad_end, jnp.arange(nblk, dtype=jnp.int32) * MOE_BLOCK, side='right'), E - 1)

    def expert_block(args):
        e, toks, gts = args
        xb = h[toks]
        hid = jax.nn.silu(xb @ w_gate[e]) * (xb @ w_up[e])
        return (hid @ w_down[e]) * gts.astype(h.dtype)[:, None]

    out = lax.map(expert_block, (blk_expert, slot_tok.reshape(nblk, MOE_BLOCK),
                                 slot_gate.reshape(nblk, MOE_BLOCK)))
    return jnp.zeros((N, D), h.dtype).at[slot_tok].add(out.reshape(P, D))


def setup_inputs(seed: int = 0) -> dict:
    key = jax.random.key(seed)
    ks = jax.random.split(key, 22)
    ne = (DEPTH + 1) // 2
    no = DEPTH // 2
    D = D_MODEL
    nrm = jax.random.normal
    x = nrm(ks[0], (BATCH, SEQ, D), F32)
    positions = jnp.broadcast_to(jnp.arange(SEQ, dtype=jnp.int32), (BATCH, SEQ))
    mix_norm = 1.0 + 0.05 * nrm(ks[1], (DEPTH, D), F32)
    ffn_norm = 1.0 + 0.05 * nrm(ks[2], (DEPTH, D), F32)
    hyb_w_in = nrm(ks[3], (ne, D, HYB_IN), F32) * D ** -0.5
    attn_q_norm = 1.0 + 0.05 * nrm(ks[4], (ne, ATTN_HEAD_DIM), F32)
    attn_k_norm = 1.0 + 0.05 * nrm(ks[5], (ne, ATTN_HEAD_DIM), F32)
    idx_k_norm = 1.0 + 0.05 * nrm(ks[6], (ne, IDX_HEAD_DIM), F32)
    ssm_conv_w = nrm(ks[7], (ne, SSM_CONV, SSM_CONV_DIM), F32) * SSM_CONV ** -0.5
    ssm_conv_b = 0.02 * nrm(ks[8], (ne, SSM_CONV_DIM), F32)
    dt0 = jnp.exp(jax.random.uniform(ks[9], (ne, SSM_HEADS), F32, math.log(1e-3), math.log(1e-1)))
    ssm_dt_bias = dt0 + jnp.log(-jnp.expm1(-dt0))
    ssm_a_log = jnp.log(jax.random.uniform(ks[10], (ne, SSM_HEADS), F32, 1.0, 16.0))
    ssm_d = 1.0 + 0.1 * nrm(ks[11], (ne, SSM_HEADS), F32)
    ssm_norm = 1.0 + 0.05 * nrm(ks[12], (ne, SSM_INNER), F32)
    hyb_w_out = nrm(ks[13], (ne, HYB_MIX, D), F32) * HYB_MIX ** -0.5
    ret_w_in = nrm(ks[14], (no, D, RET_IN), F32) * D ** -0.5
    ret_norm = 1.0 + 0.05 * nrm(ks[15], (no, RET_V_TOTAL), F32)
    ret_w_out = nrm(ks[16], (no, RET_V_TOTAL, D), F32) * RET_V_TOTAL ** -0.5
    moe_router_group = nrm(ks[17], (DEPTH, D, MOE_GROUPS), F32) * D ** -0.5
    moe_router_expert = nrm(ks[18], (DEPTH, D, N_EXPERTS), F32) * D ** -0.5
    moe_w_gate = nrm(ks[19], (DEPTH, N_EXPERTS, D, EXPERT_FF), F32) * D ** -0.5
    moe_w_up = nrm(ks[20], (DEPTH, N_EXPERTS, D, EXPERT_FF), F32) * D ** -0.5
    moe_w_down = nrm(ks[21], (DEPTH, N_EXPERTS, EXPERT_FF, D), F32) * EXPERT_FF ** -0.5
    return {"x": x, "positions": positions, "mix_norm": mix_norm, "ffn_norm": ffn_norm,
            "hyb_w_in": hyb_w_in, "attn_q_norm": attn_q_norm, "attn_k_norm": attn_k_norm,
            "idx_k_norm": idx_k_norm, "ssm_conv_w": ssm_conv_w, "ssm_conv_b": ssm_conv_b,
            "ssm_dt_bias": ssm_dt_bias, "ssm_a_log": ssm_a_log, "ssm_d": ssm_d,
            "ssm_norm": ssm_norm, "hyb_w_out": hyb_w_out, "ret_w_in": ret_w_in,
            "ret_norm": ret_norm, "ret_w_out": ret_w_out,
            "moe_router_group": moe_router_group, "moe_router_expert": moe_router_expert,
            "moe_w_gate": moe_w_gate, "moe_w_up": moe_w_up, "moe_w_down": moe_w_down}


def reference(x, positions, mix_norm, ffn_norm, hyb_w_in, attn_q_norm, attn_k_norm,
              idx_k_norm, ssm_conv_w, ssm_conv_b, ssm_dt_bias, ssm_a_log, ssm_d,
              ssm_norm, hyb_w_out, ret_w_in, ret_norm, ret_w_out,
              moe_router_group, moe_router_expert, moe_w_gate, moe_w_up, moe_w_down):
    B, S, D = x.shape
    k_sel = min(TOPK_MAX, S // 4)
    for layer in range(DEPTH):
        i = layer // 2
        h = rmsnorm(x, mix_norm[layer])
        if layer % 2 == 0:
            mix = hybrid_attn_ssm(h, positions, hyb_w_in[i], attn_q_norm[i], attn_k_norm[i],
                                  idx_k_norm[i], ssm_conv_w[i], ssm_conv_b[i], ssm_dt_bias[i],
                                  ssm_a_log[i], ssm_d[i], ssm_norm[i], hyb_w_out[i], k_sel)
        else:
            mix = retention(h, positions, ret_w_in[i], ret_norm[i], ret_w_out[i])
        x = x + mix
        h = rmsnorm(x, ffn_norm[layer]).reshape(B * S, D)
        x = x + hier_moe(h, moe_router_group[layer], moe_router_expert[layer],
                         moe_w_gate[layer], moe_w_up[layer], moe_w_down[layer]).reshape(B, S, D)
    return x
```

```python
import functools

import jax
import jax.numpy as jnp
import numpy as np
from jax import lax
from jax.experimental import pallas as pl
from jax.experimental.pallas import tpu as pltpu

F32 = jnp.float32
BF16 = jnp.bfloat16
I32 = jnp.int32

ATTN_HEADS = 8
ATTN_KV_HEADS = 2
ATTN_GROUP = ATTN_HEADS // ATTN_KV_HEADS
ATTN_HEAD_DIM = 128
ATTN_INNER = ATTN_HEADS * ATTN_HEAD_DIM
ATTN_KV = ATTN_KV_HEADS * ATTN_HEAD_DIM
IDX_HEADS = 8
IDX_HEAD_DIM = 64
IDX_INNER = IDX_HEADS * IDX_HEAD_DIM
TOPK_MAX = 256
Q_BLOCK = 128
ROPE_THETA = 500000.0
ROPE_FRACTION = 4

SSM_HEADS = 16
SSM_HEAD_DIM = 64
SSM_INNER = SSM_HEADS * SSM_HEAD_DIM
SSM_GROUPS = 2
SSM_STATE = 128
SSM_CONV = 4
SSM_CONV_DIM = SSM_INNER + 2 * SSM_GROUPS * SSM_STATE
SSM_CHUNK = 128
SSM_GROUP_INNER = SSM_INNER // SSM_GROUPS

RET_HEADS = 8
RET_QK_DIM = 256
RET_V_DIM = 512
RET_QK_TOTAL = RET_HEADS * RET_QK_DIM
RET_V_TOTAL = RET_HEADS * RET_V_DIM
RET_CHUNK = 128
RET_THETA = 10000.0

MOE_GROUPS = 8
MOE_EXPERTS_PER_GROUP = 8
N_EXPERTS = MOE_GROUPS * MOE_EXPERTS_PER_GROUP
MOE_TOPK = 2
EXPERT_FF = 512
MOE_BLOCK = 128

EPS = 1e-6

LANES = 128
SUBLANES = 8
VMEM_LIMIT = 52 * 1024 * 1024

NEG_BIG = -1e30
INT_MIN = -2147483648

COL_XBC = 0
COL_K = SSM_CONV_DIM
COL_V = COL_K + ATTN_KV
COL_Z = COL_V + ATTN_KV
COL_Q = COL_Z + SSM_INNER
COL_QI = COL_Q + ATTN_INNER
COL_TAIL = COL_QI + IDX_INNER
HYB_COLS = 5120
TAIL_KI = 0
TAIL_WI = IDX_HEAD_DIM
TAIL_DT = TAIL_WI + IDX_HEADS


def _cparams(n_axes):
    return pltpu.CompilerParams(dimension_semantics=("arbitrary",) * n_axes,
                                vmem_limit_bytes=VMEM_LIMIT)


def _pick(n, pref):
    t = min(n, pref)
    while n % t:
        t //= 2
    return t


def _mm_kernel(*refs, n_a, has_norm, has_res, row_chunk):
    a_refs = refs[:n_a]
    pos = n_a
    g_ref = None
    if has_norm:
        g_ref = refs[pos]
        pos += 1
    w_refs = refs[pos:pos + n_a]
    pos += n_a
    res_ref = None
    if has_res:
        res_ref = refs[pos]
        pos += 1
    out_ref = refs[pos]
    pos += 1
    xn_ref = refs[pos] if has_norm else None
    tm = out_ref.shape[0]

    if has_norm:
        @pl.when(pl.program_id(1) == 0)
        def _():
            def body(r, carry):
                rows = pl.ds(pl.multiple_of(r * row_chunk, row_chunk), row_chunk)
                x = a_refs[0][rows, :]
                ms = jnp.mean(x * x, axis=-1, keepdims=True)
                xn_ref[rows, :] = (x * lax.rsqrt(ms + EPS) * g_ref[...]).astype(BF16)
                return carry
            lax.fori_loop(0, tm // row_chunk, body, 0)
        acc = jnp.dot(xn_ref[...], w_refs[0][...], preferred_element_type=F32)
    else:
        acc = jnp.dot(a_refs[0][...], w_refs[0][...], preferred_element_type=F32)
        for p in range(1, n_a):
            acc = acc + jnp.dot(a_refs[p][...], w_refs[p][...], preferred_element_type=F32)
    if has_res:
        acc = acc + res_ref[...]
    out_ref[...] = acc.astype(out_ref.dtype)


def fused_matmul(a_list, w, *, g=None, res=None, out_dtype=F32, tm=1024, tn=512, name="mm"):
    n = a_list[0].shape[0]
    kp = a_list[0].shape[1]
    m = w.shape[1]
    tm = _pick(n, tm)
    tn = _pick(m, tn)
    n_a = len(a_list)
    has_norm = g is not None
    has_res = res is not None
    in_specs = [pl.BlockSpec((tm, kp), lambda i, j: (i, 0)) for _ in a_list]
    args = list(a_list)
    if has_norm:
        in_specs.append(pl.BlockSpec((1, kp), lambda i, j: (0, 0)))
        args.append(g.reshape(1, kp).astype(F32))
    for p in range(n_a):
        in_specs.append(pl.BlockSpec((kp, tn), lambda i, j, p=p: (p, j)))
        args.append(w)
    if has_res:
        in_specs.append(pl.BlockSpec((tm, tn), lambda i, j: (i, j)))
        args.append(res)
    scratch = [pltpu.VMEM((tm, kp), BF16)] if has_norm else []
    kern = functools.partial(_mm_kernel, n_a=n_a, has_norm=has_norm, has_res=has_res,
                             row_chunk=_pick(tm, 128))
    return pl.pallas_call(
        kern,
        grid=(n // tm, m // tn),
        in_specs=in_specs,
        out_specs=pl.BlockSpec((tm, tn), lambda i, j: (i, j)),
        out_shape=jax.ShapeDtypeStruct((n, m), out_dtype),
        scratch_shapes=scratch,
        compiler_params=_cparams(2),
        name=name,
    )(*args)


def _rope_lanes(x, cosf, s_neg, s_pos, half):
    width = x.shape[-1]
    return (x * cosf + pltpu.roll(x, width - half, axis=1) * s_neg
            + pltpu.roll(x, half, axis=1) * s_pos)


def _prep_kernel(q_ref, k_ref, v_ref, qi_ref, tail_ref, pos_ref, inva_ref, invi_ref, invt_ref,
                 qn_ref, kn_ref, kin_ref,
                 qo_ref, ko_ref, vo_ref, qio_ref, kio_ref, wio_ref):
    tb = q_ref.shape[0]
    posf = pos_ref[...].astype(F32)
    lane = lax.broadcasted_iota(I32, (tb, LANES), 1)

    half = ATTN_HEAD_DIM // ROPE_FRACTION // 2
    ang = posf * inva_ref[...]
    cosf = jnp.cos(ang)
    sinf = jnp.sin(ang)
    s_neg = jnp.where(lane < half, -sinf, 0.0)
    s_pos = jnp.where((lane >= half) & (lane < 2 * half), sinf, 0.0)
    scale = ATTN_HEAD_DIM ** -0.5
    for h in range(ATTN_HEADS):
        x = q_ref[:, h * LANES:(h + 1) * LANES]
        ms = jnp.mean(x * x, axis=-1, keepdims=True)
        x = x * lax.rsqrt(ms + EPS) * qn_ref[...]
        x = _rope_lanes(x, cosf, s_neg, s_pos, half)
        qo_ref[h] = (x * scale).astype(BF16)
    for h in range(ATTN_KV_HEADS):
        x = k_ref[:, h * LANES:(h + 1) * LANES]
        ms = jnp.mean(x * x, axis=-1, keepdims=True)
        x = x * lax.rsqrt(ms + EPS) * kn_ref[...]
        x = _rope_lanes(x, cosf, s_neg, s_pos, half)
        ko_ref[h] = x.astype(BF16)
        vo_ref[h] = v_ref[:, h * LANES:(h + 1) * LANES].astype(BF16)

    half_i = IDX_HEAD_DIM // ROPE_FRACTION // 2
    lane_i = lax.broadcasted_iota(I32, (tb, IDX_INNER), 1) & (IDX_HEAD_DIM - 1)
    ang_i = posf * invi_ref[...]
    cos_i = jnp.cos(ang_i)
    sin_i = jnp.sin(ang_i)
    sn_i = jnp.where(lane_i < half_i, -sin_i, 0.0)
    sp_i = jnp.where((lane_i >= half_i) & (lane_i < 2 * half_i), sin_i, 0.0)
    qi = _rope_lanes(qi_ref[...], cos_i, sn_i, sp_i, half_i)
    for h in range(IDX_HEADS):
        slab = qi[:, (h // 2) * LANES:(h // 2 + 1) * LANES]
        if h % 2:
            slab = pltpu.roll(slab, IDX_HEAD_DIM, axis=1)
        qio_ref[h] = jnp.where(lane < IDX_HEAD_DIM, slab, 0.0).astype(BF16)

    t = tail_ref[...]
    is_ki = lane < IDX_HEAD_DIM
    ms = jnp.sum(jnp.where(is_ki, t * t, 0.0), axis=-1, keepdims=True) * (1.0 / IDX_HEAD_DIM)
    kin = t * lax.rsqrt(ms + EPS) * kin_ref[...]
    ang_t = posf * invt_ref[...]
    cos_t = jnp.cos(ang_t)
    sin_t = jnp.sin(ang_t)
    sn_t = jnp.where(lane < half_i, -sin_t, 0.0)
    sp_t = jnp.where((lane >= half_i) & (lane < 2 * half_i), sin_t, 0.0)
    kin = _rope_lanes(kin, cos_t, sn_t, sp_t, half_i)
    kio_ref[...] = jnp.where(is_ki, kin, 0.0).astype(BF16)
    wio_ref[...] = t * (IDX_HEADS ** -0.5 * IDX_HEAD_DIM ** -0.5)


def hyb_prep(proj, pos, inv_a, inv_i, inv_t, q_norm, k_norm, kidx_norm_pad):
    n = proj.shape[0]
    tb = _pick(n, 256)

    def col(width, off):
        return pl.BlockSpec((tb, width), lambda i: (i, off // width))

    def full(shape):
        return pl.BlockSpec(shape, lambda i: (0,) * len(shape))

    in_specs = [col(ATTN_INNER, COL_Q), col(ATTN_KV, COL_K), col(ATTN_KV, COL_V),
                col(IDX_INNER, COL_QI), col(LANES, COL_TAIL),
                pl.BlockSpec((tb, 1), lambda i: (i, 0)),
                full((1, LANES)), full((1, IDX_INNER)), full((1, LANES)),
                full((1, LANES)), full((1, LANES)), full((1, LANES))]
    out_shape = [jax.ShapeDtypeStruct((ATTN_HEADS, n, LANES), BF16),
                 jax.ShapeDtypeStruct((ATTN_KV_HEADS, n, LANES), BF16),
                 jax.ShapeDtypeStruct((ATTN_KV_HEADS, n, LANES), BF16),
                 jax.ShapeDtypeStruct((IDX_HEADS, n, LANES), BF16),
                 jax.ShapeDtypeStruct((n, LANES), BF16),
                 jax.ShapeDtypeStruct((n, LANES), F32)]
    out_specs = [pl.BlockSpec((ATTN_HEADS, tb, LANES), lambda i: (0, i, 0)),
                 pl.BlockSpec((ATTN_KV_HEADS, tb, LANES), lambda i: (0, i, 0)),
                 pl.BlockSpec((ATTN_KV_HEADS, tb, LANES), lambda i: (0, i, 0)),
                 pl.BlockSpec((IDX_HEADS, tb, LANES), lambda i: (0, i, 0)),
                 pl.BlockSpec((tb, LANES), lambda i: (i, 0)),
                 pl.BlockSpec((tb, LANES), lambda i: (i, 0))]
    return pl.pallas_call(
        _prep_kernel, grid=(n // tb,), in_specs=in_specs, out_specs=out_specs,
        out_shape=out_shape, compiler_params=_cparams(1), name="hyb_prep",
    )(proj, proj, proj, proj, proj, pos, inv_a, inv_i, inv_t, q_norm, k_norm, kidx_norm_pad)


def _sortable(x):
    b = pltpu.bitcast(x, I32)
    return jnp.where(b < 0, b ^ jnp.int32(0x7FFFFFFF), b)


def _dsa_kernel(q_ref, qi_ref, wi_ref, k_ref, v_ref, ki_ref, triu_ref, out_ref,
                keys_ref, bias_ref, wb_ref, m_ref, l_ref, acc_ref, *, k_sel, kc):
    i = pl.program_id(1)
    tq = Q_BLOCK
    nkc = (i * tq + tq + kc - 1) // kc
    row = i * tq + lax.broadcasted_iota(I32, (tq, kc), 0)
    lane = lax.broadcasted_iota(I32, (tq, kc), 1)

    w = wi_ref[...]
    for h in range(IDX_HEADS):
        wb_ref[h] = jnp.broadcast_to(w[:, TAIL_WI + h:TAIL_WI + h + 1], (tq, kc))

    def idx_body(c, carry):
        cols = pl.ds(pl.multiple_of(c * kc, kc), kc)
        kic = ki_ref[cols, :]
        sc = jnp.zeros((tq, kc), F32)
        for h in range(IDX_HEADS):
            d = lax.dot_general(qi_ref[h], kic, (((1,), (1,)), ((), ())),
                                preferred_element_type=F32)
            sc = sc + jnp.maximum(d, 0.0) * wb_ref[h]
        causal = (c * kc + lane) <= row
        keys_ref[:, cols] = jnp.where(causal, _sortable(sc), jnp.int32(INT_MIN))
        return carry
    lax.fori_loop(0, nkc, idx_body, 0)

    def fold(x):
        acc = x[:, 0:LANES]
        for j in range(1, kc // LANES):
            acc = acc + x[:, j * LANES:(j + 1) * LANES]
        return acc

    def count(cand, strict):
        def body(c, acc):
            cols = pl.ds(pl.multiple_of(c * kc, kc), kc)
            kk = keys_ref[:, cols]
            hit = (kk > cand) if strict else (kk >= cand)
            return acc + fold(jnp.where(hit, 1.0, 0.0))
        acc = lax.fori_loop(0, nkc, body, jnp.zeros((tq, LANES), F32))
        return jnp.sum(acc, axis=1, keepdims=True)

    kf = jnp.float32(k_sel)
    zero = jnp.zeros((tq, 1), I32)
    lo = jnp.where(count(zero, False) >= kf, zero, jnp.full((tq, 1), INT_MIN, I32))

    def bis_body(it, lo):
        cand = lo + jnp.left_shift(jnp.int32(1), jnp.int32(30) - it)
        return jnp.where(count(cand, False) >= kf, cand, lo)
    thr = lax.fori_loop(0, 31, bis_body, lo)
    need = kf - count(thr, True)

    def fin_body(c, run):
        cols = pl.ds(pl.multiple_of(c * kc, kc), kc)
        kk = keys_ref[:, cols]
        eq = kk == thr
        eqf = jnp.where(eq, 1.0, 0.0)
        incl = jnp.dot(eqf.astype(BF16), triu_ref[...], preferred_element_type=F32)
        sel = (kk > thr) | (eq & ((run + incl) <= need))
        causal = (c * kc + lane) <= row
        bias_ref[:, cols] = jnp.where(sel & causal, 0.0, NEG_BIG)
        return run + jnp.sum(eqf, axis=1, keepdims=True)
    lax.fori_loop(0, nkc, fin_body, jnp.zeros((tq, 1), F32))

    m_ref[...] = jnp.full(m_ref.shape, NEG_BIG, F32)
    l_ref[...] = jnp.zeros(l_ref.shape, F32)
    acc_ref[...] = jnp.zeros(acc_ref.shape, F32)

    def att_body(c, carry):
        cols = pl.ds(pl.multiple_of(c * kc, kc), kc)
        bias = bias_ref[:, cols]
        for j in range(ATTN_KV_HEADS):
            kch = k_ref[j, cols, :]
            vch = v_ref[j, cols, :]
            for g in range(ATTN_GROUP):
                h = j * ATTN_GROUP + g
                s = lax.dot_general(q_ref[h], kch, (((1,), (1,)), ((), ())),
                                    preferred_element_type=F32) + bias
                m_old = m_ref[h]
                m_new = jnp.maximum(m_old, jnp.max(s, axis=1, keepdims=True))
                alpha = jnp.exp(m_old - m_new)
                p = jnp.exp(s - m_new)
                l_ref[h] = alpha * l_ref[h] + jnp.sum(p, axis=1, keepdims=True)
                acc_ref[h] = alpha * acc_ref[h] + jnp.dot(p.astype(BF16), vch,
                                                          preferred_element_type=F32)
                m_ref[h] = m_new
        return carry
    lax.fori_loop(0, nkc, att_body, 0)

    for h in range(ATTN_HEADS):
        out_ref[:, h * LANES:(h + 1) * LANES] = (acc_ref[h] / l_ref[h]).astype(out_ref.dtype)


def dsa_attention(q_r, qi_r, wi, k_r, v_r, ki_r, batch, seq, k_sel):
    n = batch * seq
    nq = seq // Q_BLOCK
    kc = _pick(seq, 512)
    triu = (np.arange(kc)[:, None] <= np.arange(kc)[None, :]).astype(np.float32)
    triu = jnp.asarray(triu, BF16)
    kern = functools.partial(_dsa_kernel, k_sel=k_sel, kc=kc)
    in_specs = [
        pl.BlockSpec((ATTN_HEADS, Q_BLOCK, LANES), lambda b, i: (0, b * nq + i, 0)),
        pl.BlockSpec((IDX_HEADS, Q_BLOCK, LANES), lambda b, i: (0, b * nq + i, 0)),
        pl.BlockSpec((Q_BLOCK, LANES), lambda b, i: (b * nq + i, 0)),
        pl.BlockSpec((ATTN_KV_HEADS, seq, LANES), lambda b, i: (0, b, 0)),
        pl.BlockSpec((ATTN_KV_HEADS, seq, LANES), lambda b, i: (0, b, 0)),
        pl.BlockSpec((seq, LANES), lambda b, i: (b, 0)),
        pl.BlockSpec((kc, kc), lambda b, i: (0, 0)),
    ]
    scratch = [pltpu.VMEM((Q_BLOCK, seq), I32), pltpu.VMEM((Q_BLOCK, seq), F32),
               pltpu.VMEM((IDX_HEADS, Q_BLOCK, kc), F32),
               pltpu.VMEM((ATTN_HEADS, Q_BLOCK, 1), F32),
               pltpu.VMEM((ATTN_HEADS, Q_BLOCK, 1), F32),
               pltpu.VMEM((ATTN_HEADS, Q_BLOCK, LANES), F32)]
    return pl.pallas_call(
        kern, grid=(batch, nq), in_specs=in_specs,
        out_specs=pl.BlockSpec((Q_BLOCK, ATTN_INNER), lambda b, i: (b * nq + i, 0)),
        out_shape=jax.ShapeDtypeStruct((n, ATTN_INNER), BF16),
        scratch_shapes=scratch, compiler_params=_cparams(2), name="dsa_attention",
    )(q_r, qi_r, wi, k_r, v_r, ki_r, triu)


def _silu(x):
    return x * (1.0 / (1.0 + jnp.exp(-x)))


def _ssd_kernel(xbc_ref, z_ref, tail_ref, cw_ref, cb_ref, dtb_ref, alog_ref, dexp_ref, ng_ref,
                tri_ref, e_ref, out_ref, state_ref, ext_ref, y_ref):
    q = SSM_CHUNK
    hi = lax.Precision.HIGHEST

    @pl.when(pl.program_id(1) == 0)
    def _():
        state_ref[...] = jnp.zeros(state_ref.shape, F32)
        ext_ref[0:SUBLANES, :] = jnp.zeros((SUBLANES, SSM_CONV_DIM), F32)

    x = xbc_ref[...]
    ext_ref[SUBLANES:SUBLANES + q, :] = x
    conv = x * cw_ref[SSM_CONV - 1:SSM_CONV, :] + cb_ref[...]
    for k in range(1, SSM_CONV):
        conv = conv + ext_ref[SUBLANES - k:SUBLANES - k + q, :] * cw_ref[SSM_CONV - 1 - k:SSM_CONV - k, :]
    ext_ref[0:SUBLANES, :] = x[q - SUBLANES:q, :]
    xbc = _silu(conv)
    xs = xbc[:, 0:SSM_INNER]
    bm = xbc[:, SSM_INNER:SSM_INNER + SSM_GROUPS * SSM_STATE]
    cm = xbc[:, SSM_INNER + SSM_GROUPS * SSM_STATE:SSM_CONV_DIM]

    lane = lax.broadcasted_iota(I32, (q, LANES), 1)
    is_dt = (lane >= TAIL_DT) & (lane < TAIL_DT + SSM_HEADS)
    raw = tail_ref[...] + dtb_ref[...]
    dt = jnp.where(is_dt, jnp.maximum(raw, 0.0) + jnp.log1p(jnp.exp(-jnp.abs(raw))), 0.0)
    a = -jnp.exp(alog_ref[...])
    da = jnp.where(is_dt, dt * a, 0.0)
    acum = jnp.dot(tri_ref[...], da, precision=hi, preferred_element_type=F32)
    acum_t = acum.T
    acum_x = jnp.dot(acum, e_ref[...], precision=hi, preferred_element_type=F32)
    dt_x = jnp.dot(dt, e_ref[...], precision=hi, preferred_element_type=F32)
    last_x = acum_x[q - 1:q, :]
    xdt = xs * dt_x
    xdec = xdt * jnp.exp(last_x - acum_x)
    ri = lax.broadcasted_iota(I32, (q, q), 0)
    ci = lax.broadcasted_iota(I32, (q, q), 1)
    tril = ri >= ci
    pairs_per_group = SSM_GROUP_INNER // LANES

    for g in range(SSM_GROUPS):
        bg = bm[:, g * SSM_STATE:(g + 1) * SSM_STATE]
        cg = cm[:, g * SSM_STATE:(g + 1) * SSM_STATE]
        cgb = cg.astype(BF16)
        gsl = slice(g * SSM_GROUP_INNER, (g + 1) * SSM_GROUP_INNER)
        cb = lax.dot_general(cgb, bg.astype(BF16), (((1,), (1,)), ((), ())),
                             preferred_element_type=F32)
        y_off = jnp.dot(cgb, state_ref[g].astype(BF16), preferred_element_type=F32)
        y_ref[:, gsl] = y_off * jnp.exp(acum_x[:, gsl])
        for mth in range(pairs_per_group):
            slab = slice((g * pairs_per_group + mth) * LANES, (g * pairs_per_group + mth + 1) * LANES)
            xp = xdt[:, slab]
            yd = None
            for side in range(2):
                hl = TAIL_DT + (g * pairs_per_group + mth) * 2 + side
                seg = acum[:, hl:hl + 1] - acum_t[hl:hl + 1, :]
                lmat = jnp.exp(jnp.where(tril, seg, NEG_BIG))
                if side == 0:
                    xh = jnp.where(lane < SSM_HEAD_DIM, xp, 0.0)
                else:
                    xh = jnp.where(lane >= SSM_HEAD_DIM, xp, 0.0)
                part = jnp.dot((cb * lmat).astype(BF16), xh.astype(BF16), preferred_element_type=F32)
                yd = part if yd is None else yd + part
            y_ref[:, slab] = y_ref[:, slab] + yd
        st = jnp.dot(bg.T.astype(BF16), xdec[:, gsl].astype(BF16), preferred_element_type=F32)
        state_ref[g] = state_ref[g] * jnp.exp(last_x[:, gsl]) + st

    y = (y_ref[...] + dexp_ref[...] * xs) * _silu(z_ref[...])
    for g in range(SSM_GROUPS):
        gsl = slice(g * SSM_GROUP_INNER, (g + 1) * SSM_GROUP_INNER)
        yg = y[:, gsl]
        ms = jnp.mean(yg * yg, axis=-1, keepdims=True)
        out_ref[:, gsl] = (yg * lax.rsqrt(ms + EPS) * ng_ref[:, gsl]).astype(out_ref.dtype)


def ssd_mixer(proj, conv_w, conv_b, dtb_pad, alog_pad, d_exp, norm_g, batch, seq):
    n = batch * seq
    q = SSM_CHUNK
    nc = seq // q
    tri = jnp.asarray((np.arange(q)[:, None] >= np.arange(q)[None, :]).astype(np.float32))
    e = np.zeros((LANES, SSM_INNER), np.float32)
    for h in range(SSM_HEADS):
        e[TAIL_DT + h, h * SSM_HEAD_DIM:(h + 1) * SSM_HEAD_DIM] = 1.0
    e = jnp.asarray(e)

    def col(width, off):
        return pl.BlockSpec((q, width), lambda b, c: (b * nc + c, off // width))

    def full(shape):
        return pl.BlockSpec(shape, lambda b, c: (0,) * len(shape))

    in_specs = [col(SSM_CONV_DIM, COL_XBC), col(SSM_INNER, COL_Z), col(LANES, COL_TAIL),
                full((SSM_CONV, SSM_CONV_DIM)), full((1, SSM_CONV_DIM)), full((1, LANES)),
                full((1, LANES)), full((1, SSM_INNER)), full((1, SSM_INNER)),
                full((q, q)), full((LANES, SSM_INNER))]
    scratch = [pltpu.VMEM((SSM_GROUPS, SSM_STATE, SSM_GROUP_INNER), F32),
               pltpu.VMEM((SUBLANES + q, SSM_CONV_DIM), F32),
               pltpu.VMEM((q, SSM_INNER), F32)]
    return pl.pallas_call(
        _ssd_kernel, grid=(batch, nc), in_specs=in_specs,
        out_specs=pl.BlockSpec((q, SSM_INNER), lambda b, c: (b * nc + c, 0)),
        out_shape=jax.ShapeDtypeStruct((n, SSM_INNER), BF16),
        scratch_shapes=scratch, compiler_params=_cparams(2), name="ssd_mixer",
    )(proj, proj, proj, conv_w, conv_b, dtb_pad, alog_pad, d_exp, norm_g, tri, e)


def _ret_kernel(q_ref, k_ref, v_ref, g_ref, pos_ref, inv_ref, dmat_ref, qdec_ref, kdec_ref,
                cdec_ref, ng_ref, out_ref, state_ref):
    half = RET_QK_DIM // 2

    @pl.when(pl.program_id(1) == 0)
    def _():
        state_ref[...] = jnp.zeros(state_ref.shape, F32)

    ang = pos_ref[...].astype(F32) * inv_ref[...]
    cosf = jnp.cos(ang)
    sinf = jnp.sin(ang)

    def rope(x):
        x1 = x[:, 0:half]
        x2 = x[:, half:2 * half]
        return jnp.concatenate([x1 * cosf - x2 * sinf, x2 * cosf + x1 * sinf], axis=-1)

    for h in range(RET_HEADS):
        qh = rope(q_ref[:, h * RET_QK_DIM:(h + 1) * RET_QK_DIM].astype(F32))
        kh = rope(k_ref[:, h * RET_QK_DIM:(h + 1) * RET_QK_DIM].astype(F32)) * (RET_QK_DIM ** -0.5)
        vh = v_ref[:, h * RET_V_DIM:(h + 1) * RET_V_DIM]
        qb = qh.astype(BF16)
        s = lax.dot_general(qb, kh.astype(BF16), (((1,), (1,)), ((), ())),
                            preferred_element_type=F32) * dmat_ref[h]
        inner = jnp.dot(s.astype(BF16), vh, preferred_element_type=F32)
        st = state_ref[h]
        cross = jnp.dot(qb, st.astype(BF16), preferred_element_type=F32) * qdec_ref[h]
        kd = (kh * kdec_ref[h]).T.astype(BF16)
        state_ref[h] = st * cdec_ref[h] + jnp.dot(kd, vh, preferred_element_type=F32)
        o = inner + cross
        ms = jnp.mean(o * o, axis=-1, keepdims=True)
        o = o * lax.rsqrt(ms + EPS) * ng_ref[:, h * RET_V_DIM:(h + 1) * RET_V_DIM]
        gate = _silu(g_ref[:, h * RET_V_DIM:(h + 1) * RET_V_DIM].astype(F32))
        out_ref[:, h * RET_V_DIM:(h + 1) * RET_V_DIM] = (gate * o).astype(out_ref.dtype)


def retention_mixer(proj, pos, norm_g, batch, seq):
    n = batch * seq
    q = RET_CHUNK
    nc = seq // q
    half = RET_QK_DIM // 2
    inv = jnp.power(RET_THETA, -(jnp.arange(half, dtype=F32) * 2.0 / RET_QK_DIM)).reshape(1, half)
    log_gamma = jnp.log(1.0 - jnp.power(2.0, -5.0 - jnp.arange(RET_HEADS, dtype=F32)))
    i = jnp.arange(q, dtype=F32)
    diff = i[:, None] - i[None, :]
    dmat = jnp.where(diff[None] >= 0,
                     jnp.exp(jnp.maximum(diff, 0.0)[None] * log_gamma[:, None, None]), 0.0)
    qdec = jnp.exp((i + 1.0)[None, :, None] * log_gamma[:, None, None])
    kdec = jnp.exp((q - 1.0 - i)[None, :, None] * log_gamma[:, None, None])
    cdec = jnp.broadcast_to(jnp.exp(q * log_gamma)[:, None, None], (RET_HEADS, 1, RET_V_DIM))

    def col(width, off):
        return pl.BlockSpec((q, width), lambda b, c: (b * nc + c, off // width))

    def full(shape):
        return pl.BlockSpec(shape, lambda b, c: (0,) * len(shape))

    in_specs = [col(RET_QK_TOTAL, 0), col(RET_QK_TOTAL, RET_QK_TOTAL),
                col(RET_V_TOTAL, 2 * RET_QK_TOTAL), col(RET_V_TOTAL, 2 * RET_QK_TOTAL + RET_V_TOTAL),
                pl.BlockSpec((q, 1), lambda b, c: (b * nc + c, 0)),
                full((1, half)), full((RET_HEADS, q, q)), full((RET_HEADS, q, 1)),
                full((RET_HEADS, q, 1)), full((RET_HEADS, 1, RET_V_DIM)), full((1, RET_V_TOTAL))]
    return pl.pallas_call(
        _ret_kernel, grid=(batch, nc), in_specs=in_specs,
        out_specs=pl.BlockSpec((q, RET_V_TOTAL), lambda b, c: (b * nc + c, 0)),
        out_shape=jax.ShapeDtypeStruct((n, RET_V_TOTAL), BF16),
        scratch_shapes=[pltpu.VMEM((RET_HEADS, RET_QK_DIM, RET_V_DIM), F32)],
        compiler_params=_cparams(2), name="retention",
    )(proj, proj, proj, proj, pos, inv, dmat, qdec, kdec, cdec, norm_g.reshape(1, RET_V_TOTAL))


ROUTE_E0 = MOE_GROUPS
INFO_EID, INFO_GATE, INFO_RANK = 0, 2, 4


def _router_kernel(x_ref, g_ref, wr_ref, lt_ref, h_ref, info_ref, cnt_ref, run_ref):
    tb = x_ref.shape[0]

    @pl.when(pl.program_id(0) == 0)
    def _():
        run_ref[...] = jnp.zeros(run_ref.shape, F32)

    x = x_ref[...]
    ms = jnp.mean(x * x, axis=-1, keepdims=True)
    hn = x * lax.rsqrt(ms + EPS) * g_ref[...]
    h_ref[...] = hn.astype(BF16)
    logits = jnp.dot(hn, wr_ref[...], precision=lax.Precision.HIGHEST, preferred_element_type=F32)

    lane = lax.broadcasted_iota(I32, (tb, LANES), 1).astype(F32)
    far = jnp.float32(4 * LANES)
    ninf = jnp.float32(-jnp.inf)
    gl = jnp.where(lane < MOE_GROUPS, logits, ninf)
    gmax = jnp.max(gl, axis=1, keepdims=True)
    grp = jnp.min(jnp.where(gl == gmax, lane, far), axis=1, keepdims=True)
    p_grp = 1.0 / jnp.sum(jnp.exp(gl - gmax), axis=1, keepdims=True)
    lo = ROUTE_E0 + grp * MOE_EXPERTS_PER_GROUP
    el = jnp.where((lane >= lo) & (lane < lo + MOE_EXPERTS_PER_GROUP), logits, ninf)
    m1 = jnp.max(el, axis=1, keepdims=True)
    i1 = jnp.min(jnp.where(el == m1, lane, far), axis=1, keepdims=True)
    el2 = jnp.where(lane == i1, ninf, el)
    m2 = jnp.max(el2, axis=1, keepdims=True)
    i2 = jnp.min(jnp.where(el2 == m2, lane, far), axis=1, keepdims=True)
    e2 = jnp.exp(m2 - m1)
    g1 = p_grp / (1.0 + e2)
    g2 = p_grp * e2 / (1.0 + e2)
    oh1 = jnp.where(lane == i1, 1.0, 0.0)
    oh2 = jnp.where(lane == i2, 1.0, 0.0)
    cnt = oh1 + oh2
    before = jnp.dot(lt_ref[...], cnt.astype(BF16), preferred_element_type=F32) + run_ref[...]
    r1 = jnp.sum(oh1 * before, axis=1, keepdims=True)
    r2 = jnp.sum(oh2 * before, axis=1, keepdims=True)
    run_ref[...] = run_ref[...] + jnp.sum(cnt, axis=0, keepdims=True)
    cnt_ref[...] = run_ref[...]

    info = jnp.where(lane == INFO_EID, i1 - ROUTE_E0, 0.0)
    info = jnp.where(lane == INFO_EID + 1, i2 - ROUTE_E0, info)
    info = jnp.where(lane == INFO_GATE, g1, info)
    info = jnp.where(lane == INFO_GATE + 1, g2, info)
    info = jnp.where(lane == INFO_RANK, r1, info)
    info = jnp.where(lane == INFO_RANK + 1, r2, info)
    info_ref[...] = info


def moe_router(x, g, w_router_pad):
    n, d = x.shape
    tb = _pick(n, 512)
    lt = jnp.asarray((np.arange(tb)[:, None] > np.arange(tb)[None, :]).astype(np.float32), BF16)
    return pl.pallas_call(
        _router_kernel, grid=(n // tb,),
        in_specs=[pl.BlockSpec((tb, d), lambda i: (i, 0)), pl.BlockSpec((1, d), lambda i: (0, 0)),
                  pl.BlockSpec((d, LANES), lambda i: (0, 0)), pl.BlockSpec((tb, tb), lambda i: (0, 0))],
        out_specs=[pl.BlockSpec((tb, d), lambda i: (i, 0)), pl.BlockSpec((tb, LANES), lambda i: (i, 0)),
                   pl.BlockSpec((1, LANES), lambda i: (0, 0))],
        out_shape=[jax.ShapeDtypeStruct((n, d), BF16), jax.ShapeDtypeStruct((n, LANES), F32),
                   jax.ShapeDtypeStruct((1, LANES), F32)],
        scratch_shapes=[pltpu.VMEM((1, LANES), F32)],
        compiler_params=_cparams(1), name="moe_router",
    )(x, g.reshape(1, d), w_router_pad, lt)


def _expert_kernel(be_ref, nused_ref, x_ref, wg_ref, wu_ref, wd_ref, out_ref, wgu_s, wd_s):
    i = pl.program_id(0)
    d = x_ref.shape[1]
    ff = EXPERT_FF

    @pl.when(i < nused_ref[0])
    def _():
        changed = jnp.logical_or(i == 0, be_ref[i] != be_ref[jnp.maximum(i - 1, 0)])

        @pl.when(changed)
        def _():
            rc = _pick(d, 256)

            def body(r, carry):
                rows = pl.ds(pl.multiple_of(r * rc, rc), rc)
                wgu_s[rows, 0:ff] = wg_ref[0, rows, :].astype(BF16)
                wgu_s[rows, ff:2 * ff] = wu_ref[0, rows, :].astype(BF16)
                return carry
            lax.fori_loop(0, d // rc, body, 0)
            wd_s[...] = wd_ref[0].astype(BF16)

        gu = jnp.dot(x_ref[...], wgu_s[...], preferred_element_type=F32)
        hid = _silu(gu[:, 0:ff]) * gu[:, ff:2 * ff]
        out_ref[...] = jnp.dot(hid.astype(BF16), wd_s[...], preferred_element_type=F32)

    @pl.when(i >= nused_ref[0])
    def _():
        out_ref[...] = jnp.zeros(out_ref.shape, out_ref.dtype)


def moe_experts(xs, blk_expert, nused, w_gate, w_up, w_down):
    p, d = xs.shape
    nblk = p // MOE_BLOCK
    ff = EXPERT_FF
    grid_spec = pltpu.PrefetchScalarGridSpec(
        num_scalar_prefetch=2, grid=(nblk,),
        in_specs=[pl.BlockSpec((MOE_BLOCK, d), lambda i, be, nu: (i, 0)),
                  pl.BlockSpec((1, d, ff), lambda i, be, nu: (be[i], 0, 0)),
                  pl.BlockSpec((1, d, ff), lambda i, be, nu: (be[i], 0, 0)),
                  pl.BlockSpec((1, ff, d), lambda i, be, nu: (be[i], 0, 0))],
        out_specs=pl.BlockSpec((MOE_BLOCK, d), lambda i, be, nu: (i, 0)),
        scratch_shapes=[pltpu.VMEM((d, 2 * ff), BF16), pltpu.VMEM((ff, d), BF16)])
    return pl.pallas_call(
        _expert_kernel, grid_spec=grid_spec,
        out_shape=jax.ShapeDtypeStruct((p, d), F32),
        compiler_params=_cparams(1), name="moe_experts",
    )(blk_expert, nused, xs, w_gate, w_up, w_down)


def hier_moe_layer(x, norm_g, w_rg, w_re, w_gate, w_up, w_down):
    n, d = x.shape
    w_router = jnp.concatenate(
        [w_rg, w_re, jnp.zeros((d, LANES - MOE_GROUPS - N_EXPERTS), F32)], axis=1)
    h, info, cnt = moe_router(x, norm_g, w_router)
    counts = cnt[0, ROUTE_E0:ROUTE_E0 + N_EXPERTS].astype(I32)
    eid = info[:, INFO_EID:INFO_EID + MOE_TOPK].astype(I32)
    gate = info[:, INFO_GATE:INFO_GATE + MOE_TOPK]
    rank = info[:, INFO_RANK:INFO_RANK + MOE_TOPK].astype(I32)
    padded = (counts + MOE_BLOCK - 1) // MOE_BLOCK * MOE_BLOCK
    pad_end = jnp.cumsum(padded)
    pad_start = pad_end - padded
    dest = pad_start[eid] + rank
    a = n * MOE_TOPK
    p = (-(-a // MOE_BLOCK) + N_EXPERTS) * MOE_BLOCK
    nblk = p // MOE_BLOCK
    blk_expert = jnp.minimum(
        jnp.searchsorted(pad_end, jnp.arange(nblk, dtype=I32) * MOE_BLOCK, side='right'),
        N_EXPERTS - 1).astype(I32)
    nused = (pad_end[-1:] // MOE_BLOCK).astype(I32)
    tok = jnp.repeat(jnp.arange(n, dtype=I32), MOE_TOPK)
    slot_tok = jnp.zeros((p,), I32).at[dest.reshape(-1)].set(tok)
    xs = h[slot_tok]
    y = moe_experts(xs, blk_expert, nused, w_gate, w_up, w_down)
    yt = y[dest]
    return x + yt[:, 0] * gate[:, 0:1] + yt[:, 1] * gate[:, 1:2]


def _pad_lanes(v, offset, width=LANES):
    out = jnp.zeros((1, width), F32)
    return out.at[0, offset:offset + v.shape[0]].set(v.astype(F32))


def hybrid_layer(x, pos, norm_g, w_in, q_norm, k_norm, kidx_norm, conv_w, conv_b, dt_bias, a_log,
                 d_skip, ssm_norm, w_out, batch, seq, k_sel):
    n, d = x.shape
    offs = np.cumsum([0, ATTN_INNER, ATTN_KV, ATTN_KV, IDX_INNER, IDX_HEAD_DIM, IDX_HEADS,
                      SSM_INNER, SSM_CONV_DIM, SSM_HEADS])
    seg = {name: w_in[:, offs[j]:offs[j + 1]] for j, name in
           enumerate(["q", "k", "v", "qi", "ki", "wi", "z", "xbc", "dt"])}
    tail_pad = LANES - IDX_HEAD_DIM - IDX_HEADS - SSM_HEADS
    w_perm = jnp.concatenate(
        [seg["xbc"], seg["k"], seg["v"], seg["z"], seg["q"], seg["qi"], seg["ki"], seg["wi"],
         seg["dt"], jnp.zeros((d, tail_pad + HYB_COLS - COL_TAIL - LANES), F32)], axis=1).astype(BF16)
    proj = fused_matmul([x], w_perm, g=norm_g, out_dtype=F32, name="hyb_in_proj")

    rot = ATTN_HEAD_DIM // ROPE_FRACTION
    rot_i = IDX_HEAD_DIM // ROPE_FRACTION
    inv16 = jnp.power(ROPE_THETA, -(jnp.arange(rot // 2, dtype=F32) * 2.0 / rot))
    inv8 = jnp.power(ROPE_THETA, -(jnp.arange(rot_i // 2, dtype=F32) * 2.0 / rot_i))
    inv_a = jnp.concatenate([inv16, inv16, jnp.zeros((LANES - rot,), F32)]).reshape(1, LANES)
    inv_h = jnp.concatenate([inv8, inv8, jnp.zeros((IDX_HEAD_DIM - rot_i,), F32)])
    inv_i = jnp.tile(inv_h, IDX_HEADS).reshape(1, IDX_INNER)
    inv_t = jnp.concatenate([inv_h, jnp.zeros((LANES - IDX_HEAD_DIM,), F32)]).reshape(1, LANES)
    q_r, k_r, v_r, qi_r, ki_r, wi = hyb_prep(
        proj, pos, inv_a, inv_i, inv_t, q_norm.reshape(1, LANES), k_norm.reshape(1, LANES),
        _pad_lanes(kidx_norm, TAIL_KI))
    attn = dsa_attention(q_r, qi_r, wi, k_r, v_r, ki_r, batch, seq, k_sel)

    ssm = ssd_mixer(proj, conv_w, conv_b.reshape(1, SSM_CONV_DIM), _pad_lanes(dt_bias, TAIL_DT),
                    _pad_lanes(a_log, TAIL_DT), jnp.repeat(d_skip, SSM_HEAD_DIM).reshape(1, SSM_INNER),
                    ssm_norm.reshape(1, SSM_INNER), batch, seq)
    return fused_matmul([attn, ssm], w_out.astype(BF16), res=x, out_dtype=F32, name="hyb_out_proj")


def retention_layer(x, pos, norm_g, w_in, ret_norm, w_out, batch, seq):
    proj = fused_matmul([x], w_in.astype(BF16), g=norm_g, out_dtype=BF16, name="ret_in_proj")
    o = retention_mixer(proj, pos, ret_norm, batch, seq)
    return fused_matmul([o], w_out.astype(BF16), res=x, out_dtype=F32, name="ret_out_proj")


def kernel(x, positions, mix_norm, ffn_norm, hyb_w_in, attn_q_norm, attn_k_norm, idx_k_norm, ssm_conv_w, ssm_conv_b, ssm_dt_bias, ssm_a_log, ssm_d, ssm_norm, hyb_w_out, ret_w_in, ret_norm, ret_w_out, moe_router_group, moe_router_expert, moe_w_gate, moe_w_up, moe_w_down):
    batch, seq, d = x.shape
    depth = mix_norm.shape[0]
    k_sel = min(TOPK_MAX, seq // 4)
    n = batch * seq
    xf = x.reshape(n, d)
    pos = positions.reshape(n, 1).astype(I32)
    for layer in range(depth):
        i = layer // 2
        if layer % 2 == 0:
            xf = hybrid_layer(xf, pos, mix_norm[layer], hyb_w_in[i], attn_q_norm[i], attn_k_norm[i],
                              idx_k_norm[i], ssm_conv_w[i], ssm_conv_b[i], ssm_dt_bias[i],
                              ssm_a_log[i], ssm_d[i], ssm_norm[i], hyb_w_out[i], batch, seq, k_sel)
        else:
            xf = retention_layer(xf, pos, mix_norm[layer], ret_w_in[i], ret_norm[i], ret_w_out[i],
                                 batch, seq)
        xf = hier_moe_layer(xf, ffn_norm[layer], moe_router_group[layer], moe_router_expert[layer],
                            moe_w_gate[layer], moe_w_up[layer], moe_w_down[layer])
    return xf.reshape(batch, seq, d)
```

```python
import functools

import jax
import jax.numpy as jnp
import numpy as np
from jax import lax
from jax.experimental import pallas as pl
from jax.experimental.pallas import tpu as pltpu

F32 = jnp.float32
BF16 = jnp.bfloat16
I32 = jnp.int32

ATTN_HEADS = 8
ATTN_KV_HEADS = 2
ATTN_GROUP = ATTN_HEADS // ATTN_KV_HEADS
ATTN_HEAD_DIM = 128
ATTN_INNER = ATTN_HEADS * ATTN_HEAD_DIM
ATTN_KV = ATTN_KV_HEADS * ATTN_HEAD_DIM
IDX_HEADS = 8
IDX_HEAD_DIM = 64
IDX_INNER = IDX_HEADS * IDX_HEAD_DIM
TOPK_MAX = 256
Q_BLOCK = 128
ROPE_THETA = 500000.0
ROPE_FRACTION = 4

SSM_HEADS = 16
SSM_HEAD_DIM = 64
SSM_INNER = SSM_HEADS * SSM_HEAD_DIM
SSM_GROUPS = 2
SSM_STATE = 128
SSM_CONV = 4
SSM_CONV_DIM = SSM_INNER + 2 * SSM_GROUPS * SSM_STATE
SSM_CHUNK = 128
SSM_GROUP_INNER = SSM_INNER // SSM_GROUPS

RET_HEADS = 8
RET_QK_DIM = 256
RET_V_DIM = 512
RET_QK_TOTAL = RET_HEADS * RET_QK_DIM
RET_V_TOTAL = RET_HEADS * RET_V_DIM
RET_CHUNK = 128
RET_THETA = 10000.0

MOE_GROUPS = 8
MOE_EXPERTS_PER_GROUP = 8
N_EXPERTS = MOE_GROUPS * MOE_EXPERTS_PER_GROUP
MOE_TOPK = 2
EXPERT_FF = 512
MOE_BLOCK = 128

EPS = 1e-6

LANES = 128
SUBLANES = 8
VMEM_LIMIT = 52 * 1024 * 1024

LOG2E = 1.4426950408889634
NEG_BIG = -1e30
INT_MIN = -2147483648

COL_XBC = 0
COL_K = SSM_CONV_DIM
COL_V = COL_K + ATTN_KV
COL_Z = COL_V + ATTN_KV
COL_Q = COL_Z + SSM_INNER
COL_QI = COL_Q + ATTN_INNER
COL_TAIL = COL_QI + IDX_INNER
HYB_COLS = 5120
TAIL_KI = 0
TAIL_WI = IDX_HEAD_DIM
TAIL_DT = TAIL_WI + IDX_HEADS


def _cparams(n_axes):
    return pltpu.CompilerParams(dimension_semantics=("arbitrary",) * n_axes,
                                vmem_limit_bytes=VMEM_LIMIT)


def _pick(n, pref):
    t = min(n, pref)
    while n % t:
        t //= 2
    return t


def _mm_kernel(*refs, n_a, has_norm, has_res, row_chunk):
    a_refs = refs[:n_a]
    pos = n_a
    g_ref = None
    if has_norm:
        g_ref = refs[pos]
        pos += 1
    w_refs = refs[pos:pos + n_a]
    pos += n_a
    res_ref = None
    if has_res:
        res_ref = refs[pos]
        pos += 1
    out_ref = refs[pos]
    pos += 1
    xn_ref = refs[pos] if has_norm else None
    tm = out_ref.shape[0]

    if has_norm:
        @pl.when(pl.program_id(1) == 0)
        def _():
            def body(r, carry):
                rows = pl.ds(pl.multiple_of(r * row_chunk, row_chunk), row_chunk)
                x = a_refs[0][rows, :]
                ms = jnp.mean(x * x, axis=-1, keepdims=True)
                xn_ref[rows, :] = (x * lax.rsqrt(ms + EPS) * g_ref[...]).astype(BF16)
                return carry
            lax.fori_loop(0, tm // row_chunk, body, 0)
        acc = jnp.dot(xn_ref[...], w_refs[0][...], preferred_element_type=F32)
    else:
        acc = jnp.dot(a_refs[0][...], w_refs[0][...], preferred_element_type=F32)
        for p in range(1, n_a):
            acc = acc + jnp.dot(a_refs[p][...], w_refs[p][...], preferred_element_type=F32)
    if has_res:
        acc = acc + res_ref[...]
    out_ref[...] = acc.astype(out_ref.dtype)


def fused_matmul(a_list, w, *, g=None, res=None, out_dtype=F32, tm=1024, tn=512, name="mm"):
    n = a_list[0].shape[0]
    kp = a_list[0].shape[1]
    m = w.shape[1]
    tm = _pick(n, tm)
    tn = _pick(m, tn)
    n_a = len(a_list)
    has_norm = g is not None
    has_res = res is not None
    in_specs = [pl.BlockSpec((tm, kp), lambda i, j: (i, 0)) for _ in a_list]
    args = list(a_list)
    if has_norm:
        in_specs.append(pl.BlockSpec((1, kp), lambda i, j: (0, 0)))
        args.append(g.reshape(1, kp).astype(F32))
    for p in range(n_a):
        in_specs.append(pl.BlockSpec((kp, tn), lambda i, j, p=p: (p, j)))
        args.append(w)
    if has_res:
        in_specs.append(pl.BlockSpec((tm, tn), lambda i, j: (i, j)))
        args.append(res)
    scratch = [pltpu.VMEM((tm, kp), BF16)] if has_norm else []
    kern = functools.partial(_mm_kernel, n_a=n_a, has_norm=has_norm, has_res=has_res,
                             row_chunk=_pick(tm, 128))
    return pl.pallas_call(
        kern,
        grid=(n // tm, m // tn),
        in_specs=in_specs,
        out_specs=pl.BlockSpec((tm, tn), lambda i, j: (i, j)),
        out_shape=jax.ShapeDtypeStruct((n, m), out_dtype),
        scratch_shapes=scratch,
        compiler_params=_cparams(2),
        name=name,
    )(*args)


def _rope_lanes(x, cosf, s_neg, s_pos, half):
    width = x.shape[-1]
    return (x * cosf + pltpu.roll(x, width - half, axis=1) * s_neg
            + pltpu.roll(x, half, axis=1) * s_pos)


def _prep_kernel(q_ref, k_ref, v_ref, qi_ref, tail_ref, pos_ref, inva_ref, invi_ref, invt_ref,
                 qn_ref, kn_ref, kin_ref,
                 qo_ref, ko_ref, vo_ref, qio_ref, kio_ref, wio_ref):
    tb = q_ref.shape[0]
    posf = pos_ref[...].astype(F32)
    lane = lax.broadcasted_iota(I32, (tb, LANES), 1)

    half = ATTN_HEAD_DIM // ROPE_FRACTION // 2
    ang = posf * inva_ref[...]
    cosf = jnp.cos(ang)
    sinf = jnp.sin(ang)
    s_neg = jnp.where(lane < half, -sinf, 0.0)
    s_pos = jnp.where((lane >= half) & (lane < 2 * half), sinf, 0.0)
    scale = ATTN_HEAD_DIM ** -0.5 * LOG2E
    for h in range(ATTN_HEADS):
        x = q_ref[:, h * LANES:(h + 1) * LANES]
        ms = jnp.mean(x * x, axis=-1, keepdims=True)
        x = x * lax.rsqrt(ms + EPS) * qn_ref[...]
        x = _rope_lanes(x, cosf, s_neg, s_pos, half)
        qo_ref[h] = (x * scale).astype(BF16)
    for h in range(ATTN_KV_HEADS):
        x = k_ref[:, h * LANES:(h + 1) * LANES]
        ms = jnp.mean(x * x, axis=-1, keepdims=True)
        x = x * lax.rsqrt(ms + EPS) * kn_ref[...]
        x = _rope_lanes(x, cosf, s_neg, s_pos, half)
        ko_ref[h] = x.astype(BF16)
        vo_ref[h] = v_ref[:, h * LANES:(h + 1) * LANES].astype(BF16)

    half_i = IDX_HEAD_DIM // ROPE_FRACTION // 2
    lane_i = lax.broadcasted_iota(I32, (tb, IDX_INNER), 1) & (IDX_HEAD_DIM - 1)
    ang_i = posf * invi_ref[...]
    cos_i = jnp.cos(ang_i)
    sin_i = jnp.sin(ang_i)
    sn_i = jnp.where(lane_i < half_i, -sin_i, 0.0)
    sp_i = jnp.where((lane_i >= half_i) & (lane_i < 2 * half_i), sin_i, 0.0)
    qi = _rope_lanes(qi_ref[...], cos_i, sn_i, sp_i, half_i)
    for h in range(IDX_HEADS):
        slab = qi[:, (h // 2) * LANES:(h // 2 + 1) * LANES]
        if h % 2:
            slab = pltpu.roll(slab, IDX_HEAD_DIM, axis=1)
        qio_ref[h] = jnp.where(lane < IDX_HEAD_DIM, slab, 0.0).astype(BF16)

    t = tail_ref[...]
    is_ki = lane < IDX_HEAD_DIM
    ms = jnp.sum(jnp.where(is_ki, t * t, 0.0), axis=-1, keepdims=True) * (1.0 / IDX_HEAD_DIM)
    kin = t * lax.rsqrt(ms + EPS) * kin_ref[...]
    ang_t = posf * invt_ref[...]
    cos_t = jnp.cos(ang_t)
    sin_t = jnp.sin(ang_t)
    sn_t = jnp.where(lane < half_i, -sin_t, 0.0)
    sp_t = jnp.where((lane >= half_i) & (lane < 2 * half_i), sin_t, 0.0)
    kin = _rope_lanes(kin, cos_t, sn_t, sp_t, half_i)
    kio_ref[...] = jnp.where(is_ki, kin, 0.0).astype(BF16)
    wio_ref[...] = t * (IDX_HEADS ** -0.5 * IDX_HEAD_DIM ** -0.5)


def hyb_prep(proj, pos, inv_a, inv_i, inv_t, q_norm, k_norm, kidx_norm_pad):
    n = proj.shape[0]
    tb = _pick(n, 256)

    def col(width, off):
        return pl.BlockSpec((tb, width), lambda i: (i, off // width))

    def full(shape):
        return pl.BlockSpec(shape, lambda i: (0,) * len(shape))

    in_specs = [col(ATTN_INNER, COL_Q), col(ATTN_KV, COL_K), col(ATTN_KV, COL_V),
                col(IDX_INNER, COL_QI), col(LANES, COL_TAIL),
                pl.BlockSpec((tb, 1), lambda i: (i, 0)),
                full((1, LANES)), full((1, IDX_INNER)), full((1, LANES)),
                full((1, LANES)), full((1, LANES)), full((1, LANES))]
    out_shape = [jax.ShapeDtypeStruct((ATTN_HEADS, n, LANES), BF16),
                 jax.ShapeDtypeStruct((ATTN_KV_HEADS, n, LANES), BF16),
                 jax.ShapeDtypeStruct((ATTN_KV_HEADS, n, LANES), BF16),
                 jax.ShapeDtypeStruct((IDX_HEADS, n, LANES), BF16),
                 jax.ShapeDtypeStruct((n, LANES), BF16),
                 jax.ShapeDtypeStruct((n, LANES), F32)]
    out_specs = [pl.BlockSpec((ATTN_HEADS, tb, LANES), lambda i: (0, i, 0)),
                 pl.BlockSpec((ATTN_KV_HEADS, tb, LANES), lambda i: (0, i, 0)),
                 pl.BlockSpec((ATTN_KV_HEADS, tb, LANES), lambda i: (0, i, 0)),
                 pl.BlockSpec((IDX_HEADS, tb, LANES), lambda i: (0, i, 0)),
                 pl.BlockSpec((tb, LANES), lambda i: (i, 0)),
                 pl.BlockSpec((tb, LANES), lambda i: (i, 0))]
    return pl.pallas_call(
        _prep_kernel, grid=(n // tb,), in_specs=in_specs, out_specs=out_specs,
        out_shape=out_shape, compiler_params=_cparams(1), name="hyb_prep",
    )(proj, proj, proj, proj, proj, pos, inv_a, inv_i, inv_t, q_norm, k_norm, kidx_norm_pad)


def _sortable(x):
    b = pltpu.bitcast(x, I32)
    return jnp.where(b < 0, b ^ jnp.int32(0x7FFFFFFF), b)


def _dsa_kernel(q_ref, qi_ref, wi_ref, k_ref, v_ref, ki_ref, triu_ref, eye_ref, onec_ref, out_ref,
                keys_ref, bt_ref, wb_ref, mx_ref, acc_ref, *, k_sel, kc):
    i = pl.program_id(1)
    tq = Q_BLOCK
    nkc = (i * tq + tq + kc - 1) // kc
    row = i * tq + lax.broadcasted_iota(I32, (tq, kc), 0)
    lane = lax.broadcasted_iota(I32, (tq, kc), 1)
    nslab = kc // LANES

    w = wi_ref[...]
    for h in range(IDX_HEADS):
        wb_ref[h] = jnp.broadcast_to(w[:, TAIL_WI + h:TAIL_WI + h + 1], (tq, kc))

    def idx_body(c, carry):
        cols = pl.ds(pl.multiple_of(c * kc, kc), kc)
        kic = ki_ref[cols, :]
        d = lax.dot_general(qi_ref[...].reshape(IDX_HEADS * tq, LANES), kic,
                            (((1,), (1,)), ((), ())), preferred_element_type=F32)
        sc = jnp.zeros((tq, kc), F32)
        for h in range(IDX_HEADS):
            sc = sc + jnp.maximum(d[h * tq:(h + 1) * tq, :], 0.0) * wb_ref[h]
        causal = (c * kc + lane) <= row
        keys_ref[:, cols] = jnp.where(causal, _sortable(sc), jnp.int32(INT_MIN))
        return carry
    lax.fori_loop(0, nkc, idx_body, 0)

    def fold(x, op):
        acc = x[:, 0:LANES]
        for j in range(1, nslab):
            acc = op(acc, x[:, j * LANES:(j + 1) * LANES])
        return acc

    def count(cand, strict):
        def body(c, acc):
            cols = pl.ds(pl.multiple_of(c * kc, kc), kc)
            kk = keys_ref[:, cols]
            hit = (kk > cand) if strict else (kk >= cand)
            return acc + fold(jnp.where(hit, 1.0, 0.0), jnp.add)
        acc = lax.fori_loop(0, nkc, body, jnp.zeros((tq, LANES), F32))
        return jnp.sum(acc, axis=1, keepdims=True)

    kf = jnp.float32(k_sel)
    zero = jnp.zeros((tq, 1), I32)
    lo = jnp.where(count(zero, False) >= kf, zero, jnp.full((tq, 1), INT_MIN, I32))

    def bis_body(it, lo):
        cand = lo + jnp.left_shift(jnp.int32(1), jnp.int32(30) - it)
        return jnp.where(count(cand, False) >= kf, cand, lo)
    thr = lax.fori_loop(0, 31, bis_body, lo)
    need = kf - count(thr, True)

    def fin_body(c, run):
        cols = pl.ds(pl.multiple_of(c * kc, kc), kc)
        kk = keys_ref[:, cols]
        eq = kk == thr
        eqf = jnp.where(eq, 1.0, 0.0)
        incl = jnp.dot(eqf.astype(BF16), triu_ref[...], preferred_element_type=F32)
        sel = (kk > thr) | (eq & ((run + incl) <= need))
        causal = (c * kc + lane) <= row
        bias = jnp.where(sel & causal, 0.0, NEG_BIG)
        bt_ref[cols, :] = bias.T.astype(BF16)
        return run + jnp.sum(eqf, axis=1, keepdims=True)
    lax.fori_loop(0, nkc, fin_body, jnp.zeros((tq, 1), F32))

    contract_last = (((1,), (1,)), ((), ()))

    gq = ATTN_GROUP * tq

    def logits(j, kx):
        qg = q_ref[j * ATTN_GROUP:(j + 1) * ATTN_GROUP].reshape(gq, LANES)
        qx = jnp.concatenate([qg, eye_ref[...]], axis=1)
        return lax.dot_general(qx, kx, contract_last, preferred_element_type=F32)

    mx_ref[...] = jnp.full(mx_ref.shape, NEG_BIG, F32)

    def max_body(c, carry):
        cols = pl.ds(pl.multiple_of(c * kc, kc), kc)
        bt = bt_ref[cols, :]
        for j in range(ATTN_KV_HEADS):
            kx = jnp.concatenate([k_ref[j, cols, :], bt], axis=1)
            mx_ref[j] = jnp.maximum(mx_ref[j], fold(logits(j, kx), jnp.maximum))
        return carry
    lax.fori_loop(0, nkc, max_body, 0)
    for j in range(ATTN_KV_HEADS):
        mx_ref[j] = jnp.broadcast_to(jnp.max(mx_ref[j], axis=1, keepdims=True), (gq, LANES))

    acc_ref[...] = jnp.zeros(acc_ref.shape, F32)

    def att_body(c, carry):
        cols = pl.ds(pl.multiple_of(c * kc, kc), kc)
        bt = bt_ref[cols, :]
        for j in range(ATTN_KV_HEADS):
            kx = jnp.concatenate([k_ref[j, cols, :], bt], axis=1)
            vx = jnp.concatenate([v_ref[j, cols, :], onec_ref[...]], axis=1)
            s = logits(j, kx)
            mb = mx_ref[j]
            p = jnp.concatenate(
                [jnp.exp2(s[:, a * LANES:(a + 1) * LANES] - mb) for a in range(nslab)], axis=1)
            acc_ref[j] = acc_ref[j] + jnp.dot(p.astype(BF16), vx, preferred_element_type=F32)
        return carry
    lax.fori_loop(0, nkc, att_body, 0)

    for h in range(ATTN_HEADS):
        a = acc_ref[h // ATTN_GROUP, (h % ATTN_GROUP) * tq:(h % ATTN_GROUP + 1) * tq, :]
        out_ref[:, h * LANES:(h + 1) * LANES] = (
            a[:, 0:LANES] / a[:, LANES:LANES + 1]).astype(out_ref.dtype)


def dsa_attention(q_r, qi_r, wi, k_r, v_r, ki_r, batch, seq, k_sel):
    n = batch * seq
    nq = seq // Q_BLOCK
    kc = _pick(seq, 512)
    triu = jnp.asarray((np.arange(kc)[:, None] <= np.arange(kc)[None, :]).astype(np.float32), BF16)
    gq = ATTN_GROUP * Q_BLOCK
    eye = jnp.asarray(np.tile(np.eye(Q_BLOCK, dtype=np.float32), (ATTN_GROUP, 1)), BF16)
    onec = np.zeros((kc, LANES), np.float32)
    onec[:, 0] = 1.0
    onec = jnp.asarray(onec, BF16)
    kern = functools.partial(_dsa_kernel, k_sel=k_sel, kc=kc)
    in_specs = [
        pl.BlockSpec((ATTN_HEADS, Q_BLOCK, LANES), lambda b, i: (0, b * nq + i, 0)),
        pl.BlockSpec((IDX_HEADS, Q_BLOCK, LANES), lambda b, i: (0, b * nq + i, 0)),
        pl.BlockSpec((Q_BLOCK, LANES), lambda b, i: (b * nq + i, 0)),
        pl.BlockSpec((ATTN_KV_HEADS, seq, LANES), lambda b, i: (0, b, 0)),
        pl.BlockSpec((ATTN_KV_HEADS, seq, LANES), lambda b, i: (0, b, 0)),
        pl.BlockSpec((seq, LANES), lambda b, i: (b, 0)),
        pl.BlockSpec((kc, kc), lambda b, i: (0, 0)),
        pl.BlockSpec((gq, Q_BLOCK), lambda b, i: (0, 0)),
        pl.BlockSpec((kc, LANES), lambda b, i: (0, 0)),
    ]
    scratch = [pltpu.VMEM((Q_BLOCK, seq), I32), pltpu.VMEM((seq, Q_BLOCK), BF16),
               pltpu.VMEM((IDX_HEADS, Q_BLOCK, kc), F32),
               pltpu.VMEM((ATTN_KV_HEADS, gq, LANES), F32),
               pltpu.VMEM((ATTN_KV_HEADS, gq, 2 * LANES), F32)]
    return pl.pallas_call(
        kern, grid=(batch, nq), in_specs=in_specs,
        out_specs=pl.BlockSpec((Q_BLOCK, ATTN_INNER), lambda b, i: (b * nq + i, 0)),
        out_shape=jax.ShapeDtypeStruct((n, ATTN_INNER), BF16),
        scratch_shapes=scratch, compiler_params=_cparams(2), name="dsa_attention",
    )(q_r, qi_r, wi, k_r, v_r, ki_r, triu, eye, onec)


def _silu(x):
    return x * (1.0 / (1.0 + jnp.exp(-x)))


def _ssd_kernel(xbc_ref, z_ref, tail_ref, cw_ref, cb_ref, dtb_ref, alog_ref, dexp_ref, ng_ref,
                tri_ref, e_ref, out_ref, state_ref, ext_ref, y_ref):
    q = SSM_CHUNK
    hi = lax.Precision.HIGHEST

    @pl.when(pl.program_id(1) == 0)
    def _():
        state_ref[...] = jnp.zeros(state_ref.shape, F32)
        ext_ref[0:SUBLANES, :] = jnp.zeros((SUBLANES, SSM_CONV_DIM), F32)

    x = xbc_ref[...]
    ext_ref[SUBLANES:SUBLANES + q, :] = x
    conv = x * cw_ref[SSM_CONV - 1:SSM_CONV, :] + cb_ref[...]
    for k in range(1, SSM_CONV):
        conv = conv + ext_ref[SUBLANES - k:SUBLANES - k + q, :] * cw_ref[SSM_CONV - 1 - k:SSM_CONV - k, :]
    ext_ref[0:SUBLANES, :] = x[q - SUBLANES:q, :]
    xbc = _silu(conv)
    xs = xbc[:, 0:SSM_INNER]
    bm = xbc[:, SSM_INNER:SSM_INNER + SSM_GROUPS * SSM_STATE]
    cm = xbc[:, SSM_INNER + SSM_GROUPS * SSM_STATE:SSM_CONV_DIM]

    lane = lax.broadcasted_iota(I32, (q, LANES), 1)
    is_dt = (lane >= TAIL_DT) & (lane < TAIL_DT + SSM_HEADS)
    raw = tail_ref[...] + dtb_ref[...]
    dt = jnp.where(is_dt, jnp.maximum(raw, 0.0) + jnp.log1p(jnp.exp(-jnp.abs(raw))), 0.0)
    a = -jnp.exp(alog_ref[...])
    da = jnp.where(is_dt, dt * a, 0.0)
    acum = jnp.dot(tri_ref[...], da, precision=hi, preferred_element_type=F32)
    acum_t = acum.T
    acum_x = jnp.dot(acum, e_ref[...], precision=hi, preferred_element_type=F32)
    dt_x = jnp.dot(dt, e_ref[...], precision=hi, preferred_element_type=F32)
    last_x = acum_x[q - 1:q, :]
    xdt = xs * dt_x
    xdec = xdt * jnp.exp(last_x - acum_x)
    ri = lax.broadcasted_iota(I32, (q, q), 0)
    ci = lax.broadcasted_iota(I32, (q, q), 1)
    tril = ri >= ci
    pairs_per_group = SSM_GROUP_INNER // LANES

    for g in range(SSM_GROUPS):
        bg = bm[:, g * SSM_STATE:(g + 1) * SSM_STATE]
        cg = cm[:, g * SSM_STATE:(g + 1) * SSM_STATE]
        cgb = cg.astype(BF16)
        gsl = slice(g * SSM_GROUP_INNER, (g + 1) * SSM_GROUP_INNER)
        cb = lax.dot_general(cgb, bg.astype(BF16), (((1,), (1,)), ((), ())),
                             preferred_element_type=F32)
        y_off = jnp.dot(cgb, state_ref[g].astype(BF16), preferred_element_type=F32)
        y_ref[:, gsl] = y_off * jnp.exp(acum_x[:, gsl])
        for mth in range(pairs_per_group):
            slab = slice((g * pairs_per_group + mth) * LANES, (g * pairs_per_group + mth + 1) * LANES)
            xp = xdt[:, slab]
            yd = None
            for side in range(2):
                hl = TAIL_DT + (g * pairs_per_group + mth) * 2 + side
                seg = acum[:, hl:hl + 1] - acum_t[hl:hl + 1, :]
                lmat = jnp.exp(jnp.where(tril, seg, NEG_BIG))
                if side == 0:
                    xh = jnp.where(lane < SSM_HEAD_DIM, xp, 0.0)
                else:
                    xh = jnp.where(lane >= SSM_HEAD_DIM, xp, 0.0)
                part = jnp.dot((cb * lmat).astype(BF16), xh.astype(BF16), preferred_element_type=F32)
                yd = part if yd is None else yd + part
            y_ref[:, slab] = y_ref[:, slab] + yd
        st = jnp.dot(bg.T.astype(BF16), xdec[:, gsl].astype(BF16), preferred_element_type=F32)
        state_ref[g] = state_ref[g] * jnp.exp(last_x[:, gsl]) + st

    y = (y_ref[...] + dexp_ref[...] * xs) * _silu(z_ref[...])
    for g in range(SSM_GROUPS):
        gsl = slice(g * SSM_GROUP_INNER, (g + 1) * SSM_GROUP_INNER)
        yg = y[:, gsl]
        ms = jnp.mean(yg * yg, axis=-1, keepdims=True)
        out_ref[:, gsl] = (yg * lax.rsqrt(ms + EPS) * ng_ref[:, gsl]).astype(out_ref.dtype)


def ssd_mixer(proj, conv_w, conv_b, dtb_pad, alog_pad, d_exp, norm_g, batch, seq):
    n = batch * seq
    q = SSM_CHUNK
    nc = seq // q
    tri = jnp.asarray((np.arange(q)[:, None] >= np.arange(q)[None, :]).astype(np.float32))
    e = np.zeros((LANES, SSM_INNER), np.float32)
    for h in range(SSM_HEADS):
        e[TAIL_DT + h, h * SSM_HEAD_DIM:(h + 1) * SSM_HEAD_DIM] = 1.0
    e = jnp.asarray(e)

    def col(width, off):
        return pl.BlockSpec((q, width), lambda b, c: (b * nc + c, off // width))

    def full(shape):
        return pl.BlockSpec(shape, lambda b, c: (0,) * len(shape))

    in_specs = [col(SSM_CONV_DIM, COL_XBC), col(SSM_INNER, COL_Z), col(LANES, COL_TAIL),
                full((SSM_CONV, SSM_CONV_DIM)), full((1, SSM_CONV_DIM)), full((1, LANES)),
                full((1, LANES)), full((1, SSM_INNER)), full((1, SSM_INNER)),
                full((q, q)), full((LANES, SSM_INNER))]
    scratch = [pltpu.VMEM((SSM_GROUPS, SSM_STATE, SSM_GROUP_INNER), F32),
               pltpu.VMEM((SUBLANES + q, SSM_CONV_DIM), F32),
               pltpu.VMEM((q, SSM_INNER), F32)]
    return pl.pallas_call(
        _ssd_kernel, grid=(batch, nc), in_specs=in_specs,
        out_specs=pl.BlockSpec((q, SSM_INNER), lambda b, c: (b * nc + c, 0)),
        out_shape=jax.ShapeDtypeStruct((n, SSM_INNER), BF16),
        scratch_shapes=scratch, compiler_params=_cparams(2), name="ssd_mixer",
    )(proj, proj, proj, conv_w, conv_b, dtb_pad, alog_pad, d_exp, norm_g, tri, e)


def _ret_kernel(q_ref, k_ref, v_ref, g_ref, pos_ref, inv_ref, dmat_ref, qdec_ref, kdec_ref,
                cdec_ref, ng_ref, out_ref, state_ref):
    half = RET_QK_DIM // 2

    @pl.when(pl.program_id(1) == 0)
    def _():
        state_ref[...] = jnp.zeros(state_ref.shape, F32)

    ang = pos_ref[...].astype(F32) * inv_ref[...]
    cosf = jnp.cos(ang)
    sinf = jnp.sin(ang)

    def rope(x):
        x1 = x[:, 0:half]
        x2 = x[:, half:2 * half]
        return jnp.concatenate([x1 * cosf - x2 * sinf, x2 * cosf + x1 * sinf], axis=-1)

    for h in range(RET_HEADS):
        qh = rope(q_ref[:, h * RET_QK_DIM:(h + 1) * RET_QK_DIM].astype(F32))
        kh = rope(k_ref[:, h * RET_QK_DIM:(h + 1) * RET_QK_DIM].astype(F32)) * (RET_QK_DIM ** -0.5)
        vh = v_ref[:, h * RET_V_DIM:(h + 1) * RET_V_DIM]
        qb = qh.astype(BF16)
        s = lax.dot_general(qb, kh.astype(BF16), (((1,), (1,)), ((), ())),
                            preferred_element_type=F32) * dmat_ref[h]
        inner = jnp.dot(s.astype(BF16), vh, preferred_element_type=F32)
        st = state_ref[h]
        cross = jnp.dot(qb, st.astype(BF16), preferred_element_type=F32) * qdec_ref[h]
        kd = (kh * kdec_ref[h]).T.astype(BF16)
        state_ref[h] = st * cdec_ref[h] + jnp.dot(kd, vh, preferred_element_type=F32)
        o = inner + cross
        ms = jnp.mean(o * o, axis=-1, keepdims=True)
        o = o * lax.rsqrt(ms + EPS) * ng_ref[:, h * RET_V_DIM:(h + 1) * RET_V_DIM]
        gate = _silu(g_ref[:, h * RET_V_DIM:(h + 1) * RET_V_DIM].astype(F32))
        out_ref[:, h * RET_V_DIM:(h + 1) * RET_V_DIM] = (gate * o).astype(out_ref.dtype)


def retention_mixer(proj, pos, norm_g, batch, seq):
    n = batch * seq
    q = RET_CHUNK
    nc = seq // q
    half = RET_QK_DIM // 2
    inv = jnp.power(RET_THETA, -(jnp.arange(half, dtype=F32) * 2.0 / RET_QK_DIM)).reshape(1, half)
    log_gamma = jnp.log(1.0 - jnp.power(2.0, -5.0 - jnp.arange(RET_HEADS, dtype=F32)))
    i = jnp.arange(q, dtype=F32)
    diff = i[:, None] - i[None, :]
    dmat = jnp.where(diff[None] >= 0,
                     jnp.exp(jnp.maximum(diff, 0.0)[None] * log_gamma[:, None, None]), 0.0)
    qdec = jnp.exp((i + 1.0)[None, :, None] * log_gamma[:, None, None])
    kdec = jnp.exp((q - 1.0 - i)[None, :, None] * log_gamma[:, None, None])
    cdec = jnp.broadcast_to(jnp.exp(q * log_gamma)[:, None, None], (RET_HEADS, 1, RET_V_DIM))

    def col(width, off):
        return pl.BlockSpec((q, width), lambda b, c: (b * nc + c, off // width))

    def full(shape):
        return pl.BlockSpec(shape, lambda b, c: (0,) * len(shape))

    in_specs = [col(RET_QK_TOTAL, 0), col(RET_QK_TOTAL, RET_QK_TOTAL),
                col(RET_V_TOTAL, 2 * RET_QK_TOTAL), col(RET_V_TOTAL, 2 * RET_QK_TOTAL + RET_V_TOTAL),
                pl.BlockSpec((q, 1), lambda b, c: (b * nc + c, 0)),
                full((1, half)), full((RET_HEADS, q, q)), full((RET_HEADS, q, 1)),
                full((RET_HEADS, q, 1)), full((RET_HEADS, 1, RET_V_DIM)), full((1, RET_V_TOTAL))]
    return pl.pallas_call(
        _ret_kernel, grid=(batch, nc), in_specs=in_specs,
        out_specs=pl.BlockSpec((q, RET_V_TOTAL), lambda b, c: (b * nc + c, 0)),
        out_shape=jax.ShapeDtypeStruct((n, RET_V_TOTAL), BF16),
        scratch_shapes=[pltpu.VMEM((RET_HEADS, RET_QK_DIM, RET_V_DIM), F32)],
        compiler_params=_cparams(2), name="retention",
    )(proj, proj, proj, proj, pos, inv, dmat, qdec, kdec, cdec, norm_g.reshape(1, RET_V_TOTAL))


ROUTE_E0 = MOE_GROUPS
INFO_EID, INFO_GATE, INFO_RANK = 0, 2, 4


def _router_kernel(x_ref, g_ref, wr_ref, lt_ref, h_ref, info_ref, cnt_ref, run_ref):
    tb = x_ref.shape[0]

    @pl.when(pl.program_id(0) == 0)
    def _():
        run_ref[...] = jnp.zeros(run_ref.shape, F32)

    x = x_ref[...]
    ms = jnp.mean(x * x, axis=-1, keepdims=True)
    hn = x * lax.rsqrt(ms + EPS) * g_ref[...]
    h_ref[...] = hn
    logits = jnp.dot(hn, wr_ref[...], precision=lax.Precision.HIGHEST, preferred_element_type=F32)

    lane = lax.broadcasted_iota(I32, (tb, LANES), 1).astype(F32)
    far = jnp.float32(4 * LANES)
    ninf = jnp.float32(-jnp.inf)
    gl = jnp.where(lane < MOE_GROUPS, logits, ninf)
    gmax = jnp.max(gl, axis=1, keepdims=True)
    grp = jnp.min(jnp.where(gl == gmax, lane, far), axis=1, keepdims=True)
    p_grp = 1.0 / jnp.sum(jnp.exp(gl - gmax), axis=1, keepdims=True)
    lo = ROUTE_E0 + grp * MOE_EXPERTS_PER_GROUP
    el = jnp.where((lane >= lo) & (lane < lo + MOE_EXPERTS_PER_GROUP), logits, ninf)
    m1 = jnp.max(el, axis=1, keepdims=True)
    i1 = jnp.min(jnp.where(el == m1, lane, far), axis=1, keepdims=True)
    el2 = jnp.where(lane == i1, ninf, el)
    m2 = jnp.max(el2, axis=1, keepdims=True)
    i2 = jnp.min(jnp.where(el2 == m2, lane, far), axis=1, keepdims=True)
    e2 = jnp.exp(m2 - m1)
    g1 = p_grp / (1.0 + e2)
    g2 = p_grp * e2 / (1.0 + e2)
    oh1 = jnp.where(lane == i1, 1.0, 0.0)
    oh2 = jnp.where(lane == i2, 1.0, 0.0)
    cnt = oh1 + oh2
    before = jnp.dot(lt_ref[...], cnt.astype(BF16), preferred_element_type=F32) + run_ref[...]
    r1 = jnp.sum(oh1 * before, axis=1, keepdims=True)
    r2 = jnp.sum(oh2 * before, axis=1, keepdims=True)
    run_ref[...] = run_ref[...] + jnp.sum(cnt, axis=0, keepdims=True)
    cnt_ref[...] = run_ref[...]

    info = jnp.where(lane == INFO_EID, i1 - ROUTE_E0, 0.0)
    info = jnp.where(lane == INFO_EID + 1, i2 - ROUTE_E0, info)
    info = jnp.where(lane == INFO_GATE, g1, info)
    info = jnp.where(lane == INFO_GATE + 1, g2, info)
    info = jnp.where(lane == INFO_RANK, r1, info)
    info = jnp.where(lane == INFO_RANK + 1, r2, info)
    info_ref[...] = info


def moe_router(x, g, w_router_pad):
    n, d = x.shape
    tb = _pick(n, 512)
    lt = jnp.asarray((np.arange(tb)[:, None] > np.arange(tb)[None, :]).astype(np.float32), BF16)
    return pl.pallas_call(
        _router_kernel, grid=(n // tb,),
        in_specs=[pl.BlockSpec((tb, d), lambda i: (i, 0)), pl.BlockSpec((1, d), lambda i: (0, 0)),
                  pl.BlockSpec((d, LANES), lambda i: (0, 0)), pl.BlockSpec((tb, tb), lambda i: (0, 0))],
        out_specs=[pl.BlockSpec((tb, d), lambda i: (i, 0)), pl.BlockSpec((tb, LANES), lambda i: (i, 0)),
                   pl.BlockSpec((1, LANES), lambda i: (0, 0))],
        out_shape=[jax.ShapeDtypeStruct((n, d), F32), jax.ShapeDtypeStruct((n, LANES), F32),
                   jax.ShapeDtypeStruct((1, LANES), F32)],
        scratch_shapes=[pltpu.VMEM((1, LANES), F32)],
        compiler_params=_cparams(1), name="moe_router",
    )(x, g.reshape(1, d), w_router_pad, lt)


def _expert_kernel(be_ref, nused_ref, x_ref, wg_ref, wu_ref, wd_ref, out_ref, wgu_s, wd_s):
    i = pl.program_id(0)
    d = x_ref.shape[1]
    ff = EXPERT_FF

    @pl.when(i < nused_ref[0])
    def _():
        changed = jnp.logical_or(i == 0, be_ref[i] != be_ref[jnp.maximum(i - 1, 0)])

        @pl.when(changed)
        def _():
            rc = _pick(d, 256)

            def body(r, carry):
                rows = pl.ds(pl.multiple_of(r * rc, rc), rc)
                wgu_s[rows, 0:ff] = wg_ref[0, 0, rows, :].astype(BF16)
                wgu_s[rows, ff:2 * ff] = wu_ref[0, 0, rows, :].astype(BF16)
                return carry
            lax.fori_loop(0, d // rc, body, 0)
            wd_s[...] = wd_ref[0, 0].astype(BF16)

        gu = jnp.dot(x_ref[...].astype(BF16), wgu_s[...], preferred_element_type=F32)
        hid = _silu(gu[:, 0:ff]) * gu[:, ff:2 * ff]
        out_ref[...] = jnp.dot(hid.astype(BF16), wd_s[...], preferred_element_type=F32)

    @pl.when(i >= nused_ref[0])
    def _():
        out_ref[...] = jnp.zeros(out_ref.shape, out_ref.dtype)


def moe_experts(xs, blk_expert, nused, w_gate, w_up, w_down, layer):
    p, d = xs.shape
    nblk = p // MOE_BLOCK
    ff = EXPERT_FF
    grid_spec = pltpu.PrefetchScalarGridSpec(
        num_scalar_prefetch=2, grid=(nblk,),
        in_specs=[pl.BlockSpec((MOE_BLOCK, d), lambda i, be, nu: (i, 0)),
                  pl.BlockSpec((1, 1, d, ff), lambda i, be, nu: (layer, be[i], 0, 0)),
                  pl.BlockSpec((1, 1, d, ff), lambda i, be, nu: (layer, be[i], 0, 0)),
                  pl.BlockSpec((1, 1, ff, d), lambda i, be, nu: (layer, be[i], 0, 0))],
        out_specs=pl.BlockSpec((MOE_BLOCK, d), lambda i, be, nu: (i, 0)),
        scratch_shapes=[pltpu.VMEM((d, 2 * ff), BF16), pltpu.VMEM((ff, d), BF16)])
    return pl.pallas_call(
        _expert_kernel, grid_spec=grid_spec,
        out_shape=jax.ShapeDtypeStruct((p, d), F32),
        compiler_params=_cparams(1), name="moe_experts",
    )(blk_expert, nused, xs, w_gate, w_up, w_down)


def _row_copy(src_ref, src_row, dst_ref, dst_row, sem):
    return pltpu.make_async_copy(src_ref.at[pl.ds(src_row, 1), :], dst_ref.at[pl.ds(dst_row, 1), :], sem)


def _dispatch_kernel(dest_ref, h_hbm, xs_init_hbm, xs_hbm, sem, *, tb, nsteps):
    del xs_init_hbm
    i = pl.program_id(0)
    slot = i % 2

    def issue(r, carry):
        for k in range(MOE_TOPK):
            _row_copy(h_hbm, i * tb + r, xs_hbm, dest_ref[0, 0, MOE_TOPK * r + k], sem.at[slot]).start()
        return carry
    lax.fori_loop(0, tb, issue, 0, unroll=8)

    def drain(s):
        def body(r, carry):
            for _ in range(MOE_TOPK):
                _row_copy(h_hbm, 0, xs_hbm, 0, sem.at[s]).wait()
            return carry
        lax.fori_loop(0, tb, body, 0, unroll=8)

    @pl.when(i > 0)
    def _():
        drain(1 - slot)

    @pl.when(i == nsteps - 1)
    def _():
        drain(slot)


def moe_dispatch(h, dest, p):
    n, d = h.shape
    tb = _pick(n, 256)
    nsteps = n // tb
    kern = functools.partial(_dispatch_kernel, tb=tb, nsteps=nsteps)
    return pl.pallas_call(
        kern, grid=(nsteps,),
        in_specs=[pl.BlockSpec((1, 1, MOE_TOPK * tb), lambda i: (i, 0, 0), memory_space=pltpu.SMEM),
                  pl.BlockSpec(memory_space=pl.ANY), pl.BlockSpec(memory_space=pl.ANY)],
        out_specs=pl.BlockSpec(memory_space=pl.ANY),
        out_shape=jax.ShapeDtypeStruct((p, d), h.dtype),
        scratch_shapes=[pltpu.SemaphoreType.DMA((2,))],
        input_output_aliases={2: 0},
        compiler_params=_cparams(1), name="moe_dispatch",
    )(dest.reshape(nsteps, 1, MOE_TOPK * tb), h, jnp.zeros((p, d), h.dtype))


def _combine_kernel(dcur_ref, dnxt_ref, info_ref, x_ref, y_hbm, out_ref, ybuf, sem, *, tb, nsteps):
    i = pl.program_id(0)
    slot = i % 2

    def issue(dref, s):
        def body(r, carry):
            for k in range(MOE_TOPK):
                _row_copy(y_hbm, dref[0, 0, MOE_TOPK * r + k], ybuf.at[s, k], r, sem.at[s]).start()
            return carry
        lax.fori_loop(0, tb, body, 0, unroll=8)

    @pl.when(i == 0)
    def _():
        issue(dcur_ref, 0)

    @pl.when(i + 1 < nsteps)
    def _():
        issue(dnxt_ref, 1 - slot)

    def drain(r, carry):
        for k in range(MOE_TOPK):
            _row_copy(y_hbm, 0, ybuf.at[slot, k], 0, sem.at[slot]).wait()
        return carry
    lax.fori_loop(0, tb, drain, 0, unroll=8)

    info = info_ref[...]
    out = x_ref[...]
    for k in range(MOE_TOPK):
        out = out + ybuf[slot, k] * info[:, INFO_GATE + k:INFO_GATE + k + 1]
    out_ref[...] = out


def moe_combine(x, y, dest, info):
    n, d = x.shape
    tb = _pick(n, MOE_BLOCK)
    nsteps = n // tb
    kern = functools.partial(_combine_kernel, tb=tb, nsteps=nsteps)
    dest3 = dest.reshape(nsteps, 1, MOE_TOPK * tb)
    return pl.pallas_call(
        kern, grid=(nsteps,),
        in_specs=[pl.BlockSpec((1, 1, MOE_TOPK * tb), lambda i: (i, 0, 0), memory_space=pltpu.SMEM),
                  pl.BlockSpec((1, 1, MOE_TOPK * tb), lambda i: (jnp.minimum(i + 1, nsteps - 1), 0, 0),
                               memory_space=pltpu.SMEM),
                  pl.BlockSpec((tb, LANES), lambda i: (i, 0)),
                  pl.BlockSpec((tb, d), lambda i: (i, 0)),
                  pl.BlockSpec(memory_space=pl.ANY)],
        out_specs=pl.BlockSpec((tb, d), lambda i: (i, 0)),
        out_shape=jax.ShapeDtypeStruct((n, d), x.dtype),
        scratch_shapes=[pltpu.VMEM((2, MOE_TOPK, tb, d), y.dtype), pltpu.SemaphoreType.DMA((2,))],
        compiler_params=_cparams(1), name="moe_combine",
    )(dest3, dest3, info, x, y)


def hier_moe_layer(x, norm_g, w_rg, w_re, w_gate, w_up, w_down, layer):
    n, d = x.shape
    w_router = jnp.concatenate(
        [w_rg, w_re, jnp.zeros((d, LANES - MOE_GROUPS - N_EXPERTS), F32)], axis=1)
    h, info, cnt = moe_router(x, norm_g, w_router)
    counts = cnt[0, ROUTE_E0:ROUTE_E0 + N_EXPERTS].astype(I32)
    eid = info[:, INFO_EID:INFO_EID + MOE_TOPK].astype(I32)
    rank = info[:, INFO_RANK:INFO_RANK + MOE_TOPK].astype(I32)
    padded = (counts + MOE_BLOCK - 1) // MOE_BLOCK * MOE_BLOCK
    pad_end = jnp.cumsum(padded)
    pad_start = pad_end - padded
    dest = pad_start[eid] + rank
    a = n * MOE_TOPK
    p = (-(-a // MOE_BLOCK) + N_EXPERTS) * MOE_BLOCK
    nblk = p // MOE_BLOCK
    blk_start = jnp.arange(nblk, dtype=I32) * MOE_BLOCK
    blk_expert = jnp.minimum(jnp.sum((pad_end[None, :] <= blk_start[:, None]).astype(I32), axis=1),
                             N_EXPERTS - 1).astype(I32)
    nused = (pad_end[-1:] // MOE_BLOCK).astype(I32)
    xs = moe_dispatch(h, dest, p)
    y = moe_experts(xs, blk_expert, nused, w_gate, w_up, w_down, layer)
    return moe_combine(x, y, dest, info)


def _pad_lanes(v, offset, width=LANES):
    out = jnp.zeros((1, width), F32)
    return out.at[0, offset:offset + v.shape[0]].set(v.astype(F32))


def hybrid_layer(x, pos, norm_g, w_in, q_norm, k_norm, kidx_norm, conv_w, conv_b, dt_bias, a_log,
                 d_skip, ssm_norm, w_out, batch, seq, k_sel):
    n, d = x.shape
    offs = np.cumsum([0, ATTN_INNER, ATTN_KV, ATTN_KV, IDX_INNER, IDX_HEAD_DIM, IDX_HEADS,
                      SSM_INNER, SSM_CONV_DIM, SSM_HEADS])
    seg = {name: w_in[:, offs[j]:offs[j + 1]] for j, name in
           enumerate(["q", "k", "v", "qi", "ki", "wi", "z", "xbc", "dt"])}
    tail_pad = LANES - IDX_HEAD_DIM - IDX_HEADS - SSM_HEADS
    w_perm = jnp.concatenate(
        [seg["xbc"], seg["k"], seg["v"], seg["z"], seg["q"], seg["qi"], seg["ki"], seg["wi"],
         seg["dt"], jnp.zeros((d, tail_pad + HYB_COLS - COL_TAIL - LANES), F32)], axis=1).astype(BF16)
    proj = fused_matmul([x], w_perm, g=norm_g, out_dtype=F32, name="hyb_in_proj")

    rot = ATTN_HEAD_DIM // ROPE_FRACTION
    rot_i = IDX_HEAD_DIM // ROPE_FRACTION
    inv16 = jnp.power(ROPE_THETA, -(jnp.arange(rot // 2, dtype=F32) * 2.0 / rot))
    inv8 = jnp.power(ROPE_THETA, -(jnp.arange(rot_i // 2, dtype=F32) * 2.0 / rot_i))
    inv_a = jnp.concatenate([inv16, inv16, jnp.zeros((LANES - rot,), F32)]).reshape(1, LANES)
    inv_h = jnp.concatenate([inv8, inv8, jnp.zeros((IDX_HEAD_DIM - rot_i,), F32)])
    inv_i = jnp.tile(inv_h, IDX_HEADS).reshape(1, IDX_INNER)
    inv_t = jnp.concatenate([inv_h, jnp.zeros((LANES - IDX_HEAD_DIM,), F32)]).reshape(1, LANES)
    q_r, k_r, v_r, qi_r, ki_r, wi = hyb_prep(
        proj, pos, inv_a, inv_i, inv_t, q_norm.reshape(1, LANES), k_norm.reshape(1, LANES),
        _pad_lanes(kidx_norm, TAIL_KI))
    attn = dsa_attention(q_r, qi_r, wi, k_r, v_r, ki_r, batch, seq, k_sel)

    ssm = ssd_mixer(proj, conv_w, conv_b.reshape(1, SSM_CONV_DIM), _pad_lanes(dt_bias, TAIL_DT),
                    _pad_lanes(a_log, TAIL_DT), jnp.repeat(d_skip, SSM_HEAD_DIM).reshape(1, SSM_INNER),
                    ssm_norm.reshape(1, SSM_INNER), batch, seq)
    return fused_matmul([attn, ssm], w_out.astype(BF16), res=x, out_dtype=F32, name="hyb_out_proj")


def retention_layer(x, pos, norm_g, w_in, ret_norm, w_out, batch, seq):
    proj = fused_matmul([x], w_in.astype(BF16), g=norm_g, out_dtype=BF16, name="ret_in_proj")
    o = retention_mixer(proj, pos, ret_norm, batch, seq)
    return fused_matmul([o], w_out.astype(BF16), res=x, out_dtype=F32, name="ret_out_proj")


def kernel(x, positions, mix_norm, ffn_norm, hyb_w_in, attn_q_norm, attn_k_norm, idx_k_norm, ssm_conv_w, ssm_conv_b, ssm_dt_bias, ssm_a_log, ssm_d, ssm_norm, hyb_w_out, ret_w_in, ret_norm, ret_w_out, moe_router_group, moe_router_expert, moe_w_gate, moe_w_up, moe_w_down):
    batch, seq, d = x.shape
    depth = mix_norm.shape[0]
    k_sel = min(TOPK_MAX, seq // 4)
    n = batch * seq
    xf = x.reshape(n, d)
    pos = positions.reshape(n, 1).astype(I32)
    for layer in range(depth):
        i = layer // 2
        if layer % 2 == 0:
            xf = hybrid_layer(xf, pos, mix_norm[layer], hyb_w_in[i], attn_q_norm[i], attn_k_norm[i],
                              idx_k_norm[i], ssm_conv_w[i], ssm_conv_b[i], ssm_dt_bias[i],
                              ssm_a_log[i], ssm_d[i], ssm_norm[i], hyb_w_out[i], batch, seq, k_sel)
        else:
            xf = retention_layer(xf, pos, mix_norm[layer], ret_w_in[i], ret_norm[i], ret_w_out[i],
                                 batch, seq)
        xf = hier_moe_layer(xf, ffn_norm[layer], moe_router_group[layer], moe_router_expert[layer],
                            moe_w_gate, moe_w_up, moe_w_down, layer)
    return xf.reshape(batch, seq, d)
```

```python
import functools

import jax
import jax.numpy as jnp
import numpy as np
from jax import lax
from jax.experimental import pallas as pl
from jax.experimental.pallas import tpu as pltpu

F32 = jnp.float32
BF16 = jnp.bfloat16
I32 = jnp.int32
I16 = jnp.int16
HALF_BITS = 16

ATTN_HEADS = 8
ATTN_KV_HEADS = 2
ATTN_GROUP = ATTN_HEADS // ATTN_KV_HEADS
ATTN_HEAD_DIM = 128
ATTN_INNER = ATTN_HEADS * ATTN_HEAD_DIM
ATTN_KV = ATTN_KV_HEADS * ATTN_HEAD_DIM
IDX_HEADS = 8
IDX_HEAD_DIM = 64
IDX_INNER = IDX_HEADS * IDX_HEAD_DIM
TOPK_MAX = 256
Q_BLOCK = 128
ROPE_THETA = 500000.0
ROPE_FRACTION = 4

SSM_HEADS = 16
SSM_HEAD_DIM = 64
SSM_INNER = SSM_HEADS * SSM_HEAD_DIM
SSM_GROUPS = 2
SSM_STATE = 128
SSM_CONV = 4
SSM_CONV_DIM = SSM_INNER + 2 * SSM_GROUPS * SSM_STATE
SSM_CHUNK = 128
SSM_GROUP_INNER = SSM_INNER // SSM_GROUPS

RET_HEADS = 8
RET_QK_DIM = 256
RET_V_DIM = 512
RET_QK_TOTAL = RET_HEADS * RET_QK_DIM
RET_V_TOTAL = RET_HEADS * RET_V_DIM
RET_CHUNK = 128
RET_THETA = 10000.0

MOE_GROUPS = 8
MOE_EXPERTS_PER_GROUP = 8
N_EXPERTS = MOE_GROUPS * MOE_EXPERTS_PER_GROUP
MOE_TOPK = 2
EXPERT_FF = 512
MOE_BLOCK = 128

EPS = 1e-6

LANES = 128
SUBLANES = 8
VMEM_LIMIT = 52 * 1024 * 1024

LOG2E = 1.4426950408889634
NEG_BIG = -1e30
INT_MIN = -2147483648

COL_XBC = 0
COL_K = SSM_CONV_DIM
COL_V = COL_K + ATTN_KV
COL_Z = COL_V + ATTN_KV
COL_Q = COL_Z + SSM_INNER
COL_QI = COL_Q + ATTN_INNER
COL_TAIL = COL_QI + IDX_INNER
HYB_COLS = 5120
TAIL_KI = 0
TAIL_WI = IDX_HEAD_DIM
TAIL_DT = TAIL_WI + IDX_HEADS


def _cparams(n_axes):
    return pltpu.CompilerParams(dimension_semantics=("arbitrary",) * n_axes,
                                vmem_limit_bytes=VMEM_LIMIT)


def _pick(n, pref):
    t = min(n, pref)
    while n % t:
        t //= 2
    return t


def _mm_kernel(*refs, n_a, has_norm, has_res, row_chunk):
    a_refs = refs[:n_a]
    pos = n_a
    g_ref = None
    if has_norm:
        g_ref = refs[pos]
        pos += 1
    w_refs = refs[pos:pos + n_a]
    pos += n_a
    res_ref = None
    if has_res:
        res_ref = refs[pos]
        pos += 1
    out_ref = refs[pos]
    pos += 1
    xn_ref = refs[pos] if has_norm else None
    tm = out_ref.shape[0]

    if has_norm:
        @pl.when(pl.program_id(1) == 0)
        def _():
            def body(r, carry):
                rows = pl.ds(pl.multiple_of(r * row_chunk, row_chunk), row_chunk)
                x = a_refs[0][rows, :]
                ms = jnp.mean(x * x, axis=-1, keepdims=True)
                xn_ref[rows, :] = (x * lax.rsqrt(ms + EPS) * g_ref[...]).astype(BF16)
                return carry
            lax.fori_loop(0, tm // row_chunk, body, 0)
        acc = jnp.dot(xn_ref[...], w_refs[0][...], preferred_element_type=F32)
    else:
        acc = jnp.dot(a_refs[0][...], w_refs[0][...], preferred_element_type=F32)
        for p in range(1, n_a):
            acc = acc + jnp.dot(a_refs[p][...], w_refs[p][...], preferred_element_type=F32)
    if has_res:
        acc = acc + res_ref[...]
    out_ref[...] = acc.astype(out_ref.dtype)


def fused_matmul(a_list, w, *, g=None, res=None, out_dtype=F32, tm=1024, tn=512, name="mm"):
    n = a_list[0].shape[0]
    kp = a_list[0].shape[1]
    m = w.shape[1]
    tm = _pick(n, tm)
    tn = _pick(m, tn)
    n_a = len(a_list)
    has_norm = g is not None
    has_res = res is not None
    in_specs = [pl.BlockSpec((tm, kp), lambda i, j: (i, 0)) for _ in a_list]
    args = list(a_list)
    if has_norm:
        in_specs.append(pl.BlockSpec((1, kp), lambda i, j: (0, 0)))
        args.append(g.reshape(1, kp).astype(F32))
    for p in range(n_a):
        in_specs.append(pl.BlockSpec((kp, tn), lambda i, j, p=p: (p, j)))
        args.append(w)
    if has_res:
        in_specs.append(pl.BlockSpec((tm, tn), lambda i, j: (i, j)))
        args.append(res)
    scratch = [pltpu.VMEM((tm, kp), BF16)] if has_norm else []
    kern = functools.partial(_mm_kernel, n_a=n_a, has_norm=has_norm, has_res=has_res,
                             row_chunk=_pick(tm, 128))
    return pl.pallas_call(
        kern,
        grid=(n // tm, m // tn),
        in_specs=in_specs,
        out_specs=pl.BlockSpec((tm, tn), lambda i, j: (i, j)),
        out_shape=jax.ShapeDtypeStruct((n, m), out_dtype),
        scratch_shapes=scratch,
        compiler_params=_cparams(2),
        name=name,
    )(*args)


def _rope_lanes(x, cosf, s_neg, s_pos, half):
    width = x.shape[-1]
    return (x * cosf + pltpu.roll(x, width - half, axis=1) * s_neg
            + pltpu.roll(x, half, axis=1) * s_pos)


def _prep_kernel(q_ref, k_ref, v_ref, qi_ref, tail_ref, pos_ref, inva_ref, invi_ref, invt_ref,
                 qn_ref, kn_ref, kin_ref,
                 qo_ref, ko_ref, vo_ref, qio_ref, kio_ref, wio_ref):
    tb = q_ref.shape[0]
    posf = pos_ref[...].astype(F32)
    lane = lax.broadcasted_iota(I32, (tb, LANES), 1)

    half = ATTN_HEAD_DIM // ROPE_FRACTION // 2
    ang = posf * inva_ref[...]
    cosf = jnp.cos(ang)
    sinf = jnp.sin(ang)
    s_neg = jnp.where(lane < half, -sinf, 0.0)
    s_pos = jnp.where((lane >= half) & (lane < 2 * half), sinf, 0.0)
    scale = ATTN_HEAD_DIM ** -0.5 * LOG2E
    for h in range(ATTN_HEADS):
        x = q_ref[:, h * LANES:(h + 1) * LANES]
        ms = jnp.mean(x * x, axis=-1, keepdims=True)
        x = x * lax.rsqrt(ms + EPS) * qn_ref[...]
        x = _rope_lanes(x, cosf, s_neg, s_pos, half)
        qo_ref[h] = (x * scale).astype(BF16)
    for h in range(ATTN_KV_HEADS):
        x = k_ref[:, h * LANES:(h + 1) * LANES]
        ms = jnp.mean(x * x, axis=-1, keepdims=True)
        x = x * lax.rsqrt(ms + EPS) * kn_ref[...]
        x = _rope_lanes(x, cosf, s_neg, s_pos, half)
        ko_ref[h] = x.astype(BF16)
        vo_ref[h] = v_ref[:, h * LANES:(h + 1) * LANES].astype(BF16)

    half_i = IDX_HEAD_DIM // ROPE_FRACTION // 2
    lane_i = lax.broadcasted_iota(I32, (tb, IDX_INNER), 1) & (IDX_HEAD_DIM - 1)
    ang_i = posf * invi_ref[...]
    cos_i = jnp.cos(ang_i)
    sin_i = jnp.sin(ang_i)
    sn_i = jnp.where(lane_i < half_i, -sin_i, 0.0)
    sp_i = jnp.where((lane_i >= half_i) & (lane_i < 2 * half_i), sin_i, 0.0)
    qi = _rope_lanes(qi_ref[...], cos_i, sn_i, sp_i, half_i)
    for h in range(IDX_HEADS):
        slab = qi[:, (h // 2) * LANES:(h // 2 + 1) * LANES]
        if h % 2:
            slab = pltpu.roll(slab, IDX_HEAD_DIM, axis=1)
        qio_ref[h] = jnp.where(lane < IDX_HEAD_DIM, slab, 0.0).astype(BF16)

    t = tail_ref[...]
    is_ki = lane < IDX_HEAD_DIM
    ms = jnp.sum(jnp.where(is_ki, t * t, 0.0), axis=-1, keepdims=True) * (1.0 / IDX_HEAD_DIM)
    kin = t * lax.rsqrt(ms + EPS) * kin_ref[...]
    ang_t = posf * invt_ref[...]
    cos_t = jnp.cos(ang_t)
    sin_t = jnp.sin(ang_t)
    sn_t = jnp.where(lane < half_i, -sin_t, 0.0)
    sp_t = jnp.where((lane >= half_i) & (lane < 2 * half_i), sin_t, 0.0)
    kin = _rope_lanes(kin, cos_t, sn_t, sp_t, half_i)
    kio_ref[...] = jnp.where(is_ki, kin, 0.0).astype(BF16)
    wio_ref[...] = t * (IDX_HEADS ** -0.5 * IDX_HEAD_DIM ** -0.5)


def hyb_prep(proj, pos, inv_a, inv_i, inv_t, q_norm, k_norm, kidx_norm_pad):
    n = proj.shape[0]
    tb = _pick(n, 256)

    def col(width, off):
        return pl.BlockSpec((tb, width), lambda i: (i, off // width))

    def full(shape):
        return pl.BlockSpec(shape, lambda i: (0,) * len(shape))

    in_specs = [col(ATTN_INNER, COL_Q), col(ATTN_KV, COL_K), col(ATTN_KV, COL_V),
                col(IDX_INNER, COL_QI), col(LANES, COL_TAIL),
                pl.BlockSpec((tb, 1), lambda i: (i, 0)),
                full((1, LANES)), full((1, IDX_INNER)), full((1, LANES)),
                full((1, LANES)), full((1, LANES)), full((1, LANES))]
    out_shape = [jax.ShapeDtypeStruct((ATTN_HEADS, n, LANES), BF16),
                 jax.ShapeDtypeStruct((ATTN_KV_HEADS, n, LANES), BF16),
                 jax.ShapeDtypeStruct((ATTN_KV_HEADS, n, LANES), BF16),
                 jax.ShapeDtypeStruct((IDX_HEADS, n, LANES), BF16),
                 jax.ShapeDtypeStruct((n, LANES), BF16),
                 jax.ShapeDtypeStruct((n, LANES), F32)]
    out_specs = [pl.BlockSpec((ATTN_HEADS, tb, LANES), lambda i: (0, i, 0)),
                 pl.BlockSpec((ATTN_KV_HEADS, tb, LANES), lambda i: (0, i, 0)),
                 pl.BlockSpec((ATTN_KV_HEADS, tb, LANES), lambda i: (0, i, 0)),
                 pl.BlockSpec((IDX_HEADS, tb, LANES), lambda i: (0, i, 0)),
                 pl.BlockSpec((tb, LANES), lambda i: (i, 0)),
                 pl.BlockSpec((tb, LANES), lambda i: (i, 0))]
    return pl.pallas_call(
        _prep_kernel, grid=(n // tb,), in_specs=in_specs, out_specs=out_specs,
        out_shape=out_shape, compiler_params=_cparams(1), name="hyb_prep",
    )(proj, proj, proj, proj, proj, pos, inv_a, inv_i, inv_t, q_norm, k_norm, kidx_norm_pad)


def _sortable(x):
    b = pltpu.bitcast(x, I32)
    return jnp.where(b < 0, b ^ jnp.int32(0x7FFFFFFF), b)


def _dsa_kernel(q_ref, qi_ref, wi_ref, k_ref, v_ref, ki_ref, triu_ref, eye_ref, onec_ref, out_ref,
                keys_ref, half_ref, bt_ref, wb_ref, mx_ref, acc_ref, *, k_sel, kc):
    i = pl.program_id(1)
    tq = Q_BLOCK
    nkc = (i * tq + tq + kc - 1) // kc
    row = i * tq + lax.broadcasted_iota(I32, (tq, kc), 0)
    lane = lax.broadcasted_iota(I32, (tq, kc), 1)
    nslab = kc // LANES

    w = wi_ref[...]
    for h in range(IDX_HEADS):
        wb_ref[h] = jnp.broadcast_to(w[:, TAIL_WI + h:TAIL_WI + h + 1], (tq, kc))

    def idx_body(c, carry):
        cols = pl.ds(pl.multiple_of(c * kc, kc), kc)
        kic = ki_ref[cols, :]
        d = lax.dot_general(qi_ref[...].reshape(IDX_HEADS * tq, LANES), kic,
                            (((1,), (1,)), ((), ())), preferred_element_type=F32)
        sc = jnp.zeros((tq, kc), F32)
        for h in range(IDX_HEADS):
            sc = sc + jnp.maximum(d[h * tq:(h + 1) * tq, :], 0.0) * wb_ref[h]
        causal = (c * kc + lane) <= row
        keys_ref[:, cols] = jnp.where(causal, _sortable(sc), jnp.int32(INT_MIN))
        return carry
    lax.fori_loop(0, nkc, idx_body, 0)

    def fold(x, op):
        acc = x[:, 0:LANES]
        for j in range(1, nslab):
            acc = op(acc, x[:, j * LANES:(j + 1) * LANES])
        return acc

    i16_min = -(1 << (HALF_BITS - 1))

    def count16(cand, strict):
        c16 = cand.astype(I16)

        def body(c, acc):
            cols = pl.ds(pl.multiple_of(c * kc, kc), kc)
            kk = half_ref[:, cols]
            hit = (kk > c16) if strict else (kk >= c16)
            return acc + fold(jnp.where(hit, jnp.int16(1), jnp.int16(0)), jnp.add)
        acc = lax.fori_loop(0, nkc, body, jnp.zeros((tq, LANES), I16))
        return jnp.sum(acc.astype(F32), axis=1, keepdims=True)

    def kth_largest16(kf):
        zero = jnp.zeros((tq, 1), I32)
        lo = jnp.where(count16(zero, False) >= kf, zero, jnp.full((tq, 1), i16_min, I32))

        def bis_body(it, lo):
            cand = lo + jnp.left_shift(jnp.int32(1), jnp.int32(HALF_BITS - 2) - it)
            return jnp.where(count16(cand, False) >= kf, cand, lo)
        return lax.fori_loop(0, HALF_BITS - 1, bis_body, lo)

    def fill_half(fn):
        def body(c, carry):
            cols = pl.ds(pl.multiple_of(c * kc, kc), kc)
            half_ref[:, cols] = fn(keys_ref[:, cols]).astype(I16)
            return carry
        lax.fori_loop(0, nkc, body, 0)

    kf = jnp.float32(k_sel)
    fill_half(lambda kk: jnp.right_shift(kk, HALF_BITS))
    thr_hi = kth_largest16(kf)
    above = count16(thr_hi, True)
    low_mask = (1 << HALF_BITS) - 1
    fill_half(lambda kk: jnp.where(jnp.right_shift(kk, HALF_BITS) == thr_hi,
                                   (kk & low_mask) + i16_min, i16_min))
    thr_lo = kth_largest16(kf - above)
    thr = jnp.left_shift(thr_hi, HALF_BITS) + (thr_lo - i16_min)
    need = kf - above - count16(thr_lo, True)

    def fin_body(c, run):
        cols = pl.ds(pl.multiple_of(c * kc, kc), kc)
        kk = keys_ref[:, cols]
        eq = kk == thr
        eqf = jnp.where(eq, 1.0, 0.0)
        incl = jnp.dot(eqf.astype(BF16), triu_ref[...], preferred_element_type=F32)
        sel = (kk > thr) | (eq & ((run + incl) <= need))
        causal = (c * kc + lane) <= row
        bias = jnp.where(sel & causal, 0.0, NEG_BIG)
        bt_ref[cols, :] = bias.T.astype(BF16)
        return run + jnp.sum(eqf, axis=1, keepdims=True)
    lax.fori_loop(0, nkc, fin_body, jnp.zeros((tq, 1), F32))

    contract_last = (((1,), (1,)), ((), ()))

    gq = ATTN_GROUP * tq

    def logits(j, kx):
        qg = q_ref[j * ATTN_GROUP:(j + 1) * ATTN_GROUP].reshape(gq, LANES)
        qx = jnp.concatenate([qg, eye_ref[...]], axis=1)
        return lax.dot_general(qx, kx, contract_last, preferred_element_type=F32)

    mx_ref[...] = jnp.full(mx_ref.shape, NEG_BIG, F32)

    def max_body(c, carry):
        cols = pl.ds(pl.multiple_of(c * kc, kc), kc)
        bt = bt_ref[cols, :]
        for j in range(ATTN_KV_HEADS):
            kx = jnp.concatenate([k_ref[j, cols, :], bt], axis=1)
            mx_ref[j] = jnp.maximum(mx_ref[j], fold(logits(j, kx), jnp.maximum))
        return carry
    lax.fori_loop(0, nkc, max_body, 0)
    for j in range(ATTN_KV_HEADS):
        mx_ref[j] = jnp.broadcast_to(jnp.max(mx_ref[j], axis=1, keepdims=True), (gq, LANES))

    acc_ref[...] = jnp.zeros(acc_ref.shape, F32)

    def att_body(c, carry):
        cols = pl.ds(pl.multiple_of(c * kc, kc), kc)
        bt = bt_ref[cols, :]
        for j in range(ATTN_KV_HEADS):
            kx = jnp.concatenate([k_ref[j, cols, :], bt], axis=1)
            vx = jnp.concatenate([v_ref[j, cols, :], onec_ref[...]], axis=1)
            s = logits(j, kx)
            mb = mx_ref[j]
            p = jnp.concatenate(
                [jnp.exp2(s[:, a * LANES:(a + 1) * LANES] - mb) for a in range(nslab)], axis=1)
            acc_ref[j] = acc_ref[j] + jnp.dot(p.astype(BF16), vx, preferred_element_type=F32)
        return carry
    lax.fori_loop(0, nkc, att_body, 0)

    for h in range(ATTN_HEADS):
        a = acc_ref[h // ATTN_GROUP, (h % ATTN_GROUP) * tq:(h % ATTN_GROUP + 1) * tq, :]
        out_ref[:, h * LANES:(h + 1) * LANES] = (
            a[:, 0:LANES] / a[:, LANES:LANES + 1]).astype(out_ref.dtype)


def dsa_attention(q_r, qi_r, wi, k_r, v_r, ki_r, batch, seq, k_sel):
    n = batch * seq
    nq = seq // Q_BLOCK
    kc = _pick(seq, 512)
    triu = jnp.asarray((np.arange(kc)[:, None] <= np.arange(kc)[None, :]).astype(np.float32), BF16)
    gq = ATTN_GROUP * Q_BLOCK
    eye = jnp.asarray(np.tile(np.eye(Q_BLOCK, dtype=np.float32), (ATTN_GROUP, 1)), BF16)
    onec = np.zeros((kc, LANES), np.float32)
    onec[:, 0] = 1.0
    onec = jnp.asarray(onec, BF16)
    kern = functools.partial(_dsa_kernel, k_sel=k_sel, kc=kc)
    in_specs = [
        pl.BlockSpec((ATTN_HEADS, Q_BLOCK, LANES), lambda b, i: (0, b * nq + i, 0)),
        pl.BlockSpec((IDX_HEADS, Q_BLOCK, LANES), lambda b, i: (0, b * nq + i, 0)),
        pl.BlockSpec((Q_BLOCK, LANES), lambda b, i: (b * nq + i, 0)),
        pl.BlockSpec((ATTN_KV_HEADS, seq, LANES), lambda b, i: (0, b, 0)),
        pl.BlockSpec((ATTN_KV_HEADS, seq, LANES), lambda b, i: (0, b, 0)),
        pl.BlockSpec((seq, LANES), lambda b, i: (b, 0)),
        pl.BlockSpec((kc, kc), lambda b, i: (0, 0)),
        pl.BlockSpec((gq, Q_BLOCK), lambda b, i: (0, 0)),
        pl.BlockSpec((kc, LANES), lambda b, i: (0, 0)),
    ]
    scratch = [pltpu.VMEM((Q_BLOCK, seq), I32), pltpu.VMEM((Q_BLOCK, seq), I16),
               pltpu.VMEM((seq, Q_BLOCK), BF16),
               pltpu.VMEM((IDX_HEADS, Q_BLOCK, kc), F32),
               pltpu.VMEM((ATTN_KV_HEADS, gq, LANES), F32),
               pltpu.VMEM((ATTN_KV_HEADS, gq, 2 * LANES), F32)]
    return pl.pallas_call(
        kern, grid=(batch, nq), in_specs=in_specs,
        out_specs=pl.BlockSpec((Q_BLOCK, ATTN_INNER), lambda b, i: (b * nq + i, 0)),
        out_shape=jax.ShapeDtypeStruct((n, ATTN_INNER), BF16),
        scratch_shapes=scratch, compiler_params=_cparams(2), name="dsa_attention",
    )(q_r, qi_r, wi, k_r, v_r, ki_r, triu, eye, onec)


def _silu(x):
    return x * (1.0 / (1.0 + jnp.exp(-x)))


def _ssd_kernel(xbc_ref, z_ref, tail_ref, cw_ref, cb_ref, dtb_ref, alog_ref, dexp_ref, ng_ref,
                tri_ref, e_ref, out_ref, state_ref, ext_ref, y_ref):
    q = SSM_CHUNK
    hi = lax.Precision.HIGHEST

    @pl.when(pl.program_id(1) == 0)
    def _():
        state_ref[...] = jnp.zeros(state_ref.shape, F32)
        ext_ref[0:SUBLANES, :] = jnp.zeros((SUBLANES, SSM_CONV_DIM), F32)

    x = xbc_ref[...]
    ext_ref[SUBLANES:SUBLANES + q, :] = x
    conv = x * cw_ref[SSM_CONV - 1:SSM_CONV, :] + cb_ref[...]
    for k in range(1, SSM_CONV):
        conv = conv + ext_ref[SUBLANES - k:SUBLANES - k + q, :] * cw_ref[SSM_CONV - 1 - k:SSM_CONV - k, :]
    ext_ref[0:SUBLANES, :] = x[q - SUBLANES:q, :]
    xbc = _silu(conv)
    xs = xbc[:, 0:SSM_INNER]
    bm = xbc[:, SSM_INNER:SSM_INNER + SSM_GROUPS * SSM_STATE]
    cm = xbc[:, SSM_INNER + SSM_GROUPS * SSM_STATE:SSM_CONV_DIM]

    lane = lax.broadcasted_iota(I32, (q, LANES), 1)
    is_dt = (lane >= TAIL_DT) & (lane < TAIL_DT + SSM_HEADS)
    raw = tail_ref[...] + dtb_ref[...]
    dt = jnp.where(is_dt, jnp.maximum(raw, 0.0) + jnp.log1p(jnp.exp(-jnp.abs(raw))), 0.0)
    a = -jnp.exp(alog_ref[...])
    da = jnp.where(is_dt, dt * a, 0.0)
    acum = jnp.dot(tri_ref[...], da, precision=hi, preferred_element_type=F32)
    acum_t = acum.T
    acum_x = jnp.dot(acum, e_ref[...], precision=hi, preferred_element_type=F32)
    dt_x = jnp.dot(dt, e_ref[...], precision=hi, preferred_element_type=F32)
    last_x = acum_x[q - 1:q, :]
    xdt = xs * dt_x
    xdec = xdt * jnp.exp(last_x - acum_x)
    ri = lax.broadcasted_iota(I32, (q, q), 0)
    ci = lax.broadcasted_iota(I32, (q, q), 1)
    tril = ri >= ci
    pairs_per_group = SSM_GROUP_INNER // LANES

    for g in range(SSM_GROUPS):
        bg = bm[:, g * SSM_STATE:(g + 1) * SSM_STATE]
        cg = cm[:, g * SSM_STATE:(g + 1) * SSM_STATE]
        cgb = cg.astype(BF16)
        gsl = slice(g * SSM_GROUP_INNER, (g + 1) * SSM_GROUP_INNER)
        cb = lax.dot_general(cgb, bg.astype(BF16), (((1,), (1,)), ((), ())),
                             preferred_element_type=F32)
        y_off = jnp.dot(cgb, state_ref[g].astype(BF16), preferred_element_type=F32)
        y_ref[:, gsl] = y_off * jnp.exp(acum_x[:, gsl])
        for mth in range(pairs_per_group):
            slab = slice((g * pairs_per_group + mth) * LANES, (g * pairs_per_group + mth + 1) * LANES)
            xp = xdt[:, slab]
            yd = None
            for side in range(2):
                hl = TAIL_DT + (g * pairs_per_group + mth) * 2 + side
                seg = acum[:, hl:hl + 1] - acum_t[hl:hl + 1, :]
                lmat = jnp.exp(jnp.where(tril, seg, NEG_BIG))
                if side == 0:
                    xh = jnp.where(lane < SSM_HEAD_DIM, xp, 0.0)
                else:
                    xh = jnp.where(lane >= SSM_HEAD_DIM, xp, 0.0)
                part = jnp.dot((cb * lmat).astype(BF16), xh.astype(BF16), preferred_element_type=F32)
                yd = part if yd is None else yd + part
            y_ref[:, slab] = y_ref[:, slab] + yd
        st = jnp.dot(bg.T.astype(BF16), xdec[:, gsl].astype(BF16), preferred_element_type=F32)
        state_ref[g] = state_ref[g] * jnp.exp(last_x[:, gsl]) + st

    y = (y_ref[...] + dexp_ref[...] * xs) * _silu(z_ref[...])
    for g in range(SSM_GROUPS):
        gsl = slice(g * SSM_GROUP_INNER, (g + 1) * SSM_GROUP_INNER)
        yg = y[:, gsl]
        ms = jnp.mean(yg * yg, axis=-1, keepdims=True)
        out_ref[:, gsl] = (yg * lax.rsqrt(ms + EPS) * ng_ref[:, gsl]).astype(out_ref.dtype)


def ssd_mixer(proj, conv_w, conv_b, dtb_pad, alog_pad, d_exp, norm_g, batch, seq):
    n = batch * seq
    q = SSM_CHUNK
    nc = seq // q
    tri = jnp.asarray((np.arange(q)[:, None] >= np.arange(q)[None, :]).astype(np.float32))
    e = np.zeros((LANES, SSM_INNER), np.float32)
    for h in range(SSM_HEADS):
        e[TAIL_DT + h, h * SSM_HEAD_DIM:(h + 1) * SSM_HEAD_DIM] = 1.0
    e = jnp.asarray(e)

    def col(width, off):
        return pl.BlockSpec((q, width), lambda b, c: (b * nc + c, off // width))

    def full(shape):
        return pl.BlockSpec(shape, lambda b, c: (0,) * len(shape))

    in_specs = [col(SSM_CONV_DIM, COL_XBC), col(SSM_INNER, COL_Z), col(LANES, COL_TAIL),
                full((SSM_CONV, SSM_CONV_DIM)), full((1, SSM_CONV_DIM)), full((1, LANES)),
                full((1, LANES)), full((1, SSM_INNER)), full((1, SSM_INNER)),
                full((q, q)), full((LANES, SSM_INNER))]
    scratch = [pltpu.VMEM((SSM_GROUPS, SSM_STATE, SSM_GROUP_INNER), F32),
               pltpu.VMEM((SUBLANES + q, SSM_CONV_DIM), F32),
               pltpu.VMEM((q, SSM_INNER), F32)]
    return pl.pallas_call(
        _ssd_kernel, grid=(batch, nc), in_specs=in_specs,
        out_specs=pl.BlockSpec((q, SSM_INNER), lambda b, c: (b * nc + c, 0)),
        out_shape=jax.ShapeDtypeStruct((n, SSM_INNER), BF16),
        scratch_shapes=scratch, compiler_params=_cparams(2), name="ssd_mixer",
    )(proj, proj, proj, conv_w, conv_b, dtb_pad, alog_pad, d_exp, norm_g, tri, e)


def _ret_kernel(q_ref, k_ref, v_ref, g_ref, pos_ref, inv_ref, dmat_ref, qdec_ref, kdec_ref,
                cdec_ref, ng_ref, out_ref, state_ref):
    half = RET_QK_DIM // 2

    @pl.when(pl.program_id(1) == 0)
    def _():
        state_ref[...] = jnp.zeros(state_ref.shape, F32)

    ang = pos_ref[...].astype(F32) * inv_ref[...]
    cosf = jnp.cos(ang)
    sinf = jnp.sin(ang)

    def rope(x):
        x1 = x[:, 0:half]
        x2 = x[:, half:2 * half]
        return jnp.concatenate([x1 * cosf - x2 * sinf, x2 * cosf + x1 * sinf], axis=-1)

    for h in range(RET_HEADS):
        qh = rope(q_ref[:, h * RET_QK_DIM:(h + 1) * RET_QK_DIM].astype(F32))
        kh = rope(k_ref[:, h * RET_QK_DIM:(h + 1) * RET_QK_DIM].astype(F32)) * (RET_QK_DIM ** -0.5)
        vh = v_ref[:, h * RET_V_DIM:(h + 1) * RET_V_DIM]
        qb = qh.astype(BF16)
        s = lax.dot_general(qb, kh.astype(BF16), (((1,), (1,)), ((), ())),
                            preferred_element_type=F32) * dmat_ref[h]
        inner = jnp.dot(s.astype(BF16), vh, preferred_element_type=F32)
        st = state_ref[h]
        cross = jnp.dot(qb, st.astype(BF16), preferred_element_type=F32) * qdec_ref[h]
        kd = (kh * kdec_ref[h]).T.astype(BF16)
        state_ref[h] = st * cdec_ref[h] + jnp.dot(kd, vh, preferred_element_type=F32)
        o = inner + cross
        ms = jnp.mean(o * o, axis=-1, keepdims=True)
        o = o * lax.rsqrt(ms + EPS) * ng_ref[:, h * RET_V_DIM:(h + 1) * RET_V_DIM]
        gate = _silu(g_ref[:, h * RET_V_DIM:(h + 1) * RET_V_DIM].astype(F32))
        out_ref[:, h * RET_V_DIM:(h + 1) * RET_V_DIM] = (gate * o).astype(out_ref.dtype)


def retention_mixer(proj, pos, norm_g, batch, seq):
    n = batch * seq
    q = RET_CHUNK
    nc = seq // q
    half = RET_QK_DIM // 2
    inv = jnp.power(RET_THETA, -(jnp.arange(half, dtype=F32) * 2.0 / RET_QK_DIM)).reshape(1, half)
    log_gamma = jnp.log(1.0 - jnp.power(2.0, -5.0 - jnp.arange(RET_HEADS, dtype=F32)))
    i = jnp.arange(q, dtype=F32)
    diff = i[:, None] - i[None, :]
    dmat = jnp.where(diff[None] >= 0,
                     jnp.exp(jnp.maximum(diff, 0.0)[None] * log_gamma[:, None, None]), 0.0)
    qdec = jnp.exp((i + 1.0)[None, :, None] * log_gamma[:, None, None])
    kdec = jnp.exp((q - 1.0 - i)[None, :, None] * log_gamma[:, None, None])
    cdec = jnp.broadcast_to(jnp.exp(q * log_gamma)[:, None, None], (RET_HEADS, 1, RET_V_DIM))

    def col(width, off):
        return pl.BlockSpec((q, width), lambda b, c: (b * nc + c, off // width))

    def full(shape):
        return pl.BlockSpec(shape, lambda b, c: (0,) * len(shape))

    in_specs = [col(RET_QK_TOTAL, 0), col(RET_QK_TOTAL, RET_QK_TOTAL),
                col(RET_V_TOTAL, 2 * RET_QK_TOTAL), col(RET_V_TOTAL, 2 * RET_QK_TOTAL + RET_V_TOTAL),
                pl.BlockSpec((q, 1), lambda b, c: (b * nc + c, 0)),
                full((1, half)), full((RET_HEADS, q, q)), full((RET_HEADS, q, 1)),
                full((RET_HEADS, q, 1)), full((RET_HEADS, 1, RET_V_DIM)), full((1, RET_V_TOTAL))]
    return pl.pallas_call(
        _ret_kernel, grid=(batch, nc), in_specs=in_specs,
        out_specs=pl.BlockSpec((q, RET_V_TOTAL), lambda b, c: (b * nc + c, 0)),
        out_shape=jax.ShapeDtypeStruct((n, RET_V_TOTAL), BF16),
        scratch_shapes=[pltpu.VMEM((RET_HEADS, RET_QK_DIM, RET_V_DIM), F32)],
        compiler_params=_cparams(2), name="retention",
    )(proj, proj, proj, proj, pos, inv, dmat, qdec, kdec, cdec, norm_g.reshape(1, RET_V_TOTAL))


ROUTE_E0 = MOE_GROUPS
INFO_EID, INFO_GATE, INFO_RANK = 0, 2, 4


def _router_kernel(x_ref, g_ref, wr_ref, lt_ref, h_ref, info_ref, cnt_ref, run_ref):
    tb = x_ref.shape[0]

    @pl.when(pl.program_id(0) == 0)
    def _():
        run_ref[...] = jnp.zeros(run_ref.shape, F32)

    x = x_ref[...]
    ms = jnp.mean(x * x, axis=-1, keepdims=True)
    hn = x * lax.rsqrt(ms + EPS) * g_ref[...]
    h_ref[...] = hn
    logits = jnp.dot(hn, wr_ref[...], precision=lax.Precision.HIGHEST, preferred_element_type=F32)

    lane = lax.broadcasted_iota(I32, (tb, LANES), 1).astype(F32)
    far = jnp.float32(4 * LANES)
    ninf = jnp.float32(-jnp.inf)
    gl = jnp.where(lane < MOE_GROUPS, logits, ninf)
    gmax = jnp.max(gl, axis=1, keepdims=True)
    grp = jnp.min(jnp.where(gl == gmax, lane, far), axis=1, keepdims=True)
    p_grp = 1.0 / jnp.sum(jnp.exp(gl - gmax), axis=1, keepdims=True)
    lo = ROUTE_E0 + grp * MOE_EXPERTS_PER_GROUP
    el = jnp.where((lane >= lo) & (lane < lo + MOE_EXPERTS_PER_GROUP), logits, ninf)
    m1 = jnp.max(el, axis=1, keepdims=True)
    i1 = jnp.min(jnp.where(el == m1, lane, far), axis=1, keepdims=True)
    el2 = jnp.where(lane == i1, ninf, el)
    m2 = jnp.max(el2, axis=1, keepdims=True)
    i2 = jnp.min(jnp.where(el2 == m2, lane, far), axis=1, keepdims=True)
    e2 = jnp.exp(m2 - m1)
    g1 = p_grp / (1.0 + e2)
    g2 = p_grp * e2 / (1.0 + e2)
    oh1 = jnp.where(lane == i1, 1.0, 0.0)
    oh2 = jnp.where(lane == i2, 1.0, 0.0)
    cnt = oh1 + oh2
    before = jnp.dot(lt_ref[...], cnt.astype(BF16), preferred_element_type=F32) + run_ref[...]
    r1 = jnp.sum(oh1 * before, axis=1, keepdims=True)
    r2 = jnp.sum(oh2 * before, axis=1, keepdims=True)
    run_ref[...] = run_ref[...] + jnp.sum(cnt, axis=0, keepdims=True)
    cnt_ref[...] = run_ref[...]

    info = jnp.where(lane == INFO_EID, i1 - ROUTE_E0, 0.0)
    info = jnp.where(lane == INFO_EID + 1, i2 - ROUTE_E0, info)
    info = jnp.where(lane == INFO_GATE, g1, info)
    info = jnp.where(lane == INFO_GATE + 1, g2, info)
    info = jnp.where(lane == INFO_RANK, r1, info)
    info = jnp.where(lane == INFO_RANK + 1, r2, info)
    info_ref[...] = info


def moe_router(x, g, w_router_pad):
    n, d = x.shape
    tb = _pick(n, 512)
    lt = jnp.asarray((np.arange(tb)[:, None] > np.arange(tb)[None, :]).astype(np.float32), BF16)
    return pl.pallas_call(
        _router_kernel, grid=(n // tb,),
        in_specs=[pl.BlockSpec((tb, d), lambda i: (i, 0)), pl.BlockSpec((1, d), lambda i: (0, 0)),
                  pl.BlockSpec((d, LANES), lambda i: (0, 0)), pl.BlockSpec((tb, tb), lambda i: (0, 0))],
        out_specs=[pl.BlockSpec((tb, d), lambda i: (i, 0)), pl.BlockSpec((tb, LANES), lambda i: (i, 0)),
                   pl.BlockSpec((1, LANES), lambda i: (0, 0))],
        out_shape=[jax.ShapeDtypeStruct((n, d), F32), jax.ShapeDtypeStruct((n, LANES), F32),
                   jax.ShapeDtypeStruct((1, LANES), F32)],
        scratch_shapes=[pltpu.VMEM((1, LANES), F32)],
        compiler_params=_cparams(1), name="moe_router",
    )(x, g.reshape(1, d), w_router_pad, lt)


def _row_copy(src_ref, src_row, dst_ref, dst_row, sem):
    return pltpu.make_async_copy(src_ref.at[pl.ds(src_row, 1), :], dst_ref.at[pl.ds(dst_row, 1), :], sem)


def _expert_kernel(be_ref, nused_ref, st_ref, h_hbm, wg_ref, wu_ref, wd_ref, out_ref,
                   xbuf, sem, wgu_s, wd_s):
    i = pl.program_id(0)
    slot = i % 2
    nused = nused_ref[0]
    d = xbuf.shape[2]
    ff = EXPERT_FF

    def gather(blk, s):
        def body(r, carry):
            _row_copy(h_hbm, st_ref[blk * MOE_BLOCK + r], xbuf.at[s], r, sem.at[s]).start()
            return carry
        lax.fori_loop(0, MOE_BLOCK, body, 0, unroll=8)

    @pl.when(jnp.logical_and(i == 0, nused > 0))
    def _():
        gather(0, 0)

    @pl.when(i + 1 < nused)
    def _():
        gather(i + 1, 1 - slot)

    @pl.when(i < nused)
    def _():
        def drain(r, carry):
            _row_copy(h_hbm, 0, xbuf.at[slot], 0, sem.at[slot]).wait()
            return carry
        lax.fori_loop(0, MOE_BLOCK, drain, 0, unroll=8)

        changed = jnp.logical_or(i == 0, be_ref[i] != be_ref[jnp.maximum(i - 1, 0)])

        @pl.when(changed)
        def _():
            rc = _pick(d, 256)

            def body(r, carry):
                rows = pl.ds(pl.multiple_of(r * rc, rc), rc)
                wgu_s[rows, 0:ff] = wg_ref[0, 0, rows, :].astype(BF16)
                wgu_s[rows, ff:2 * ff] = wu_ref[0, 0, rows, :].astype(BF16)
                return carry
            lax.fori_loop(0, d // rc, body, 0)
            wd_s[...] = wd_ref[0, 0].astype(BF16)

        gu = jnp.dot(xbuf[slot].astype(BF16), wgu_s[...], preferred_element_type=F32)
        hid = _silu(gu[:, 0:ff]) * gu[:, ff:2 * ff]
        out_ref[...] = jnp.dot(hid.astype(BF16), wd_s[...], preferred_element_type=F32)

    @pl.when(i >= nused)
    def _():
        out_ref[...] = jnp.zeros(out_ref.shape, out_ref.dtype)


def moe_experts(h, slot_tok, blk_expert, nused, w_gate, w_up, w_down, layer):
    n, d = h.shape
    p = slot_tok.shape[0]
    nblk = p // MOE_BLOCK
    ff = EXPERT_FF
    grid_spec = pltpu.PrefetchScalarGridSpec(
        num_scalar_prefetch=3, grid=(nblk,),
        in_specs=[pl.BlockSpec(memory_space=pl.ANY),
                  pl.BlockSpec((1, 1, d, ff), lambda i, be, nu, st: (layer, be[i], 0, 0)),
                  pl.BlockSpec((1, 1, d, ff), lambda i, be, nu, st: (layer, be[i], 0, 0)),
                  pl.BlockSpec((1, 1, ff, d), lambda i, be, nu, st: (layer, be[i], 0, 0))],
        out_specs=pl.BlockSpec((MOE_BLOCK, d), lambda i, be, nu, st: (i, 0)),
        scratch_shapes=[pltpu.VMEM((2, MOE_BLOCK, d), h.dtype), pltpu.SemaphoreType.DMA((2,)),
                        pltpu.VMEM((d, 2 * ff), BF16), pltpu.VMEM((ff, d), BF16)])
    return pl.pallas_call(
        _expert_kernel, grid_spec=grid_spec,
        out_shape=jax.ShapeDtypeStruct((p, d), F32),
        compiler_params=_cparams(1), name="moe_experts",
    )(blk_expert, nused, slot_tok, h, w_gate, w_up, w_down)


def _slots_kernel(dest_ref, slot_ref):
    def clear(s, carry):
        slot_ref[s] = 0
        return carry
    lax.fori_loop(0, slot_ref.shape[0], clear, 0, unroll=8)

    def body(a, carry):
        slot_ref[dest_ref[a]] = lax.div(a, jnp.int32(MOE_TOPK))
        return carry
    lax.fori_loop(0, dest_ref.shape[0], body, 0, unroll=8)


def moe_slots(dest_flat, p):
    return pl.pallas_call(
        _slots_kernel,
        in_specs=[pl.BlockSpec(memory_space=pltpu.SMEM)],
        out_specs=pl.BlockSpec(memory_space=pltpu.SMEM),
        out_shape=jax.ShapeDtypeStruct((p,), I32),
        name="moe_slots",
    )(dest_flat)


def _combine_kernel(dcur_ref, dnxt_ref, info_ref, x_ref, y_hbm, out_ref, ybuf, sem, *, tb, nsteps):
    i = pl.program_id(0)
    slot = i % 2

    def issue(dref, s):
        def body(r, carry):
            for k in range(MOE_TOPK):
                _row_copy(y_hbm, dref[0, 0, MOE_TOPK * r + k], ybuf.at[s, k], r, sem.at[s]).start()
            return carry
        lax.fori_loop(0, tb, body, 0, unroll=8)

    @pl.when(i == 0)
    def _():
        issue(dcur_ref, 0)

    @pl.when(i + 1 < nsteps)
    def _():
        issue(dnxt_ref, 1 - slot)

    def drain(r, carry):
        for k in range(MOE_TOPK):
            _row_copy(y_hbm, 0, ybuf.at[slot, k], 0, sem.at[slot]).wait()
        return carry
    lax.fori_loop(0, tb, drain, 0, unroll=8)

    info = info_ref[...]
    out = x_ref[...]
    for k in range(MOE_TOPK):
        out = out + ybuf[slot, k] * info[:, INFO_GATE + k:INFO_GATE + k + 1]
    out_ref[...] = out


def moe_combine(x, y, dest, info):
    n, d = x.shape
    tb = _pick(n, MOE_BLOCK)
    nsteps = n // tb
    kern = functools.partial(_combine_kernel, tb=tb, nsteps=nsteps)
    dest3 = dest.reshape(nsteps, 1, MOE_TOPK * tb)
    return pl.pallas_call(
        kern, grid=(nsteps,),
        in_specs=[pl.BlockSpec((1, 1, MOE_TOPK * tb), lambda i: (i, 0, 0), memory_space=pltpu.SMEM),
                  pl.BlockSpec((1, 1, MOE_TOPK * tb), lambda i: (jnp.minimum(i + 1, nsteps - 1), 0, 0),
                               memory_space=pltpu.SMEM),
                  pl.BlockSpec((tb, LANES), lambda i: (i, 0)),
                  pl.BlockSpec((tb, d), lambda i: (i, 0)),
                  pl.BlockSpec(memory_space=pl.ANY)],
        out_specs=pl.BlockSpec((tb, d), lambda i: (i, 0)),
        out_shape=jax.ShapeDtypeStruct((n, d), x.dtype),
        scratch_shapes=[pltpu.VMEM((2, MOE_TOPK, tb, d), y.dtype), pltpu.SemaphoreType.DMA((2,))],
        compiler_params=_cparams(1), name="moe_combine",
    )(dest3, dest3, info, x, y)


def hier_moe_layer(x, norm_g, w_rg, w_re, w_gate, w_up, w_down, layer):
    n, d = x.shape
    w_router = jnp.concatenate(
        [w_rg, w_re, jnp.zeros((d, LANES - MOE_GROUPS - N_EXPERTS), F32)], axis=1)
    h, info, cnt = moe_router(x, norm_g, w_router)
    counts = cnt[0, ROUTE_E0:ROUTE_E0 + N_EXPERTS].astype(I32)
    eid = info[:, INFO_EID:INFO_EID + MOE_TOPK].astype(I32)
    rank = info[:, INFO_RANK:INFO_RANK + MOE_TOPK].astype(I32)
    padded = (counts + MOE_BLOCK - 1) // MOE_BLOCK * MOE_BLOCK
    pad_end = jnp.cumsum(padded)
    pad_start = pad_end - padded
    dest = pad_start[eid] + rank
    a = n * MOE_TOPK
    p = (-(-a // MOE_BLOCK) + N_EXPERTS) * MOE_BLOCK
    nblk = p // MOE_BLOCK
    blk_start = jnp.arange(nblk, dtype=I32) * MOE_BLOCK
    blk_expert = jnp.minimum(jnp.sum((pad_end[None, :] <= blk_start[:, None]).astype(I32), axis=1),
                             N_EXPERTS - 1).astype(I32)
    nused = (pad_end[-1:] // MOE_BLOCK).astype(I32)
    slot_tok = moe_slots(dest.reshape(-1), p)
    y = moe_experts(h, slot_tok, blk_expert, nused, w_gate, w_up, w_down, layer)
    return moe_combine(x, y, dest, info)


def _pad_lanes(v, offset, width=LANES):
    out = jnp.zeros((1, width), F32)
    return out.at[0, offset:offset + v.shape[0]].set(v.astype(F32))


def hybrid_layer(x, pos, norm_g, w_in, q_norm, k_norm, kidx_norm, conv_w, conv_b, dt_bias, a_log,
                 d_skip, ssm_norm, w_out, batch, seq, k_sel):
    n, d = x.shape
    offs = np.cumsum([0, ATTN_INNER, ATTN_KV, ATTN_KV, IDX_INNER, IDX_HEAD_DIM, IDX_HEADS,
                      SSM_INNER, SSM_CONV_DIM, SSM_HEADS])
    seg = {name: w_in[:, offs[j]:offs[j + 1]] for j, name in
           enumerate(["q", "k", "v", "qi", "ki", "wi", "z", "xbc", "dt"])}
    tail_pad = LANES - IDX_HEAD_DIM - IDX_HEADS - SSM_HEADS
    w_perm = jnp.concatenate(
        [seg["xbc"], seg["k"], seg["v"], seg["z"], seg["q"], seg["qi"], seg["ki"], seg["wi"],
         seg["dt"], jnp.zeros((d, tail_pad + HYB_COLS - COL_TAIL - LANES), F32)], axis=1).astype(BF16)
    proj = fused_matmul([x], w_perm, g=norm_g, out_dtype=F32, name="hyb_in_proj")

    rot = ATTN_HEAD_DIM // ROPE_FRACTION
    rot_i = IDX_HEAD_DIM // ROPE_FRACTION
    inv16 = jnp.power(ROPE_THETA, -(jnp.arange(rot // 2, dtype=F32) * 2.0 / rot))
    inv8 = jnp.power(ROPE_THETA, -(jnp.arange(rot_i // 2, dtype=F32) * 2.0 / rot_i))
    inv_a = jnp.concatenate([inv16, inv16, jnp.zeros((LANES - rot,), F32)]).reshape(1, LANES)
    inv_h = jnp.concatenate([inv8, inv8, jnp.zeros((IDX_HEAD_DIM - rot_i,), F32)])
    inv_i = jnp.tile(inv_h, IDX_HEADS).reshape(1, IDX_INNER)
    inv_t = jnp.concatenate([inv_h, jnp.zeros((LANES - IDX_HEAD_DIM,), F32)]).reshape(1, LANES)
    q_r, k_r, v_r, qi_r, ki_r, wi = hyb_prep(
        proj, pos, inv_a, inv_i, inv_t, q_norm.reshape(1, LANES), k_norm.reshape(1, LANES),
        _pad_lanes(kidx_norm, TAIL_KI))
    attn = dsa_attention(q_r, qi_r, wi, k_r, v_r, ki_r, batch, seq, k_sel)

    ssm = ssd_mixer(proj, conv_w, conv_b.reshape(1, SSM_CONV_DIM), _pad_lanes(dt_bias, TAIL_DT),
                    _pad_lanes(a_log, TAIL_DT), jnp.repeat(d_skip, SSM_HEAD_DIM).reshape(1, SSM_INNER),
                    ssm_norm.reshape(1, SSM_INNER), batch, seq)
    return fused_matmul([attn, ssm], w_out.astype(BF16), res=x, out_dtype=F32, name="hyb_out_proj")


def retention_layer(x, pos, norm_g, w_in, ret_norm, w_out, batch, seq):
    proj = fused_matmul([x], w_in.astype(BF16), g=norm_g, out_dtype=BF16, name="ret_in_proj")
    o = retention_mixer(proj, pos, ret_norm, batch, seq)
    return fused_matmul([o], w_out.astype(BF16), res=x, out_dtype=F32, name="ret_out_proj")


def kernel(x, positions, mix_norm, ffn_norm, hyb_w_in, attn_q_norm, attn_k_norm, idx_k_norm, ssm_conv_w, ssm_conv_b, ssm_dt_bias, ssm_a_log, ssm_d, ssm_norm, hyb_w_out, ret_w_in, ret_norm, ret_w_out, moe_router_group, moe_router_expert, moe_w_gate, moe_w_up, moe_w_down):
    batch, seq, d = x.shape
    depth = mix_norm.shape[0]
    k_sel = min(TOPK_MAX, seq // 4)
    n = batch * seq
    xf = x.reshape(n, d)
    pos = positions.reshape(n, 1).astype(I32)
    for layer in range(depth):
        i = layer // 2
        if layer % 2 == 0:
            xf = hybrid_layer(xf, pos, mix_norm[layer], hyb_w_in[i], attn_q_norm[i], attn_k_norm[i],
                              idx_k_norm[i], ssm_conv_w[i], ssm_conv_b[i], ssm_dt_bias[i],
                              ssm_a_log[i], ssm_d[i], ssm_norm[i], hyb_w_out[i], batch, seq, k_sel)
        else:
            xf = retention_layer(xf, pos, mix_norm[layer], ret_w_in[i], ret_norm[i], ret_w_out[i],
                                 batch, seq)
        xf = hier_moe_layer(xf, ffn_norm[layer], moe_router_group[layer], moe_router_expert[layer],
                            moe_w_gate, moe_w_up, moe_w_down, layer)
    return xf.reshape(batch, seq, d)
```

```python
import functools

import jax
import jax.numpy as jnp
import numpy as np
from jax import lax
from jax.experimental import pallas as pl
from jax.experimental.pallas import tpu as pltpu

F32 = jnp.float32
BF16 = jnp.bfloat16
I32 = jnp.int32
I16 = jnp.int16
HALF_BITS = 16

ATTN_HEADS = 8
ATTN_KV_HEADS = 2
ATTN_GROUP = ATTN_HEADS // ATTN_KV_HEADS
ATTN_HEAD_DIM = 128
ATTN_INNER = ATTN_HEADS * ATTN_HEAD_DIM
ATTN_KV = ATTN_KV_HEADS * ATTN_HEAD_DIM
IDX_HEADS = 8
IDX_HEAD_DIM = 64
IDX_INNER = IDX_HEADS * IDX_HEAD_DIM
TOPK_MAX = 256
Q_BLOCK = 128
ROPE_THETA = 500000.0
ROPE_FRACTION = 4

SSM_HEADS = 16
SSM_HEAD_DIM = 64
SSM_INNER = SSM_HEADS * SSM_HEAD_DIM
SSM_GROUPS = 2
SSM_STATE = 128
SSM_CONV = 4
SSM_CONV_DIM = SSM_INNER + 2 * SSM_GROUPS * SSM_STATE
SSM_CHUNK = 128
SSM_GROUP_INNER = SSM_INNER // SSM_GROUPS

RET_HEADS = 8
RET_QK_DIM = 256
RET_V_DIM = 512
RET_QK_TOTAL = RET_HEADS * RET_QK_DIM
RET_V_TOTAL = RET_HEADS * RET_V_DIM
RET_CHUNK = 128
RET_THETA = 10000.0

MOE_GROUPS = 8
MOE_EXPERTS_PER_GROUP = 8
N_EXPERTS = MOE_GROUPS * MOE_EXPERTS_PER_GROUP
MOE_TOPK = 2
EXPERT_FF = 512
MOE_BLOCK = 128

EPS = 1e-6

LANES = 128
SUBLANES = 8
VMEM_LIMIT = 52 * 1024 * 1024

LOG2E = 1.4426950408889634
NEG_BIG = -1e30
INT_MIN = -2147483648

COL_XBC = 0
COL_K = SSM_CONV_DIM
COL_V = COL_K + ATTN_KV
COL_Z = COL_V + ATTN_KV
COL_Q = COL_Z + SSM_INNER
COL_QI = COL_Q + ATTN_INNER
COL_TAIL = COL_QI + IDX_INNER
HYB_COLS = 5120
TAIL_KI = 0
TAIL_WI = IDX_HEAD_DIM
TAIL_DT = TAIL_WI + IDX_HEADS


def _cparams(n_axes):
    return pltpu.CompilerParams(dimension_semantics=("arbitrary",) * n_axes,
                                vmem_limit_bytes=VMEM_LIMIT)


def _pick(n, pref):
    t = min(n, pref)
    while n % t:
        t //= 2
    return t


def _mm_kernel(*refs, n_a, has_norm, has_res, row_chunk):
    a_refs = refs[:n_a]
    pos = n_a
    g_ref = None
    if has_norm:
        g_ref = refs[pos]
        pos += 1
    w_refs = refs[pos:pos + n_a]
    pos += n_a
    res_ref = None
    if has_res:
        res_ref = refs[pos]
        pos += 1
    out_ref = refs[pos]
    pos += 1
    xn_ref = refs[pos] if has_norm else None
    tm = out_ref.shape[0]

    if has_norm:
        @pl.when(pl.program_id(1) == 0)
        def _():
            def body(r, carry):
                rows = pl.ds(pl.multiple_of(r * row_chunk, row_chunk), row_chunk)
                x = a_refs[0][rows, :]
                ms = jnp.mean(x * x, axis=-1, keepdims=True)
                xn_ref[rows, :] = (x * lax.rsqrt(ms + EPS) * g_ref[...]).astype(BF16)
                return carry
            lax.fori_loop(0, tm // row_chunk, body, 0)
        acc = jnp.dot(xn_ref[...], w_refs[0][...], preferred_element_type=F32)
    else:
        acc = jnp.dot(a_refs[0][...], w_refs[0][...], preferred_element_type=F32)
        for p in range(1, n_a):
            acc = acc + jnp.dot(a_refs[p][...], w_refs[p][...], preferred_element_type=F32)
    if has_res:
        acc = acc + res_ref[...]
    out_ref[...] = acc.astype(out_ref.dtype)


def fused_matmul(a_list, w, *, g=None, res=None, out_dtype=F32, tm=1024, tn=512, name="mm"):
    n = a_list[0].shape[0]
    kp = a_list[0].shape[1]
    m = w.shape[1]
    tm = _pick(n, tm)
    tn = _pick(m, tn)
    n_a = len(a_list)
    has_norm = g is not None
    has_res = res is not None
    in_specs = [pl.BlockSpec((tm, kp), lambda i, j: (i, 0)) for _ in a_list]
    args = list(a_list)
    if has_norm:
        in_specs.append(pl.BlockSpec((1, kp), lambda i, j: (0, 0)))
        args.append(g.reshape(1, kp).astype(F32))
    for p in range(n_a):
        in_specs.append(pl.BlockSpec((kp, tn), lambda i, j, p=p: (p, j)))
        args.append(w)
    if has_res:
        in_specs.append(pl.BlockSpec((tm, tn), lambda i, j: (i, j)))
        args.append(res)
    scratch = [pltpu.VMEM((tm, kp), BF16)] if has_norm else []
    kern = functools.partial(_mm_kernel, n_a=n_a, has_norm=has_norm, has_res=has_res,
                             row_chunk=_pick(tm, 128))
    return pl.pallas_call(
        kern,
        grid=(n // tm, m // tn),
        in_specs=in_specs,
        out_specs=pl.BlockSpec((tm, tn), lambda i, j: (i, j)),
        out_shape=jax.ShapeDtypeStruct((n, m), out_dtype),
        scratch_shapes=scratch,
        compiler_params=_cparams(2),
        name=name,
    )(*args)


def _rope_lanes(x, cosf, s_neg, s_pos, half):
    width = x.shape[-1]
    return (x * cosf + pltpu.roll(x, width - half, axis=1) * s_neg
            + pltpu.roll(x, half, axis=1) * s_pos)


def _prep_kernel(q_ref, k_ref, v_ref, qi_ref, tail_ref, pos_ref, inva_ref, invi_ref, invt_ref,
                 qn_ref, kn_ref, kin_ref,
                 qo_ref, ko_ref, vo_ref, qio_ref, kio_ref, wio_ref):
    tb = q_ref.shape[0]
    posf = pos_ref[...].astype(F32)
    lane = lax.broadcasted_iota(I32, (tb, LANES), 1)

    half = ATTN_HEAD_DIM // ROPE_FRACTION // 2
    ang = posf * inva_ref[...]
    cosf = jnp.cos(ang)
    sinf = jnp.sin(ang)
    s_neg = jnp.where(lane < half, -sinf, 0.0)
    s_pos = jnp.where((lane >= half) & (lane < 2 * half), sinf, 0.0)
    scale = ATTN_HEAD_DIM ** -0.5 * LOG2E
    for h in range(ATTN_HEADS):
        x = q_ref[:, h * LANES:(h + 1) * LANES]
        ms = jnp.mean(x * x, axis=-1, keepdims=True)
        x = x * lax.rsqrt(ms + EPS) * qn_ref[...]
        x = _rope_lanes(x, cosf, s_neg, s_pos, half)
        qo_ref[h] = (x * scale).astype(BF16)
    for h in range(ATTN_KV_HEADS):
        x = k_ref[:, h * LANES:(h + 1) * LANES]
        ms = jnp.mean(x * x, axis=-1, keepdims=True)
        x = x * lax.rsqrt(ms + EPS) * kn_ref[...]
        x = _rope_lanes(x, cosf, s_neg, s_pos, half)
        ko_ref[h] = x.T.astype(BF16)
        vo_ref[h] = v_ref[:, h * LANES:(h + 1) * LANES].astype(BF16)

    half_i = IDX_HEAD_DIM // ROPE_FRACTION // 2
    lane_i = lax.broadcasted_iota(I32, (tb, IDX_INNER), 1) & (IDX_HEAD_DIM - 1)
    ang_i = posf * invi_ref[...]
    cos_i = jnp.cos(ang_i)
    sin_i = jnp.sin(ang_i)
    sn_i = jnp.where(lane_i < half_i, -sin_i, 0.0)
    sp_i = jnp.where((lane_i >= half_i) & (lane_i < 2 * half_i), sin_i, 0.0)
    qi = _rope_lanes(qi_ref[...], cos_i, sn_i, sp_i, half_i)
    for h in range(IDX_HEADS):
        slab = qi[:, (h // 2) * LANES:(h // 2 + 1) * LANES]
        if h % 2:
            slab = pltpu.roll(slab, IDX_HEAD_DIM, axis=1)
        qio_ref[h] = jnp.where(lane < IDX_HEAD_DIM, slab, 0.0).astype(BF16)

    t = tail_ref[...]
    is_ki = lane < IDX_HEAD_DIM
    ms = jnp.sum(jnp.where(is_ki, t * t, 0.0), axis=-1, keepdims=True) * (1.0 / IDX_HEAD_DIM)
    kin = t * lax.rsqrt(ms + EPS) * kin_ref[...]
    ang_t = posf * invt_ref[...]
    cos_t = jnp.cos(ang_t)
    sin_t = jnp.sin(ang_t)
    sn_t = jnp.where(lane < half_i, -sin_t, 0.0)
    sp_t = jnp.where((lane >= half_i) & (lane < 2 * half_i), sin_t, 0.0)
    kin = _rope_lanes(kin, cos_t, sn_t, sp_t, half_i)
    kio_ref[...] = jnp.where(is_ki, kin, 0.0).T.astype(BF16)
    wio_ref[...] = t * (IDX_HEADS ** -0.5 * IDX_HEAD_DIM ** -0.5)


def hyb_prep(proj, pos, inv_a, inv_i, inv_t, q_norm, k_norm, kidx_norm_pad):
    n = proj.shape[0]
    tb = _pick(n, 256)

    def col(width, off):
        return pl.BlockSpec((tb, width), lambda i: (i, off // width))

    def full(shape):
        return pl.BlockSpec(shape, lambda i: (0,) * len(shape))

    in_specs = [col(ATTN_INNER, COL_Q), col(ATTN_KV, COL_K), col(ATTN_KV, COL_V),
                col(IDX_INNER, COL_QI), col(LANES, COL_TAIL),
                pl.BlockSpec((tb, 1), lambda i: (i, 0)),
                full((1, LANES)), full((1, IDX_INNER)), full((1, LANES)),
                full((1, LANES)), full((1, LANES)), full((1, LANES))]
    out_shape = [jax.ShapeDtypeStruct((ATTN_HEADS, n, LANES), BF16),
                 jax.ShapeDtypeStruct((ATTN_KV_HEADS, LANES, n), BF16),
                 jax.ShapeDtypeStruct((ATTN_KV_HEADS, n, LANES), BF16),
                 jax.ShapeDtypeStruct((IDX_HEADS, n, LANES), BF16),
                 jax.ShapeDtypeStruct((LANES, n), BF16),
                 jax.ShapeDtypeStruct((n, LANES), F32)]
    out_specs = [pl.BlockSpec((ATTN_HEADS, tb, LANES), lambda i: (0, i, 0)),
                 pl.BlockSpec((ATTN_KV_HEADS, LANES, tb), lambda i: (0, 0, i)),
                 pl.BlockSpec((ATTN_KV_HEADS, tb, LANES), lambda i: (0, i, 0)),
                 pl.BlockSpec((IDX_HEADS, tb, LANES), lambda i: (0, i, 0)),
                 pl.BlockSpec((LANES, tb), lambda i: (0, i)),
                 pl.BlockSpec((tb, LANES), lambda i: (i, 0))]
    return pl.pallas_call(
        _prep_kernel, grid=(n // tb,), in_specs=in_specs, out_specs=out_specs,
        out_shape=out_shape, compiler_params=_cparams(1), name="hyb_prep",
    )(proj, proj, proj, proj, proj, pos, inv_a, inv_i, inv_t, q_norm, k_norm, kidx_norm_pad)


def _sortable(x):
    b = pltpu.bitcast(x, I32)
    return jnp.where(b < 0, b ^ jnp.int32(0x7FFFFFFF), b)


def _dsa_kernel(q_ref, qi_ref, wi_ref, k_ref, v_ref, ki_ref, triu_ref, eye_ref, onec_ref, out_ref,
                keys_ref, half_ref, bias_ref, wb_ref, mx_ref, acc_ref, *, k_sel, kc):
    i = pl.program_id(1)
    tq = Q_BLOCK
    nkc = (i * tq + tq + kc - 1) // kc
    row = i * tq + lax.broadcasted_iota(I32, (tq, kc), 0)
    lane = lax.broadcasted_iota(I32, (tq, kc), 1)
    nslab = kc // LANES

    w = wi_ref[...]
    for h in range(IDX_HEADS):
        wb_ref[h] = jnp.broadcast_to(w[:, TAIL_WI + h:TAIL_WI + h + 1], (tq, kc))

    def idx_body(c, carry):
        cols = pl.ds(pl.multiple_of(c * kc, kc), kc)
        d = jnp.dot(qi_ref[...].reshape(IDX_HEADS * tq, LANES), ki_ref[:, cols],
                    preferred_element_type=F32)
        sc = jnp.zeros((tq, kc), F32)
        for h in range(IDX_HEADS):
            sc = sc + jnp.maximum(d[h * tq:(h + 1) * tq, :], 0.0) * wb_ref[h]
        causal = (c * kc + lane) <= row
        keys_ref[:, cols] = jnp.where(causal, _sortable(sc), jnp.int32(INT_MIN))
        return carry
    lax.fori_loop(0, nkc, idx_body, 0)

    def fold(x, op):
        acc = x[:, 0:LANES]
        for j in range(1, nslab):
            acc = op(acc, x[:, j * LANES:(j + 1) * LANES])
        return acc

    i16_min = -(1 << (HALF_BITS - 1))

    def count16(cand, strict):
        c16 = cand.astype(I16)

        def body(c, acc):
            cols = pl.ds(pl.multiple_of(c * kc, kc), kc)
            kk = half_ref[:, cols]
            hit = (kk > c16) if strict else (kk >= c16)
            return acc + fold(jnp.where(hit, jnp.int16(1), jnp.int16(0)), jnp.add)
        acc = lax.fori_loop(0, nkc, body, jnp.zeros((tq, LANES), I16))
        return jnp.sum(acc.astype(F32), axis=1, keepdims=True)

    def kth_largest16(kf):
        zero = jnp.zeros((tq, 1), I32)
        lo = jnp.where(count16(zero, False) >= kf, zero, jnp.full((tq, 1), i16_min, I32))

        def bis_body(it, lo):
            cand = lo + jnp.left_shift(jnp.int32(1), jnp.int32(HALF_BITS - 2) - it)
            return jnp.where(count16(cand, False) >= kf, cand, lo)
        return lax.fori_loop(0, HALF_BITS - 1, bis_body, lo)

    def fill_half(fn):
        def body(c, carry):
            cols = pl.ds(pl.multiple_of(c * kc, kc), kc)
            half_ref[:, cols] = fn(keys_ref[:, cols]).astype(I16)
            return carry
        lax.fori_loop(0, nkc, body, 0)

    kf = jnp.float32(k_sel)
    fill_half(lambda kk: jnp.right_shift(kk, HALF_BITS))
    thr_hi = kth_largest16(kf)
    above = count16(thr_hi, True)
    low_mask = (1 << HALF_BITS) - 1
    fill_half(lambda kk: jnp.where(jnp.right_shift(kk, HALF_BITS) == thr_hi,
                                   (kk & low_mask) + i16_min, i16_min))
    thr_lo = kth_largest16(kf - above)
    thr = jnp.left_shift(thr_hi, HALF_BITS) + (thr_lo - i16_min)
    need = kf - above - count16(thr_lo, True)

    def fin_body(c, run):
        cols = pl.ds(pl.multiple_of(c * kc, kc), kc)
        kk = keys_ref[:, cols]
        eq = kk == thr
        eqf = jnp.where(eq, 1.0, 0.0)
        incl = jnp.dot(eqf.astype(BF16), triu_ref[...], preferred_element_type=F32)
        sel = (kk > thr) | (eq & ((run + incl) <= need))
        causal = (c * kc + lane) <= row
        bias_ref[:, cols] = jnp.where(sel & causal, 0.0, NEG_BIG).astype(BF16)
        return run + jnp.sum(eqf, axis=1, keepdims=True)
    lax.fori_loop(0, nkc, fin_body, jnp.zeros((tq, 1), F32))

    gq = ATTN_GROUP * tq

    def logits(j, kx):
        qg = q_ref[j * ATTN_GROUP:(j + 1) * ATTN_GROUP].reshape(gq, LANES)
        qx = jnp.concatenate([qg, eye_ref[...]], axis=1)
        return jnp.dot(qx, kx, preferred_element_type=F32)

    mx_ref[...] = jnp.full(mx_ref.shape, NEG_BIG, F32)

    def max_body(c, carry):
        cols = pl.ds(pl.multiple_of(c * kc, kc), kc)
        bt = bias_ref[:, cols]
        for j in range(ATTN_KV_HEADS):
            kx = jnp.concatenate([k_ref[j, :, cols], bt], axis=0)
            mx_ref[j] = jnp.maximum(mx_ref[j], fold(logits(j, kx), jnp.maximum))
        return carry
    lax.fori_loop(0, nkc, max_body, 0)
    for j in range(ATTN_KV_HEADS):
        mx_ref[j] = jnp.broadcast_to(jnp.max(mx_ref[j], axis=1, keepdims=True), (gq, LANES))

    acc_ref[...] = jnp.zeros(acc_ref.shape, F32)

    def att_body(c, carry):
        cols = pl.ds(pl.multiple_of(c * kc, kc), kc)
        bt = bias_ref[:, cols]
        for j in range(ATTN_KV_HEADS):
            kx = jnp.concatenate([k_ref[j, :, cols], bt], axis=0)
            vx = jnp.concatenate([v_ref[j, cols, :], onec_ref[...]], axis=1)
            s = logits(j, kx)
            mb = mx_ref[j]
            p = jnp.concatenate(
                [jnp.exp2(s[:, a * LANES:(a + 1) * LANES] - mb) for a in range(nslab)], axis=1)
            acc_ref[j] = acc_ref[j] + jnp.dot(p.astype(BF16), vx, preferred_element_type=F32)
        return carry
    lax.fori_loop(0, nkc, att_body, 0)

    for h in range(ATTN_HEADS):
        a = acc_ref[h // ATTN_GROUP, (h % ATTN_GROUP) * tq:(h % ATTN_GROUP + 1) * tq, :]
        out_ref[:, h * LANES:(h + 1) * LANES] = (
            a[:, 0:LANES] / a[:, LANES:LANES + 1]).astype(out_ref.dtype)


def dsa_attention(q_r, qi_r, wi, k_r, v_r, ki_r, batch, seq, k_sel):
    n = batch * seq
    nq = seq // Q_BLOCK
    kc = _pick(seq, 512)
    triu = jnp.asarray((np.arange(kc)[:, None] <= np.arange(kc)[None, :]).astype(np.float32), BF16)
    gq = ATTN_GROUP * Q_BLOCK
    eye = jnp.asarray(np.tile(np.eye(Q_BLOCK, dtype=np.float32), (ATTN_GROUP, 1)), BF16)
    onec = np.zeros((kc, LANES), np.float32)
    onec[:, 0] = 1.0
    onec = jnp.asarray(onec, BF16)
    kern = functools.partial(_dsa_kernel, k_sel=k_sel, kc=kc)
    in_specs = [
        pl.BlockSpec((ATTN_HEADS, Q_BLOCK, LANES), lambda b, i: (0, b * nq + i, 0)),
        pl.BlockSpec((IDX_HEADS, Q_BLOCK, LANES), lambda b, i: (0, b * nq + i, 0)),
        pl.BlockSpec((Q_BLOCK, LANES), lambda b, i: (b * nq + i, 0)),
        pl.BlockSpec((ATTN_KV_HEADS, LANES, seq), lambda b, i: (0, 0, b)),
        pl.BlockSpec((ATTN_KV_HEADS, seq, LANES), lambda b, i: (0, b, 0)),
        pl.BlockSpec((LANES, seq), lambda b, i: (0, b)),
        pl.BlockSpec((kc, kc), lambda b, i: (0, 0)),
        pl.BlockSpec((gq, Q_BLOCK), lambda b, i: (0, 0)),
        pl.BlockSpec((kc, LANES), lambda b, i: (0, 0)),
    ]
    scratch = [pltpu.VMEM((Q_BLOCK, seq), I32), pltpu.VMEM((Q_BLOCK, seq), I16),
               pltpu.VMEM((Q_BLOCK, seq), BF16),
               pltpu.VMEM((IDX_HEADS, Q_BLOCK, kc), F32),
               pltpu.VMEM((ATTN_KV_HEADS, gq, LANES), F32),
               pltpu.VMEM((ATTN_KV_HEADS, gq, 2 * LANES), F32)]
    return pl.pallas_call(
        kern, grid=(batch, nq), in_specs=in_specs,
        out_specs=pl.BlockSpec((Q_BLOCK, ATTN_INNER), lambda b, i: (b * nq + i, 0)),
        out_shape=jax.ShapeDtypeStruct((n, ATTN_INNER), BF16),
        scratch_shapes=scratch, compiler_params=_cparams(2), name="dsa_attention",
    )(q_r, qi_r, wi, k_r, v_r, ki_r, triu, eye, onec)


def _silu(x):
    return x * (1.0 / (1.0 + jnp.exp(-x)))


def _ssd_kernel(xbc_ref, z_ref, tail_ref, cw_ref, cb_ref, dtb_ref, alog_ref, dexp_ref, ng_ref,
                tri_ref, e_ref, out_ref, state_ref, ext_ref, y_ref):
    q = SSM_CHUNK
    hi = lax.Precision.HIGHEST

    @pl.when(pl.program_id(1) == 0)
    def _():
        state_ref[...] = jnp.zeros(state_ref.shape, F32)
        ext_ref[0:SUBLANES, :] = jnp.zeros((SUBLANES, SSM_CONV_DIM), F32)

    x = xbc_ref[...]
    ext_ref[SUBLANES:SUBLANES + q, :] = x
    conv = x * cw_ref[SSM_CONV - 1:SSM_CONV, :] + cb_ref[...]
    for k in range(1, SSM_CONV):
        conv = conv + ext_ref[SUBLANES - k:SUBLANES - k + q, :] * cw_ref[SSM_CONV - 1 - k:SSM_CONV - k, :]
    ext_ref[0:SUBLANES, :] = x[q - SUBLANES:q, :]
    xbc = _silu(conv)
    xs = xbc[:, 0:SSM_INNER]
    bm = xbc[:, SSM_INNER:SSM_INNER + SSM_GROUPS * SSM_STATE]
    cm = xbc[:, SSM_INNER + SSM_GROUPS * SSM_STATE:SSM_CONV_DIM]

    lane = lax.broadcasted_iota(I32, (q, LANES), 1)
    is_dt = (lane >= TAIL_DT) & (lane < TAIL_DT + SSM_HEADS)
    raw = tail_ref[...] + dtb_ref[...]
    dt = jnp.where(is_dt, jnp.maximum(raw, 0.0) + jnp.log1p(jnp.exp(-jnp.abs(raw))), 0.0)
    a = -jnp.exp(alog_ref[...])
    da = jnp.where(is_dt, dt * a, 0.0)
    acum = jnp.dot(tri_ref[...], da, precision=hi, preferred_element_type=F32)
    acum_t = acum.T
    acum_x = jnp.dot(acum, e_ref[...], precision=hi, preferred_element_type=F32)
    dt_x = jnp.dot(dt, e_ref[...], precision=hi, preferred_element_type=F32)
    last_x = acum_x[q - 1:q, :]
    xdt = xs * dt_x
    xdec = xdt * jnp.exp(last_x - acum_x)
    ri = lax.broadcasted_iota(I32, (q, q), 0)
    ci = lax.broadcasted_iota(I32, (q, q), 1)
    tril = ri >= ci
    pairs_per_group = SSM_GROUP_INNER // LANES

    for g in range(SSM_GROUPS):
        bg = bm[:, g * SSM_STATE:(g + 1) * SSM_STATE]
        cg = cm[:, g * SSM_STATE:(g + 1) * SSM_STATE]
        cgb = cg.astype(BF16)
        gsl = slice(g * SSM_GROUP_INNER, (g + 1) * SSM_GROUP_INNER)
        cb = lax.dot_general(cgb, bg.astype(BF16), (((1,), (1,)), ((), ())),
                             preferred_element_type=F32)
        y_off = jnp.dot(cgb, state_ref[g].astype(BF16), preferred_element_type=F32)
        y_ref[:, gsl] = y_off * jnp.exp(acum_x[:, gsl])
        for mth in range(pairs_per_group):
            slab = slice((g * pairs_per_group + mth) * LANES, (g * pairs_per_group + mth + 1) * LANES)
            xp = xdt[:, slab]
            yd = None
            for side in range(2):
                hl = TAIL_DT + (g * pairs_per_group + mth) * 2 + side
                seg = acum[:, hl:hl + 1] - acum_t[hl:hl + 1, :]
                lmat = jnp.exp(jnp.where(tril, seg, NEG_BIG))
                if side == 0:
                    xh = jnp.where(lane < SSM_HEAD_DIM, xp, 0.0)
                else:
                    xh = jnp.where(lane >= SSM_HEAD_DIM, xp, 0.0)
                part = jnp.dot((cb * lmat).astype(BF16), xh.astype(BF16), preferred_element_type=F32)
                yd = part if yd is None else yd + part
            y_ref[:, slab] = y_ref[:, slab] + yd
        st = jnp.dot(bg.T.astype(BF16), xdec[:, gsl].astype(BF16), preferred_element_type=F32)
        state_ref[g] = state_ref[g] * jnp.exp(last_x[:, gsl]) + st

    y = (y_ref[...] + dexp_ref[...] * xs) * _silu(z_ref[...])
    for g in range(SSM_GROUPS):
        gsl = slice(g * SSM_GROUP_INNER, (g + 1) * SSM_GROUP_INNER)
        yg = y[:, gsl]
        ms = jnp.mean(yg * yg, axis=-1, keepdims=True)
        out_ref[:, gsl] = (yg * lax.rsqrt(ms + EPS) * ng_ref[:, gsl]).astype(out_ref.dtype)


def ssd_mixer(proj, conv_w, conv_b, dtb_pad, alog_pad, d_exp, norm_g, batch, seq):
    n = batch * seq
    q = SSM_CHUNK
    nc = seq // q
    tri = jnp.asarray((np.arange(q)[:, None] >= np.arange(q)[None, :]).astype(np.float32))
    e = np.zeros((LANES, SSM_INNER), np.float32)
    for h in range(SSM_HEADS):
        e[TAIL_DT + h, h * SSM_HEAD_DIM:(h + 1) * SSM_HEAD_DIM] = 1.0
    e = jnp.asarray(e)

    def col(width, off):
        return pl.BlockSpec((q, width), lambda b, c: (b * nc + c, off // width))

    def full(shape):
        return pl.BlockSpec(shape, lambda b, c: (0,) * len(shape))

    in_specs = [col(SSM_CONV_DIM, COL_XBC), col(SSM_INNER, COL_Z), col(LANES, COL_TAIL),
                full((SSM_CONV, SSM_CONV_DIM)), full((1, SSM_CONV_DIM)), full((1, LANES)),
                full((1, LANES)), full((1, SSM_INNER)), full((1, SSM_INNER)),
                full((q, q)), full((LANES, SSM_INNER))]
    scratch = [pltpu.VMEM((SSM_GROUPS, SSM_STATE, SSM_GROUP_INNER), F32),
               pltpu.VMEM((SUBLANES + q, SSM_CONV_DIM), F32),
               pltpu.VMEM((q, SSM_INNER), F32)]
    return pl.pallas_call(
        _ssd_kernel, grid=(batch, nc), in_specs=in_specs,
        out_specs=pl.BlockSpec((q, SSM_INNER), lambda b, c: (b * nc + c, 0)),
        out_shape=jax.ShapeDtypeStruct((n, SSM_INNER), BF16),
        scratch_shapes=scratch, compiler_params=_cparams(2), name="ssd_mixer",
    )(proj, proj, proj, conv_w, conv_b, dtb_pad, alog_pad, d_exp, norm_g, tri, e)


def _ret_kernel(q_ref, k_ref, v_ref, g_ref, pos_ref, inv_ref, dmat_ref, qdec_ref, kdec_ref,
                cdec_ref, ng_ref, out_ref, state_ref):
    half = RET_QK_DIM // 2

    @pl.when(pl.program_id(1) == 0)
    def _():
        state_ref[...] = jnp.zeros(state_ref.shape, F32)

    ang = pos_ref[...].astype(F32) * inv_ref[...]
    cosf = jnp.cos(ang)
    sinf = jnp.sin(ang)

    def rope(x):
        x1 = x[:, 0:half]
        x2 = x[:, half:2 * half]
        return jnp.concatenate([x1 * cosf - x2 * sinf, x2 * cosf + x1 * sinf], axis=-1)

    for h in range(RET_HEADS):
        qh = rope(q_ref[:, h * RET_QK_DIM:(h + 1) * RET_QK_DIM].astype(F32))
        kh = rope(k_ref[:, h * RET_QK_DIM:(h + 1) * RET_QK_DIM].astype(F32)) * (RET_QK_DIM ** -0.5)
        vh = v_ref[:, h * RET_V_DIM:(h + 1) * RET_V_DIM]
        qb = qh.astype(BF16)
        s = lax.dot_general(qb, kh.astype(BF16), (((1,), (1,)), ((), ())),
                            preferred_element_type=F32) * dmat_ref[h]
        inner = jnp.dot(s.astype(BF16), vh, preferred_element_type=F32)
        st = state_ref[h]
        cross = jnp.dot(qb, st.astype(BF16), preferred_element_type=F32) * qdec_ref[h]
        kd = (kh * kdec_ref[h]).T.astype(BF16)
        state_ref[h] = st * cdec_ref[h] + jnp.dot(kd, vh, preferred_element_type=F32)
        o = inner + cross
        ms = jnp.mean(o * o, axis=-1, keepdims=True)
        o = o * lax.rsqrt(ms + EPS) * ng_ref[:, h * RET_V_DIM:(h + 1) * RET_V_DIM]
        gate = _silu(g_ref[:, h * RET_V_DIM:(h + 1) * RET_V_DIM].astype(F32))
        out_ref[:, h * RET_V_DIM:(h + 1) * RET_V_DIM] = (gate * o).astype(out_ref.dtype)


def retention_mixer(proj, pos, norm_g, batch, seq):
    n = batch * seq
    q = RET_CHUNK
    nc = seq // q
    half = RET_QK_DIM // 2
    inv = jnp.power(RET_THETA, -(jnp.arange(half, dtype=F32) * 2.0 / RET_QK_DIM)).reshape(1, half)
    log_gamma = jnp.log(1.0 - jnp.power(2.0, -5.0 - jnp.arange(RET_HEADS, dtype=F32)))
    i = jnp.arange(q, dtype=F32)
    diff = i[:, None] - i[None, :]
    dmat = jnp.where(diff[None] >= 0,
                     jnp.exp(jnp.maximum(diff, 0.0)[None] * log_gamma[:, None, None]), 0.0)
    qdec = jnp.exp((i + 1.0)[None, :, None] * log_gamma[:, None, None])
    kdec = jnp.exp((q - 1.0 - i)[None, :, None] * log_gamma[:, None, None])
    cdec = jnp.broadcast_to(jnp.exp(q * log_gamma)[:, None, None], (RET_HEADS, 1, RET_V_DIM))

    def col(width, off):
        return pl.BlockSpec((q, width), lambda b, c: (b * nc + c, off // width))

    def full(shape):
        return pl.BlockSpec(shape, lambda b, c: (0,) * len(shape))

    in_specs = [col(RET_QK_TOTAL, 0), col(RET_QK_TOTAL, RET_QK_TOTAL),
                col(RET_V_TOTAL, 2 * RET_QK_TOTAL), col(RET_V_TOTAL, 2 * RET_QK_TOTAL + RET_V_TOTAL),
                pl.BlockSpec((q, 1), lambda b, c: (b * nc + c, 0)),
                full((1, half)), full((RET_HEADS, q, q)), full((RET_HEADS, q, 1)),
                full((RET_HEADS, q, 1)), full((RET_HEADS, 1, RET_V_DIM)), full((1, RET_V_TOTAL))]
    return pl.pallas_call(
        _ret_kernel, grid=(batch, nc), in_specs=in_specs,
        out_specs=pl.BlockSpec((q, RET_V_TOTAL), lambda b, c: (b * nc + c, 0)),
        out_shape=jax.ShapeDtypeStruct((n, RET_V_TOTAL), BF16),
        scratch_shapes=[pltpu.VMEM((RET_HEADS, RET_QK_DIM, RET_V_DIM), F32)],
        compiler_params=_cparams(2), name="retention",
    )(proj, proj, proj, proj, pos, inv, dmat, qdec, kdec, cdec, norm_g.reshape(1, RET_V_TOTAL))


ROUTE_E0 = MOE_GROUPS
INFO_EID, INFO_GATE, INFO_RANK = 0, 2, 4


def _router_kernel(x_ref, g_ref, wr_ref, lt_ref, h_ref, info_ref, cnt_ref, run_ref):
    tb = x_ref.shape[0]

    @pl.when(pl.program_id(0) == 0)
    def _():
        run_ref[...] = jnp.zeros(run_ref.shape, F32)

    x = x_ref[...]
    ms = jnp.mean(x * x, axis=-1, keepdims=True)
    hn = x * lax.rsqrt(ms + EPS) * g_ref[...]
    h_ref[...] = hn
    logits = jnp.dot(hn, wr_ref[...], precision=lax.Precision.HIGHEST, preferred_element_type=F32)

    lane = lax.broadcasted_iota(I32, (tb, LANES), 1).astype(F32)
    far = jnp.float32(4 * LANES)
    ninf = jnp.float32(-jnp.inf)
    gl = jnp.where(lane < MOE_GROUPS, logits, ninf)
    gmax = jnp.max(gl, axis=1, keepdims=True)
    grp = jnp.min(jnp.where(gl == gmax, lane, far), axis=1, keepdims=True)
    p_grp = 1.0 / jnp.sum(jnp.exp(gl - gmax), axis=1, keepdims=True)
    lo = ROUTE_E0 + grp * MOE_EXPERTS_PER_GROUP
    el = jnp.where((lane >= lo) & (lane < lo + MOE_EXPERTS_PER_GROUP), logits, ninf)
    m1 = jnp.max(el, axis=1, keepdims=True)
    i1 = jnp.min(jnp.where(el == m1, lane, far), axis=1, keepdims=True)
    el2 = jnp.where(lane == i1, ninf, el)
    m2 = jnp.max(el2, axis=1, keepdims=True)
    i2 = jnp.min(jnp.where(el2 == m2, lane, far), axis=1, keepdims=True)
    e2 = jnp.exp(m2 - m1)
    g1 = p_grp / (1.0 + e2)
    g2 = p_grp * e2 / (1.0 + e2)
    oh1 = jnp.where(lane == i1, 1.0, 0.0)
    oh2 = jnp.where(lane == i2, 1.0, 0.0)
    cnt = oh1 + oh2
    before = jnp.dot(lt_ref[...], cnt.astype(BF16), preferred_element_type=F32) + run_ref[...]
    r1 = jnp.sum(oh1 * before, axis=1, keepdims=True)
    r2 = jnp.sum(oh2 * before, axis=1, keepdims=True)
    run_ref[...] = run_ref[...] + jnp.sum(cnt, axis=0, keepdims=True)
    cnt_ref[...] = run_ref[...]

    info = jnp.where(lane == INFO_EID, i1 - ROUTE_E0, 0.0)
    info = jnp.where(lane == INFO_EID + 1, i2 - ROUTE_E0, info)
    info = jnp.where(lane == INFO_GATE, g1, info)
    info = jnp.where(lane == INFO_GATE + 1, g2, info)
    info = jnp.where(lane == INFO_RANK, r1, info)
    info = jnp.where(lane == INFO_RANK + 1, r2, info)
    info_ref[...] = info


def moe_router(x, g, w_router_pad):
    n, d = x.shape
    tb = _pick(n, 512)
    lt = jnp.asarray((np.arange(tb)[:, None] > np.arange(tb)[None, :]).astype(np.float32), BF16)
    return pl.pallas_call(
        _router_kernel, grid=(n // tb,),
        in_specs=[pl.BlockSpec((tb, d), lambda i: (i, 0)), pl.BlockSpec((1, d), lambda i: (0, 0)),
                  pl.BlockSpec((d, LANES), lambda i: (0, 0)), pl.BlockSpec((tb, tb), lambda i: (0, 0))],
        out_specs=[pl.BlockSpec((tb, d), lambda i: (i, 0)), pl.BlockSpec((tb, LANES), lambda i: (i, 0)),
                   pl.BlockSpec((1, LANES), lambda i: (0, 0))],
        out_shape=[jax.ShapeDtypeStruct((n, d), F32), jax.ShapeDtypeStruct((n, LANES), F32),
                   jax.ShapeDtypeStruct((1, LANES), F32)],
        scratch_shapes=[pltpu.VMEM((1, LANES), F32)],
        compiler_params=_cparams(1), name="moe_router",
    )(x, g.reshape(1, d), w_router_pad, lt)


def _row_copy(src_ref, src_row, dst_ref, dst_row, sem):
    return pltpu.make_async_copy(src_ref.at[pl.ds(src_row, 1), :], dst_ref.at[pl.ds(dst_row, 1), :], sem)


def _expert_kernel(be_ref, nused_ref, st_ref, h_hbm, wg_ref, wu_ref, wd_ref, out_ref,
                   xbuf, sem, wgu_s, wd_s, *, nblk):
    i = pl.program_id(0)
    slot = i % 2
    nused = nused_ref[0]
    d = xbuf.shape[2]
    ff = EXPERT_FF
    nxt = jnp.minimum(i + 1, nblk - 1)

    def gather(blk, s, unroll):
        def body(r, carry):
            _row_copy(h_hbm, st_ref[blk * MOE_BLOCK + r], xbuf.at[s], r, sem.at[s]).start()
            return carry
        lax.fori_loop(0, MOE_BLOCK, body, 0, unroll=unroll)

    def drain(s):
        def body(r, carry):
            _row_copy(h_hbm, 0, xbuf.at[s], 0, sem.at[s]).wait()
            return carry
        lax.fori_loop(0, MOE_BLOCK, body, 0, unroll=8)

    @pl.when(i == 0)
    def _():
        gather(0, 0, 8)

    drain(slot)

    @pl.when(i < nused)
    def _():
        changed = jnp.logical_or(i == 0, be_ref[i] != be_ref[jnp.maximum(i - 1, 0)])

        @pl.when(changed)
        def _():
            rc = _pick(d, 256)

            def body(r, carry):
                rows = pl.ds(pl.multiple_of(r * rc, rc), rc)
                wgu_s[rows, 0:ff] = wg_ref[0, 0, rows, :].astype(BF16)
                wgu_s[rows, ff:2 * ff] = wu_ref[0, 0, rows, :].astype(BF16)
                return carry
            lax.fori_loop(0, d // rc, body, 0)
            wd_s[...] = wd_ref[0, 0].astype(BF16)

        gather(nxt, 1 - slot, True)
        gu = jnp.dot(xbuf[slot].astype(BF16), wgu_s[...], preferred_element_type=F32)
        hid = _silu(gu[:, 0:ff]) * gu[:, ff:2 * ff]
        out_ref[...] = jnp.dot(hid.astype(BF16), wd_s[...], preferred_element_type=F32)

    @pl.when(i >= nused)
    def _():
        gather(nxt, 1 - slot, 8)
        out_ref[...] = jnp.zeros(out_ref.shape, out_ref.dtype)

    @pl.when(i == nblk - 1)
    def _():
        drain(1 - slot)


def moe_experts(h, slot_tok, blk_expert, nused, w_gate, w_up, w_down, layer):
    n, d = h.shape
    p = slot_tok.shape[0]
    nblk = p // MOE_BLOCK
    ff = EXPERT_FF
    grid_spec = pltpu.PrefetchScalarGridSpec(
        num_scalar_prefetch=3, grid=(nblk,),
        in_specs=[pl.BlockSpec(memory_space=pl.ANY),
                  pl.BlockSpec((1, 1, d, ff), lambda i, be, nu, st: (layer, be[i], 0, 0)),
                  pl.BlockSpec((1, 1, d, ff), lambda i, be, nu, st: (layer, be[i], 0, 0)),
                  pl.BlockSpec((1, 1, ff, d), lambda i, be, nu, st: (layer, be[i], 0, 0))],
        out_specs=pl.BlockSpec((MOE_BLOCK, d), lambda i, be, nu, st: (i, 0)),
        scratch_shapes=[pltpu.VMEM((2, MOE_BLOCK, d), h.dtype), pltpu.SemaphoreType.DMA((2,)),
                        pltpu.VMEM((d, 2 * ff), BF16), pltpu.VMEM((ff, d), BF16)])
    return pl.pallas_call(
        functools.partial(_expert_kernel, nblk=nblk), grid_spec=grid_spec,
        out_shape=jax.ShapeDtypeStruct((p, d), F32),
        compiler_params=_cparams(1), name="moe_experts",
    )(blk_expert, nused, slot_tok, h, w_gate, w_up, w_down)


def _slots_kernel(dest_ref, slot_ref):
    def clear(s, carry):
        slot_ref[s] = 0
        return carry
    lax.fori_loop(0, slot_ref.shape[0], clear, 0, unroll=8)

    def body(a, carry):
        slot_ref[dest_ref[a]] = lax.div(a, jnp.int32(MOE_TOPK))
        return carry
    lax.fori_loop(0, dest_ref.shape[0], body, 0, unroll=8)


def moe_slots(dest_flat, p):
    return pl.pallas_call(
        _slots_kernel,
        in_specs=[pl.BlockSpec(memory_space=pltpu.SMEM)],
        out_specs=pl.BlockSpec(memory_space=pltpu.SMEM),
        out_shape=jax.ShapeDtypeStruct((p,), I32),
        name="moe_slots",
    )(dest_flat)


def _combine_kernel(dcur_ref, dnxt_ref, info_ref, x_ref, y_hbm, out_ref, ybuf, sem, *, tb, nsteps):
    i = pl.program_id(0)
    slot = i % 2

    def issue(dref, s):
        def body(r, carry):
            for k in range(MOE_TOPK):
                _row_copy(y_hbm, dref[0, 0, MOE_TOPK * r + k], ybuf.at[s, k], r, sem.at[s]).start()
            return carry
        lax.fori_loop(0, tb, body, 0, unroll=8)

    @pl.when(i == 0)
    def _():
        issue(dcur_ref, 0)

    @pl.when(i + 1 < nsteps)
    def _():
        issue(dnxt_ref, 1 - slot)

    def drain(r, carry):
        for k in range(MOE_TOPK):
            _row_copy(y_hbm, 0, ybuf.at[slot, k], 0, sem.at[slot]).wait()
        return carry
    lax.fori_loop(0, tb, drain, 0, unroll=8)

    info = info_ref[...]
    out = x_ref[...]
    for k in range(MOE_TOPK):
        out = out + ybuf[slot, k] * info[:, INFO_GATE + k:INFO_GATE + k + 1]
    out_ref[...] = out


def moe_combine(x, y, dest, info):
    n, d = x.shape
    tb = _pick(n, MOE_BLOCK)
    nsteps = n // tb
    kern = functools.partial(_combine_kernel, tb=tb, nsteps=nsteps)
    dest3 = dest.reshape(nsteps, 1, MOE_TOPK * tb)
    return pl.pallas_call(
        kern, grid=(nsteps,),
        in_specs=[pl.BlockSpec((1, 1, MOE_TOPK * tb), lambda i: (i, 0, 0), memory_space=pltpu.SMEM),
                  pl.BlockSpec((1, 1, MOE_TOPK * tb), lambda i: (jnp.minimum(i + 1, nsteps - 1), 0, 0),
                               memory_space=pltpu.SMEM),
                  pl.BlockSpec((tb, LANES), lambda i: (i, 0)),
                  pl.BlockSpec((tb, d), lambda i: (i, 0)),
                  pl.BlockSpec(memory_space=pl.ANY)],
        out_specs=pl.BlockSpec((tb, d), lambda i: (i, 0)),
        out_shape=jax.ShapeDtypeStruct((n, d), x.dtype),
        scratch_shapes=[pltpu.VMEM((2, MOE_TOPK, tb, d), y.dtype), pltpu.SemaphoreType.DMA((2,))],
        compiler_params=_cparams(1), name="moe_combine",
    )(dest3, dest3, info, x, y)


def hier_moe_layer(x, norm_g, w_rg, w_re, w_gate, w_up, w_down, layer):
    n, d = x.shape
    w_router = jnp.concatenate(
        [w_rg, w_re, jnp.zeros((d, LANES - MOE_GROUPS - N_EXPERTS), F32)], axis=1)
    h, info, cnt = moe_router(x, norm_g, w_router)
    counts = cnt[0, ROUTE_E0:ROUTE_E0 + N_EXPERTS].astype(I32)
    eid = info[:, INFO_EID:INFO_EID + MOE_TOPK].astype(I32)
    rank = info[:, INFO_RANK:INFO_RANK + MOE_TOPK].astype(I32)
    padded = (counts + MOE_BLOCK - 1) // MOE_BLOCK * MOE_BLOCK
    pad_end = jnp.cumsum(padded)
    pad_start = pad_end - padded
    dest = pad_start[eid] + rank
    a = n * MOE_TOPK
    p = (-(-a // MOE_BLOCK) + N_EXPERTS) * MOE_BLOCK
    nblk = p // MOE_BLOCK
    blk_start = jnp.arange(nblk, dtype=I32) * MOE_BLOCK
    blk_expert = jnp.minimum(jnp.sum((pad_end[None, :] <= blk_start[:, None]).astype(I32), axis=1),
                             N_EXPERTS - 1).astype(I32)
    nused = (pad_end[-1:] // MOE_BLOCK).astype(I32)
    slot_tok = moe_slots(dest.reshape(-1), p)
    y = moe_experts(h, slot_tok, blk_expert, nused, w_gate, w_up, w_down, layer)
    return moe_combine(x, y, dest, info)


def _pad_lanes(v, offset, width=LANES):
    out = jnp.zeros((1, width), F32)
    return out.at[0, offset:offset + v.shape[0]].set(v.astype(F32))


def hybrid_layer(x, pos, norm_g, w_in, q_norm, k_norm, kidx_norm, conv_w, conv_b, dt_bias, a_log,
                 d_skip, ssm_norm, w_out, batch, seq, k_sel):
    n, d = x.shape
    offs = np.cumsum([0, ATTN_INNER, ATTN_KV, ATTN_KV, IDX_INNER, IDX_HEAD_DIM, IDX_HEADS,
                      SSM_INNER, SSM_CONV_DIM, SSM_HEADS])
    seg = {name: w_in[:, offs[j]:offs[j + 1]] for j, name in
           enumerate(["q", "k", "v", "qi", "ki", "wi", "z", "xbc", "dt"])}
    tail_pad = LANES - IDX_HEAD_DIM - IDX_HEADS - SSM_HEADS
    w_perm = jnp.concatenate(
        [seg["xbc"], seg["k"], seg["v"], seg["z"], seg["q"], seg["qi"], seg["ki"], seg["wi"],
         seg["dt"], jnp.zeros((d, tail_pad + HYB_COLS - COL_TAIL - LANES), F32)], axis=1).astype(BF16)
    proj = fused_matmul([x], w_perm, g=norm_g, out_dtype=F32, name="hyb_in_proj")

    rot = ATTN_HEAD_DIM // ROPE_FRACTION
    rot_i = IDX_HEAD_DIM // ROPE_FRACTION
    inv16 = jnp.power(ROPE_THETA, -(jnp.arange(rot // 2, dtype=F32) * 2.0 / rot))
    inv8 = jnp.power(ROPE_THETA, -(jnp.arange(rot_i // 2, dtype=F32) * 2.0 / rot_i))
    inv_a = jnp.concatenate([inv16, inv16, jnp.zeros((LANES - rot,), F32)]).reshape(1, LANES)
    inv_h = jnp.concatenate([inv8, inv8, jnp.zeros((IDX_HEAD_DIM - rot_i,), F32)])
    inv_i = jnp.tile(inv_h, IDX_HEADS).reshape(1, IDX_INNER)
    inv_t = jnp.concatenate([inv_h, jnp.zeros((LANES - IDX_HEAD_DIM,), F32)]).reshape(1, LANES)
    q_r, k_r, v_r, qi_r, ki_r, wi = hyb_prep(
        proj, pos, inv_a, inv_i, inv_t, q_norm.reshape(1, LANES), k_norm.reshape(1, LANES),
        _pad_lanes(kidx_norm, TAIL_KI))
    attn = dsa_attention(q_r, qi_r, wi, k_r, v_r, ki_r, batch, seq, k_sel)

    ssm = ssd_mixer(proj, conv_w, conv_b.reshape(1, SSM_CONV_DIM), _pad_lanes(dt_bias, TAIL_DT),
                    _pad_lanes(a_log, TAIL_DT), jnp.repeat(d_skip, SSM_HEAD_DIM).reshape(1, SSM_INNER),
                    ssm_norm.reshape(1, SSM_INNER), batch, seq)
    return fused_matmul([attn, ssm], w_out.astype(BF16), res=x, out_dtype=F32, name="hyb_out_proj")


def retention_layer(x, pos, norm_g, w_in, ret_norm, w_out, batch, seq):
    proj = fused_matmul([x], w_in.astype(BF16), g=norm_g, out_dtype=BF16, name="ret_in_proj")
    o = retention_mixer(proj, pos, ret_norm, batch, seq)
    return fused_matmul([o], w_out.astype(BF16), res=x, out_dtype=F32, name="ret_out_proj")


def kernel(x, positions, mix_norm, ffn_norm, hyb_w_in, attn_q_norm, attn_k_norm, idx_k_norm, ssm_conv_w, ssm_conv_b, ssm_dt_bias, ssm_a_log, ssm_d, ssm_norm, hyb_w_out, ret_w_in, ret_norm, ret_w_out, moe_router_group, moe_router_expert, moe_w_gate, moe_w_up, moe_w_down):
    batch, seq, d = x.shape
    depth = mix_norm.shape[0]
    k_sel = min(TOPK_MAX, seq // 4)
    n = batch * seq
    xf = x.reshape(n, d)
    pos = positions.reshape(n, 1).astype(I32)
    for layer in range(depth):
        i = layer // 2
        if layer % 2 == 0:
            xf = hybrid_layer(xf, pos, mix_norm[layer], hyb_w_in[i], attn_q_norm[i], attn_k_norm[i],
                              idx_k_norm[i], ssm_conv_w[i], ssm_conv_b[i], ssm_dt_bias[i],
                              ssm_a_log[i], ssm_d[i], ssm_norm[i], hyb_w_out[i], batch, seq, k_sel)
        else:
            xf = retention_layer(xf, pos, mix_norm[layer], ret_w_in[i], ret_norm[i], ret_w_out[i],
                                 batch, seq)
        xf = hier_moe_layer(xf, ffn_norm[layer], moe_router_group[layer], moe_router_expert[layer],
                            moe_w_gate, moe_w_up, moe_w_down, layer)
    return xf.reshape(batch, seq, d)
```

```python
import functools

import jax
import jax.numpy as jnp
import numpy as np
from jax import lax
from jax.experimental import pallas as pl
from jax.experimental.pallas import tpu as pltpu

F32 = jnp.float32
BF16 = jnp.bfloat16
I32 = jnp.int32
I16 = jnp.int16
HALF_BITS = 16
COUNT_GROUP = 4

ATTN_HEADS = 8
ATTN_KV_HEADS = 2
ATTN_GROUP = ATTN_HEADS // ATTN_KV_HEADS
ATTN_HEAD_DIM = 128
ATTN_INNER = ATTN_HEADS * ATTN_HEAD_DIM
ATTN_KV = ATTN_KV_HEADS * ATTN_HEAD_DIM
IDX_HEADS = 8
IDX_HEAD_DIM = 64
IDX_INNER = IDX_HEADS * IDX_HEAD_DIM
TOPK_MAX = 256
Q_BLOCK = 128
ROPE_THETA = 500000.0
ROPE_FRACTION = 4

SSM_HEADS = 16
SSM_HEAD_DIM = 64
SSM_INNER = SSM_HEADS * SSM_HEAD_DIM
SSM_GROUPS = 2
SSM_STATE = 128
SSM_CONV = 4
SSM_CONV_DIM = SSM_INNER + 2 * SSM_GROUPS * SSM_STATE
SSM_CHUNK = 128
SSM_GROUP_INNER = SSM_INNER // SSM_GROUPS

RET_HEADS = 8
RET_QK_DIM = 256
RET_V_DIM = 512
RET_QK_TOTAL = RET_HEADS * RET_QK_DIM
RET_V_TOTAL = RET_HEADS * RET_V_DIM
RET_CHUNK = 128
RET_THETA = 10000.0

MOE_GROUPS = 8
MOE_EXPERTS_PER_GROUP = 8
N_EXPERTS = MOE_GROUPS * MOE_EXPERTS_PER_GROUP
MOE_TOPK = 2
EXPERT_FF = 512
MOE_BLOCK = 128

EPS = 1e-6

LANES = 128
SUBLANES = 8
VMEM_LIMIT = 52 * 1024 * 1024

LOG2E = 1.4426950408889634
NEG_BIG = -1e30
INT_MIN = -2147483648

COL_XBC = 0
COL_K = SSM_CONV_DIM
COL_V = COL_K + ATTN_KV
COL_Z = COL_V + ATTN_KV
COL_Q = COL_Z + SSM_INNER
COL_QI = COL_Q + ATTN_INNER
COL_TAIL = COL_QI + IDX_INNER
HYB_COLS = 5120
TAIL_KI = 0
TAIL_WI = IDX_HEAD_DIM
TAIL_DT = TAIL_WI + IDX_HEADS


def _cparams(n_axes):
    return pltpu.CompilerParams(dimension_semantics=("arbitrary",) * n_axes,
                                vmem_limit_bytes=VMEM_LIMIT)


def _pick(n, pref):
    t = min(n, pref)
    while n % t:
        t //= 2
    return t


def _mm_kernel(*refs, n_a, has_norm, has_res, row_chunk):
    a_refs = refs[:n_a]
    pos = n_a
    g_ref = None
    if has_norm:
        g_ref = refs[pos]
        pos += 1
    w_refs = refs[pos:pos + n_a]
    pos += n_a
    res_ref = None
    if has_res:
        res_ref = refs[pos]
        pos += 1
    out_ref = refs[pos]
    pos += 1
    xn_ref = refs[pos] if has_norm else None
    tm = out_ref.shape[0]

    if has_norm:
        @pl.when(pl.program_id(1) == 0)
        def _():
            def body(r, carry):
                rows = pl.ds(pl.multiple_of(r * row_chunk, row_chunk), row_chunk)
                x = a_refs[0][rows, :]
                ms = jnp.mean(x * x, axis=-1, keepdims=True)
                xn_ref[rows, :] = (x * lax.rsqrt(ms + EPS) * g_ref[...]).astype(BF16)
                return carry
            lax.fori_loop(0, tm // row_chunk, body, 0)
        acc = jnp.dot(xn_ref[...], w_refs[0][...], preferred_element_type=F32)
    else:
        acc = jnp.dot(a_refs[0][...], w_refs[0][...], preferred_element_type=F32)
        for p in range(1, n_a):
            acc = acc + jnp.dot(a_refs[p][...], w_refs[p][...], preferred_element_type=F32)
    if has_res:
        acc = acc + res_ref[...]
    out_ref[...] = acc.astype(out_ref.dtype)


def fused_matmul(a_list, w, *, g=None, res=None, out_dtype=F32, tm=1024, tn=512, name="mm"):
    n = a_list[0].shape[0]
    kp = a_list[0].shape[1]
    m = w.shape[1]
    tm = _pick(n, tm)
    tn = _pick(m, tn)
    n_a = len(a_list)
    has_norm = g is not None
    has_res = res is not None
    in_specs = [pl.BlockSpec((tm, kp), lambda i, j: (i, 0)) for _ in a_list]
    args = list(a_list)
    if has_norm:
        in_specs.append(pl.BlockSpec((1, kp), lambda i, j: (0, 0)))
        args.append(g.reshape(1, kp).astype(F32))
    for p in range(n_a):
        in_specs.append(pl.BlockSpec((kp, tn), lambda i, j, p=p: (p, j)))
        args.append(w)
    if has_res:
        in_specs.append(pl.BlockSpec((tm, tn), lambda i, j: (i, j)))
        args.append(res)
    scratch = [pltpu.VMEM((tm, kp), BF16)] if has_norm else []
    kern = functools.partial(_mm_kernel, n_a=n_a, has_norm=has_norm, has_res=has_res,
                             row_chunk=_pick(tm, 128))
    return pl.pallas_call(
        kern,
        grid=(n // tm, m // tn),
        in_specs=in_specs,
        out_specs=pl.BlockSpec((tm, tn), lambda i, j: (i, j)),
        out_shape=jax.ShapeDtypeStruct((n, m), out_dtype),
        scratch_shapes=scratch,
        compiler_params=_cparams(2),
        name=name,
    )(*args)


def _rope_lanes(x, cosf, s_neg, s_pos, half):
    width = x.shape[-1]
    return (x * cosf + pltpu.roll(x, width - half, axis=1) * s_neg
            + pltpu.roll(x, half, axis=1) * s_pos)


def _prep_kernel(q_ref, k_ref, v_ref, qi_ref, tail_ref, pos_ref, invc_ref, pcos_ref, pneg_ref, ppos_ref,
                 qn_ref, kn_ref, kin_ref,
                 qo_ref, ko_ref, vo_ref, qio_ref, kio_ref, wio_ref):
    tb = q_ref.shape[0]
    posf = pos_ref[...].astype(F32)
    lane = lax.broadcasted_iota(I32, (tb, LANES), 1)

    hi = lax.Precision.HIGHEST
    ang = posf * invc_ref[...]
    cos_c = jnp.cos(ang)
    sin_c = jnp.sin(ang)
    cos_all = jnp.dot(cos_c, pcos_ref[...], precision=hi, preferred_element_type=F32)
    sneg_all = jnp.dot(sin_c, pneg_ref[...], precision=hi, preferred_element_type=F32)
    spos_all = jnp.dot(sin_c, ppos_ref[...], precision=hi, preferred_element_type=F32)

    half = ATTN_HEAD_DIM // ROPE_FRACTION // 2
    cosf = cos_all[:, 0:LANES]
    s_neg = sneg_all[:, 0:LANES]
    s_pos = spos_all[:, 0:LANES]
    scale = ATTN_HEAD_DIM ** -0.5 * LOG2E
    for h in range(ATTN_HEADS):
        x = q_ref[:, h * LANES:(h + 1) * LANES]
        ms = jnp.mean(x * x, axis=-1, keepdims=True)
        x = x * lax.rsqrt(ms + EPS) * qn_ref[...]
        x = _rope_lanes(x, cosf, s_neg, s_pos, half)
        qo_ref[h] = (x * scale).astype(BF16)
    for h in range(ATTN_KV_HEADS):
        x = k_ref[:, h * LANES:(h + 1) * LANES]
        ms = jnp.mean(x * x, axis=-1, keepdims=True)
        x = x * lax.rsqrt(ms + EPS) * kn_ref[...]
        x = _rope_lanes(x, cosf, s_neg, s_pos, half)
        ko_ref[h] = x.T.astype(BF16)
        vo_ref[h] = v_ref[:, h * LANES:(h + 1) * LANES].astype(BF16)

    half_i = IDX_HEAD_DIM // ROPE_FRACTION // 2
    cos_i = cos_all[:, LANES:LANES + IDX_INNER]
    sn_i = sneg_all[:, LANES:LANES + IDX_INNER]
    sp_i = spos_all[:, LANES:LANES + IDX_INNER]
    qi = _rope_lanes(qi_ref[...], cos_i, sn_i, sp_i, half_i)
    for h in range(IDX_HEADS):
        slab = qi[:, (h // 2) * LANES:(h // 2 + 1) * LANES]
        if h % 2:
            slab = pltpu.roll(slab, IDX_HEAD_DIM, axis=1)
        qio_ref[h] = jnp.where(lane < IDX_HEAD_DIM, slab, 0.0).astype(BF16)

    t = tail_ref[...]
    is_ki = lane < IDX_HEAD_DIM
    ms = jnp.sum(jnp.where(is_ki, t * t, 0.0), axis=-1, keepdims=True) * (1.0 / IDX_HEAD_DIM)
    kin = t * lax.rsqrt(ms + EPS) * kin_ref[...]
    kin = _rope_lanes(kin, cos_i[:, 0:LANES], sn_i[:, 0:LANES], sp_i[:, 0:LANES], half_i)
    kio_ref[...] = jnp.where(is_ki, kin, 0.0).T.astype(BF16)
    wio_ref[...] = t * (IDX_HEADS ** -0.5 * IDX_HEAD_DIM ** -0.5)


def _rope_selectors():
    a_half = ATTN_HEAD_DIM // ROPE_FRACTION // 2
    i_half = IDX_HEAD_DIM // ROPE_FRACTION // 2
    unit = LANES - 1
    width = LANES + IDX_INNER
    pcos = np.zeros((LANES, width), np.float32)
    pneg = np.zeros((LANES, width), np.float32)
    ppos = np.zeros((LANES, width), np.float32)
    for m in range(LANES):
        if m < 2 * a_half:
            pcos[m % a_half, m] = 1.0
            (pneg if m < a_half else ppos)[m % a_half, m] = -1.0 if m < a_half else 1.0
        else:
            pcos[unit, m] = 1.0
    for m in range(IDX_INNER):
        r = m % IDX_HEAD_DIM
        if r < 2 * i_half:
            pcos[a_half + r % i_half, LANES + m] = 1.0
            (pneg if r < i_half else ppos)[a_half + r % i_half, LANES + m] = -1.0 if r < i_half else 1.0
        else:
            pcos[unit, LANES + m] = 1.0
    return jnp.asarray(pcos), jnp.asarray(pneg), jnp.asarray(ppos)


def hyb_prep(proj, pos, inv_c, q_norm, k_norm, kidx_norm_pad):
    n = proj.shape[0]
    tb = _pick(n, 256)
    pcos, pneg, ppos = _rope_selectors()
    sel_w = LANES + IDX_INNER

    def col(width, off):
        return pl.BlockSpec((tb, width), lambda i: (i, off // width))

    def full(shape):
        return pl.BlockSpec(shape, lambda i: (0,) * len(shape))

    in_specs = [col(ATTN_INNER, COL_Q), col(ATTN_KV, COL_K), col(ATTN_KV, COL_V),
                col(IDX_INNER, COL_QI), col(LANES, COL_TAIL),
                pl.BlockSpec((tb, 1), lambda i: (i, 0)),
                full((1, LANES)), full((LANES, sel_w)), full((LANES, sel_w)), full((LANES, sel_w)),
                full((1, LANES)), full((1, LANES)), full((1, LANES))]
    out_shape = [jax.ShapeDtypeStruct((ATTN_HEADS, n, LANES), BF16),
                 jax.ShapeDtypeStruct((ATTN_KV_HEADS, LANES, n), BF16),
                 jax.ShapeDtypeStruct((ATTN_KV_HEADS, n, LANES), BF16),
                 jax.ShapeDtypeStruct((IDX_HEADS, n, LANES), BF16),
                 jax.ShapeDtypeStruct((LANES, n), BF16),
                 jax.ShapeDtypeStruct((n, LANES), F32)]
    out_specs = [pl.BlockSpec((ATTN_HEADS, tb, LANES), lambda i: (0, i, 0)),
                 pl.BlockSpec((ATTN_KV_HEADS, LANES, tb), lambda i: (0, 0, i)),
                 pl.BlockSpec((ATTN_KV_HEADS, tb, LANES), lambda i: (0, i, 0)),
                 pl.BlockSpec((IDX_HEADS, tb, LANES), lambda i: (0, i, 0)),
                 pl.BlockSpec((LANES, tb), lambda i: (0, i)),
                 pl.BlockSpec((tb, LANES), lambda i: (i, 0))]
    return pl.pallas_call(
        _prep_kernel, grid=(n // tb,), in_specs=in_specs, out_specs=out_specs,
        out_shape=out_shape, compiler_params=_cparams(1), name="hyb_prep",
    )(proj, proj, proj, proj, proj, pos, inv_c, pcos, pneg, ppos, q_norm, k_norm, kidx_norm_pad)


def _sortable(x):
    b = pltpu.bitcast(x, I32)
    return jnp.where(b < 0, b ^ jnp.int32(0x7FFFFFFF), b)


def _dsa_kernel(q_ref, qi_ref, wi_ref, k_ref, v_ref, ki_ref, triu_ref, eye_ref, onec_ref, out_ref,
                keys_ref, half_ref, bias_ref, wb_ref, mx_ref, acc_ref, *, k_sel, kc):
    i = pl.program_id(1)
    tq = Q_BLOCK
    nkc = (i * tq + tq + kc - 1) // kc
    row = i * tq + lax.broadcasted_iota(I32, (tq, kc), 0)
    lane = lax.broadcasted_iota(I32, (tq, kc), 1)
    nslab = kc // LANES

    w = wi_ref[...]
    for h in range(IDX_HEADS):
        wb_ref[h] = jnp.broadcast_to(w[:, TAIL_WI + h:TAIL_WI + h + 1], (tq, kc))

    def idx_body(c, carry):
        cols = pl.ds(pl.multiple_of(c * kc, kc), kc)
        d = jnp.dot(qi_ref[...].reshape(IDX_HEADS * tq, LANES), ki_ref[:, cols],
                    preferred_element_type=F32)
        sc = jnp.zeros((tq, kc), F32)
        for h in range(IDX_HEADS):
            sc = sc + jnp.maximum(d[h * tq:(h + 1) * tq, :], 0.0) * wb_ref[h]
        causal = (c * kc + lane) <= row
        keys_ref[:, cols] = jnp.where(causal, _sortable(sc), jnp.int32(INT_MIN))
        return carry
    lax.fori_loop(0, nkc, idx_body, 0)

    def fold(x, op):
        acc = x[:, 0:LANES]
        for j in range(1, nslab):
            acc = op(acc, x[:, j * LANES:(j + 1) * LANES])
        return acc

    i16_min = -(1 << (HALF_BITS - 1))

    def count16(cand, strict):
        c16 = cand.astype(I16)

        def hits(start, width):
            kk = half_ref[:, pl.ds(pl.multiple_of(start, kc), width)]
            hit = (kk > c16) if strict else (kk >= c16)
            x = jnp.where(hit, jnp.int16(1), jnp.int16(0))
            acc = x[:, 0:LANES]
            for j in range(1, width // LANES):
                acc = acc + x[:, j * LANES:(j + 1) * LANES]
            return acc

        ngrp = nkc // COUNT_GROUP
        acc = lax.fori_loop(0, ngrp, lambda g, a: a + hits(g * (COUNT_GROUP * kc), COUNT_GROUP * kc),
                            jnp.zeros((tq, LANES), I16))
        acc = lax.fori_loop(ngrp * COUNT_GROUP, nkc, lambda c, a: a + hits(c * kc, kc), acc)
        return jnp.sum(acc.astype(F32), axis=1, keepdims=True)

    def kth_largest16(kf):
        zero = jnp.zeros((tq, 1), I32)
        lo = jnp.where(count16(zero, False) >= kf, zero, jnp.full((tq, 1), i16_min, I32))

        def bis_body(it, lo):
            cand = lo + jnp.left_shift(jnp.int32(1), jnp.int32(HALF_BITS - 2) - it)
            return jnp.where(count16(cand, False) >= kf, cand, lo)
        return lax.fori_loop(0, HALF_BITS - 1, bis_body, lo)

    def fill_half(fn):
        def body(c, carry):
            cols = pl.ds(pl.multiple_of(c * kc, kc), kc)
            half_ref[:, cols] = fn(keys_ref[:, cols]).astype(I16)
            return carry
        lax.fori_loop(0, nkc, body, 0)

    kf = jnp.float32(k_sel)
    fill_half(lambda kk: jnp.right_shift(kk, HALF_BITS))
    thr_hi = kth_largest16(kf)
    above = count16(thr_hi, True)
    low_mask = (1 << HALF_BITS) - 1
    fill_half(lambda kk: jnp.where(jnp.right_shift(kk, HALF_BITS) == thr_hi,
                                   (kk & low_mask) + i16_min, i16_min))
    thr_lo = kth_largest16(kf - above)
    thr = jnp.left_shift(thr_hi, HALF_BITS) + (thr_lo - i16_min)
    cnt_gt_lo = count16(thr_lo, True)
    need = kf - above - cnt_gt_lo
    n_eq = count16(thr_lo, False) - cnt_gt_lo
    row_plain = jnp.logical_or(jnp.logical_and(n_eq == need, thr_lo > i16_min),
                               thr == jnp.int32(INT_MIN))
    all_plain = jnp.min(jnp.where(row_plain, 1.0, 0.0)) > 0.5

    @pl.when(all_plain)
    def _():
        def fin_body(c, carry):
            cols = pl.ds(pl.multiple_of(c * kc, kc), kc)
            causal = (c * kc + lane) <= row
            sel = (keys_ref[:, cols] >= thr) & causal
            bias_ref[:, cols] = jnp.where(sel, 0.0, NEG_BIG).astype(BF16)
            return carry
        lax.fori_loop(0, nkc, fin_body, 0)

    @pl.when(jnp.logical_not(all_plain))
    def _():
        def fin_body(c, run):
            cols = pl.ds(pl.multiple_of(c * kc, kc), kc)
            kk = keys_ref[:, cols]
            eq = kk == thr
            eqf = jnp.where(eq, 1.0, 0.0)
            incl = jnp.dot(eqf.astype(BF16), triu_ref[...], preferred_element_type=F32)
            sel = (kk > thr) | (eq & ((run + incl) <= need))
            causal = (c * kc + lane) <= row
            bias_ref[:, cols] = jnp.where(sel & causal, 0.0, NEG_BIG).astype(BF16)
            return run + jnp.sum(eqf, axis=1, keepdims=True)
        lax.fori_loop(0, nkc, fin_body, jnp.zeros((tq, 1), F32))

    gq = ATTN_GROUP * tq

    def logits(j, kx):
        qg = q_ref[j * ATTN_GROUP:(j + 1) * ATTN_GROUP].reshape(gq, LANES)
        qx = jnp.concatenate([qg, eye_ref[...]], axis=1)
        return jnp.dot(qx, kx, preferred_element_type=F32)

    mx_ref[...] = jnp.full(mx_ref.shape, NEG_BIG, F32)

    def max_body(c, carry):
        cols = pl.ds(pl.multiple_of(c * kc, kc), kc)
        bt = bias_ref[:, cols]
        for j in range(ATTN_KV_HEADS):
            kx = jnp.concatenate([k_ref[j, :, cols], bt], axis=0)
            mx_ref[j] = jnp.maximum(mx_ref[j], fold(logits(j, kx), jnp.maximum))
        return carry
    lax.fori_loop(0, nkc, max_body, 0)
    for j in range(ATTN_KV_HEADS):
        mx_ref[j] = jnp.broadcast_to(jnp.max(mx_ref[j], axis=1, keepdims=True), (gq, LANES))

    acc_ref[...] = jnp.zeros(acc_ref.shape, F32)

    def att_body(c, carry):
        cols = pl.ds(pl.multiple_of(c * kc, kc), kc)
        bt = bias_ref[:, cols]
        for j in range(ATTN_KV_HEADS):
            kx = jnp.concatenate([k_ref[j, :, cols], bt], axis=0)
            vx = jnp.concatenate([v_ref[j, cols, :], onec_ref[...]], axis=1)
            s = logits(j, kx)
            mb = mx_ref[j]
            p = jnp.concatenate(
                [jnp.exp2(s[:, a * LANES:(a + 1) * LANES] - mb) for a in range(nslab)], axis=1)
            acc_ref[j] = acc_ref[j] + jnp.dot(p.astype(BF16), vx, preferred_element_type=F32)
        return carry
    lax.fori_loop(0, nkc, att_body, 0)

    for h in range(ATTN_HEADS):
        a = acc_ref[h // ATTN_GROUP, (h % ATTN_GROUP) * tq:(h % ATTN_GROUP + 1) * tq, :]
        out_ref[:, h * LANES:(h + 1) * LANES] = (
            a[:, 0:LANES] / a[:, LANES:LANES + 1]).astype(out_ref.dtype)


def dsa_attention(q_r, qi_r, wi, k_r, v_r, ki_r, batch, seq, k_sel):
    n = batch * seq
    nq = seq // Q_BLOCK
    kc = _pick(seq, 512)
    triu = jnp.asarray((np.arange(kc)[:, None] <= np.arange(kc)[None, :]).astype(np.float32), BF16)
    gq = ATTN_GROUP * Q_BLOCK
    eye = jnp.asarray(np.tile(np.eye(Q_BLOCK, dtype=np.float32), (ATTN_GROUP, 1)), BF16)
    onec = np.zeros((kc, LANES), np.float32)
    onec[:, 0] = 1.0
    onec = jnp.asarray(onec, BF16)
    kern = functools.partial(_dsa_kernel, k_sel=k_sel, kc=kc)
    in_specs = [
        pl.BlockSpec((ATTN_HEADS, Q_BLOCK, LANES), lambda b, i: (0, b * nq + i, 0)),
        pl.BlockSpec((IDX_HEADS, Q_BLOCK, LANES), lambda b, i: (0, b * nq + i, 0)),
        pl.BlockSpec((Q_BLOCK, LANES), lambda b, i: (b * nq + i, 0)),
        pl.BlockSpec((ATTN_KV_HEADS, LANES, seq), lambda b, i: (0, 0, b)),
        pl.BlockSpec((ATTN_KV_HEADS, seq, LANES), lambda b, i: (0, b, 0)),
        pl.BlockSpec((LANES, seq), lambda b, i: (0, b)),
        pl.BlockSpec((kc, kc), lambda b, i: (0, 0)),
        pl.BlockSpec((gq, Q_BLOCK), lambda b, i: (0, 0)),
        pl.BlockSpec((kc, LANES), lambda b, i: (0, 0)),
    ]
    scratch = [pltpu.VMEM((Q_BLOCK, seq), I32), pltpu.VMEM((Q_BLOCK, seq), I16),
               pltpu.VMEM((Q_BLOCK, seq), BF16),
               pltpu.VMEM((IDX_HEADS, Q_BLOCK, kc), F32),
               pltpu.VMEM((ATTN_KV_HEADS, gq, LANES), F32),
               pltpu.VMEM((ATTN_KV_HEADS, gq, 2 * LANES), F32)]
    return pl.pallas_call(
        kern, grid=(batch, nq), in_specs=in_specs,
        out_specs=pl.BlockSpec((Q_BLOCK, ATTN_INNER), lambda b, i: (b * nq + i, 0)),
        out_shape=jax.ShapeDtypeStruct((n, ATTN_INNER), BF16),
        scratch_shapes=scratch, compiler_params=_cparams(2), name="dsa_attention",
    )(q_r, qi_r, wi, k_r, v_r, ki_r, triu, eye, onec)


def _silu(x):
    return x * (1.0 / (1.0 + jnp.exp(-x)))


def _ssd_kernel(xbc_ref, z_ref, tail_ref, cw_ref, cb_ref, dtb_ref, alog_ref, dexp_ref, ng_ref,
                tri_ref, e_ref, out_ref, state_ref, ext_ref, y_ref):
    q = SSM_CHUNK
    hi = lax.Precision.HIGHEST

    @pl.when(pl.program_id(1) == 0)
    def _():
        state_ref[...] = jnp.zeros(state_ref.shape, F32)
        ext_ref[0:SUBLANES, :] = jnp.zeros((SUBLANES, SSM_CONV_DIM), F32)

    x = xbc_ref[...]
    ext_ref[SUBLANES:SUBLANES + q, :] = x
    conv = x * cw_ref[SSM_CONV - 1:SSM_CONV, :] + cb_ref[...]
    for k in range(1, SSM_CONV):
        conv = conv + ext_ref[SUBLANES - k:SUBLANES - k + q, :] * cw_ref[SSM_CONV - 1 - k:SSM_CONV - k, :]
    ext_ref[0:SUBLANES, :] = x[q - SUBLANES:q, :]
    xbc = _silu(conv)
    xs = xbc[:, 0:SSM_INNER]
    bm = xbc[:, SSM_INNER:SSM_INNER + SSM_GROUPS * SSM_STATE]
    cm = xbc[:, SSM_INNER + SSM_GROUPS * SSM_STATE:SSM_CONV_DIM]

    lane = lax.broadcasted_iota(I32, (q, LANES), 1)
    is_dt = (lane >= TAIL_DT) & (lane < TAIL_DT + SSM_HEADS)
    raw = tail_ref[...] + dtb_ref[...]
    dt = jnp.where(is_dt, jnp.maximum(raw, 0.0) + jnp.log1p(jnp.exp(-jnp.abs(raw))), 0.0)
    a = -jnp.exp(alog_ref[...])
    da = jnp.where(is_dt, dt * a, 0.0)
    acum = jnp.dot(tri_ref[...], da, precision=hi, preferred_element_type=F32)
    acum_t = acum.T
    acum_x = jnp.dot(acum, e_ref[...], precision=hi, preferred_element_type=F32)
    dt_x = jnp.dot(dt, e_ref[...], precision=hi, preferred_element_type=F32)
    last_x = acum_x[q - 1:q, :]
    xdt = xs * dt_x
    xdec = xdt * jnp.exp(last_x - acum_x)
    ri = lax.broadcasted_iota(I32, (q, q), 0)
    ci = lax.broadcasted_iota(I32, (q, q), 1)
    tril = ri >= ci
    pairs_per_group = SSM_GROUP_INNER // LANES

    for g in range(SSM_GROUPS):
        bg = bm[:, g * SSM_STATE:(g + 1) * SSM_STATE]
        cg = cm[:, g * SSM_STATE:(g + 1) * SSM_STATE]
        cgb = cg.astype(BF16)
        gsl = slice(g * SSM_GROUP_INNER, (g + 1) * SSM_GROUP_INNER)
        cb = lax.dot_general(cgb, bg.astype(BF16), (((1,), (1,)), ((), ())),
                             preferred_element_type=F32)
        y_off = jnp.dot(cgb, state_ref[g].astype(BF16), preferred_element_type=F32)
        y_ref[:, gsl] = y_off * jnp.exp(acum_x[:, gsl])
        for mth in range(pairs_per_group):
            slab = slice((g * pairs_per_group + mth) * LANES, (g * pairs_per_group + mth + 1) * LANES)
            xp = xdt[:, slab]
            yd = None
            for side in range(2):
                hl = TAIL_DT + (g * pairs_per_group + mth) * 2 + side
                seg = acum[:, hl:hl + 1] - acum_t[hl:hl + 1, :]
                lmat = jnp.exp(jnp.where(tril, seg, NEG_BIG))
                if side == 0:
                    xh = jnp.where(lane < SSM_HEAD_DIM, xp, 0.0)
                else:
                    xh = jnp.where(lane >= SSM_HEAD_DIM, xp, 0.0)
                part = jnp.dot((cb * lmat).astype(BF16), xh.astype(BF16), preferred_element_type=F32)
                yd = part if yd is None else yd + part
            y_ref[:, slab] = y_ref[:, slab] + yd
        st = jnp.dot(bg.T.astype(BF16), xdec[:, gsl].astype(BF16), preferred_element_type=F32)
        state_ref[g] = state_ref[g] * jnp.exp(last_x[:, gsl]) + st

    y = (y_ref[...] + dexp_ref[...] * xs) * _silu(z_ref[...])
    for g in range(SSM_GROUPS):
        gsl = slice(g * SSM_GROUP_INNER, (g + 1) * SSM_GROUP_INNER)
        yg = y[:, gsl]
        ms = jnp.mean(yg * yg, axis=-1, keepdims=True)
        out_ref[:, gsl] = (yg * lax.rsqrt(ms + EPS) * ng_ref[:, gsl]).astype(out_ref.dtype)


def ssd_mixer(proj, conv_w, conv_b, dtb_pad, alog_pad, d_exp, norm_g, batch, seq):
    n = batch * seq
    q = SSM_CHUNK
    nc = seq // q
    tri = jnp.asarray((np.arange(q)[:, None] >= np.arange(q)[None, :]).astype(np.float32))
    e = np.zeros((LANES, SSM_INNER), np.float32)
    for h in range(SSM_HEADS):
        e[TAIL_DT + h, h * SSM_HEAD_DIM:(h + 1) * SSM_HEAD_DIM] = 1.0
    e = jnp.asarray(e)

    def col(width, off):
        return pl.BlockSpec((q, width), lambda b, c: (b * nc + c, off // width))

    def full(shape):
        return pl.BlockSpec(shape, lambda b, c: (0,) * len(shape))

    in_specs = [col(SSM_CONV_DIM, COL_XBC), col(SSM_INNER, COL_Z), col(LANES, COL_TAIL),
                full((SSM_CONV, SSM_CONV_DIM)), full((1, SSM_CONV_DIM)), full((1, LANES)),
                full((1, LANES)), full((1, SSM_INNER)), full((1, SSM_INNER)),
                full((q, q)), full((LANES, SSM_INNER))]
    scratch = [pltpu.VMEM((SSM_GROUPS, SSM_STATE, SSM_GROUP_INNER), F32),
               pltpu.VMEM((SUBLANES + q, SSM_CONV_DIM), F32),
               pltpu.VMEM((q, SSM_INNER), F32)]
    return pl.pallas_call(
        _ssd_kernel, grid=(batch, nc), in_specs=in_specs,
        out_specs=pl.BlockSpec((q, SSM_INNER), lambda b, c: (b * nc + c, 0)),
        out_shape=jax.ShapeDtypeStruct((n, SSM_INNER), BF16),
        scratch_shapes=scratch, compiler_params=_cparams(2), name="ssd_mixer",
    )(proj, proj, proj, conv_w, conv_b, dtb_pad, alog_pad, d_exp, norm_g, tri, e)


def _ret_kernel(q_ref, k_ref, v_ref, g_ref, pos_ref, inv_ref, dmat_ref, qdec_ref, kdec_ref,
                cdec_ref, ng_ref, out_ref, state_ref):
    half = RET_QK_DIM // 2

    @pl.when(pl.program_id(1) == 0)
    def _():
        state_ref[...] = jnp.zeros(state_ref.shape, F32)

    ang = pos_ref[...].astype(F32) * inv_ref[...]
    cosf = jnp.cos(ang)
    sinf = jnp.sin(ang)

    def rope(x):
        x1 = x[:, 0:half]
        x2 = x[:, half:2 * half]
        return jnp.concatenate([x1 * cosf - x2 * sinf, x2 * cosf + x1 * sinf], axis=-1)

    for h in range(RET_HEADS):
        qh = rope(q_ref[:, h * RET_QK_DIM:(h + 1) * RET_QK_DIM].astype(F32))
        kh = rope(k_ref[:, h * RET_QK_DIM:(h + 1) * RET_QK_DIM].astype(F32)) * (RET_QK_DIM ** -0.5)
        vh = v_ref[:, h * RET_V_DIM:(h + 1) * RET_V_DIM]
        qb = qh.astype(BF16)
        s = lax.dot_general(qb, kh.astype(BF16), (((1,), (1,)), ((), ())),
                            preferred_element_type=F32) * dmat_ref[h]
        inner = jnp.dot(s.astype(BF16), vh, preferred_element_type=F32)
        st = state_ref[h]
        cross = jnp.dot(qb, st.astype(BF16), preferred_element_type=F32) * qdec_ref[h]
        kd = (kh * kdec_ref[h]).T.astype(BF16)
        state_ref[h] = st * cdec_ref[h] + jnp.dot(kd, vh, preferred_element_type=F32)
        o = inner + cross
        ms = jnp.mean(o * o, axis=-1, keepdims=True)
        o = o * lax.rsqrt(ms + EPS) * ng_ref[:, h * RET_V_DIM:(h + 1) * RET_V_DIM]
        gate = _silu(g_ref[:, h * RET_V_DIM:(h + 1) * RET_V_DIM].astype(F32))
        out_ref[:, h * RET_V_DIM:(h + 1) * RET_V_DIM] = (gate * o).astype(out_ref.dtype)


def retention_mixer(proj, pos, norm_g, batch, seq):
    n = batch * seq
    q = RET_CHUNK
    nc = seq // q
    half = RET_QK_DIM // 2
    inv = jnp.power(RET_THETA, -(jnp.arange(half, dtype=F32) * 2.0 / RET_QK_DIM)).reshape(1, half)
    log_gamma = jnp.log(1.0 - jnp.power(2.0, -5.0 - jnp.arange(RET_HEADS, dtype=F32)))
    i = jnp.arange(q, dtype=F32)
    diff = i[:, None] - i[None, :]
    dmat = jnp.where(diff[None] >= 0,
                     jnp.exp(jnp.maximum(diff, 0.0)[None] * log_gamma[:, None, None]), 0.0)
    qdec = jnp.exp((i + 1.0)[None, :, None] * log_gamma[:, None, None])
    kdec = jnp.exp((q - 1.0 - i)[None, :, None] * log_gamma[:, None, None])
    cdec = jnp.broadcast_to(jnp.exp(q * log_gamma)[:, None, None], (RET_HEADS, 1, RET_V_DIM))

    def col(width, off):
        return pl.BlockSpec((q, width), lambda b, c: (b * nc + c, off // width))

    def full(shape):
        return pl.BlockSpec(shape, lambda b, c: (0,) * len(shape))

    in_specs = [col(RET_QK_TOTAL, 0), col(RET_QK_TOTAL, RET_QK_TOTAL),
                col(RET_V_TOTAL, 2 * RET_QK_TOTAL), col(RET_V_TOTAL, 2 * RET_QK_TOTAL + RET_V_TOTAL),
                pl.BlockSpec((q, 1), lambda b, c: (b * nc + c, 0)),
                full((1, half)), full((RET_HEADS, q, q)), full((RET_HEADS, q, 1)),
                full((RET_HEADS, q, 1)), full((RET_HEADS, 1, RET_V_DIM)), full((1, RET_V_TOTAL))]
    return pl.pallas_call(
        _ret_kernel, grid=(batch, nc), in_specs=in_specs,
        out_specs=pl.BlockSpec((q, RET_V_TOTAL), lambda b, c: (b * nc + c, 0)),
        out_shape=jax.ShapeDtypeStruct((n, RET_V_TOTAL), BF16),
        scratch_shapes=[pltpu.VMEM((RET_HEADS, RET_QK_DIM, RET_V_DIM), F32)],
        compiler_params=_cparams(2), name="retention",
    )(proj, proj, proj, proj, pos, inv, dmat, qdec, kdec, cdec, norm_g.reshape(1, RET_V_TOTAL))


ROUTE_E0 = MOE_GROUPS
INFO_EID, INFO_GATE, INFO_RANK = 0, 2, 4


def _router_kernel(x_ref, g_ref, wr_ref, lt_ref, h_ref, info_ref, cnt_ref, run_ref):
    tb = x_ref.shape[0]

    @pl.when(pl.program_id(0) == 0)
    def _():
        run_ref[...] = jnp.zeros(run_ref.shape, F32)

    x = x_ref[...]
    ms = jnp.mean(x * x, axis=-1, keepdims=True)
    hn = x * lax.rsqrt(ms + EPS) * g_ref[...]
    h_ref[...] = hn
    logits = jnp.dot(hn, wr_ref[...], precision=lax.Precision.HIGHEST, preferred_element_type=F32)

    lane = lax.broadcasted_iota(I32, (tb, LANES), 1).astype(F32)
    far = jnp.float32(4 * LANES)
    ninf = jnp.float32(-jnp.inf)
    gl = jnp.where(lane < MOE_GROUPS, logits, ninf)
    gmax = jnp.max(gl, axis=1, keepdims=True)
    grp = jnp.min(jnp.where(gl == gmax, lane, far), axis=1, keepdims=True)
    p_grp = 1.0 / jnp.sum(jnp.exp(gl - gmax), axis=1, keepdims=True)
    lo = ROUTE_E0 + grp * MOE_EXPERTS_PER_GROUP
    el = jnp.where((lane >= lo) & (lane < lo + MOE_EXPERTS_PER_GROUP), logits, ninf)
    m1 = jnp.max(el, axis=1, keepdims=True)
    i1 = jnp.min(jnp.where(el == m1, lane, far), axis=1, keepdims=True)
    el2 = jnp.where(lane == i1, ninf, el)
    m2 = jnp.max(el2, axis=1, keepdims=True)
    i2 = jnp.min(jnp.where(el2 == m2, lane, far), axis=1, keepdims=True)
    e2 = jnp.exp(m2 - m1)
    g1 = p_grp / (1.0 + e2)
    g2 = p_grp * e2 / (1.0 + e2)
    oh1 = jnp.where(lane == i1, 1.0, 0.0)
    oh2 = jnp.where(lane == i2, 1.0, 0.0)
    cnt = oh1 + oh2
    before = jnp.dot(lt_ref[...], cnt.astype(BF16), preferred_element_type=F32) + run_ref[...]
    r1 = jnp.sum(oh1 * before, axis=1, keepdims=True)
    r2 = jnp.sum(oh2 * before, axis=1, keepdims=True)
    run_ref[...] = run_ref[...] + jnp.sum(cnt, axis=0, keepdims=True)
    cnt_ref[...] = run_ref[...]

    info = jnp.where(lane == INFO_EID, i1 - ROUTE_E0, 0.0)
    info = jnp.where(lane == INFO_EID + 1, i2 - ROUTE_E0, info)
    info = jnp.where(lane == INFO_GATE, g1, info)
    info = jnp.where(lane == INFO_GATE + 1, g2, info)
    info = jnp.where(lane == INFO_RANK, r1, info)
    info = jnp.where(lane == INFO_RANK + 1, r2, info)
    info_ref[...] = info


def moe_router(x, g, w_router_pad):
    n, d = x.shape
    tb = _pick(n, 512)
    lt = jnp.asarray((np.arange(tb)[:, None] > np.arange(tb)[None, :]).astype(np.float32), BF16)
    return pl.pallas_call(
        _router_kernel, grid=(n // tb,),
        in_specs=[pl.BlockSpec((tb, d), lambda i: (i, 0)), pl.BlockSpec((1, d), lambda i: (0, 0)),
                  pl.BlockSpec((d, LANES), lambda i: (0, 0)), pl.BlockSpec((tb, tb), lambda i: (0, 0))],
        out_specs=[pl.BlockSpec((tb, d), lambda i: (i, 0)), pl.BlockSpec((tb, LANES), lambda i: (i, 0)),
                   pl.BlockSpec((1, LANES), lambda i: (0, 0))],
        out_shape=[jax.ShapeDtypeStruct((n, d), F32), jax.ShapeDtypeStruct((n, LANES), F32),
                   jax.ShapeDtypeStruct((1, LANES), F32)],
        scratch_shapes=[pltpu.VMEM((1, LANES), F32)],
        compiler_params=_cparams(1), name="moe_router",
    )(x, g.reshape(1, d), w_router_pad, lt)


def _row_copy(src_ref, src_row, dst_ref, dst_row, sem):
    return pltpu.make_async_copy(src_ref.at[pl.ds(src_row, 1), :], dst_ref.at[pl.ds(dst_row, 1), :], sem)


def _expert_kernel(be_ref, nused_ref, st_ref, h_hbm, wg_ref, wu_ref, wd_ref, out_ref,
                   xbuf, sem, wgu_s, wd_s):
    i = pl.program_id(0)
    slot = i % 2
    nused = nused_ref[0]
    d = xbuf.shape[2]
    ff = EXPERT_FF

    def gather(blk, s):
        def body(r, carry):
            _row_copy(h_hbm, st_ref[blk * MOE_BLOCK + r], xbuf.at[s], r, sem.at[s]).start()
            return carry
        lax.fori_loop(0, MOE_BLOCK, body, 0, unroll=8)

    @pl.when(jnp.logical_and(i == 0, nused > 0))
    def _():
        gather(0, 0)

    @pl.when(i + 1 < nused)
    def _():
        gather(i + 1, 1 - slot)

    @pl.when(i < nused)
    def _():
        def drain(r, carry):
            _row_copy(h_hbm, 0, xbuf.at[slot], 0, sem.at[slot]).wait()
            return carry
        lax.fori_loop(0, MOE_BLOCK, drain, 0, unroll=8)

        changed = jnp.logical_or(i == 0, be_ref[i] != be_ref[jnp.maximum(i - 1, 0)])

        @pl.when(changed)
        def _():
            rc = _pick(d, 256)

            def body(r, carry):
                rows = pl.ds(pl.multiple_of(r * rc, rc), rc)
                wgu_s[rows, 0:ff] = wg_ref[0, 0, rows, :].astype(BF16)
                wgu_s[rows, ff:2 * ff] = wu_ref[0, 0, rows, :].astype(BF16)
                return carry
            lax.fori_loop(0, d // rc, body, 0)
            wd_s[...] = wd_ref[0, 0].astype(BF16)

        gu = jnp.dot(xbuf[slot].astype(BF16), wgu_s[...], preferred_element_type=F32)
        hid = _silu(gu[:, 0:ff]) * gu[:, ff:2 * ff]
        out_ref[...] = jnp.dot(hid.astype(BF16), wd_s[...], preferred_element_type=F32)

    @pl.when(i >= nused)
    def _():
        out_ref[...] = jnp.zeros(out_ref.shape, out_ref.dtype)


def moe_experts(h, slot_tok, blk_expert, nused, w_gate, w_up, w_down, layer):
    n, d = h.shape
    p = slot_tok.shape[0]
    nblk = p // MOE_BLOCK
    ff = EXPERT_FF
    grid_spec = pltpu.PrefetchScalarGridSpec(
        num_scalar_prefetch=3, grid=(nblk,),
        in_specs=[pl.BlockSpec(memory_space=pl.ANY),
                  pl.BlockSpec((1, 1, d, ff), lambda i, be, nu, st: (layer, be[i], 0, 0)),
                  pl.BlockSpec((1, 1, d, ff), lambda i, be, nu, st: (layer, be[i], 0, 0)),
                  pl.BlockSpec((1, 1, ff, d), lambda i, be, nu, st: (layer, be[i], 0, 0))],
        out_specs=pl.BlockSpec((MOE_BLOCK, d), lambda i, be, nu, st: (i, 0)),
        scratch_shapes=[pltpu.VMEM((2, MOE_BLOCK, d), h.dtype), pltpu.SemaphoreType.DMA((2,)),
                        pltpu.VMEM((d, 2 * ff), BF16), pltpu.VMEM((ff, d), BF16)])
    return pl.pallas_call(
        _expert_kernel, grid_spec=grid_spec,
        out_shape=jax.ShapeDtypeStruct((p, d), F32),
        compiler_params=_cparams(1), name="moe_experts",
    )(blk_expert, nused, slot_tok, h, w_gate, w_up, w_down)


def _slots_kernel(dest_ref, slot_ref):
    def clear(s, carry):
        slot_ref[s] = 0
        return carry
    lax.fori_loop(0, slot_ref.shape[0], clear, 0, unroll=8)

    def body(a, carry):
        slot_ref[dest_ref[a]] = lax.div(a, jnp.int32(MOE_TOPK))
        return carry
    lax.fori_loop(0, dest_ref.shape[0], body, 0, unroll=8)


def moe_slots(dest_flat, p):
    return pl.pallas_call(
        _slots_kernel,
        in_specs=[pl.BlockSpec(memory_space=pltpu.SMEM)],
        out_specs=pl.BlockSpec(memory_space=pltpu.SMEM),
        out_shape=jax.ShapeDtypeStruct((p,), I32),
        name="moe_slots",
    )(dest_flat)


def _combine_kernel(dcur_ref, dnxt_ref, info_ref, x_ref, y_hbm, out_ref, ybuf, sem, *, tb, nsteps):
    i = pl.program_id(0)
    slot = i % 2

    def issue(dref, s):
        def body(r, carry):
            for k in range(MOE_TOPK):
                _row_copy(y_hbm, dref[0, 0, MOE_TOPK * r + k], ybuf.at[s, k], r, sem.at[s]).start()
            return carry
        lax.fori_loop(0, tb, body, 0, unroll=8)

    @pl.when(i == 0)
    def _():
        issue(dcur_ref, 0)

    @pl.when(i + 1 < nsteps)
    def _():
        issue(dnxt_ref, 1 - slot)

    def drain(r, carry):
        for k in range(MOE_TOPK):
            _row_copy(y_hbm, 0, ybuf.at[slot, k], 0, sem.at[slot]).wait()
        return carry
    lax.fori_loop(0, tb, drain, 0, unroll=8)

    info = info_ref[...]
    out = x_ref[...]
    for k in range(MOE_TOPK):
        out = out + ybuf[slot, k] * info[:, INFO_GATE + k:INFO_GATE + k + 1]
    out_ref[...] = out


def moe_combine(x, y, dest, info):
    n, d = x.shape
    tb = _pick(n, MOE_BLOCK)
    nsteps = n // tb
    kern = functools.partial(_combine_kernel, tb=tb, nsteps=nsteps)
    dest3 = dest.reshape(nsteps, 1, MOE_TOPK * tb)
    return pl.pallas_call(
        kern, grid=(nsteps,),
        in_specs=[pl.BlockSpec((1, 1, MOE_TOPK * tb), lambda i: (i, 0, 0), memory_space=pltpu.SMEM),
                  pl.BlockSpec((1, 1, MOE_TOPK * tb), lambda i: (jnp.minimum(i + 1, nsteps - 1), 0, 0),
                               memory_space=pltpu.SMEM),
                  pl.BlockSpec((tb, LANES), lambda i: (i, 0)),
                  pl.BlockSpec((tb, d), lambda i: (i, 0)),
                  pl.BlockSpec(memory_space=pl.ANY)],
        out_specs=pl.BlockSpec((tb, d), lambda i: (i, 0)),
        out_shape=jax.ShapeDtypeStruct((n, d), x.dtype),
        scratch_shapes=[pltpu.VMEM((2, MOE_TOPK, tb, d), y.dtype), pltpu.SemaphoreType.DMA((2,))],
        compiler_params=_cparams(1), name="moe_combine",
    )(dest3, dest3, info, x, y)


def hier_moe_layer(x, norm_g, w_rg, w_re, w_gate, w_up, w_down, layer):
    n, d = x.shape
    w_router = jnp.concatenate(
        [w_rg, w_re, jnp.zeros((d, LANES - MOE_GROUPS - N_EXPERTS), F32)], axis=1)
    h, info, cnt = moe_router(x, norm_g, w_router)
    counts = cnt[0, ROUTE_E0:ROUTE_E0 + N_EXPERTS].astype(I32)
    eid = info[:, INFO_EID:INFO_EID + MOE_TOPK].astype(I32)
    rank = info[:, INFO_RANK:INFO_RANK + MOE_TOPK].astype(I32)
    padded = (counts + MOE_BLOCK - 1) // MOE_BLOCK * MOE_BLOCK
    pad_end = jnp.cumsum(padded)
    pad_start = pad_end - padded
    dest = pad_start[eid] + rank
    a = n * MOE_TOPK
    p = (-(-a // MOE_BLOCK) + N_EXPERTS) * MOE_BLOCK
    nblk = p // MOE_BLOCK
    blk_start = jnp.arange(nblk, dtype=I32) * MOE_BLOCK
    blk_expert = jnp.minimum(jnp.sum((pad_end[None, :] <= blk_start[:, None]).astype(I32), axis=1),
                             N_EXPERTS - 1).astype(I32)
    nused = (pad_end[-1:] // MOE_BLOCK).astype(I32)
    slot_tok = moe_slots(dest.reshape(-1), p)
    y = moe_experts(h, slot_tok, blk_expert, nused, w_gate, w_up, w_down, layer)
    return moe_combine(x, y, dest, info)


def _pad_lanes(v, offset, width=LANES):
    out = jnp.zeros((1, width), F32)
    return out.at[0, offset:offset + v.shape[0]].set(v.astype(F32))


def hybrid_layer(x, pos, norm_g, w_in, q_norm, k_norm, kidx_norm, conv_w, conv_b, dt_bias, a_log,
                 d_skip, ssm_norm, w_out, batch, seq, k_sel):
    n, d = x.shape
    offs = np.cumsum([0, ATTN_INNER, ATTN_KV, ATTN_KV, IDX_INNER, IDX_HEAD_DIM, IDX_HEADS,
                      SSM_INNER, SSM_CONV_DIM, SSM_HEADS])
    seg = {name: w_in[:, offs[j]:offs[j + 1]] for j, name in
           enumerate(["q", "k", "v", "qi", "ki", "wi", "z", "xbc", "dt"])}
    tail_pad = LANES - IDX_HEAD_DIM - IDX_HEADS - SSM_HEADS
    w_perm = jnp.concatenate(
        [seg["xbc"], seg["k"], seg["v"], seg["z"], seg["q"], seg["qi"], seg["ki"], seg["wi"],
         seg["dt"], jnp.zeros((d, tail_pad + HYB_COLS - COL_TAIL - LANES), F32)], axis=1).astype(BF16)
    proj = fused_matmul([x], w_perm, g=norm_g, out_dtype=F32, name="hyb_in_proj")

    rot = ATTN_HEAD_DIM // ROPE_FRACTION
    rot_i = IDX_HEAD_DIM // ROPE_FRACTION
    inv16 = jnp.power(ROPE_THETA, -(jnp.arange(rot // 2, dtype=F32) * 2.0 / rot))
    inv8 = jnp.power(ROPE_THETA, -(jnp.arange(rot_i // 2, dtype=F32) * 2.0 / rot_i))
    inv_c = jnp.concatenate(
        [inv16, inv8, jnp.zeros((LANES - rot // 2 - rot_i // 2,), F32)]).reshape(1, LANES)
    q_r, k_r, v_r, qi_r, ki_r, wi = hyb_prep(
        proj, pos, inv_c, q_norm.reshape(1, LANES), k_norm.reshape(1, LANES),
        _pad_lanes(kidx_norm, TAIL_KI))
    attn = dsa_attention(q_r, qi_r, wi, k_r, v_r, ki_r, batch, seq, k_sel)

    ssm = ssd_mixer(proj, conv_w, conv_b.reshape(1, SSM_CONV_DIM), _pad_lanes(dt_bias, TAIL_DT),
                    _pad_lanes(a_log, TAIL_DT), jnp.repeat(d_skip, SSM_HEAD_DIM).reshape(1, SSM_INNER),
                    ssm_norm.reshape(1, SSM_INNER), batch, seq)
    return fused_matmul([attn, ssm], w_out.astype(BF16), res=x, out_dtype=F32, name="hyb_out_proj")


def retention_layer(x, pos, norm_g, w_in, ret_norm, w_out, batch, seq):
    proj = fused_matmul([x], w_in.astype(BF16), g=norm_g, out_dtype=BF16, name="ret_in_proj")
    o = retention_mixer(proj, pos, ret_norm, batch, seq)
    return fused_matmul([o], w_out.astype(BF16), res=x, out_dtype=F32, name="ret_out_proj")


def kernel(x, positions, mix_norm, ffn_norm, hyb_w_in, attn_q_norm, attn_k_norm, idx_k_norm, ssm_conv_w, ssm_conv_b, ssm_dt_bias, ssm_a_log, ssm_d, ssm_norm, hyb_w_out, ret_w_in, ret_norm, ret_w_out, moe_router_group, moe_router_expert, moe_w_gate, moe_w_up, moe_w_down):
    batch, seq, d = x.shape
    depth = mix_norm.shape[0]
    k_sel = min(TOPK_MAX, seq // 4)
    n = batch * seq
    xf = x.reshape(n, d)
    pos = positions.reshape(n, 1).astype(I32)
    for layer in range(depth):
        i = layer // 2
        if layer % 2 == 0:
            xf = hybrid_layer(xf, pos, mix_norm[layer], hyb_w_in[i], attn_q_norm[i], attn_k_norm[i],
                              idx_k_norm[i], ssm_conv_w[i], ssm_conv_b[i], ssm_dt_bias[i],
                              ssm_a_log[i], ssm_d[i], ssm_norm[i], hyb_w_out[i], batch, seq, k_sel)
        else:
            xf = retention_layer(xf, pos, mix_norm[layer], ret_w_in[i], ret_norm[i], ret_w_out[i],
                                 batch, seq)
        xf = hier_moe_layer(xf, ffn_norm[layer], moe_router_group[layer], moe_router_expert[layer],
                            moe_w_gate, moe_w_up, moe_w_down, layer)
    return xf.reshape(batch, seq, d)
```

```python
import functools

import jax
import jax.numpy as jnp
import numpy as np
from jax import lax
from jax.experimental import pallas as pl
from jax.experimental.pallas import tpu as pltpu

F32 = jnp.float32
BF16 = jnp.bfloat16
I32 = jnp.int32
I16 = jnp.int16
HALF_BITS = 16
COUNT_GROUP = 4

ATTN_HEADS = 8
ATTN_KV_HEADS = 2
ATTN_GROUP = ATTN_HEADS // ATTN_KV_HEADS
ATTN_HEAD_DIM = 128
ATTN_INNER = ATTN_HEADS * ATTN_HEAD_DIM
ATTN_KV = ATTN_KV_HEADS * ATTN_HEAD_DIM
IDX_HEADS = 8
IDX_HEAD_DIM = 64
IDX_INNER = IDX_HEADS * IDX_HEAD_DIM
TOPK_MAX = 256
Q_BLOCK = 128
ROPE_THETA = 500000.0
ROPE_FRACTION = 4

SSM_HEADS = 16
SSM_HEAD_DIM = 64
SSM_INNER = SSM_HEADS * SSM_HEAD_DIM
SSM_GROUPS = 2
SSM_STATE = 128
SSM_CONV = 4
SSM_CONV_DIM = SSM_INNER + 2 * SSM_GROUPS * SSM_STATE
SSM_CHUNK = 128
SSM_GROUP_INNER = SSM_INNER // SSM_GROUPS

RET_HEADS = 8
RET_QK_DIM = 256
RET_V_DIM = 512
RET_QK_TOTAL = RET_HEADS * RET_QK_DIM
RET_V_TOTAL = RET_HEADS * RET_V_DIM
RET_CHUNK = 128
RET_THETA = 10000.0

MOE_GROUPS = 8
MOE_EXPERTS_PER_GROUP = 8
N_EXPERTS = MOE_GROUPS * MOE_EXPERTS_PER_GROUP
MOE_TOPK = 2
EXPERT_FF = 512
MOE_BLOCK = 256
COMBINE_ROWS = 128

EPS = 1e-6

LANES = 128
SUBLANES = 8
VMEM_LIMIT = 52 * 1024 * 1024

LOG2E = 1.4426950408889634
NEG_BIG = -1e30
INT_MIN = -2147483648

COL_XBC = 0
COL_K = SSM_CONV_DIM
COL_V = COL_K + ATTN_KV
COL_Z = COL_V + ATTN_KV
COL_Q = COL_Z + SSM_INNER
COL_QI = COL_Q + ATTN_INNER
COL_TAIL = COL_QI + IDX_INNER
HYB_COLS = 5120
TAIL_KI = 0
TAIL_WI = IDX_HEAD_DIM
TAIL_DT = TAIL_WI + IDX_HEADS


def _cparams(n_axes):
    return pltpu.CompilerParams(dimension_semantics=("arbitrary",) * n_axes,
                                vmem_limit_bytes=VMEM_LIMIT)


def _pick(n, pref):
    t = min(n, pref)
    while n % t:
        t //= 2
    return t


def _mm_kernel(*refs, n_a, has_norm, has_res, row_chunk):
    a_refs = refs[:n_a]
    pos = n_a
    g_ref = None
    if has_norm:
        g_ref = refs[pos]
        pos += 1
    w_refs = refs[pos:pos + n_a]
    pos += n_a
    res_ref = None
    if has_res:
        res_ref = refs[pos]
        pos += 1
    out_ref = refs[pos]
    pos += 1
    xn_ref = refs[pos] if has_norm else None
    tm = out_ref.shape[0]

    if has_norm:
        @pl.when(pl.program_id(1) == 0)
        def _():
            def body(r, carry):
                rows = pl.ds(pl.multiple_of(r * row_chunk, row_chunk), row_chunk)
                x = a_refs[0][rows, :]
                ms = jnp.mean(x * x, axis=-1, keepdims=True)
                xn_ref[rows, :] = (x * lax.rsqrt(ms + EPS) * g_ref[...]).astype(BF16)
                return carry
            lax.fori_loop(0, tm // row_chunk, body, 0)
        acc = jnp.dot(xn_ref[...], w_refs[0][...], preferred_element_type=F32)
    else:
        acc = jnp.dot(a_refs[0][...], w_refs[0][...], preferred_element_type=F32)
        for p in range(1, n_a):
            acc = acc + jnp.dot(a_refs[p][...], w_refs[p][...], preferred_element_type=F32)
    if has_res:
        acc = acc + res_ref[...]
    out_ref[...] = acc.astype(out_ref.dtype)


def fused_matmul(a_list, w, *, g=None, res=None, out_dtype=F32, tm=1024, tn=512, name="mm"):
    n = a_list[0].shape[0]
    kp = a_list[0].shape[1]
    m = w.shape[1]
    tm = _pick(n, tm)
    tn = _pick(m, tn)
    n_a = len(a_list)
    has_norm = g is not None
    has_res = res is not None
    in_specs = [pl.BlockSpec((tm, kp), lambda i, j: (i, 0)) for _ in a_list]
    args = list(a_list)
    if has_norm:
        in_specs.append(pl.BlockSpec((1, kp), lambda i, j: (0, 0)))
        args.append(g.reshape(1, kp).astype(F32))
    for p in range(n_a):
        in_specs.append(pl.BlockSpec((kp, tn), lambda i, j, p=p: (p, j)))
        args.append(w)
    if has_res:
        in_specs.append(pl.BlockSpec((tm, tn), lambda i, j: (i, j)))
        args.append(res)
    scratch = [pltpu.VMEM((tm, kp), BF16)] if has_norm else []
    kern = functools.partial(_mm_kernel, n_a=n_a, has_norm=has_norm, has_res=has_res,
                             row_chunk=_pick(tm, 128))
    return pl.pallas_call(
        kern,
        grid=(n // tm, m // tn),
        in_specs=in_specs,
        out_specs=pl.BlockSpec((tm, tn), lambda i, j: (i, j)),
        out_shape=jax.ShapeDtypeStruct((n, m), out_dtype),
        scratch_shapes=scratch,
        compiler_params=_cparams(2),
        name=name,
    )(*args)


def _rope_lanes(x, cosf, s_neg, s_pos, half):
    width = x.shape[-1]
    return (x * cosf + pltpu.roll(x, width - half, axis=1) * s_neg
            + pltpu.roll(x, half, axis=1) * s_pos)


def _prep_kernel(q_ref, k_ref, v_ref, qi_ref, tail_ref, pos_ref, invc_ref, pcos_ref, pneg_ref, ppos_ref,
                 qn_ref, kn_ref, kin_ref,
                 qo_ref, ko_ref, vo_ref, qio_ref, kio_ref, wio_ref):
    tb = q_ref.shape[0]
    posf = pos_ref[...].astype(F32)
    lane = lax.broadcasted_iota(I32, (tb, LANES), 1)

    hi = lax.Precision.HIGHEST
    ang = posf * invc_ref[...]
    cos_c = jnp.cos(ang)
    sin_c = jnp.sin(ang)
    cos_all = jnp.dot(cos_c, pcos_ref[...], precision=hi, preferred_element_type=F32)
    sneg_all = jnp.dot(sin_c, pneg_ref[...], precision=hi, preferred_element_type=F32)
    spos_all = jnp.dot(sin_c, ppos_ref[...], precision=hi, preferred_element_type=F32)

    half = ATTN_HEAD_DIM // ROPE_FRACTION // 2
    cosf = cos_all[:, 0:LANES]
    s_neg = sneg_all[:, 0:LANES]
    s_pos = spos_all[:, 0:LANES]
    scale = ATTN_HEAD_DIM ** -0.5 * LOG2E
    for h in range(ATTN_HEADS):
        x = q_ref[:, h * LANES:(h + 1) * LANES]
        ms = jnp.mean(x * x, axis=-1, keepdims=True)
        x = x * lax.rsqrt(ms + EPS) * qn_ref[...]
        x = _rope_lanes(x, cosf, s_neg, s_pos, half)
        qo_ref[h] = (x * scale).astype(BF16)
    for h in range(ATTN_KV_HEADS):
        x = k_ref[:, h * LANES:(h + 1) * LANES]
        ms = jnp.mean(x * x, axis=-1, keepdims=True)
        x = x * lax.rsqrt(ms + EPS) * kn_ref[...]
        x = _rope_lanes(x, cosf, s_neg, s_pos, half)
        ko_ref[h] = x.T.astype(BF16)
        vo_ref[h] = v_ref[:, h * LANES:(h + 1) * LANES].astype(BF16)

    half_i = IDX_HEAD_DIM // ROPE_FRACTION // 2
    cos_i = cos_all[:, LANES:LANES + IDX_INNER]
    sn_i = sneg_all[:, LANES:LANES + IDX_INNER]
    sp_i = spos_all[:, LANES:LANES + IDX_INNER]
    qi = _rope_lanes(qi_ref[...], cos_i, sn_i, sp_i, half_i)
    for h in range(IDX_HEADS):
        slab = qi[:, (h // 2) * LANES:(h // 2 + 1) * LANES]
        if h % 2:
            slab = pltpu.roll(slab, IDX_HEAD_DIM, axis=1)
        qio_ref[h] = jnp.where(lane < IDX_HEAD_DIM, slab, 0.0).astype(BF16)

    t = tail_ref[...]
    is_ki = lane < IDX_HEAD_DIM
    ms = jnp.sum(jnp.where(is_ki, t * t, 0.0), axis=-1, keepdims=True) * (1.0 / IDX_HEAD_DIM)
    kin = t * lax.rsqrt(ms + EPS) * kin_ref[...]
    kin = _rope_lanes(kin, cos_i[:, 0:LANES], sn_i[:, 0:LANES], sp_i[:, 0:LANES], half_i)
    kio_ref[...] = jnp.where(is_ki, kin, 0.0).T.astype(BF16)
    wio_ref[...] = t * (IDX_HEADS ** -0.5 * IDX_HEAD_DIM ** -0.5)


def _rope_selectors():
    a_half = ATTN_HEAD_DIM // ROPE_FRACTION // 2
    i_half = IDX_HEAD_DIM // ROPE_FRACTION // 2
    unit = LANES - 1
    width = LANES + IDX_INNER
    pcos = np.zeros((LANES, width), np.float32)
    pneg = np.zeros((LANES, width), np.float32)
    ppos = np.zeros((LANES, width), np.float32)
    for m in range(LANES):
        if m < 2 * a_half:
            pcos[m % a_half, m] = 1.0
            (pneg if m < a_half else ppos)[m % a_half, m] = -1.0 if m < a_half else 1.0
        else:
            pcos[unit, m] = 1.0
    for m in range(IDX_INNER):
        r = m % IDX_HEAD_DIM
        if r < 2 * i_half:
            pcos[a_half + r % i_half, LANES + m] = 1.0
            (pneg if r < i_half else ppos)[a_half + r % i_half, LANES + m] = -1.0 if r < i_half else 1.0
        else:
            pcos[unit, LANES + m] = 1.0
    return jnp.asarray(pcos), jnp.asarray(pneg), jnp.asarray(ppos)


def hyb_prep(proj, pos, inv_c, q_norm, k_norm, kidx_norm_pad):
    n = proj.shape[0]
    tb = _pick(n, 256)
    pcos, pneg, ppos = _rope_selectors()
    sel_w = LANES + IDX_INNER

    def col(width, off):
        return pl.BlockSpec((tb, width), lambda i: (i, off // width))

    def full(shape):
        return pl.BlockSpec(shape, lambda i: (0,) * len(shape))

    in_specs = [col(ATTN_INNER, COL_Q), col(ATTN_KV, COL_K), col(ATTN_KV, COL_V),
                col(IDX_INNER, COL_QI), col(LANES, COL_TAIL),
                pl.BlockSpec((tb, 1), lambda i: (i, 0)),
                full((1, LANES)), full((LANES, sel_w)), full((LANES, sel_w)), full((LANES, sel_w)),
                full((1, LANES)), full((1, LANES)), full((1, LANES))]
    out_shape = [jax.ShapeDtypeStruct((ATTN_HEADS, n, LANES), BF16),
                 jax.ShapeDtypeStruct((ATTN_KV_HEADS, LANES, n), BF16),
                 jax.ShapeDtypeStruct((ATTN_KV_HEADS, n, LANES), BF16),
                 jax.ShapeDtypeStruct((IDX_HEADS, n, LANES), BF16),
                 jax.ShapeDtypeStruct((LANES, n), BF16),
                 jax.ShapeDtypeStruct((n, LANES), F32)]
    out_specs = [pl.BlockSpec((ATTN_HEADS, tb, LANES), lambda i: (0, i, 0)),
                 pl.BlockSpec((ATTN_KV_HEADS, LANES, tb), lambda i: (0, 0, i)),
                 pl.BlockSpec((ATTN_KV_HEADS, tb, LANES), lambda i: (0, i, 0)),
                 pl.BlockSpec((IDX_HEADS, tb, LANES), lambda i: (0, i, 0)),
                 pl.BlockSpec((LANES, tb), lambda i: (0, i)),
                 pl.BlockSpec((tb, LANES), lambda i: (i, 0))]
    return pl.pallas_call(
        _prep_kernel, grid=(n // tb,), in_specs=in_specs, out_specs=out_specs,
        out_shape=out_shape, compiler_params=_cparams(1), name="hyb_prep",
    )(proj, proj, proj, proj, proj, pos, inv_c, pcos, pneg, ppos, q_norm, k_norm, kidx_norm_pad)


def _sortable(x):
    b = pltpu.bitcast(x, I32)
    return jnp.where(b < 0, b ^ jnp.int32(0x7FFFFFFF), b)


def _dsa_kernel(q_ref, qi_ref, wi_ref, k_ref, v_ref, ki_ref, triu_ref, eye_ref, onec_ref, out_ref,
                keys_ref, half_ref, bias_ref, wb_ref, mx_ref, acc_ref, *, k_sel, kc):
    i = pl.program_id(1)
    tq = Q_BLOCK
    nkc = (i * tq + tq + kc - 1) // kc
    row = i * tq + lax.broadcasted_iota(I32, (tq, kc), 0)
    lane = lax.broadcasted_iota(I32, (tq, kc), 1)
    nslab = kc // LANES

    w = wi_ref[...]
    for h in range(IDX_HEADS):
        wb_ref[h] = jnp.broadcast_to(w[:, TAIL_WI + h:TAIL_WI + h + 1], (tq, kc))

    def idx_body(c, carry):
        cols = pl.ds(pl.multiple_of(c * kc, kc), kc)
        d = jnp.dot(qi_ref[...].reshape(IDX_HEADS * tq, LANES), ki_ref[:, cols],
                    preferred_element_type=F32)
        sc = jnp.zeros((tq, kc), F32)
        for h in range(IDX_HEADS):
            sc = sc + jnp.maximum(d[h * tq:(h + 1) * tq, :], 0.0) * wb_ref[h]
        causal = (c * kc + lane) <= row
        keys_ref[:, cols] = jnp.where(causal, _sortable(sc), jnp.int32(INT_MIN))
        return carry
    lax.fori_loop(0, nkc, idx_body, 0)

    def fold(x, op):
        acc = x[:, 0:LANES]
        for j in range(1, nslab):
            acc = op(acc, x[:, j * LANES:(j + 1) * LANES])
        return acc

    i16_min = -(1 << (HALF_BITS - 1))

    def count16(cand, strict):
        c16 = cand.astype(I16)

        def hits(start, width):
            kk = half_ref[:, pl.ds(pl.multiple_of(start, kc), width)]
            hit = (kk > c16) if strict else (kk >= c16)
            x = jnp.where(hit, jnp.int16(1), jnp.int16(0))
            acc = x[:, 0:LANES]
            for j in range(1, width // LANES):
                acc = acc + x[:, j * LANES:(j + 1) * LANES]
            return acc

        ngrp = nkc // COUNT_GROUP
        acc = lax.fori_loop(0, ngrp, lambda g, a: a + hits(g * (COUNT_GROUP * kc), COUNT_GROUP * kc),
                            jnp.zeros((tq, LANES), I16))
        acc = lax.fori_loop(ngrp * COUNT_GROUP, nkc, lambda c, a: a + hits(c * kc, kc), acc)
        return jnp.sum(acc.astype(F32), axis=1, keepdims=True)

    def kth_largest16(kf):
        zero = jnp.zeros((tq, 1), I32)
        lo = jnp.where(count16(zero, False) >= kf, zero, jnp.full((tq, 1), i16_min, I32))

        def bis_body(it, lo):
            cand = lo + jnp.left_shift(jnp.int32(1), jnp.int32(HALF_BITS - 2) - it)
            return jnp.where(count16(cand, False) >= kf, cand, lo)
        return lax.fori_loop(0, HALF_BITS - 1, bis_body, lo)

    def fill_half(fn):
        def body(c, carry):
            cols = pl.ds(pl.multiple_of(c * kc, kc), kc)
            half_ref[:, cols] = fn(keys_ref[:, cols]).astype(I16)
            return carry
        lax.fori_loop(0, nkc, body, 0)

    kf = jnp.float32(k_sel)
    fill_half(lambda kk: jnp.right_shift(kk, HALF_BITS))
    thr_hi = kth_largest16(kf)
    above = count16(thr_hi, True)
    low_mask = (1 << HALF_BITS) - 1
    fill_half(lambda kk: jnp.where(jnp.right_shift(kk, HALF_BITS) == thr_hi,
                                   (kk & low_mask) + i16_min, i16_min))
    thr_lo = kth_largest16(kf - above)
    thr = jnp.left_shift(thr_hi, HALF_BITS) + (thr_lo - i16_min)
    cnt_gt_lo = count16(thr_lo, True)
    need = kf - above - cnt_gt_lo
    n_eq = count16(thr_lo, False) - cnt_gt_lo
    row_plain = jnp.logical_or(jnp.logical_and(n_eq == need, thr_lo > i16_min),
                               thr == jnp.int32(INT_MIN))
    all_plain = jnp.min(jnp.where(row_plain, 1.0, 0.0)) > 0.5

    @pl.when(all_plain)
    def _():
        def fin_body(c, carry):
            cols = pl.ds(pl.multiple_of(c * kc, kc), kc)
            causal = (c * kc + lane) <= row
            sel = (keys_ref[:, cols] >= thr) & causal
            bias_ref[:, cols] = jnp.where(sel, 0.0, NEG_BIG).astype(BF16)
            return carry
        lax.fori_loop(0, nkc, fin_body, 0)

    @pl.when(jnp.logical_not(all_plain))
    def _():
        def fin_body(c, run):
            cols = pl.ds(pl.multiple_of(c * kc, kc), kc)
            kk = keys_ref[:, cols]
            eq = kk == thr
            eqf = jnp.where(eq, 1.0, 0.0)
            incl = jnp.dot(eqf.astype(BF16), triu_ref[...], preferred_element_type=F32)
            sel = (kk > thr) | (eq & ((run + incl) <= need))
            causal = (c * kc + lane) <= row
            bias_ref[:, cols] = jnp.where(sel & causal, 0.0, NEG_BIG).astype(BF16)
            return run + jnp.sum(eqf, axis=1, keepdims=True)
        lax.fori_loop(0, nkc, fin_body, jnp.zeros((tq, 1), F32))

    gq = ATTN_GROUP * tq

    def logits(j, kx):
        qg = q_ref[j * ATTN_GROUP:(j + 1) * ATTN_GROUP].reshape(gq, LANES)
        qx = jnp.concatenate([qg, eye_ref[...]], axis=1)
        return jnp.dot(qx, kx, preferred_element_type=F32)

    mx_ref[...] = jnp.full(mx_ref.shape, NEG_BIG, F32)

    def max_body(c, carry):
        cols = pl.ds(pl.multiple_of(c * kc, kc), kc)
        bt = bias_ref[:, cols]
        for j in range(ATTN_KV_HEADS):
            kx = jnp.concatenate([k_ref[j, :, cols], bt], axis=0)
            mx_ref[j] = jnp.maximum(mx_ref[j], fold(logits(j, kx), jnp.maximum))
        return carry
    lax.fori_loop(0, nkc, max_body, 0)
    for j in range(ATTN_KV_HEADS):
        mx_ref[j] = jnp.broadcast_to(jnp.max(mx_ref[j], axis=1, keepdims=True), (gq, LANES))

    acc_ref[...] = jnp.zeros(acc_ref.shape, F32)

    def att_body(c, carry):
        cols = pl.ds(pl.multiple_of(c * kc, kc), kc)
        bt = bias_ref[:, cols]
        for j in range(ATTN_KV_HEADS):
            kx = jnp.concatenate([k_ref[j, :, cols], bt], axis=0)
            vx = jnp.concatenate([v_ref[j, cols, :], onec_ref[...]], axis=1)
            s = logits(j, kx)
            mb = mx_ref[j]
            p = jnp.concatenate(
                [jnp.exp2(s[:, a * LANES:(a + 1) * LANES] - mb) for a in range(nslab)], axis=1)
            acc_ref[j] = acc_ref[j] + jnp.dot(p.astype(BF16), vx, preferred_element_type=F32)
        return carry
    lax.fori_loop(0, nkc, att_body, 0)

    for h in range(ATTN_HEADS):
        a = acc_ref[h // ATTN_GROUP, (h % ATTN_GROUP) * tq:(h % ATTN_GROUP + 1) * tq, :]
        out_ref[:, h * LANES:(h + 1) * LANES] = (
            a[:, 0:LANES] / a[:, LANES:LANES + 1]).astype(out_ref.dtype)


def dsa_attention(q_r, qi_r, wi, k_r, v_r, ki_r, batch, seq, k_sel):
    n = batch * seq
    nq = seq // Q_BLOCK
    kc = _pick(seq, 1024)
    triu =jnp.asarray((np.arange(kc)[:, None] <= np.arange(kc)[None, :]).astype(np.float32), BF16)
    gq = ATTN_GROUP * Q_BLOCK
    eye = jnp.asarray(np.tile(np.eye(Q_BLOCK, dtype=np.float32), (ATTN_GROUP, 1)), BF16)
    onec = np.zeros((kc, LANES), np.float32)
    onec[:, 0] = 1.0
    onec = jnp.asarray(onec, BF16)
    kern = functools.partial(_dsa_kernel, k_sel=k_sel, kc=kc)
    in_specs = [
        pl.BlockSpec((ATTN_HEADS, Q_BLOCK, LANES), lambda b, i: (0, b * nq + i, 0)),
        pl.BlockSpec((IDX_HEADS, Q_BLOCK, LANES), lambda b, i: (0, b * nq + i, 0)),
        pl.BlockSpec((Q_BLOCK, LANES), lambda b, i: (b * nq + i, 0)),
        pl.BlockSpec((ATTN_KV_HEADS, LANES, seq), lambda b, i: (0, 0, b)),
        pl.BlockSpec((ATTN_KV_HEADS, seq, LANES), lambda b, i: (0, b, 0)),
        pl.BlockSpec((LANES, seq), lambda b, i: (0, b)),
        pl.BlockSpec((kc, kc), lambda b, i: (0, 0)),
        pl.BlockSpec((gq, Q_BLOCK), lambda b, i: (0, 0)),
        pl.BlockSpec((kc, LANES), lambda b, i: (0, 0)),
    ]
    scratch = [pltpu.VMEM((Q_BLOCK, seq), I32), pltpu.VMEM((Q_BLOCK, seq), I16),
               pltpu.VMEM((Q_BLOCK, seq), BF16),
               pltpu.VMEM((IDX_HEADS, Q_BLOCK, kc), F32),
               pltpu.VMEM((ATTN_KV_HEADS, gq, LANES), F32),
               pltpu.VMEM((ATTN_KV_HEADS, gq, 2 * LANES), F32)]
    return pl.pallas_call(
        kern, grid=(batch, nq), in_specs=in_specs,
        out_specs=pl.BlockSpec((Q_BLOCK, ATTN_INNER), lambda b, i: (b * nq + i, 0)),
        out_shape=jax.ShapeDtypeStruct((n, ATTN_INNER), BF16),
        scratch_shapes=scratch, compiler_params=_cparams(2), name="dsa_attention",
    )(q_r, qi_r, wi, k_r, v_r, ki_r, triu, eye, onec)


def _silu(x):
    return x * (1.0 / (1.0 + jnp.exp(-x)))


def _ssd_kernel(xbc_ref, z_ref, tail_ref, cw_ref, cb_ref, dtb_ref, alog_ref, dexp_ref, ng_ref,
                tri_ref, e_ref, out_ref, state_ref, ext_ref, y_ref):
    q = SSM_CHUNK
    hi = lax.Precision.HIGHEST

    @pl.when(pl.program_id(1) == 0)
    def _():
        state_ref[...] = jnp.zeros(state_ref.shape, F32)
        ext_ref[0:SUBLANES, :] = jnp.zeros((SUBLANES, SSM_CONV_DIM), F32)

    x = xbc_ref[...]
    ext_ref[SUBLANES:SUBLANES + q, :] = x
    conv = x * cw_ref[SSM_CONV - 1:SSM_CONV, :] + cb_ref[...]
    for k in range(1, SSM_CONV):
        conv = conv + ext_ref[SUBLANES - k:SUBLANES - k + q, :] * cw_ref[SSM_CONV - 1 - k:SSM_CONV - k, :]
    ext_ref[0:SUBLANES, :] = x[q - SUBLANES:q, :]
    xbc = _silu(conv)
    xs = xbc[:, 0:SSM_INNER]
    bm = xbc[:, SSM_INNER:SSM_INNER + SSM_GROUPS * SSM_STATE]
    cm = xbc[:, SSM_INNER + SSM_GROUPS * SSM_STATE:SSM_CONV_DIM]

    lane = lax.broadcasted_iota(I32, (q, LANES), 1)
    is_dt = (lane >= TAIL_DT) & (lane < TAIL_DT + SSM_HEADS)
    raw = tail_ref[...] + dtb_ref[...]
    dt = jnp.where(is_dt, jnp.maximum(raw, 0.0) + jnp.log1p(jnp.exp(-jnp.abs(raw))), 0.0)
    a = -jnp.exp(alog_ref[...])
    da = jnp.where(is_dt, dt * a, 0.0)
    acum = jnp.dot(tri_ref[...], da, precision=hi, preferred_element_type=F32)
    acum_t = acum.T
    acum_x = jnp.dot(acum, e_ref[...], precision=hi, preferred_element_type=F32)
    dt_x = jnp.dot(dt, e_ref[...], precision=hi, preferred_element_type=F32)
    last_x = acum_x[q - 1:q, :]
    xdt = xs * dt_x
    xdec = xdt * jnp.exp(last_x - acum_x)
    ri = lax.broadcasted_iota(I32, (q, q), 0)
    ci = lax.broadcasted_iota(I32, (q, q), 1)
    tril = ri >= ci
    pairs_per_group = SSM_GROUP_INNER // LANES

    for g in range(SSM_GROUPS):
        bg = bm[:, g * SSM_STATE:(g + 1) * SSM_STATE]
        cg = cm[:, g * SSM_STATE:(g + 1) * SSM_STATE]
        cgb = cg.astype(BF16)
        gsl = slice(g * SSM_GROUP_INNER, (g + 1) * SSM_GROUP_INNER)
        cb = lax.dot_general(cgb, bg.astype(BF16), (((1,), (1,)), ((), ())),
                             preferred_element_type=F32)
        y_off = jnp.dot(cgb, state_ref[g].astype(BF16), preferred_element_type=F32)
        y_ref[:, gsl] = y_off * jnp.exp(acum_x[:, gsl])
        for mth in range(pairs_per_group):
            slab = slice((g * pairs_per_group + mth) * LANES, (g * pairs_per_group + mth + 1) * LANES)
            xp = xdt[:, slab]
            yd = None
            for side in range(2):
                hl = TAIL_DT + (g * pairs_per_group + mth) * 2 + side
                seg = acum[:, hl:hl + 1] - acum_t[hl:hl + 1, :]
                lmat = jnp.exp(jnp.where(tril, seg, NEG_BIG))
                if side == 0:
                    xh = jnp.where(lane < SSM_HEAD_DIM, xp, 0.0)
                else:
                    xh = jnp.where(lane >= SSM_HEAD_DIM, xp, 0.0)
                part = jnp.dot((cb * lmat).astype(BF16), xh.astype(BF16), preferred_element_type=F32)
                yd = part if yd is None else yd + part
            y_ref[:, slab] = y_ref[:, slab] + yd
        st = jnp.dot(bg.T.astype(BF16), xdec[:, gsl].astype(BF16), preferred_element_type=F32)
        state_ref[g] = state_ref[g] * jnp.exp(last_x[:, gsl]) + st

    y = (y_ref[...] + dexp_ref[...] * xs) * _silu(z_ref[...])
    for g in range(SSM_GROUPS):
        gsl = slice(g * SSM_GROUP_INNER, (g + 1) * SSM_GROUP_INNER)
        yg = y[:, gsl]
        ms = jnp.mean(yg * yg, axis=-1, keepdims=True)
        out_ref[:, gsl] = (yg * lax.rsqrt(ms + EPS) * ng_ref[:, gsl]).astype(out_ref.dtype)


def ssd_mixer(proj, conv_w, conv_b, dtb_pad, alog_pad, d_exp, norm_g, batch, seq):
    n = batch * seq
    q = SSM_CHUNK
    nc = seq // q
    tri = jnp.asarray((np.arange(q)[:, None] >= np.arange(q)[None, :]).astype(np.float32))
    e = np.zeros((LANES, SSM_INNER), np.float32)
    for h in range(SSM_HEADS):
        e[TAIL_DT + h, h * SSM_HEAD_DIM:(h + 1) * SSM_HEAD_DIM] = 1.0
    e = jnp.asarray(e)

    def col(width, off):
        return pl.BlockSpec((q, width), lambda b, c: (b * nc + c, off // width))

    def full(shape):
        return pl.BlockSpec(shape, lambda b, c: (0,) * len(shape))

    in_specs = [col(SSM_CONV_DIM, COL_XBC), col(SSM_INNER, COL_Z), col(LANES, COL_TAIL),
                full((SSM_CONV, SSM_CONV_DIM)), full((1, SSM_CONV_DIM)), full((1, LANES)),
                full((1, LANES)), full((1, SSM_INNER)), full((1, SSM_INNER)),
                full((q, q)), full((LANES, SSM_INNER))]
    scratch = [pltpu.VMEM((SSM_GROUPS, SSM_STATE, SSM_GROUP_INNER), F32),
               pltpu.VMEM((SUBLANES + q, SSM_CONV_DIM), F32),
               pltpu.VMEM((q, SSM_INNER), F32)]
    return pl.pallas_call(
        _ssd_kernel, grid=(batch, nc), in_specs=in_specs,
        out_specs=pl.BlockSpec((q, SSM_INNER), lambda b, c: (b * nc + c, 0)),
        out_shape=jax.ShapeDtypeStruct((n, SSM_INNER), BF16),
        scratch_shapes=scratch, compiler_params=_cparams(2), name="ssd_mixer",
    )(proj, proj, proj, conv_w, conv_b, dtb_pad, alog_pad, d_exp, norm_g, tri, e)


def _ret_kernel(q_ref, k_ref, v_ref, g_ref, pos_ref, inv_ref, dmat_ref, qdec_ref, kdec_ref,
                cdec_ref, ng_ref, out_ref, state_ref):
    half = RET_QK_DIM // 2

    @pl.when(pl.program_id(1) == 0)
    def _():
        state_ref[...] = jnp.zeros(state_ref.shape, F32)

    ang = pos_ref[...].astype(F32) * inv_ref[...]
    cosf = jnp.cos(ang)
    sinf = jnp.sin(ang)

    def rope(x):
        x1 = x[:, 0:half]
        x2 = x[:, half:2 * half]
        return jnp.concatenate([x1 * cosf - x2 * sinf, x2 * cosf + x1 * sinf], axis=-1)

    for h in range(RET_HEADS):
        qh = rope(q_ref[:, h * RET_QK_DIM:(h + 1) * RET_QK_DIM].astype(F32))
        kh = rope(k_ref[:, h * RET_QK_DIM:(h + 1) * RET_QK_DIM].astype(F32)) * (RET_QK_DIM ** -0.5)
        vh = v_ref[:, h * RET_V_DIM:(h + 1) * RET_V_DIM]
        qb = qh.astype(BF16)
        s = lax.dot_general(qb, kh.astype(BF16), (((1,), (1,)), ((), ())),
                            preferred_element_type=F32) * dmat_ref[h]
        inner = jnp.dot(s.astype(BF16), vh, preferred_element_type=F32)
        st = state_ref[h]
        cross = jnp.dot(qb, st.astype(BF16), preferred_element_type=F32) * qdec_ref[h]
        kd = (kh * kdec_ref[h]).T.astype(BF16)
        state_ref[h] = st * cdec_ref[h] + jnp.dot(kd, vh, preferred_element_type=F32)
        o = inner + cross
        ms = jnp.mean(o * o, axis=-1, keepdims=True)
        o = o * lax.rsqrt(ms + EPS) * ng_ref[:, h * RET_V_DIM:(h + 1) * RET_V_DIM]
        gate = _silu(g_ref[:, h * RET_V_DIM:(h + 1) * RET_V_DIM].astype(F32))
        out_ref[:, h * RET_V_DIM:(h + 1) * RET_V_DIM] = (gate * o).astype(out_ref.dtype)


def retention_mixer(proj, pos, norm_g, batch, seq):
    n = batch * seq
    q = RET_CHUNK
    nc = seq // q
    half = RET_QK_DIM // 2
    inv = jnp.power(RET_THETA, -(jnp.arange(half, dtype=F32) * 2.0 / RET_QK_DIM)).reshape(1, half)
    log_gamma = jnp.log(1.0 - jnp.power(2.0, -5.0 - jnp.arange(RET_HEADS, dtype=F32)))
    i = jnp.arange(q, dtype=F32)
    diff = i[:, None] - i[None, :]
    dmat = jnp.where(diff[None] >= 0,
                     jnp.exp(jnp.maximum(diff, 0.0)[None] * log_gamma[:, None, None]), 0.0)
    qdec = jnp.exp((i + 1.0)[None, :, None] * log_gamma[:, None, None])
    kdec = jnp.exp((q - 1.0 - i)[None, :, None] * log_gamma[:, None, None])
    cdec = jnp.broadcast_to(jnp.exp(q * log_gamma)[:, None, None], (RET_HEADS, 1, RET_V_DIM))

    def col(width, off):
        return pl.BlockSpec((q, width), lambda b, c: (b * nc + c, off // width))

    def full(shape):
        return pl.BlockSpec(shape, lambda b, c: (0,) * len(shape))

    in_specs = [col(RET_QK_TOTAL, 0), col(RET_QK_TOTAL, RET_QK_TOTAL),
                col(RET_V_TOTAL, 2 * RET_QK_TOTAL), col(RET_V_TOTAL, 2 * RET_QK_TOTAL + RET_V_TOTAL),
                pl.BlockSpec((q, 1), lambda b, c: (b * nc + c, 0)),
                full((1, half)), full((RET_HEADS, q, q)), full((RET_HEADS, q, 1)),
                full((RET_HEADS, q, 1)), full((RET_HEADS, 1, RET_V_DIM)), full((1, RET_V_TOTAL))]
    return pl.pallas_call(
        _ret_kernel, grid=(batch, nc), in_specs=in_specs,
        out_specs=pl.BlockSpec((q, RET_V_TOTAL), lambda b, c: (b * nc + c, 0)),
        out_shape=jax.ShapeDtypeStruct((n, RET_V_TOTAL), BF16),
        scratch_shapes=[pltpu.VMEM((RET_HEADS, RET_QK_DIM, RET_V_DIM), F32)],
        compiler_params=_cparams(2), name="retention",
    )(proj, proj, proj, proj, pos, inv, dmat, qdec, kdec, cdec, norm_g.reshape(1, RET_V_TOTAL))


ROUTE_E0 = MOE_GROUPS
INFO_EID, INFO_GATE, INFO_RANK = 0, 2, 4


def _router_kernel(x_ref, g_ref, wr_ref, lt_ref, h_ref, info_ref, cnt_ref, run_ref):
    tb = x_ref.shape[0]

    @pl.when(pl.program_id(0) == 0)
    def _():
        run_ref[...] = jnp.zeros(run_ref.shape, F32)

    x = x_ref[...]
    ms = jnp.mean(x * x, axis=-1, keepdims=True)
    hn = x * lax.rsqrt(ms + EPS) * g_ref[...]
    h_ref[...] = hn
    h_hi = hn.astype(BF16)
    h_lo = (hn - h_hi.astype(F32)).astype(BF16)
    both = jnp.dot(h_hi, wr_ref[...], preferred_element_type=F32)
    logits = (both[:, 0:LANES] + both[:, LANES:2 * LANES]
              + jnp.dot(h_lo, wr_ref[:, 0:LANES], preferred_element_type=F32))

    lane = lax.broadcasted_iota(I32, (tb, LANES), 1).astype(F32)
    far = jnp.float32(4 * LANES)
    ninf = jnp.float32(-jnp.inf)
    gl = jnp.where(lane < MOE_GROUPS, logits, ninf)
    gmax = jnp.max(gl, axis=1, keepdims=True)
    grp = jnp.min(jnp.where(gl == gmax, lane, far), axis=1, keepdims=True)
    p_grp = 1.0 / jnp.sum(jnp.exp(gl - gmax), axis=1, keepdims=True)
    lo = ROUTE_E0 + grp * MOE_EXPERTS_PER_GROUP
    el = jnp.where((lane >= lo) & (lane < lo + MOE_EXPERTS_PER_GROUP), logits, ninf)
    m1 = jnp.max(el, axis=1, keepdims=True)
    i1 = jnp.min(jnp.where(el == m1, lane, far), axis=1, keepdims=True)
    el2 = jnp.where(lane == i1, ninf, el)
    m2 = jnp.max(el2, axis=1, keepdims=True)
    i2 = jnp.min(jnp.where(el2 == m2, lane, far), axis=1, keepdims=True)
    e2 = jnp.exp(m2 - m1)
    g1 = p_grp / (1.0 + e2)
    g2 = p_grp * e2 / (1.0 + e2)
    oh1 = jnp.where(lane == i1, 1.0, 0.0)
    oh2 = jnp.where(lane == i2, 1.0, 0.0)
    cnt = oh1 + oh2
    before = jnp.dot(lt_ref[...], cnt.astype(BF16), preferred_element_type=F32) + run_ref[...]
    r1 = jnp.sum(oh1 * before, axis=1, keepdims=True)
    r2 = jnp.sum(oh2 * before, axis=1, keepdims=True)
    run_ref[...] = run_ref[...] + jnp.sum(cnt, axis=0, keepdims=True)
    cnt_ref[...] = run_ref[...]

    info = jnp.where(lane == INFO_EID, i1 - ROUTE_E0, 0.0)
    info = jnp.where(lane == INFO_EID + 1, i2 - ROUTE_E0, info)
    info = jnp.where(lane == INFO_GATE, g1, info)
    info = jnp.where(lane == INFO_GATE + 1, g2, info)
    info = jnp.where(lane == INFO_RANK, r1, info)
    info = jnp.where(lane == INFO_RANK + 1, r2, info)
    info_ref[...] = info


def moe_router(x, g, w_router_pad):
    n, d = x.shape
    tb = _pick(n, 512)
    lt = jnp.asarray((np.arange(tb)[:, None] > np.arange(tb)[None, :]).astype(np.float32), BF16)
    w_hi = w_router_pad.astype(BF16)
    w_lo = (w_router_pad - w_hi.astype(F32)).astype(BF16)
    w_split = jnp.concatenate([w_hi, w_lo], axis=1)
    return pl.pallas_call(
        _router_kernel, grid=(n // tb,),
        in_specs=[pl.BlockSpec((tb, d), lambda i: (i, 0)), pl.BlockSpec((1, d), lambda i: (0, 0)),
                  pl.BlockSpec((d, 2 * LANES), lambda i: (0, 0)), pl.BlockSpec((tb, tb), lambda i: (0, 0))],
        out_specs=[pl.BlockSpec((tb, d), lambda i: (i, 0)), pl.BlockSpec((tb, LANES), lambda i: (i, 0)),
                   pl.BlockSpec((1, LANES), lambda i: (0, 0))],
        out_shape=[jax.ShapeDtypeStruct((n, d), F32), jax.ShapeDtypeStruct((n, LANES), F32),
                   jax.ShapeDtypeStruct((1, LANES), F32)],
        scratch_shapes=[pltpu.VMEM((1, LANES), F32)],
        compiler_params=_cparams(1), name="moe_router",
    )(x, g.reshape(1, d), w_split, lt)


def _row_copy(src_ref, src_row, dst_ref, dst_row, sem):
    return pltpu.make_async_copy(src_ref.at[pl.ds(src_row, 1), :], dst_ref.at[pl.ds(dst_row, 1), :], sem)


def _expert_kernel(be_ref, nused_ref, st_ref, h_hbm, wg_ref, wu_ref, wd_ref, out_ref,
                   xbuf, sem, wgu_s, wd_s):
    i = pl.program_id(0)
    slot = i % 2
    nused = nused_ref[0]
    d = xbuf.shape[2]
    ff = EXPERT_FF

    def gather(blk, s):
        def body(r, carry):
            _row_copy(h_hbm, st_ref[blk * MOE_BLOCK + r], xbuf.at[s], r, sem.at[s]).start()
            return carry
        lax.fori_loop(0, MOE_BLOCK, body, 0, unroll=8)

    @pl.when(jnp.logical_and(i == 0, nused > 0))
    def _():
        gather(0, 0)

    @pl.when(i + 1 < nused)
    def _():
        gather(i + 1, 1 - slot)

    @pl.when(i < nused)
    def _():
        def drain(r, carry):
            _row_copy(h_hbm, 0, xbuf.at[slot], 0, sem.at[slot]).wait()
            return carry
        lax.fori_loop(0, MOE_BLOCK, drain, 0, unroll=8)

        changed = jnp.logical_or(i == 0, be_ref[i] != be_ref[jnp.maximum(i - 1, 0)])

        @pl.when(changed)
        def _():
            rc = _pick(d, 256)

            def body(r, carry):
                rows = pl.ds(pl.multiple_of(r * rc, rc), rc)
                wgu_s[rows, 0:ff] = wg_ref[0, 0, rows, :].astype(BF16)
                wgu_s[rows, ff:2 * ff] = wu_ref[0, 0, rows, :].astype(BF16)
                return carry
            lax.fori_loop(0, d // rc, body, 0)
            wd_s[...] = wd_ref[0, 0].astype(BF16)

        gu = jnp.dot(xbuf[slot].astype(BF16), wgu_s[...], preferred_element_type=F32)
        hid = _silu(gu[:, 0:ff]) * gu[:, ff:2 * ff]
        out_ref[...] = jnp.dot(hid.astype(BF16), wd_s[...], preferred_element_type=F32)

    @pl.when(i >= nused)
    def _():
        out_ref[...] = jnp.zeros(out_ref.shape, out_ref.dtype)


def moe_experts(h, slot_tok, blk_expert, nused, w_gate, w_up, w_down, layer):
    n, d = h.shape
    p = slot_tok.shape[0]
    nblk = p // MOE_BLOCK
    ff = EXPERT_FF
    grid_spec = pltpu.PrefetchScalarGridSpec(
        num_scalar_prefetch=3, grid=(nblk,),
        in_specs=[pl.BlockSpec(memory_space=pl.ANY),
                  pl.BlockSpec((1, 1, d, ff), lambda i, be, nu, st: (layer, be[i], 0, 0)),
                  pl.BlockSpec((1, 1, d, ff), lambda i, be, nu, st: (layer, be[i], 0, 0)),
                  pl.BlockSpec((1, 1, ff, d), lambda i, be, nu, st: (layer, be[i], 0, 0))],
        out_specs=pl.BlockSpec((MOE_BLOCK, d), lambda i, be, nu, st: (i, 0)),
        scratch_shapes=[pltpu.VMEM((2, MOE_BLOCK, d), h.dtype), pltpu.SemaphoreType.DMA((2,)),
                        pltpu.VMEM((d, 2 * ff), BF16), pltpu.VMEM((ff, d), BF16)])
    return pl.pallas_call(
        _expert_kernel, grid_spec=grid_spec,
        out_shape=jax.ShapeDtypeStruct((p, d), F32),
        compiler_params=_cparams(1), name="moe_experts",
    )(blk_expert, nused, slot_tok, h, w_gate, w_up, w_down)


def _slots_kernel(dest_ref, fill_lo_ref, fill_hi_ref, slot_ref):
    n_seg = fill_lo_ref.shape[0]

    def seg(e, carry):
        def clear(s, c):
            slot_ref[s] = 0
            return c
        lax.fori_loop(fill_lo_ref[e], fill_hi_ref[e], clear, 0)
        return carry
    lax.fori_loop(0, n_seg, seg, 0)

    def body(a, carry):
        slot_ref[dest_ref[a]] = lax.div(a, jnp.int32(MOE_TOPK))
        return carry
    lax.fori_loop(0, dest_ref.shape[0], body, 0, unroll=8)


def moe_slots(dest_flat, fill_lo, fill_hi, p):
    smem = pl.BlockSpec(memory_space=pltpu.SMEM)
    return pl.pallas_call(
        _slots_kernel,
        in_specs=[smem, smem, smem],
        out_specs=smem,
        out_shape=jax.ShapeDtypeStruct((p,), I32),
        name="moe_slots",
    )(dest_flat, fill_lo, fill_hi)


def _combine_kernel(dcur_ref, dnxt_ref, info_ref, x_ref, y_hbm, out_ref, ybuf, sem, *, tb, nsteps):
    i = pl.program_id(0)
    slot = i % 2

    def issue(dref, s):
        def body(r, carry):
            for k in range(MOE_TOPK):
                _row_copy(y_hbm, dref[0, 0, MOE_TOPK * r + k], ybuf.at[s, k], r, sem.at[s]).start()
            return carry
        lax.fori_loop(0, tb, body, 0, unroll=8)

    @pl.when(i == 0)
    def _():
        issue(dcur_ref, 0)

    @pl.when(i + 1 < nsteps)
    def _():
        issue(dnxt_ref, 1 - slot)

    def drain(r, carry):
        for k in range(MOE_TOPK):
            _row_copy(y_hbm, 0, ybuf.at[slot, k], 0, sem.at[slot]).wait()
        return carry
    lax.fori_loop(0, tb, drain, 0, unroll=8)

    info = info_ref[...]
    out = x_ref[...]
    for k in range(MOE_TOPK):
        out = out + ybuf[slot, k] * info[:, INFO_GATE + k:INFO_GATE + k + 1]
    out_ref[...] = out


def moe_combine(x, y, dest, info):
    n, d = x.shape
    tb = _pick(n, COMBINE_ROWS)
    nsteps = n // tb
    kern = functools.partial(_combine_kernel, tb=tb, nsteps=nsteps)
    dest3 = dest.reshape(nsteps, 1, MOE_TOPK * tb)
    return pl.pallas_call(
        kern, grid=(nsteps,),
        in_specs=[pl.BlockSpec((1, 1, MOE_TOPK * tb), lambda i: (i, 0, 0), memory_space=pltpu.SMEM),
                  pl.BlockSpec((1, 1, MOE_TOPK * tb), lambda i: (jnp.minimum(i + 1, nsteps - 1), 0, 0),
                               memory_space=pltpu.SMEM),
                  pl.BlockSpec((tb, LANES), lambda i: (i, 0)),
                  pl.BlockSpec((tb, d), lambda i: (i, 0)),
                  pl.BlockSpec(memory_space=pl.ANY)],
        out_specs=pl.BlockSpec((tb, d), lambda i: (i, 0)),
        out_shape=jax.ShapeDtypeStruct((n, d), x.dtype),
        scratch_shapes=[pltpu.VMEM((2, MOE_TOPK, tb, d), y.dtype), pltpu.SemaphoreType.DMA((2,))],
        compiler_params=_cparams(1), name="moe_combine",
    )(dest3, dest3, info, x, y)


def hier_moe_layer(x, norm_g, w_rg, w_re, w_gate, w_up, w_down, layer):
    n, d = x.shape
    w_router = jnp.concatenate(
        [w_rg, w_re, jnp.zeros((d, LANES - MOE_GROUPS - N_EXPERTS), F32)], axis=1)
    h, info, cnt = moe_router(x, norm_g, w_router)
    counts = cnt[0, ROUTE_E0:ROUTE_E0 + N_EXPERTS].astype(I32)
    eid = info[:, INFO_EID:INFO_EID + MOE_TOPK].astype(I32)
    rank = info[:, INFO_RANK:INFO_RANK + MOE_TOPK].astype(I32)
    padded = (counts + MOE_BLOCK - 1) // MOE_BLOCK * MOE_BLOCK
    pad_end = jnp.cumsum(padded)
    pad_start = pad_end - padded
    dest = pad_start[eid] + rank
    a = n * MOE_TOPK
    p = (-(-a // MOE_BLOCK) + N_EXPERTS) * MOE_BLOCK
    nblk = p // MOE_BLOCK
    blk_start = jnp.arange(nblk, dtype=I32) * MOE_BLOCK
    blk_expert = jnp.minimum(jnp.sum((pad_end[None, :] <= blk_start[:, None]).astype(I32), axis=1),
                             N_EXPERTS - 1).astype(I32)
    nused = (pad_end[-1:] // MOE_BLOCK).astype(I32)
    fill_lo = jnp.concatenate([pad_start + counts, pad_end[-1:]]).astype(I32)
    fill_hi = jnp.concatenate([pad_end, jnp.full((1,), p, I32)]).astype(I32)
    slot_tok = moe_slots(dest.reshape(-1), fill_lo, fill_hi, p)
    y = moe_experts(h, slot_tok, blk_expert, nused, w_gate, w_up, w_down, layer)
    return moe_combine(x, y, dest, info)


def _pad_lanes(v, offset, width=LANES):
    out = jnp.zeros((1, width), F32)
    return out.at[0, offset:offset + v.shape[0]].set(v.astype(F32))


def hybrid_layer(x, pos, norm_g, w_in, q_norm, k_norm, kidx_norm, conv_w, conv_b, dt_bias, a_log,
                 d_skip, ssm_norm, w_out, batch, seq, k_sel):
    n, d = x.shape
    offs = np.cumsum([0, ATTN_INNER, ATTN_KV, ATTN_KV, IDX_INNER, IDX_HEAD_DIM, IDX_HEADS,
                      SSM_INNER, SSM_CONV_DIM, SSM_HEADS])
    seg = {name: w_in[:, offs[j]:offs[j + 1]] for j, name in
           enumerate(["q", "k", "v", "qi", "ki", "wi", "z", "xbc", "dt"])}
    tail_pad = LANES - IDX_HEAD_DIM - IDX_HEADS - SSM_HEADS
    w_perm = jnp.concatenate(
        [seg["xbc"], seg["k"], seg["v"], seg["z"], seg["q"], seg["qi"], seg["ki"], seg["wi"],
         seg["dt"], jnp.zeros((d, tail_pad + HYB_COLS - COL_TAIL - LANES), F32)], axis=1).astype(BF16)
    proj = fused_matmul([x], w_perm, g=norm_g, out_dtype=F32, name="hyb_in_proj")

    rot = ATTN_HEAD_DIM // ROPE_FRACTION
    rot_i = IDX_HEAD_DIM // ROPE_FRACTION
    inv16 = jnp.power(ROPE_THETA, -(jnp.arange(rot // 2, dtype=F32) * 2.0 / rot))
    inv8 = jnp.power(ROPE_THETA, -(jnp.arange(rot_i // 2, dtype=F32) * 2.0 / rot_i))
    inv_c = jnp.concatenate(
        [inv16, inv8, jnp.zeros((LANES - rot // 2 - rot_i // 2,), F32)]).reshape(1, LANES)
    q_r, k_r, v_r, qi_r, ki_r, wi = hyb_prep(
        proj, pos, inv_c, q_norm.reshape(1, LANES), k_norm.reshape(1, LANES),
        _pad_lanes(kidx_norm, TAIL_KI))
    attn = dsa_attention(q_r, qi_r, wi, k_r, v_r, ki_r, batch, seq, k_sel)

    ssm = ssd_mixer(proj, conv_w, conv_b.reshape(1, SSM_CONV_DIM), _pad_lanes(dt_bias, TAIL_DT),
                    _pad_lanes(a_log, TAIL_DT), jnp.repeat(d_skip, SSM_HEAD_DIM).reshape(1, SSM_INNER),
                    ssm_norm.reshape(1, SSM_INNER), batch, seq)
    return fused_matmul([attn, ssm], w_out.astype(BF16), res=x, out_dtype=F32, name="hyb_out_proj")


def retention_layer(x, pos, norm_g, w_in, ret_norm, w_out, batch, seq):
    proj = fused_matmul([x], w_in.astype(BF16), g=norm_g, out_dtype=BF16, name="ret_in_proj")
    o = retention_mixer(proj, pos, ret_norm, batch, seq)
    return fused_matmul([o], w_out.astype(BF16), res=x, out_dtype=F32, name="ret_out_proj")


def kernel(x, positions, mix_norm, ffn_norm, hyb_w_in, attn_q_norm, attn_k_norm, idx_k_norm, ssm_conv_w, ssm_conv_b, ssm_dt_bias, ssm_a_log, ssm_d, ssm_norm, hyb_w_out, ret_w_in, ret_norm, ret_w_out, moe_router_group, moe_router_expert, moe_w_gate, moe_w_up, moe_w_down):
    batch, seq, d = x.shape
    depth = mix_norm.shape[0]
    k_sel = min(TOPK_MAX, seq // 4)
    n = batch * seq
    xf = x.reshape(n, d)
    pos = positions.reshape(n, 1).astype(I32)
    for layer in range(depth):
        i = layer // 2
        if layer % 2 == 0:
            xf = hybrid_layer(xf, pos, mix_norm[layer], hyb_w_in[i], attn_q_norm[i], attn_k_norm[i],
                              idx_k_norm[i], ssm_conv_w[i], ssm_conv_b[i], ssm_dt_bias[i],
                              ssm_a_log[i], ssm_d[i], ssm_norm[i], hyb_w_out[i], batch, seq, k_sel)
        else:
            xf = retention_layer(xf, pos, mix_norm[layer], ret_w_in[i], ret_norm[i], ret_w_out[i],
                                 batch, seq)
        xf = hier_moe_layer(xf, ffn_norm[layer], moe_router_group[layer], moe_router_expert[layer],
                            moe_w_gate, moe_w_up, moe_w_down, layer)
    return xf.reshape(batch, seq, d)
```

```python
import functools

import jax
import jax.numpy as jnp
import numpy as np
from jax import lax
from jax.experimental import pallas as pl
from jax.experimental.pallas import tpu as pltpu

F32 = jnp.float32
BF16 = jnp.bfloat16
I32 = jnp.int32
I16 = jnp.int16
HALF_BITS = 16
COUNT_GROUP = 4

ATTN_HEADS = 8
ATTN_KV_HEADS = 2
ATTN_GROUP = ATTN_HEADS // ATTN_KV_HEADS
ATTN_HEAD_DIM = 128
ATTN_INNER = ATTN_HEADS * ATTN_HEAD_DIM
ATTN_KV = ATTN_KV_HEADS * ATTN_HEAD_DIM
IDX_HEADS = 8
IDX_HEAD_DIM = 64
IDX_INNER = IDX_HEADS * IDX_HEAD_DIM
TOPK_MAX = 256
Q_BLOCK = 128
ROPE_THETA = 500000.0
ROPE_FRACTION = 4

SSM_HEADS = 16
SSM_HEAD_DIM = 64
SSM_INNER = SSM_HEADS * SSM_HEAD_DIM
SSM_GROUPS = 2
SSM_STATE = 128
SSM_CONV = 4
SSM_CONV_DIM = SSM_INNER + 2 * SSM_GROUPS * SSM_STATE
SSM_CHUNK = 128
SSM_GROUP_INNER = SSM_INNER // SSM_GROUPS

RET_HEADS = 8
RET_QK_DIM = 256
RET_V_DIM = 512
RET_QK_TOTAL = RET_HEADS * RET_QK_DIM
RET_V_TOTAL = RET_HEADS * RET_V_DIM
RET_CHUNK = 128
RET_THETA = 10000.0

MOE_GROUPS = 8
MOE_EXPERTS_PER_GROUP = 8
N_EXPERTS = MOE_GROUPS * MOE_EXPERTS_PER_GROUP
MOE_TOPK = 2
EXPERT_FF = 512
MOE_BLOCK = 256
COMBINE_ROWS = 128

EPS = 1e-6

LANES = 128
SUBLANES = 8
VMEM_LIMIT = 52 * 1024 * 1024

LOG2E = 1.4426950408889634
NEG_BIG = -1e30
INT_MIN = -2147483648

COL_XBC = 0
COL_K = SSM_CONV_DIM
COL_V = COL_K + ATTN_KV
COL_Z = COL_V + ATTN_KV
COL_Q = COL_Z + SSM_INNER
COL_QI = COL_Q + ATTN_INNER
COL_TAIL = COL_QI + IDX_INNER
HYB_COLS = 5120
TAIL_KI = 0
TAIL_WI = IDX_HEAD_DIM
TAIL_DT = TAIL_WI + IDX_HEADS


def _cparams(n_axes):
    return pltpu.CompilerParams(dimension_semantics=("arbitrary",) * n_axes,
                                vmem_limit_bytes=VMEM_LIMIT)


def _pick(n, pref):
    t = min(n, pref)
    while n % t:
        t //= 2
    return t


def _mm_kernel(*refs, n_a, has_norm, has_res, row_chunk):
    a_refs = refs[:n_a]
    pos = n_a
    g_ref = None
    if has_norm:
        g_ref = refs[pos]
        pos += 1
    w_refs = refs[pos:pos + n_a]
    pos += n_a
    res_ref = None
    if has_res:
        res_ref = refs[pos]
        pos += 1
    out_ref = refs[pos]
    pos += 1
    xn_ref = refs[pos] if has_norm else None
    tm = out_ref.shape[0]

    if has_norm:
        @pl.when(pl.program_id(1) == 0)
        def _():
            def body(r, carry):
                rows = pl.ds(pl.multiple_of(r * row_chunk, row_chunk), row_chunk)
                x = a_refs[0][rows, :]
                ms = jnp.mean(x * x, axis=-1, keepdims=True)
                xn_ref[rows, :] = (x * lax.rsqrt(ms + EPS) * g_ref[...]).astype(BF16)
                return carry
            lax.fori_loop(0, tm // row_chunk, body, 0)
        acc = jnp.dot(xn_ref[...], w_refs[0][...], preferred_element_type=F32)
    else:
        acc = jnp.dot(a_refs[0][...], w_refs[0][...], preferred_element_type=F32)
        for p in range(1, n_a):
            acc = acc + jnp.dot(a_refs[p][...], w_refs[p][...], preferred_element_type=F32)
    if has_res:
        acc = acc + res_ref[...]
    out_ref[...] = acc.astype(out_ref.dtype)


def fused_matmul(a_list, w, *, g=None, res=None, out_dtype=F32, tm=1024, tn=512, name="mm"):
    n = a_list[0].shape[0]
    kp = a_list[0].shape[1]
    m = w.shape[1]
    tm = _pick(n, tm)
    tn = _pick(m, tn)
    n_a = len(a_list)
    has_norm = g is not None
    has_res = res is not None
    in_specs = [pl.BlockSpec((tm, kp), lambda i, j: (i, 0)) for _ in a_list]
    args = list(a_list)
    if has_norm:
        in_specs.append(pl.BlockSpec((1, kp), lambda i, j: (0, 0)))
        args.append(g.reshape(1, kp).astype(F32))
    for p in range(n_a):
        in_specs.append(pl.BlockSpec((kp, tn), lambda i, j, p=p: (p, j)))
        args.append(w)
    if has_res:
        in_specs.append(pl.BlockSpec((tm, tn), lambda i, j: (i, j)))
        args.append(res)
    scratch = [pltpu.VMEM((tm, kp), BF16)] if has_norm else []
    kern = functools.partial(_mm_kernel, n_a=n_a, has_norm=has_norm, has_res=has_res,
                             row_chunk=_pick(tm, 128))
    return pl.pallas_call(
        kern,
        grid=(n // tm, m // tn),
        in_specs=in_specs,
        out_specs=pl.BlockSpec((tm, tn), lambda i, j: (i, j)),
        out_shape=jax.ShapeDtypeStruct((n, m), out_dtype),
        scratch_shapes=scratch,
        compiler_params=_cparams(2),
        name=name,
    )(*args)


def _rope_lanes(x, cosf, s_neg, s_pos, half):
    width = x.shape[-1]
    return (x * cosf + pltpu.roll(x, width - half, axis=1) * s_neg
            + pltpu.roll(x, half, axis=1) * s_pos)


def _prep_kernel(q_ref, k_ref, v_ref, qi_ref, tail_ref, pos_ref, invc_ref, pcos_ref, pneg_ref, ppos_ref,
                 qn_ref, kn_ref, kin_ref,
                 qo_ref, ko_ref, vo_ref, qio_ref, kio_ref, wio_ref):
    tb = q_ref.shape[0]
    posf = pos_ref[...].astype(F32)
    lane = lax.broadcasted_iota(I32, (tb, LANES), 1)

    hi = lax.Precision.HIGHEST
    ang = posf * invc_ref[...]
    cos_c = jnp.cos(ang)
    sin_c = jnp.sin(ang)
    cos_all = jnp.dot(cos_c, pcos_ref[...], precision=hi, preferred_element_type=F32)
    sneg_all = jnp.dot(sin_c, pneg_ref[...], precision=hi, preferred_element_type=F32)
    spos_all = jnp.dot(sin_c, ppos_ref[...], precision=hi, preferred_element_type=F32)

    half = ATTN_HEAD_DIM // ROPE_FRACTION // 2
    cosf = cos_all[:, 0:LANES]
    s_neg = sneg_all[:, 0:LANES]
    s_pos = spos_all[:, 0:LANES]
    scale = ATTN_HEAD_DIM ** -0.5 * LOG2E
    for h in range(ATTN_HEADS):
        x = q_ref[:, h * LANES:(h + 1) * LANES]
        ms = jnp.mean(x * x, axis=-1, keepdims=True)
        x = x * lax.rsqrt(ms + EPS) * qn_ref[...]
        x = _rope_lanes(x, cosf, s_neg, s_pos, half)
        qo_ref[h] = (x * scale).astype(BF16)
    for h in range(ATTN_KV_HEADS):
        x = k_ref[:, h * LANES:(h + 1) * LANES]
        ms = jnp.mean(x * x, axis=-1, keepdims=True)
        x = x * lax.rsqrt(ms + EPS) * kn_ref[...]
        x = _rope_lanes(x, cosf, s_neg, s_pos, half)
        ko_ref[h] = x.T.astype(BF16)
        vo_ref[h] = v_ref[:, h * LANES:(h + 1) * LANES].astype(BF16)

    half_i = IDX_HEAD_DIM // ROPE_FRACTION // 2
    cos_i = cos_all[:, LANES:LANES + IDX_INNER]
    sn_i = sneg_all[:, LANES:LANES + IDX_INNER]
    sp_i = spos_all[:, LANES:LANES + IDX_INNER]
    qi = _rope_lanes(qi_ref[...], cos_i, sn_i, sp_i, half_i)
    for h in range(IDX_HEADS):
        slab = qi[:, (h // 2) * LANES:(h // 2 + 1) * LANES]
        if h % 2:
            slab = pltpu.roll(slab, IDX_HEAD_DIM, axis=1)
        qio_ref[h] = jnp.where(lane < IDX_HEAD_DIM, slab, 0.0).astype(BF16)

    t = tail_ref[...]
    is_ki = lane < IDX_HEAD_DIM
    ms = jnp.sum(jnp.where(is_ki, t * t, 0.0), axis=-1, keepdims=True) * (1.0 / IDX_HEAD_DIM)
    kin = t * lax.rsqrt(ms + EPS) * kin_ref[...]
    kin = _rope_lanes(kin, cos_i[:, 0:LANES], sn_i[:, 0:LANES], sp_i[:, 0:LANES], half_i)
    kio_ref[...] = jnp.where(is_ki, kin, 0.0).T.astype(BF16)
    wio_ref[...] = t * (IDX_HEADS ** -0.5 * IDX_HEAD_DIM ** -0.5)


def _rope_selectors():
    a_half = ATTN_HEAD_DIM // ROPE_FRACTION // 2
    i_half = IDX_HEAD_DIM // ROPE_FRACTION // 2
    unit = LANES - 1
    width = LANES + IDX_INNER
    pcos = np.zeros((LANES, width), np.float32)
    pneg = np.zeros((LANES, width), np.float32)
    ppos = np.zeros((LANES, width), np.float32)
    for m in range(LANES):
        if m < 2 * a_half:
            pcos[m % a_half, m] = 1.0
            (pneg if m < a_half else ppos)[m % a_half, m] = -1.0 if m < a_half else 1.0
        else:
            pcos[unit, m] = 1.0
    for m in range(IDX_INNER):
        r = m % IDX_HEAD_DIM
        if r < 2 * i_half:
            pcos[a_half + r % i_half, LANES + m] = 1.0
            (pneg if r < i_half else ppos)[a_half + r % i_half, LANES + m] = -1.0 if r < i_half else 1.0
        else:
            pcos[unit, LANES + m] = 1.0
    return jnp.asarray(pcos), jnp.asarray(pneg), jnp.asarray(ppos)


def hyb_prep(proj, pos, inv_c, q_norm, k_norm, kidx_norm_pad):
    n = proj.shape[0]
    tb = _pick(n, 256)
    pcos, pneg, ppos = _rope_selectors()
    sel_w = LANES + IDX_INNER

    def col(width, off):
        return pl.BlockSpec((tb, width), lambda i: (i, off // width))

    def full(shape):
        return pl.BlockSpec(shape, lambda i: (0,) * len(shape))

    in_specs = [col(ATTN_INNER, COL_Q), col(ATTN_KV, COL_K), col(ATTN_KV, COL_V),
                col(IDX_INNER, COL_QI), col(LANES, COL_TAIL),
                pl.BlockSpec((tb, 1), lambda i: (i, 0)),
                full((1, LANES)), full((LANES, sel_w)), full((LANES, sel_w)), full((LANES, sel_w)),
                full((1, LANES)), full((1, LANES)), full((1, LANES))]
    out_shape = [jax.ShapeDtypeStruct((ATTN_HEADS, n, LANES), BF16),
                 jax.ShapeDtypeStruct((ATTN_KV_HEADS, LANES, n), BF16),
                 jax.ShapeDtypeStruct((ATTN_KV_HEADS, n, LANES), BF16),
                 jax.ShapeDtypeStruct((IDX_HEADS, n, LANES), BF16),
                 jax.ShapeDtypeStruct((LANES, n), BF16),
                 jax.ShapeDtypeStruct((n, LANES), F32)]
    out_specs = [pl.BlockSpec((ATTN_HEADS, tb, LANES), lambda i: (0, i, 0)),
                 pl.BlockSpec((ATTN_KV_HEADS, LANES, tb), lambda i: (0, 0, i)),
                 pl.BlockSpec((ATTN_KV_HEADS, tb, LANES), lambda i: (0, i, 0)),
                 pl.BlockSpec((IDX_HEADS, tb, LANES), lambda i: (0, i, 0)),
                 pl.BlockSpec((LANES, tb), lambda i: (0, i)),
                 pl.BlockSpec((tb, LANES), lambda i: (i, 0))]
    return pl.pallas_call(
        _prep_kernel, grid=(n // tb,), in_specs=in_specs, out_specs=out_specs,
        out_shape=out_shape, compiler_params=_cparams(1), name="hyb_prep",
    )(proj, proj, proj, proj, proj, pos, inv_c, pcos, pneg, ppos, q_norm, k_norm, kidx_norm_pad)


def _sortable(x):
    b = pltpu.bitcast(x, I32)
    return jnp.where(b < 0, b ^ jnp.int32(0x7FFFFFFF), b)


def _dsa_kernel(q_ref, qi_ref, wi_ref, k_ref, v_ref, ki_ref, triu_ref, eye_ref, onec_ref, out_ref,
                keys_ref, half_ref, bias_ref, wb_ref, mx_ref, acc_ref, *, k_sel, kc):
    i = pl.program_id(1)
    tq = Q_BLOCK
    nkc = (i * tq + tq + kc - 1) // kc
    row = i * tq + lax.broadcasted_iota(I32, (tq, kc), 0)
    lane = lax.broadcasted_iota(I32, (tq, kc), 1)
    nslab = kc // LANES

    w = wi_ref[...]
    for h in range(IDX_HEADS):
        wb_ref[h] = jnp.broadcast_to(w[:, TAIL_WI + h:TAIL_WI + h + 1], (tq, kc))

    def idx_body(c, carry):
        cols = pl.ds(pl.multiple_of(c * kc, kc), kc)
        d = jnp.dot(qi_ref[...].reshape(IDX_HEADS * tq, LANES), ki_ref[:, cols],
                    preferred_element_type=F32)
        sc = jnp.zeros((tq, kc), F32)
        for h in range(IDX_HEADS):
            sc = sc + jnp.maximum(d[h * tq:(h + 1) * tq, :], 0.0) * wb_ref[h]
        causal = (c * kc + lane) <= row
        keys_ref[:, cols] = jnp.where(causal, _sortable(sc), jnp.int32(INT_MIN))
        return carry
    lax.fori_loop(0, nkc, idx_body, 0)

    def fold(x, op):
        acc = x[:, 0:LANES]
        for j in range(1, nslab):
            acc = op(acc, x[:, j * LANES:(j + 1) * LANES])
        return acc

    i16_min = -(1 << (HALF_BITS - 1))

    def count16(cand, strict):
        c16 = cand.astype(I16)

        def hits(start, width):
            kk = half_ref[:, pl.ds(pl.multiple_of(start, kc), width)]
            hit = (kk > c16) if strict else (kk >= c16)
            x = jnp.where(hit, jnp.int16(1), jnp.int16(0))
            acc = x[:, 0:LANES]
            for j in range(1, width // LANES):
                acc = acc + x[:, j * LANES:(j + 1) * LANES]
            return acc

        ngrp = nkc // COUNT_GROUP
        acc = lax.fori_loop(0, ngrp, lambda g, a: a + hits(g * (COUNT_GROUP * kc), COUNT_GROUP * kc),
                            jnp.zeros((tq, LANES), I16))
        acc = lax.fori_loop(ngrp * COUNT_GROUP, nkc, lambda c, a: a + hits(c * kc, kc), acc)
        return jnp.sum(acc.astype(F32), axis=1, keepdims=True)

    def kth_largest16(kf):
        zero = jnp.zeros((tq, 1), I32)
        lo = jnp.where(count16(zero, False) >= kf, zero, jnp.full((tq, 1), i16_min, I32))

        def bis_body(it, lo):
            cand = lo + jnp.left_shift(jnp.int32(1), jnp.int32(HALF_BITS - 2) - it)
            return jnp.where(count16(cand, False) >= kf, cand, lo)
        return lax.fori_loop(0, HALF_BITS - 1, bis_body, lo)

    def fill_half(fn):
        def body(c, carry):
            cols = pl.ds(pl.multiple_of(c * kc, kc), kc)
            half_ref[:, cols] = fn(keys_ref[:, cols]).astype(I16)
            return carry
        lax.fori_loop(0, nkc, body, 0)

    kf = jnp.float32(k_sel)
    fill_half(lambda kk: jnp.right_shift(kk, HALF_BITS))
    thr_hi = kth_largest16(kf)
    above = count16(thr_hi, True)
    low_mask = (1 << HALF_BITS) - 1
    fill_half(lambda kk: jnp.where(jnp.right_shift(kk, HALF_BITS) == thr_hi,
                                   (kk & low_mask) + i16_min, i16_min))
    thr_lo = kth_largest16(kf - above)
    thr = jnp.left_shift(thr_hi, HALF_BITS) + (thr_lo - i16_min)
    cnt_gt_lo = count16(thr_lo, True)
    need = kf - above - cnt_gt_lo
    n_eq = count16(thr_lo, False) - cnt_gt_lo
    row_plain = jnp.logical_or(jnp.logical_and(n_eq == need, thr_lo > i16_min),
                               thr == jnp.int32(INT_MIN))
    all_plain = jnp.min(jnp.where(row_plain, 1.0, 0.0)) > 0.5

    @pl.when(all_plain)
    def _():
        def fin_body(c, carry):
            cols = pl.ds(pl.multiple_of(c * kc, kc), kc)
            causal = (c * kc + lane) <= row
            sel = (keys_ref[:, cols] >= thr) & causal
            bias_ref[:, cols] = jnp.where(sel, 0.0, NEG_BIG).astype(BF16)
            return carry
        lax.fori_loop(0, nkc, fin_body, 0)

    @pl.when(jnp.logical_not(all_plain))
    def _():
        def fin_body(c, run):
            cols = pl.ds(pl.multiple_of(c * kc, kc), kc)
            kk = keys_ref[:, cols]
            eq = kk == thr
            eqf = jnp.where(eq, 1.0, 0.0)
            incl = jnp.dot(eqf.astype(BF16), triu_ref[...], preferred_element_type=F32)
            sel = (kk > thr) | (eq & ((run + incl) <= need))
            causal = (c * kc + lane) <= row
            bias_ref[:, cols] = jnp.where(sel & causal, 0.0, NEG_BIG).astype(BF16)
            return run + jnp.sum(eqf, axis=1, keepdims=True)
        lax.fori_loop(0, nkc, fin_body, jnp.zeros((tq, 1), F32))

    gq = ATTN_GROUP * tq

    def logits(j, kx):
        qg = q_ref[j * ATTN_GROUP:(j + 1) * ATTN_GROUP].reshape(gq, LANES)
        qx = jnp.concatenate([qg, eye_ref[...]], axis=1)
        return jnp.dot(qx, kx, preferred_element_type=F32)

    mx_ref[...] = jnp.full(mx_ref.shape, NEG_BIG, F32)

    def max_body(c, carry):
        cols = pl.ds(pl.multiple_of(c * kc, kc), kc)
        bt = bias_ref[:, cols]
        for j in range(ATTN_KV_HEADS):
            kx = jnp.concatenate([k_ref[j, :, cols], bt], axis=0)
            mx_ref[j] = jnp.maximum(mx_ref[j], fold(logits(j, kx), jnp.maximum))
        return carry
    lax.fori_loop(0, nkc, max_body, 0)
    for j in range(ATTN_KV_HEADS):
        mx_ref[j] = jnp.broadcast_to(jnp.max(mx_ref[j], axis=1, keepdims=True), (gq, LANES))

    acc_ref[...] = jnp.zeros(acc_ref.shape, F32)

    def att_body(c, carry):
        cols = pl.ds(pl.multiple_of(c * kc, kc), kc)
        bt = bias_ref[:, cols]
        for j in range(ATTN_KV_HEADS):
            kx = jnp.concatenate([k_ref[j, :, cols], bt], axis=0)
            vx = jnp.concatenate([v_ref[j, cols, :], onec_ref[...]], axis=1)
            s = logits(j, kx)
            mb = mx_ref[j]
            p = jnp.concatenate(
                [jnp.exp2(s[:, a * LANES:(a + 1) * LANES] - mb) for a in range(nslab)], axis=1)
            acc_ref[j] = acc_ref[j] + jnp.dot(p.astype(BF16), vx, preferred_element_type=F32)
        return carry
    lax.fori_loop(0, nkc, att_body, 0)

    for h in range(ATTN_HEADS):
        a = acc_ref[h // ATTN_GROUP, (h % ATTN_GROUP) * tq:(h % ATTN_GROUP + 1) * tq, :]
        out_ref[:, h * LANES:(h + 1) * LANES] = (
            a[:, 0:LANES] / a[:, LANES:LANES + 1]).astype(out_ref.dtype)


def dsa_attention(q_r, qi_r, wi, k_r, v_r, ki_r, batch, seq, k_sel):
    n = batch * seq
    nq = seq // Q_BLOCK
    kc = _pick(seq, 1024)
    triu =jnp.asarray((np.arange(kc)[:, None] <= np.arange(kc)[None, :]).astype(np.float32), BF16)
    gq = ATTN_GROUP * Q_BLOCK
    eye = jnp.asarray(np.tile(np.eye(Q_BLOCK, dtype=np.float32), (ATTN_GROUP, 1)), BF16)
    onec = np.zeros((kc, LANES), np.float32)
    onec[:, 0] = 1.0
    onec = jnp.asarray(onec, BF16)
    kern = functools.partial(_dsa_kernel, k_sel=k_sel, kc=kc)
    in_specs = [
        pl.BlockSpec((ATTN_HEADS, Q_BLOCK, LANES), lambda b, i: (0, b * nq + i, 0)),
        pl.BlockSpec((IDX_HEADS, Q_BLOCK, LANES), lambda b, i: (0, b * nq + i, 0)),
        pl.BlockSpec((Q_BLOCK, LANES), lambda b, i: (b * nq + i, 0)),
        pl.BlockSpec((ATTN_KV_HEADS, LANES, seq), lambda b, i: (0, 0, b)),
        pl.BlockSpec((ATTN_KV_HEADS, seq, LANES), lambda b, i: (0, b, 0)),
        pl.BlockSpec((LANES, seq), lambda b, i: (0, b)),
        pl.BlockSpec((kc, kc), lambda b, i: (0, 0)),
        pl.BlockSpec((gq, Q_BLOCK), lambda b, i: (0, 0)),
        pl.BlockSpec((kc, LANES), lambda b, i: (0, 0)),
    ]
    scratch = [pltpu.VMEM((Q_BLOCK, seq), I32), pltpu.VMEM((Q_BLOCK, seq), I16),
               pltpu.VMEM((Q_BLOCK, seq), BF16),
               pltpu.VMEM((IDX_HEADS, Q_BLOCK, kc), F32),
               pltpu.VMEM((ATTN_KV_HEADS, gq, LANES), F32),
               pltpu.VMEM((ATTN_KV_HEADS, gq, 2 * LANES), F32)]
    return pl.pallas_call(
        kern, grid=(batch, nq), in_specs=in_specs,
        out_specs=pl.BlockSpec((Q_BLOCK, ATTN_INNER), lambda b, i: (b * nq + i, 0)),
        out_shape=jax.ShapeDtypeStruct((n, ATTN_INNER), BF16),
        scratch_shapes=scratch, compiler_params=_cparams(2), name="dsa_attention",
    )(q_r, qi_r, wi, k_r, v_r, ki_r, triu, eye, onec)


def _silu(x):
    return x * (1.0 / (1.0 + jnp.exp(-x)))


def _ssd_kernel(xbc_ref, z_ref, tail_ref, cw_ref, cb_ref, dtb_ref, alog_ref, dexp_ref, ng_ref,
                tri_ref, e_ref, out_ref, state_ref, ext_ref, y_ref):
    q = SSM_CHUNK
    hi = lax.Precision.HIGHEST

    @pl.when(pl.program_id(1) == 0)
    def _():
        state_ref[...] = jnp.zeros(state_ref.shape, F32)
        ext_ref[0:SUBLANES, :] = jnp.zeros((SUBLANES, SSM_CONV_DIM), F32)

    x = xbc_ref[...]
    ext_ref[SUBLANES:SUBLANES + q, :] = x
    conv = x * cw_ref[SSM_CONV - 1:SSM_CONV, :] + cb_ref[...]
    for k in range(1, SSM_CONV):
        conv = conv + ext_ref[SUBLANES - k:SUBLANES - k + q, :] * cw_ref[SSM_CONV - 1 - k:SSM_CONV - k, :]
    ext_ref[0:SUBLANES, :] = x[q - SUBLANES:q, :]
    xbc = _silu(conv)
    xs = xbc[:, 0:SSM_INNER]
    bm = xbc[:, SSM_INNER:SSM_INNER + SSM_GROUPS * SSM_STATE]
    cm = xbc[:, SSM_INNER + SSM_GROUPS * SSM_STATE:SSM_CONV_DIM]

    lane = lax.broadcasted_iota(I32, (q, LANES), 1)
    is_dt = (lane >= TAIL_DT) & (lane < TAIL_DT + SSM_HEADS)
    raw = tail_ref[...] + dtb_ref[...]
    dt = jnp.where(is_dt, jnp.maximum(raw, 0.0) + jnp.log1p(jnp.exp(-jnp.abs(raw))), 0.0)
    a = -jnp.exp(alog_ref[...])
    da = jnp.where(is_dt, dt * a, 0.0)
    acum = jnp.dot(tri_ref[...], da, precision=hi, preferred_element_type=F32)
    acum_t = acum.T
    acum_x = jnp.dot(acum, e_ref[...], precision=hi, preferred_element_type=F32)
    dt_x = jnp.dot(dt, e_ref[...], precision=hi, preferred_element_type=F32)
    last_x = acum_x[q - 1:q, :]
    xdt = xs * dt_x
    xdec = xdt * jnp.exp(last_x - acum_x)
    ri = lax.broadcasted_iota(I32, (q, q), 0)
    ci = lax.broadcasted_iota(I32, (q, q), 1)
    tril = ri >= ci
    pairs_per_group = SSM_GROUP_INNER // LANES

    for g in range(SSM_GROUPS):
        bg = bm[:, g * SSM_STATE:(g + 1) * SSM_STATE]
        cg = cm[:, g * SSM_STATE:(g + 1) * SSM_STATE]
        cgb = cg.astype(BF16)
        gsl = slice(g * SSM_GROUP_INNER, (g + 1) * SSM_GROUP_INNER)
        cb = lax.dot_general(cgb, bg.astype(BF16), (((1,), (1,)), ((), ())),
                             preferred_element_type=F32)
        y_off = jnp.dot(cgb, state_ref[g].astype(BF16), preferred_element_type=F32)
        y_ref[:, gsl] = y_off * jnp.exp(acum_x[:, gsl])
        for mth in range(pairs_per_group):
            slab = slice((g * pairs_per_group + mth) * LANES, (g * pairs_per_group + mth + 1) * LANES)
            xp = xdt[:, slab]
            yd = None
            for side in range(2):
                hl = TAIL_DT + (g * pairs_per_group + mth) * 2 + side
                seg = acum[:, hl:hl + 1] - acum_t[hl:hl + 1, :]
                lmat = jnp.exp(jnp.where(tril, seg, NEG_BIG))
                if side == 0:
                    xh = jnp.where(lane < SSM_HEAD_DIM, xp, 0.0)
                else:
                    xh = jnp.where(lane >= SSM_HEAD_DIM, xp, 0.0)
                part = jnp.dot((cb * lmat).astype(BF16), xh.astype(BF16), preferred_element_type=F32)
                yd = part if yd is None else yd + part
            y_ref[:, slab] = y_ref[:, slab] + yd
        st = jnp.dot(bg.T.astype(BF16), xdec[:, gsl].astype(BF16), preferred_element_type=F32)
        state_ref[g] = state_ref[g] * jnp.exp(last_x[:, gsl]) + st

    y = (y_ref[...] + dexp_ref[...] * xs) * _silu(z_ref[...])
    for g in range(SSM_GROUPS):
        gsl = slice(g * SSM_GROUP_INNER, (g + 1) * SSM_GROUP_INNER)
        yg = y[:, gsl]
        ms = jnp.mean(yg * yg, axis=-1, keepdims=True)
        out_ref[:, gsl] = (yg * lax.rsqrt(ms + EPS) * ng_ref[:, gsl]).astype(out_ref.dtype)


def ssd_mixer(proj, conv_w, conv_b, dtb_pad, alog_pad, d_exp, norm_g, batch, seq):
    n = batch * seq
    q = SSM_CHUNK
    nc = seq // q
    tri = jnp.asarray((np.arange(q)[:, None] >= np.arange(q)[None, :]).astype(np.float32))
    e = np.zeros((LANES, SSM_INNER), np.float32)
    for h in range(SSM_HEADS):
        e[TAIL_DT + h, h * SSM_HEAD_DIM:(h + 1) * SSM_HEAD_DIM] = 1.0
    e = jnp.asarray(e)

    def col(width, off):
        return pl.BlockSpec((q, width), lambda b, c: (b * nc + c, off // width))

    def full(shape):
        return pl.BlockSpec(shape, lambda b, c: (0,) * len(shape))

    in_specs = [col(SSM_CONV_DIM, COL_XBC), col(SSM_INNER, COL_Z), col(LANES, COL_TAIL),
                full((SSM_CONV, SSM_CONV_DIM)), full((1, SSM_CONV_DIM)), full((1, LANES)),
                full((1, LANES)), full((1, SSM_INNER)), full((1, SSM_INNER)),
                full((q, q)), full((LANES, SSM_INNER))]
    scratch = [pltpu.VMEM((SSM_GROUPS, SSM_STATE, SSM_GROUP_INNER), F32),
               pltpu.VMEM((SUBLANES + q, SSM_CONV_DIM), F32),
               pltpu.VMEM((q, SSM_INNER), F32)]
    return pl.pallas_call(
        _ssd_kernel, grid=(batch, nc), in_specs=in_specs,
        out_specs=pl.BlockSpec((q, SSM_INNER), lambda b, c: (b * nc + c, 0)),
        out_shape=jax.ShapeDtypeStruct((n, SSM_INNER), BF16),
        scratch_shapes=scratch, compiler_params=_cparams(2), name="ssd_mixer",
    )(proj, proj, proj, conv_w, conv_b, dtb_pad, alog_pad, d_exp, norm_g, tri, e)


def _ret_kernel(q_ref, k_ref, v_ref, g_ref, pos_ref, inv_ref, dmat_ref, qdec_ref, kdec_ref,
                cdec_ref, ng_ref, out_ref, state_ref):
    half = RET_QK_DIM // 2

    @pl.when(pl.program_id(1) == 0)
    def _():
        state_ref[...] = jnp.zeros(state_ref.shape, F32)

    ang = pos_ref[...].astype(F32) * inv_ref[...]
    cosf = jnp.cos(ang)
    sinf = jnp.sin(ang)

    def rope(x):
        x1 = x[:, 0:half]
        x2 = x[:, half:2 * half]
        return jnp.concatenate([x1 * cosf - x2 * sinf, x2 * cosf + x1 * sinf], axis=-1)

    for h in range(RET_HEADS):
        qh = rope(q_ref[:, h * RET_QK_DIM:(h + 1) * RET_QK_DIM].astype(F32))
        kh = rope(k_ref[:, h * RET_QK_DIM:(h + 1) * RET_QK_DIM].astype(F32)) * (RET_QK_DIM ** -0.5)
        vh = v_ref[:, h * RET_V_DIM:(h + 1) * RET_V_DIM]
        qb = qh.astype(BF16)
        s = lax.dot_general(qb, kh.astype(BF16), (((1,), (1,)), ((), ())),
                            preferred_element_type=F32) * dmat_ref[h]
        inner = jnp.dot(s.astype(BF16), vh, preferred_element_type=F32)
        st = state_ref[h]
        cross = jnp.dot(qb, st.astype(BF16), preferred_element_type=F32) * qdec_ref[h]
        kd = (kh * kdec_ref[h]).T.astype(BF16)
        state_ref[h] = st * cdec_ref[h] + jnp.dot(kd, vh, preferred_element_type=F32)
        o = inner + cross
        ms = jnp.mean(o * o, axis=-1, keepdims=True)
        o = o * lax.rsqrt(ms + EPS) * ng_ref[:, h * RET_V_DIM:(h + 1) * RET_V_DIM]
        gate = _silu(g_ref[:, h * RET_V_DIM:(h + 1) * RET_V_DIM].astype(F32))
        out_ref[:, h * RET_V_DIM:(h + 1) * RET_V_DIM] = (gate * o).astype(out_ref.dtype)


def retention_mixer(proj, pos, norm_g, batch, seq):
    n = batch * seq
    q = RET_CHUNK
    nc = seq // q
    half = RET_QK_DIM // 2
    inv = jnp.power(RET_THETA, -(jnp.arange(half, dtype=F32) * 2.0 / RET_QK_DIM)).reshape(1, half)
    log_gamma = jnp.log(1.0 - jnp.power(2.0, -5.0 - jnp.arange(RET_HEADS, dtype=F32)))
    i = jnp.arange(q, dtype=F32)
    diff = i[:, None] - i[None, :]
    dmat = jnp.where(diff[None] >= 0,
                     jnp.exp(jnp.maximum(diff, 0.0)[None] * log_gamma[:, None, None]), 0.0)
    qdec = jnp.exp((i + 1.0)[None, :, None] * log_gamma[:, None, None])
    kdec = jnp.exp((q - 1.0 - i)[None, :, None] * log_gamma[:, None, None])
    cdec = jnp.broadcast_to(jnp.exp(q * log_gamma)[:, None, None], (RET_HEADS, 1, RET_V_DIM))

    def col(width, off):
        return pl.BlockSpec((q, width), lambda b, c: (b * nc + c, off // width))

    def full(shape):
        return pl.BlockSpec(shape, lambda b, c: (0,) * len(shape))

    in_specs = [col(RET_QK_TOTAL, 0), col(RET_QK_TOTAL, RET_QK_TOTAL),
                col(RET_V_TOTAL, 2 * RET_QK_TOTAL), col(RET_V_TOTAL, 2 * RET_QK_TOTAL + RET_V_TOTAL),
                pl.BlockSpec((q, 1), lambda b, c: (b * nc + c, 0)),
                full((1, half)), full((RET_HEADS, q, q)), full((RET_HEADS, q, 1)),
                full((RET_HEADS, q, 1)), full((RET_HEADS, 1, RET_V_DIM)), full((1, RET_V_TOTAL))]
    return pl.pallas_call(
        _ret_kernel, grid=(batch, nc), in_specs=in_specs,
        out_specs=pl.BlockSpec((q, RET_V_TOTAL), lambda b, c: (b * nc + c, 0)),
        out_shape=jax.ShapeDtypeStruct((n, RET_V_TOTAL), BF16),
        scratch_shapes=[pltpu.VMEM((RET_HEADS, RET_QK_DIM, RET_V_DIM), F32)],
        compiler_params=_cparams(2), name="retention",
    )(proj, proj, proj, proj, pos, inv, dmat, qdec, kdec, cdec, norm_g.reshape(1, RET_V_TOTAL))


ROUTE_E0 = MOE_GROUPS
INFO_EID, INFO_GATE, INFO_RANK = 0, 2, 4


def _router_kernel(x_ref, g_ref, wr_ref, lt_ref, h_ref, info_ref, cnt_ref, run_ref):
    tb = x_ref.shape[0]

    @pl.when(pl.program_id(0) == 0)
    def _():
        run_ref[...] = jnp.zeros(run_ref.shape, F32)

    x = x_ref[...]
    ms = jnp.mean(x * x, axis=-1, keepdims=True)
    hn = x * lax.rsqrt(ms + EPS) * g_ref[...]
    h_ref[...] = hn
    h_hi = hn.astype(BF16)
    h_lo = (hn - h_hi.astype(F32)).astype(BF16)
    both = jnp.dot(h_hi, wr_ref[...], preferred_element_type=F32)
    logits = (both[:, 0:LANES] + both[:, LANES:2 * LANES]
              + jnp.dot(h_lo, wr_ref[:, 0:LANES], preferred_element_type=F32))

    lane = lax.broadcasted_iota(I32, (tb, LANES), 1).astype(F32)
    far = jnp.float32(4 * LANES)
    ninf = jnp.float32(-jnp.inf)
    gl = jnp.where(lane < MOE_GROUPS, logits, ninf)
    gmax = jnp.max(gl, axis=1, keepdims=True)
    grp = jnp.min(jnp.where(gl == gmax, lane, far), axis=1, keepdims=True)
    p_grp = 1.0 / jnp.sum(jnp.exp(gl - gmax), axis=1, keepdims=True)
    lo = ROUTE_E0 + grp * MOE_EXPERTS_PER_GROUP
    el = jnp.where((lane >= lo) & (lane < lo + MOE_EXPERTS_PER_GROUP), logits, ninf)
    m1 = jnp.max(el, axis=1, keepdims=True)
    i1 = jnp.min(jnp.where(el == m1, lane, far), axis=1, keepdims=True)
    el2 = jnp.where(lane == i1, ninf, el)
    m2 = jnp.max(el2, axis=1, keepdims=True)
    i2 = jnp.min(jnp.where(el2 == m2, lane, far), axis=1, keepdims=True)
    e2 = jnp.exp(m2 - m1)
    g1 = p_grp / (1.0 + e2)
    g2 = p_grp * e2 / (1.0 + e2)
    oh1 = jnp.where(lane == i1, 1.0, 0.0)
    oh2 = jnp.where(lane == i2, 1.0, 0.0)
    cnt = oh1 + oh2
    before = jnp.dot(lt_ref[...], cnt.astype(BF16), preferred_element_type=F32) + run_ref[...]
    r1 = jnp.sum(oh1 * before, axis=1, keepdims=True)
    r2 = jnp.sum(oh2 * before, axis=1, keepdims=True)
    run_ref[...] = run_ref[...] + jnp.sum(cnt, axis=0, keepdims=True)
    cnt_ref[...] = run_ref[...]

    info = jnp.where(lane == INFO_EID, i1 - ROUTE_E0, 0.0)
    info = jnp.where(lane == INFO_EID + 1, i2 - ROUTE_E0, info)
    info = jnp.where(lane == INFO_GATE, g1, info)
    info = jnp.where(lane == INFO_GATE + 1, g2, info)
    info = jnp.where(lane == INFO_RANK, r1, info)
    info = jnp.where(lane == INFO_RANK + 1, r2, info)
    info_ref[...] = info


def moe_router(x, g, w_router_pad):
    n, d = x.shape
    tb = _pick(n, 512)
    lt = jnp.asarray((np.arange(tb)[:, None] > np.arange(tb)[None, :]).astype(np.float32), BF16)
    w_hi = w_router_pad.astype(BF16)
    w_lo = (w_router_pad - w_hi.astype(F32)).astype(BF16)
    w_split = jnp.concatenate([w_hi, w_lo], axis=1)
    return pl.pallas_call(
        _router_kernel, grid=(n // tb,),
        in_specs=[pl.BlockSpec((tb, d), lambda i: (i, 0)), pl.BlockSpec((1, d), lambda i: (0, 0)),
                  pl.BlockSpec((d, 2 * LANES), lambda i: (0, 0)), pl.BlockSpec((tb, tb), lambda i: (0, 0))],
        out_specs=[pl.BlockSpec((tb, d), lambda i: (i, 0)), pl.BlockSpec((tb, LANES), lambda i: (i, 0)),
                   pl.BlockSpec((1, LANES), lambda i: (0, 0))],
        out_shape=[jax.ShapeDtypeStruct((n, d), F32), jax.ShapeDtypeStruct((n, LANES), F32),
                   jax.ShapeDtypeStruct((1, LANES), F32)],
        scratch_shapes=[pltpu.VMEM((1, LANES), F32)],
        compiler_params=_cparams(1), name="moe_router",
    )(x, g.reshape(1, d), w_split, lt)


def _row_copy(src_ref, src_row, dst_ref, dst_row, sem):
    return pltpu.make_async_copy(src_ref.at[pl.ds(src_row, 1), :], dst_ref.at[pl.ds(dst_row, 1), :], sem)


def _expert_kernel(be_ref, nused_ref, st_ref, h_hbm, wg_ref, wu_ref, wd_ref, out_ref,
                   xbuf, sem, wgu_s, wd_s):
    i = pl.program_id(0)
    slot = i % 2
    nused = nused_ref[0]
    d = xbuf.shape[2]
    ff = EXPERT_FF

    def gather(blk, s):
        def body(r, carry):
            _row_copy(h_hbm, st_ref[blk * MOE_BLOCK + r], xbuf.at[s], r, sem.at[s]).start(priority=1)
            return carry
        lax.fori_loop(0, MOE_BLOCK, body, 0, unroll=8)

    @pl.when(jnp.logical_and(i == 0, nused > 0))
    def _():
        gather(0, 0)

    @pl.when(i + 1 < nused)
    def _():
        gather(i + 1, 1 - slot)

    @pl.when(i < nused)
    def _():
        def drain(r, carry):
            _row_copy(h_hbm, 0, xbuf.at[slot], 0, sem.at[slot]).wait()
            return carry
        lax.fori_loop(0, MOE_BLOCK, drain, 0, unroll=8)

        changed = jnp.logical_or(i == 0, be_ref[i] != be_ref[jnp.maximum(i - 1, 0)])

        @pl.when(changed)
        def _():
            rc = _pick(d, 256)

            def body(r, carry):
                rows = pl.ds(pl.multiple_of(r * rc, rc), rc)
                wgu_s[rows, 0:ff] = wg_ref[0, 0, rows, :].astype(BF16)
                wgu_s[rows, ff:2 * ff] = wu_ref[0, 0, rows, :].astype(BF16)
                return carry
            lax.fori_loop(0, d // rc, body, 0)
            wd_s[...] = wd_ref[0, 0].astype(BF16)

        gu = jnp.dot(xbuf[slot].astype(BF16), wgu_s[...], preferred_element_type=F32)
        hid = _silu(gu[:, 0:ff]) * gu[:, ff:2 * ff]
        out_ref[...] = jnp.dot(hid.astype(BF16), wd_s[...], preferred_element_type=F32)

    @pl.when(i >= nused)
    def _():
        out_ref[...] = jnp.zeros(out_ref.shape, out_ref.dtype)


def moe_experts(h, slot_tok, blk_expert, nused, w_gate, w_up, w_down, layer):
    n, d = h.shape
    p = slot_tok.shape[0]
    nblk = p // MOE_BLOCK
    ff = EXPERT_FF
    grid_spec = pltpu.PrefetchScalarGridSpec(
        num_scalar_prefetch=3, grid=(nblk,),
        in_specs=[pl.BlockSpec(memory_space=pl.ANY),
                  pl.BlockSpec((1, 1, d, ff), lambda i, be, nu, st: (layer, be[i], 0, 0)),
                  pl.BlockSpec((1, 1, d, ff), lambda i, be, nu, st: (layer, be[i], 0, 0)),
                  pl.BlockSpec((1, 1, ff, d), lambda i, be, nu, st: (layer, be[i], 0, 0))],
        out_specs=pl.BlockSpec((MOE_BLOCK, d), lambda i, be, nu, st: (i, 0)),
        scratch_shapes=[pltpu.VMEM((2, MOE_BLOCK, d), h.dtype), pltpu.SemaphoreType.DMA((2,)),
                        pltpu.VMEM((d, 2 * ff), BF16), pltpu.VMEM((ff, d), BF16)])
    return pl.pallas_call(
        _expert_kernel, grid_spec=grid_spec,
        out_shape=jax.ShapeDtypeStruct((p, d), F32),
        compiler_params=_cparams(1), name="moe_experts",
    )(blk_expert, nused, slot_tok, h, w_gate, w_up, w_down)


def _slots_kernel(dest_ref, slot_ref):
    def clear(s, carry):
        slot_ref[s] = 0
        return carry
    lax.fori_loop(0, slot_ref.shape[0], clear, 0, unroll=16)

    def body(t, carry):
        for k in range(MOE_TOPK):
            slot_ref[dest_ref[t * MOE_TOPK + k]] = t
        return carry
    lax.fori_loop(0, dest_ref.shape[0] // MOE_TOPK, body, 0, unroll=8)


def moe_slots(dest_flat, p):
    smem = pl.BlockSpec(memory_space=pltpu.SMEM)
    return pl.pallas_call(
        _slots_kernel,
        in_specs=[smem],
        out_specs=smem,
        out_shape=jax.ShapeDtypeStruct((p,), I32),
        name="moe_slots",
    )(dest_flat)


def _combine_kernel(dcur_ref, dnxt_ref, info_ref, x_ref, y_hbm, out_ref, ybuf, sem, *, tb, nsteps):
    i = pl.program_id(0)
    slot = i % 2

    def issue(dref, s):
        def body(r, carry):
            for k in range(MOE_TOPK):
                _row_copy(y_hbm, dref[0, 0, MOE_TOPK * r + k], ybuf.at[s, k], r, sem.at[s]).start(
                    priority=k % 2)
            return carry
        lax.fori_loop(0, tb, body, 0, unroll=8)

    @pl.when(i == 0)
    def _():
        issue(dcur_ref, 0)

    @pl.when(i + 1 < nsteps)
    def _():
        issue(dnxt_ref, 1 - slot)

    def drain(r, carry):
        for k in range(MOE_TOPK):
            _row_copy(y_hbm, 0, ybuf.at[slot, k], 0, sem.at[slot]).wait()
        return carry
    lax.fori_loop(0, tb, drain, 0, unroll=8)

    info = info_ref[...]
    out = x_ref[...]
    for k in range(MOE_TOPK):
        out = out + ybuf[slot, k] * info[:, INFO_GATE + k:INFO_GATE + k + 1]
    out_ref[...] = out


def moe_combine(x, y, dest, info):
    n, d = x.shape
    tb = _pick(n, COMBINE_ROWS)
    nsteps = n // tb
    kern = functools.partial(_combine_kernel, tb=tb, nsteps=nsteps)
    dest3 = dest.reshape(nsteps, 1, MOE_TOPK * tb)
    return pl.pallas_call(
        kern, grid=(nsteps,),
        in_specs=[pl.BlockSpec((1, 1, MOE_TOPK * tb), lambda i: (i, 0, 0), memory_space=pltpu.SMEM),
                  pl.BlockSpec((1, 1, MOE_TOPK * tb), lambda i: (jnp.minimum(i + 1, nsteps - 1), 0, 0),
                               memory_space=pltpu.SMEM),
                  pl.BlockSpec((tb, LANES), lambda i: (i, 0)),
                  pl.BlockSpec((tb, d), lambda i: (i, 0)),
                  pl.BlockSpec(memory_space=pl.ANY)],
        out_specs=pl.BlockSpec((tb, d), lambda i: (i, 0)),
        out_shape=jax.ShapeDtypeStruct((n, d), x.dtype),
        scratch_shapes=[pltpu.VMEM((2, MOE_TOPK, tb, d), y.dtype), pltpu.SemaphoreType.DMA((2,))],
        compiler_params=_cparams(1), name="moe_combine",
    )(dest3, dest3, info, x, y)


def hier_moe_layer(x, norm_g, w_rg, w_re, w_gate, w_up, w_down, layer):
    n, d = x.shape
    w_router = jnp.concatenate(
        [w_rg, w_re, jnp.zeros((d, LANES - MOE_GROUPS - N_EXPERTS), F32)], axis=1)
    h, info, cnt = moe_router(x, norm_g, w_router)
    counts = cnt[0, ROUTE_E0:ROUTE_E0 + N_EXPERTS].astype(I32)
    eid = info[:, INFO_EID:INFO_EID + MOE_TOPK].astype(I32)
    rank = info[:, INFO_RANK:INFO_RANK + MOE_TOPK].astype(I32)
    padded = (counts + MOE_BLOCK - 1) // MOE_BLOCK * MOE_BLOCK
    pad_end = jnp.cumsum(padded)
    pad_start = pad_end - padded
    onehot = (eid[..., None] == jnp.arange(N_EXPERTS, dtype=I32)).astype(F32)
    dest = jnp.dot(onehot, pad_start.astype(F32), precision=lax.Precision.HIGHEST).astype(I32) + rank
    a = n * MOE_TOPK
    p = (-(-a // MOE_BLOCK) + N_EXPERTS) * MOE_BLOCK
    nblk = p // MOE_BLOCK
    blk_start = jnp.arange(nblk, dtype=I32) * MOE_BLOCK
    blk_expert = jnp.minimum(jnp.sum((pad_end[None, :] <= blk_start[:, None]).astype(I32), axis=1),
                             N_EXPERTS - 1).astype(I32)
    nused = (pad_end[-1:] // MOE_BLOCK).astype(I32)
    slot_tok = moe_slots(dest.reshape(-1), p)
    y = moe_experts(h, slot_tok, blk_expert, nused, w_gate, w_up, w_down, layer)
    return moe_combine(x, y, dest, info)


def _pad_lanes(v, offset, width=LANES):
    out = jnp.zeros((1, width), F32)
    return out.at[0, offset:offset + v.shape[0]].set(v.astype(F32))


def hybrid_layer(x, pos, norm_g, w_in, q_norm, k_norm, kidx_norm, conv_w, conv_b, dt_bias, a_log,
                 d_skip, ssm_norm, w_out, batch, seq, k_sel):
    n, d = x.shape
    offs = np.cumsum([0, ATTN_INNER, ATTN_KV, ATTN_KV, IDX_INNER, IDX_HEAD_DIM, IDX_HEADS,
                      SSM_INNER, SSM_CONV_DIM, SSM_HEADS])
    seg = {name: w_in[:, offs[j]:offs[j + 1]] for j, name in
           enumerate(["q", "k", "v", "qi", "ki", "wi", "z", "xbc", "dt"])}
    tail_pad = LANES - IDX_HEAD_DIM - IDX_HEADS - SSM_HEADS
    w_perm = jnp.concatenate(
        [seg["xbc"], seg["k"], seg["v"], seg["z"], seg["q"], seg["qi"], seg["ki"], seg["wi"],
         seg["dt"], jnp.zeros((d, tail_pad + HYB_COLS - COL_TAIL - LANES), F32)], axis=1).astype(BF16)
    proj = fused_matmul([x], w_perm, g=norm_g, out_dtype=F32, tn=1024, name="hyb_in_proj")

    rot = ATTN_HEAD_DIM // ROPE_FRACTION
    rot_i = IDX_HEAD_DIM // ROPE_FRACTION
    inv16 = jnp.power(ROPE_THETA, -(jnp.arange(rot // 2, dtype=F32) * 2.0 / rot))
    inv8 = jnp.power(ROPE_THETA, -(jnp.arange(rot_i // 2, dtype=F32) * 2.0 / rot_i))
    inv_c = jnp.concatenate(
        [inv16, inv8, jnp.zeros((LANES - rot // 2 - rot_i // 2,), F32)]).reshape(1, LANES)
    q_r, k_r, v_r, qi_r, ki_r, wi = hyb_prep(
        proj, pos, inv_c, q_norm.reshape(1, LANES), k_norm.reshape(1, LANES),
        _pad_lanes(kidx_norm, TAIL_KI))
    attn = dsa_attention(q_r, qi_r, wi, k_r, v_r, ki_r, batch, seq, k_sel)

    ssm = ssd_mixer(proj, conv_w, conv_b.reshape(1, SSM_CONV_DIM), _pad_lanes(dt_bias, TAIL_DT),
                    _pad_lanes(a_log, TAIL_DT), jnp.repeat(d_skip, SSM_HEAD_DIM).reshape(1, SSM_INNER),
                    ssm_norm.reshape(1, SSM_INNER), batch, seq)
    return fused_matmul([attn, ssm], w_out.astype(BF16), res=x, out_dtype=F32, name="hyb_out_proj")


def retention_layer(x, pos, norm_g, w_in, ret_norm, w_out, batch, seq):
    proj = fused_matmul([x], w_in.astype(BF16), g=norm_g, out_dtype=BF16, tn=1024, name="ret_in_proj")
    o = retention_mixer(proj, pos, ret_norm, batch, seq)
    return fused_matmul([o], w_out.astype(BF16), res=x, out_dtype=F32, name="ret_out_proj")


def kernel(x, positions, mix_norm, ffn_norm, hyb_w_in, attn_q_norm, attn_k_norm, idx_k_norm, ssm_conv_w, ssm_conv_b, ssm_dt_bias, ssm_a_log, ssm_d, ssm_norm, hyb_w_out, ret_w_in, ret_norm, ret_w_out, moe_router_group, moe_router_expert, moe_w_gate, moe_w_up, moe_w_down):
    batch, seq, d = x.shape
    depth = mix_norm.shape[0]
    k_sel = min(TOPK_MAX, seq // 4)
    n = batch * seq
    xf = x.reshape(n, d)
    pos = positions.reshape(n, 1).astype(I32)
    for layer in range(depth):
        i = layer // 2
        if layer % 2 == 0:
            xf = hybrid_layer(xf, pos, mix_norm[layer], hyb_w_in[i], attn_q_norm[i], attn_k_norm[i],
                              idx_k_norm[i], ssm_conv_w[i], ssm_conv_b[i], ssm_dt_bias[i],
                              ssm_a_log[i], ssm_d[i], ssm_norm[i], hyb_w_out[i], batch, seq, k_sel)
        else:
            xf = retention_layer(xf, pos, mix_norm[layer], ret_w_in[i], ret_norm[i], ret_w_out[i],
                                 batch, seq)
        xf = hier_moe_layer(xf, ffn_norm[layer], moe_router_group[layer], moe_router_expert[layer],
                            moe_w_gate, moe_w_up, moe_w_down, layer)
    return xf.reshape(batch, seq, d)
```

```python
import functools

import jax
import jax.numpy as jnp
import numpy as np
from jax import lax
from jax.experimental import pallas as pl
from jax.experimental.pallas import tpu as pltpu

F32 = jnp.float32
BF16 = jnp.bfloat16
I32 = jnp.int32
I16 = jnp.int16
HALF_BITS = 16
COUNT_GROUP = 4

ATTN_HEADS = 8
ATTN_KV_HEADS = 2
ATTN_GROUP = ATTN_HEADS // ATTN_KV_HEADS
ATTN_HEAD_DIM = 128
ATTN_INNER = ATTN_HEADS * ATTN_HEAD_DIM
ATTN_KV = ATTN_KV_HEADS * ATTN_HEAD_DIM
IDX_HEADS = 8
IDX_HEAD_DIM = 64
IDX_INNER = IDX_HEADS * IDX_HEAD_DIM
TOPK_MAX = 256
Q_BLOCK = 128
ROPE_THETA = 500000.0
ROPE_FRACTION = 4

SSM_HEADS = 16
SSM_HEAD_DIM = 64
SSM_INNER = SSM_HEADS * SSM_HEAD_DIM
SSM_GROUPS = 2
SSM_STATE = 128
SSM_CONV = 4
SSM_CONV_DIM = SSM_INNER + 2 * SSM_GROUPS * SSM_STATE
SSM_CHUNK = 128
SSM_GROUP_INNER = SSM_INNER // SSM_GROUPS

RET_HEADS = 8
RET_QK_DIM = 256
RET_V_DIM = 512
RET_QK_TOTAL = RET_HEADS * RET_QK_DIM
RET_V_TOTAL = RET_HEADS * RET_V_DIM
RET_CHUNK = 128
RET_THETA = 10000.0

MOE_GROUPS = 8
MOE_EXPERTS_PER_GROUP = 8
N_EXPERTS = MOE_GROUPS * MOE_EXPERTS_PER_GROUP
MOE_TOPK = 2
EXPERT_FF = 512
MOE_BLOCK = 256
COMBINE_ROWS = 128

EPS = 1e-6

LANES = 128
SUBLANES = 8
VMEM_LIMIT = 52 * 1024 * 1024

LOG2E = 1.4426950408889634
NEG_BIG = -1e30
INT_MIN = -2147483648

COL_XBC = 0
COL_K = SSM_CONV_DIM
COL_V = COL_K + ATTN_KV
COL_Z = COL_V + ATTN_KV
COL_Q = COL_Z + SSM_INNER
COL_QI = COL_Q + ATTN_INNER
COL_TAIL = COL_QI + IDX_INNER
HYB_COLS = 5120
TAIL_KI = 0
TAIL_WI = IDX_HEAD_DIM
TAIL_DT = TAIL_WI + IDX_HEADS


def _cparams(n_axes):
    return pltpu.CompilerParams(dimension_semantics=("arbitrary",) * n_axes,
                                vmem_limit_bytes=VMEM_LIMIT)


def _pick(n, pref):
    t = min(n, pref)
    while n % t:
        t //= 2
    return t


def _mm_kernel(*refs, n_a, has_norm, has_res, row_chunk):
    a_refs = refs[:n_a]
    pos = n_a
    g_ref = None
    if has_norm:
        g_ref = refs[pos]
        pos += 1
    w_refs = refs[pos:pos + n_a]
    pos += n_a
    res_ref = None
    if has_res:
        res_ref = refs[pos]
        pos += 1
    out_ref = refs[pos]
    pos += 1
    xn_ref = refs[pos] if has_norm else None
    tm = out_ref.shape[0]

    if has_norm:
        @pl.when(pl.program_id(1) == 0)
        def _():
            def body(r, carry):
                rows = pl.ds(pl.multiple_of(r * row_chunk, row_chunk), row_chunk)
                x = a_refs[0][rows, :]
                ms = jnp.mean(x * x, axis=-1, keepdims=True)
                xn_ref[rows, :] = (x * lax.rsqrt(ms + EPS) * g_ref[...]).astype(BF16)
                return carry
            lax.fori_loop(0, tm // row_chunk, body, 0)
        acc = jnp.dot(xn_ref[...], w_refs[0][...], preferred_element_type=F32)
    else:
        acc = jnp.dot(a_refs[0][...], w_refs[0][...], preferred_element_type=F32)
        for p in range(1, n_a):
            acc = acc + jnp.dot(a_refs[p][...], w_refs[p][...], preferred_element_type=F32)
    if has_res:
        acc = acc + res_ref[...]
    out_ref[...] = acc.astype(out_ref.dtype)


def fused_matmul(a_list, w, *, g=None, res=None, out_dtype=F32, tm=1024, tn=512, name="mm"):
    n = a_list[0].shape[0]
    kp = a_list[0].shape[1]
    m = w.shape[1]
    tm = _pick(n, tm)
    tn = _pick(m, tn)
    n_a = len(a_list)
    has_norm = g is not None
    has_res = res is not None
    in_specs = [pl.BlockSpec((tm, kp), lambda i, j: (i, 0)) for _ in a_list]
    args = list(a_list)
    if has_norm:
        in_specs.append(pl.BlockSpec((1, kp), lambda i, j: (0, 0)))
        args.append(g.reshape(1, kp).astype(F32))
    for p in range(n_a):
        in_specs.append(pl.BlockSpec((kp, tn), lambda i, j, p=p: (p, j)))
        args.append(w)
    if has_res:
        in_specs.append(pl.BlockSpec((tm, tn), lambda i, j: (i, j)))
        args.append(res)
    scratch = [pltpu.VMEM((tm, kp), BF16)] if has_norm else []
    kern = functools.partial(_mm_kernel, n_a=n_a, has_norm=has_norm, has_res=has_res,
                             row_chunk=_pick(tm, 128))
    return pl.pallas_call(
        kern,
        grid=(n // tm, m // tn),
        in_specs=in_specs,
        out_specs=pl.BlockSpec((tm, tn), lambda i, j: (i, j)),
        out_shape=jax.ShapeDtypeStruct((n, m), out_dtype),
        scratch_shapes=scratch,
        compiler_params=_cparams(2),
        name=name,
    )(*args)


def _rope_lanes(x, cosf, s_neg, s_pos, half):
    width = x.shape[-1]
    return (x * cosf + pltpu.roll(x, width - half, axis=1) * s_neg
            + pltpu.roll(x, half, axis=1) * s_pos)


def _prep_kernel(q_ref, k_ref, v_ref, qi_ref, tail_ref, pos_ref, invc_ref, pcos_ref, pneg_ref, ppos_ref,
                 qn_ref, kn_ref, kin_ref,
                 qo_ref, ko_ref, vo_ref, qio_ref, kio_ref, wio_ref):
    tb = q_ref.shape[0]
    posf = pos_ref[...].astype(F32)
    lane = lax.broadcasted_iota(I32, (tb, LANES), 1)

    hi = lax.Precision.HIGHEST
    ang = posf * invc_ref[...]
    cos_c = jnp.cos(ang)
    sin_c = jnp.sin(ang)
    cos_all = jnp.dot(cos_c, pcos_ref[...], precision=hi, preferred_element_type=F32)
    sneg_all = jnp.dot(sin_c, pneg_ref[...], precision=hi, preferred_element_type=F32)
    spos_all = jnp.dot(sin_c, ppos_ref[...], precision=hi, preferred_element_type=F32)

    half = ATTN_HEAD_DIM // ROPE_FRACTION // 2
    cosf = cos_all[:, 0:LANES]
    s_neg = sneg_all[:, 0:LANES]
    s_pos = spos_all[:, 0:LANES]
    scale = ATTN_HEAD_DIM ** -0.5 * LOG2E
    for h in range(ATTN_HEADS):
        x = q_ref[:, h * LANES:(h + 1) * LANES]
        ms = jnp.mean(x * x, axis=-1, keepdims=True)
        x = x * lax.rsqrt(ms + EPS) * qn_ref[...]
        x = _rope_lanes(x, cosf, s_neg, s_pos, half)
        qo_ref[h] = (x * scale).astype(BF16)
    for h in range(ATTN_KV_HEADS):
        x = k_ref[:, h * LANES:(h + 1) * LANES]
        ms = jnp.mean(x * x, axis=-1, keepdims=True)
        x = x * lax.rsqrt(ms + EPS) * kn_ref[...]
        x = _rope_lanes(x, cosf, s_neg, s_pos, half)
        ko_ref[h] = x.T.astype(BF16)
        vo_ref[h] = v_ref[:, h * LANES:(h + 1) * LANES].astype(BF16)

    half_i = IDX_HEAD_DIM // ROPE_FRACTION // 2
    cos_i = cos_all[:, LANES:LANES + IDX_INNER]
    sn_i = sneg_all[:, LANES:LANES + IDX_INNER]
    sp_i = spos_all[:, LANES:LANES + IDX_INNER]
    qi = _rope_lanes(qi_ref[...], cos_i, sn_i, sp_i, half_i)
    for h in range(IDX_HEADS):
        slab = qi[:, (h // 2) * LANES:(h // 2 + 1) * LANES]
        if h % 2:
            slab = pltpu.roll(slab, IDX_HEAD_DIM, axis=1)
        qio_ref[h] = jnp.where(lane < IDX_HEAD_DIM, slab, 0.0).astype(BF16)

    t = tail_ref[...]
    is_ki = lane < IDX_HEAD_DIM
    ms = jnp.sum(jnp.where(is_ki, t * t, 0.0), axis=-1, keepdims=True) * (1.0 / IDX_HEAD_DIM)
    kin = t * lax.rsqrt(ms + EPS) * kin_ref[...]
    kin = _rope_lanes(kin, cos_i[:, 0:LANES], sn_i[:, 0:LANES], sp_i[:, 0:LANES], half_i)
    kio_ref[...] = jnp.where(is_ki, kin, 0.0).T.astype(BF16)
    wio_ref[...] = t * (IDX_HEADS ** -0.5 * IDX_HEAD_DIM ** -0.5)


def _rope_selectors():
    a_half = ATTN_HEAD_DIM // ROPE_FRACTION // 2
    i_half = IDX_HEAD_DIM // ROPE_FRACTION // 2
    unit = LANES - 1
    width = LANES + IDX_INNER
    pcos = np.zeros((LANES, width), np.float32)
    pneg = np.zeros((LANES, width), np.float32)
    ppos = np.zeros((LANES, width), np.float32)
    for m in range(LANES):
        if m < 2 * a_half:
            pcos[m % a_half, m] = 1.0
            (pneg if m < a_half else ppos)[m % a_half, m] = -1.0 if m < a_half else 1.0
        else:
            pcos[unit, m] = 1.0
    for m in range(IDX_INNER):
        r = m % IDX_HEAD_DIM
        if r < 2 * i_half:
            pcos[a_half + r % i_half, LANES + m] = 1.0
            (pneg if r < i_half else ppos)[a_half + r % i_half, LANES + m] = -1.0 if r < i_half else 1.0
        else:
            pcos[unit, LANES + m] = 1.0
    return jnp.asarray(pcos), jnp.asarray(pneg), jnp.asarray(ppos)


def hyb_prep(proj, pos, inv_c, q_norm, k_norm, kidx_norm_pad):
    n = proj.shape[0]
    tb = _pick(n, 256)
    pcos, pneg, ppos = _rope_selectors()
    sel_w = LANES + IDX_INNER

    def col(width, off):
        return pl.BlockSpec((tb, width), lambda i: (i, off // width))

    def full(shape):
        return pl.BlockSpec(shape, lambda i: (0,) * len(shape))

    in_specs = [col(ATTN_INNER, COL_Q), col(ATTN_KV, COL_K), col(ATTN_KV, COL_V),
                col(IDX_INNER, COL_QI), col(LANES, COL_TAIL),
                pl.BlockSpec((tb, 1), lambda i: (i, 0)),
                full((1, LANES)), full((LANES, sel_w)), full((LANES, sel_w)), full((LANES, sel_w)),
                full((1, LANES)), full((1, LANES)), full((1, LANES))]
    out_shape = [jax.ShapeDtypeStruct((ATTN_HEADS, n, LANES), BF16),
                 jax.ShapeDtypeStruct((ATTN_KV_HEADS, LANES, n), BF16),
                 jax.ShapeDtypeStruct((ATTN_KV_HEADS, n, LANES), BF16),
                 jax.ShapeDtypeStruct((IDX_HEADS, n, LANES), BF16),
                 jax.ShapeDtypeStruct((LANES, n), BF16),
                 jax.ShapeDtypeStruct((n, LANES), F32)]
    out_specs = [pl.BlockSpec((ATTN_HEADS, tb, LANES), lambda i: (0, i, 0)),
                 pl.BlockSpec((ATTN_KV_HEADS, LANES, tb), lambda i: (0, 0, i)),
                 pl.BlockSpec((ATTN_KV_HEADS, tb, LANES), lambda i: (0, i, 0)),
                 pl.BlockSpec((IDX_HEADS, tb, LANES), lambda i: (0, i, 0)),
                 pl.BlockSpec((LANES, tb), lambda i: (0, i)),
                 pl.BlockSpec((tb, LANES), lambda i: (i, 0))]
    return pl.pallas_call(
        _prep_kernel, grid=(n // tb,), in_specs=in_specs, out_specs=out_specs,
        out_shape=out_shape, compiler_params=_cparams(1), name="hyb_prep",
    )(proj, proj, proj, proj, proj, pos, inv_c, pcos, pneg, ppos, q_norm, k_norm, kidx_norm_pad)


def _sortable(x):
    b = pltpu.bitcast(x, I32)
    return jnp.where(b < 0, b ^ jnp.int32(0x7FFFFFFF), b)


def _dsa_kernel(q_ref, qi_ref, wi_ref, k_ref, v_ref, ki_ref, triu_ref, eye_ref, onec_ref, out_ref,
                keys_ref, half_ref, bias_ref, wb_ref, mx_ref, acc_ref, *, k_sel, kc):
    i = pl.program_id(1)
    tq = Q_BLOCK
    nkc = (i * tq + tq + kc - 1) // kc
    row = i * tq + lax.broadcasted_iota(I32, (tq, kc), 0)
    lane = lax.broadcasted_iota(I32, (tq, kc), 1)
    nslab = kc // LANES

    w = wi_ref[...]
    for h in range(IDX_HEADS):
        wb_ref[h] = jnp.broadcast_to(w[:, TAIL_WI + h:TAIL_WI + h + 1], (tq, kc))

    def idx_body(c, carry):
        cols = pl.ds(pl.multiple_of(c * kc, kc), kc)
        d = jnp.dot(qi_ref[...].reshape(IDX_HEADS * tq, LANES), ki_ref[:, cols],
                    preferred_element_type=F32)
        sc = jnp.zeros((tq, kc), F32)
        for h in range(IDX_HEADS):
            sc = sc + jnp.maximum(d[h * tq:(h + 1) * tq, :], 0.0) * wb_ref[h]
        causal = (c * kc + lane) <= row
        keys_ref[:, cols] = jnp.where(causal, _sortable(sc), jnp.int32(INT_MIN))
        return carry
    lax.fori_loop(0, nkc, idx_body, 0)

    def fold(x, op):
        acc = x[:, 0:LANES]
        for j in range(1, nslab):
            acc = op(acc, x[:, j * LANES:(j + 1) * LANES])
        return acc

    i16_min = -(1 << (HALF_BITS - 1))

    def count16(cand, strict):
        c16 = cand.astype(I16)

        def hits(start, width):
            kk = half_ref[:, pl.ds(pl.multiple_of(start, kc), width)]
            hit = (kk > c16) if strict else (kk >= c16)
            x = jnp.where(hit, jnp.int16(1), jnp.int16(0))
            acc = x[:, 0:LANES]
            for j in range(1, width // LANES):
                acc = acc + x[:, j * LANES:(j + 1) * LANES]
            return acc

        ngrp = nkc // COUNT_GROUP
        acc = lax.fori_loop(0, ngrp, lambda g, a: a + hits(g * (COUNT_GROUP * kc), COUNT_GROUP * kc),
                            jnp.zeros((tq, LANES), I16))
        acc = lax.fori_loop(ngrp * COUNT_GROUP, nkc, lambda c, a: a + hits(c * kc, kc), acc)
        return jnp.sum(acc.astype(F32), axis=1, keepdims=True)

    def kth_largest16(kf):
        zero = jnp.zeros((tq, 1), I32)
        lo = jnp.where(count16(zero, False) >= kf, zero, jnp.full((tq, 1), i16_min, I32))

        def bis_body(it, lo):
            cand = lo + jnp.left_shift(jnp.int32(1), jnp.int32(HALF_BITS - 2) - it)
            return jnp.where(count16(cand, False) >= kf, cand, lo)
        return lax.fori_loop(0, HALF_BITS - 1, bis_body, lo)

    def fill_half(fn):
        def body(c, carry):
            cols = pl.ds(pl.multiple_of(c * kc, kc), kc)
            half_ref[:, cols] = fn(keys_ref[:, cols]).astype(I16)
            return carry
        lax.fori_loop(0, nkc, body, 0)

    kf = jnp.float32(k_sel)
    fill_half(lambda kk: jnp.right_shift(kk, HALF_BITS))
    thr_hi = kth_largest16(kf)
    above = count16(thr_hi, True)
    low_mask = (1 << HALF_BITS) - 1
    fill_half(lambda kk: jnp.where(jnp.right_shift(kk, HALF_BITS) == thr_hi,
                                   (kk & low_mask) + i16_min, i16_min))
    thr_lo = kth_largest16(kf - above)
    thr = jnp.left_shift(thr_hi, HALF_BITS) + (thr_lo - i16_min)
    cnt_gt_lo = count16(thr_lo, True)
    need = kf - above - cnt_gt_lo
    n_eq = count16(thr_lo, False) - cnt_gt_lo
    row_plain = jnp.logical_or(jnp.logical_and(n_eq == need, thr_lo > i16_min),
                               thr == jnp.int32(INT_MIN))
    all_plain = jnp.min(jnp.where(row_plain, 1.0, 0.0)) > 0.5

    @pl.when(all_plain)
    def _():
        def fin_body(c, carry):
            cols = pl.ds(pl.multiple_of(c * kc, kc), kc)
            causal = (c * kc + lane) <= row
            sel = (keys_ref[:, cols] >= thr) & causal
            bias_ref[:, cols] = jnp.where(sel, 0.0, NEG_BIG).astype(BF16)
            return carry
        lax.fori_loop(0, nkc, fin_body, 0)

    @pl.when(jnp.logical_not(all_plain))
    def _():
        def fin_body(c, run):
            cols = pl.ds(pl.multiple_of(c * kc, kc), kc)
            kk = keys_ref[:, cols]
            eq = kk == thr
            eqf = jnp.where(eq, 1.0, 0.0)
            incl = jnp.dot(eqf.astype(BF16), triu_ref[...], preferred_element_type=F32)
            sel = (kk > thr) | (eq & ((run + incl) <= need))
            causal = (c * kc + lane) <= row
            bias_ref[:, cols] = jnp.where(sel & causal, 0.0, NEG_BIG).astype(BF16)
            return run + jnp.sum(eqf, axis=1, keepdims=True)
        lax.fori_loop(0, nkc, fin_body, jnp.zeros((tq, 1), F32))

    gq = ATTN_GROUP * tq

    def logits(j, kx):
        qg = q_ref[j * ATTN_GROUP:(j + 1) * ATTN_GROUP].reshape(gq, LANES)
        qx = jnp.concatenate([qg, eye_ref[...]], axis=1)
        return jnp.dot(qx, kx, preferred_element_type=F32)

    mx_ref[...] = jnp.full(mx_ref.shape, NEG_BIG, F32)

    def max_body(c, carry):
        cols = pl.ds(pl.multiple_of(c * kc, kc), kc)
        bt = bias_ref[:, cols]
        for j in range(ATTN_KV_HEADS):
            kx = jnp.concatenate([k_ref[j, :, cols], bt], axis=0)
            mx_ref[j] = jnp.maximum(mx_ref[j], fold(logits(j, kx), jnp.maximum))
        return carry
    lax.fori_loop(0, nkc, max_body, 0)
    for j in range(ATTN_KV_HEADS):
        mx_ref[j] = jnp.broadcast_to(jnp.max(mx_ref[j], axis=1, keepdims=True), (gq, LANES))

    acc_ref[...] = jnp.zeros(acc_ref.shape, F32)

    def att_body(c, carry):
        cols = pl.ds(pl.multiple_of(c * kc, kc), kc)
        bt = bias_ref[:, cols]
        for j in range(ATTN_KV_HEADS):
            kx = jnp.concatenate([k_ref[j, :, cols], bt], axis=0)
            vx = jnp.concatenate([v_ref[j, cols, :], onec_ref[...]], axis=1)
            s = logits(j, kx)
            mb = mx_ref[j]
            p = jnp.concatenate(
                [jnp.exp2(s[:, a * LANES:(a + 1) * LANES] - mb) for a in range(nslab)], axis=1)
            acc_ref[j] = acc_ref[j] + jnp.dot(p.astype(BF16), vx, preferred_element_type=F32)
        return carry
    lax.fori_loop(0, nkc, att_body, 0)

    for h in range(ATTN_HEADS):
        a = acc_ref[h // ATTN_GROUP, (h % ATTN_GROUP) * tq:(h % ATTN_GROUP + 1) * tq, :]
        out_ref[:, h * LANES:(h + 1) * LANES] = (
            a[:, 0:LANES] / a[:, LANES:LANES + 1]).astype(out_ref.dtype)


def dsa_attention(q_r, qi_r, wi, k_r, v_r, ki_r, batch, seq, k_sel):
    n = batch * seq
    nq = seq // Q_BLOCK
    kc = _pick(seq, 1024)
    triu =jnp.asarray((np.arange(kc)[:, None] <= np.arange(kc)[None, :]).astype(np.float32), BF16)
    gq = ATTN_GROUP * Q_BLOCK
    eye = jnp.asarray(np.tile(np.eye(Q_BLOCK, dtype=np.float32), (ATTN_GROUP, 1)), BF16)
    onec = np.zeros((kc, LANES), np.float32)
    onec[:, 0] = 1.0
    onec = jnp.asarray(onec, BF16)
    kern = functools.partial(_dsa_kernel, k_sel=k_sel, kc=kc)
    in_specs = [
        pl.BlockSpec((ATTN_HEADS, Q_BLOCK, LANES), lambda b, i: (0, b * nq + i, 0)),
        pl.BlockSpec((IDX_HEADS, Q_BLOCK, LANES), lambda b, i: (0, b * nq + i, 0)),
        pl.BlockSpec((Q_BLOCK, LANES), lambda b, i: (b * nq + i, 0)),
        pl.BlockSpec((ATTN_KV_HEADS, LANES, seq), lambda b, i: (0, 0, b)),
        pl.BlockSpec((ATTN_KV_HEADS, seq, LANES), lambda b, i: (0, b, 0)),
        pl.BlockSpec((LANES, seq), lambda b, i: (0, b)),
        pl.BlockSpec((kc, kc), lambda b, i: (0, 0)),
        pl.BlockSpec((gq, Q_BLOCK), lambda b, i: (0, 0)),
        pl.BlockSpec((kc, LANES), lambda b, i: (0, 0)),
    ]
    scratch = [pltpu.VMEM((Q_BLOCK, seq), I32), pltpu.VMEM((Q_BLOCK, seq), I16),
               pltpu.VMEM((Q_BLOCK, seq), BF16),
               pltpu.VMEM((IDX_HEADS, Q_BLOCK, kc), F32),
               pltpu.VMEM((ATTN_KV_HEADS, gq, LANES), F32),
               pltpu.VMEM((ATTN_KV_HEADS, gq, 2 * LANES), F32)]
    return pl.pallas_call(
        kern, grid=(batch, nq), in_specs=in_specs,
        out_specs=pl.BlockSpec((Q_BLOCK, ATTN_INNER), lambda b, i: (b * nq + i, 0)),
        out_shape=jax.ShapeDtypeStruct((n, ATTN_INNER), BF16),
        scratch_shapes=scratch, compiler_params=_cparams(2), name="dsa_attention",
    )(q_r, qi_r, wi, k_r, v_r, ki_r, triu, eye, onec)


def _silu(x):
    return x * (1.0 / (1.0 + jnp.exp(-x)))


def _ssd_kernel(xbc_ref, z_ref, tail_ref, cw_ref, cb_ref, dtb_ref, alog_ref, dexp_ref, ng_ref,
                tri_ref, e_ref, out_ref, state_ref, ext_ref, y_ref):
    q = SSM_CHUNK
    hi = lax.Precision.HIGHEST

    @pl.when(pl.program_id(1) == 0)
    def _():
        state_ref[...] = jnp.zeros(state_ref.shape, F32)
        ext_ref[0:SUBLANES, :] = jnp.zeros((SUBLANES, SSM_CONV_DIM), F32)

    x = xbc_ref[...]
    ext_ref[SUBLANES:SUBLANES + q, :] = x
    conv = x * cw_ref[SSM_CONV - 1:SSM_CONV, :] + cb_ref[...]
    for k in range(1, SSM_CONV):
        conv = conv + ext_ref[SUBLANES - k:SUBLANES - k + q, :] * cw_ref[SSM_CONV - 1 - k:SSM_CONV - k, :]
    ext_ref[0:SUBLANES, :] = x[q - SUBLANES:q, :]
    xbc = _silu(conv)
    xs = xbc[:, 0:SSM_INNER]
    bm = xbc[:, SSM_INNER:SSM_INNER + SSM_GROUPS * SSM_STATE]
    cm = xbc[:, SSM_INNER + SSM_GROUPS * SSM_STATE:SSM_CONV_DIM]

    lane = lax.broadcasted_iota(I32, (q, LANES), 1)
    is_dt = (lane >= TAIL_DT) & (lane < TAIL_DT + SSM_HEADS)
    raw = tail_ref[...] + dtb_ref[...]
    dt = jnp.where(is_dt, jnp.maximum(raw, 0.0) + jnp.log1p(jnp.exp(-jnp.abs(raw))), 0.0)
    a = -jnp.exp(alog_ref[...])
    da = jnp.where(is_dt, dt * a, 0.0)
    acum = jnp.dot(tri_ref[...], da, precision=hi, preferred_element_type=F32)
    acum_t = acum.T
    acum_x = jnp.dot(acum, e_ref[...], precision=hi, preferred_element_type=F32)
    dt_x = jnp.dot(dt, e_ref[...], precision=hi, preferred_element_type=F32)
    last_x = acum_x[q - 1:q, :]
    xdt = xs * dt_x
    xdec = xdt * jnp.exp(last_x - acum_x)
    ri = lax.broadcasted_iota(I32, (q, q), 0)
    ci = lax.broadcasted_iota(I32, (q, q), 1)
    tril = ri >= ci
    pairs_per_group = SSM_GROUP_INNER // LANES

    for g in range(SSM_GROUPS):
        bg = bm[:, g * SSM_STATE:(g + 1) * SSM_STATE]
        cg = cm[:, g * SSM_STATE:(g + 1) * SSM_STATE]
        cgb = cg.astype(BF16)
        gsl = slice(g * SSM_GROUP_INNER, (g + 1) * SSM_GROUP_INNER)
        cb = lax.dot_general(cgb, bg.astype(BF16), (((1,), (1,)), ((), ())),
                             preferred_element_type=F32)
        y_off = jnp.dot(cgb, state_ref[g].astype(BF16), preferred_element_type=F32)
        y_ref[:, gsl] = y_off * jnp.exp(acum_x[:, gsl])
        for mth in range(pairs_per_group):
            slab = slice((g * pairs_per_group + mth) * LANES, (g * pairs_per_group + mth + 1) * LANES)
            xp = xdt[:, slab]
            yd = None
            for side in range(2):
                hl = TAIL_DT + (g * pairs_per_group + mth) * 2 + side
                seg = acum[:, hl:hl + 1] - acum_t[hl:hl + 1, :]
                lmat = jnp.exp(jnp.where(tril, seg, NEG_BIG))
                if side == 0:
                    xh = jnp.where(lane < SSM_HEAD_DIM, xp, 0.0)
                else:
                    xh = jnp.where(lane >= SSM_HEAD_DIM, xp, 0.0)
                part = jnp.dot((cb * lmat).astype(BF16), xh.astype(BF16), preferred_element_type=F32)
                yd = part if yd is None else yd + part
            y_ref[:, slab] = y_ref[:, slab] + yd
        st = jnp.dot(bg.T.astype(BF16), xdec[:, gsl].astype(BF16), preferred_element_type=F32)
        state_ref[g] = state_ref[g] * jnp.exp(last_x[:, gsl]) + st

    y = (y_ref[...] + dexp_ref[...] * xs) * _silu(z_ref[...])
    for g in range(SSM_GROUPS):
        gsl = slice(g * SSM_GROUP_INNER, (g + 1) * SSM_GROUP_INNER)
        yg = y[:, gsl]
        ms = jnp.mean(yg * yg, axis=-1, keepdims=True)
        out_ref[:, gsl] = (yg * lax.rsqrt(ms + EPS) * ng_ref[:, gsl]).astype(out_ref.dtype)


def ssd_mixer(proj, conv_w, conv_b, dtb_pad, alog_pad, d_exp, norm_g, batch, seq):
    n = batch * seq
    q = SSM_CHUNK
    nc = seq // q
    tri = jnp.asarray((np.arange(q)[:, None] >= np.arange(q)[None, :]).astype(np.float32))
    e = np.zeros((LANES, SSM_INNER), np.float32)
    for h in range(SSM_HEADS):
        e[TAIL_DT + h, h * SSM_HEAD_DIM:(h + 1) * SSM_HEAD_DIM] = 1.0
    e = jnp.asarray(e)

    def col(width, off):
        return pl.BlockSpec((q, width), lambda b, c: (b * nc + c, off // width))

    def full(shape):
        return pl.BlockSpec(shape, lambda b, c: (0,) * len(shape))

    in_specs = [col(SSM_CONV_DIM, COL_XBC), col(SSM_INNER, COL_Z), col(LANES, COL_TAIL),
                full((SSM_CONV, SSM_CONV_DIM)), full((1, SSM_CONV_DIM)), full((1, LANES)),
                full((1, LANES)), full((1, SSM_INNER)), full((1, SSM_INNER)),
                full((q, q)), full((LANES, SSM_INNER))]
    scratch = [pltpu.VMEM((SSM_GROUPS, SSM_STATE, SSM_GROUP_INNER), F32),
               pltpu.VMEM((SUBLANES + q, SSM_CONV_DIM), F32),
               pltpu.VMEM((q, SSM_INNER), F32)]
    return pl.pallas_call(
        _ssd_kernel, grid=(batch, nc), in_specs=in_specs,
        out_specs=pl.BlockSpec((q, SSM_INNER), lambda b, c: (b * nc + c, 0)),
        out_shape=jax.ShapeDtypeStruct((n, SSM_INNER), BF16),
        scratch_shapes=scratch, compiler_params=_cparams(2), name="ssd_mixer",
    )(proj, proj, proj, conv_w, conv_b, dtb_pad, alog_pad, d_exp, norm_g, tri, e)


def _ret_kernel(q_ref, k_ref, v_ref, g_ref, pos_ref, inv_ref, dmat_ref, qdec_ref, kdec_ref,
                cdec_ref, ng_ref, out_ref, state_ref):
    half = RET_QK_DIM // 2

    @pl.when(pl.program_id(1) == 0)
    def _():
        state_ref[...] = jnp.zeros(state_ref.shape, F32)

    ang = pos_ref[...].astype(F32) * inv_ref[...]
    cosf = jnp.cos(ang)
    sinf = jnp.sin(ang)

    def rope(x):
        x1 = x[:, 0:half]
        x2 = x[:, half:2 * half]
        return jnp.concatenate([x1 * cosf - x2 * sinf, x2 * cosf + x1 * sinf], axis=-1)

    for h in range(RET_HEADS):
        qh = rope(q_ref[:, h * RET_QK_DIM:(h + 1) * RET_QK_DIM].astype(F32))
        kh = rope(k_ref[:, h * RET_QK_DIM:(h + 1) * RET_QK_DIM].astype(F32)) * (RET_QK_DIM ** -0.5)
        vh = v_ref[:, h * RET_V_DIM:(h + 1) * RET_V_DIM]
        qb = qh.astype(BF16)
        s = lax.dot_general(qb, kh.astype(BF16), (((1,), (1,)), ((), ())),
                            preferred_element_type=F32) * dmat_ref[h]
        inner = jnp.dot(s.astype(BF16), vh, preferred_element_type=F32)
        st = state_ref[h]
        cross = jnp.dot(qb, st.astype(BF16), preferred_element_type=F32) * qdec_ref[h]
        kd = (kh * kdec_ref[h]).T.astype(BF16)
        state_ref[h] = st * cdec_ref[h] + jnp.dot(kd, vh, preferred_element_type=F32)
        o = inner + cross
        ms = jnp.mean(o * o, axis=-1, keepdims=True)
        o = o * lax.rsqrt(ms + EPS) * ng_ref[:, h * RET_V_DIM:(h + 1) * RET_V_DIM]
        gate = _silu(g_ref[:, h * RET_V_DIM:(h + 1) * RET_V_DIM].astype(F32))
        out_ref[:, h * RET_V_DIM:(h + 1) * RET_V_DIM] = (gate * o).astype(out_ref.dtype)


def retention_mixer(proj, pos, norm_g, batch, seq):
    n = batch * seq
    q = RET_CHUNK
    nc = seq // q
    half = RET_QK_DIM // 2
    inv = jnp.power(RET_THETA, -(jnp.arange(half, dtype=F32) * 2.0 / RET_QK_DIM)).reshape(1, half)
    log_gamma = jnp.log(1.0 - jnp.power(2.0, -5.0 - jnp.arange(RET_HEADS, dtype=F32)))
    i = jnp.arange(q, dtype=F32)
    diff = i[:, None] - i[None, :]
    dmat = jnp.where(diff[None] >= 0,
                     jnp.exp(jnp.maximum(diff, 0.0)[None] * log_gamma[:, None, None]), 0.0)
    qdec = jnp.exp((i + 1.0)[None, :, None] * log_gamma[:, None, None])
    kdec = jnp.exp((q - 1.0 - i)[None, :, None] * log_gamma[:, None, None])
    cdec = jnp.broadcast_to(jnp.exp(q * log_gamma)[:, None, None], (RET_HEADS, 1, RET_V_DIM))

    def col(width, off):
        return pl.BlockSpec((q, width), lambda b, c: (b * nc + c, off // width))

    def full(shape):
        return pl.BlockSpec(shape, lambda b, c: (0,) * len(shape))

    in_specs = [col(RET_QK_TOTAL, 0), col(RET_QK_TOTAL, RET_QK_TOTAL),
                col(RET_V_TOTAL, 2 * RET_QK_TOTAL), col(RET_V_TOTAL, 2 * RET_QK_TOTAL + RET_V_TOTAL),
                pl.BlockSpec((q, 1), lambda b, c: (b * nc + c, 0)),
                full((1, half)), full((RET_HEADS, q, q)), full((RET_HEADS, q, 1)),
                full((RET_HEADS, q, 1)), full((RET_HEADS, 1, RET_V_DIM)), full((1, RET_V_TOTAL))]
    return pl.pallas_call(
        _ret_kernel, grid=(batch, nc), in_specs=in_specs,
        out_specs=pl.BlockSpec((q, RET_V_TOTAL), lambda b, c: (b * nc + c, 0)),
        out_shape=jax.ShapeDtypeStruct((n, RET_V_TOTAL), BF16),
        scratch_shapes=[pltpu.VMEM((RET_HEADS, RET_QK_DIM, RET_V_DIM), F32)],
        compiler_params=_cparams(2), name="retention",
    )(proj, proj, proj, proj, pos, inv, dmat, qdec, kdec, cdec, norm_g.reshape(1, RET_V_TOTAL))


ROUTE_E0 = MOE_GROUPS
INFO_EID, INFO_GATE, INFO_RANK = 0, 2, 4


def _router_kernel(x_ref, g_ref, wr_ref, lt_ref, h_ref, info_ref, cnt_ref, run_ref):
    tb = x_ref.shape[0]

    @pl.when(pl.program_id(0) == 0)
    def _():
        run_ref[...] = jnp.zeros(run_ref.shape, F32)

    x = x_ref[...]
    ms = jnp.mean(x * x, axis=-1, keepdims=True)
    hn = x * lax.rsqrt(ms + EPS) * g_ref[...]
    h_ref[...] = hn
    h_hi = hn.astype(BF16)
    h_lo = (hn - h_hi.astype(F32)).astype(BF16)
    both = jnp.dot(h_hi, wr_ref[...], preferred_element_type=F32)
    logits = (both[:, 0:LANES] + both[:, LANES:2 * LANES]
              + jnp.dot(h_lo, wr_ref[:, 0:LANES], preferred_element_type=F32))

    lane = lax.broadcasted_iota(I32, (tb, LANES), 1).astype(F32)
    far = jnp.float32(4 * LANES)
    ninf = jnp.float32(-jnp.inf)
    gl = jnp.where(lane < MOE_GROUPS, logits, ninf)
    gmax = jnp.max(gl, axis=1, keepdims=True)
    grp = jnp.min(jnp.where(gl == gmax, lane, far), axis=1, keepdims=True)
    p_grp = 1.0 / jnp.sum(jnp.exp(gl - gmax), axis=1, keepdims=True)
    lo = ROUTE_E0 + grp * MOE_EXPERTS_PER_GROUP
    el = jnp.where((lane >= lo) & (lane < lo + MOE_EXPERTS_PER_GROUP), logits, ninf)
    m1 = jnp.max(el, axis=1, keepdims=True)
    i1 = jnp.min(jnp.where(el == m1, lane, far), axis=1, keepdims=True)
    el2 = jnp.where(lane == i1, ninf, el)
    m2 = jnp.max(el2, axis=1, keepdims=True)
    i2 = jnp.min(jnp.where(el2 == m2, lane, far), axis=1, keepdims=True)
    e2 = jnp.exp(m2 - m1)
    g1 = p_grp / (1.0 + e2)
    g2 = p_grp * e2 / (1.0 + e2)
    oh1 = jnp.where(lane == i1, 1.0, 0.0)
    oh2 = jnp.where(lane == i2, 1.0, 0.0)
    cnt = oh1 + oh2
    before = jnp.dot(lt_ref[...], cnt.astype(BF16), preferred_element_type=F32) + run_ref[...]
    r1 = jnp.sum(oh1 * before, axis=1, keepdims=True)
    r2 = jnp.sum(oh2 * before, axis=1, keepdims=True)
    run_ref[...] = run_ref[...] + jnp.sum(cnt, axis=0, keepdims=True)
    cnt_ref[...] = run_ref[...]

    info = jnp.where(lane == INFO_EID, i1 - ROUTE_E0, 0.0)
    info = jnp.where(lane == INFO_EID + 1, i2 - ROUTE_E0, info)
    info = jnp.where(lane == INFO_GATE, g1, info)
    info = jnp.where(lane == INFO_GATE + 1, g2, info)
    info = jnp.where(lane == INFO_RANK, r1, info)
    info = jnp.where(lane == INFO_RANK + 1, r2, info)
    info_ref[...] = info


def moe_router(x, g, w_router_pad):
    n, d = x.shape
    tb = _pick(n, 512)
    lt = jnp.asarray((np.arange(tb)[:, None] > np.arange(tb)[None, :]).astype(np.float32), BF16)
    w_hi = w_router_pad.astype(BF16)
    w_lo = (w_router_pad - w_hi.astype(F32)).astype(BF16)
    w_split = jnp.concatenate([w_hi, w_lo], axis=1)
    return pl.pallas_call(
        _router_kernel, grid=(n // tb,),
        in_specs=[pl.BlockSpec((tb, d), lambda i: (i, 0)), pl.BlockSpec((1, d), lambda i: (0, 0)),
                  pl.BlockSpec((d, 2 * LANES), lambda i: (0, 0)), pl.BlockSpec((tb, tb), lambda i: (0, 0))],
        out_specs=[pl.BlockSpec((tb, d), lambda i: (i, 0)), pl.BlockSpec((tb, LANES), lambda i: (i, 0)),
                   pl.BlockSpec((1, LANES), lambda i: (0, 0))],
        out_shape=[jax.ShapeDtypeStruct((n, d), F32), jax.ShapeDtypeStruct((n, LANES), F32),
                   jax.ShapeDtypeStruct((1, LANES), F32)],
        scratch_shapes=[pltpu.VMEM((1, LANES), F32)],
        compiler_params=_cparams(1), name="moe_router",
    )(x, g.reshape(1, d), w_split, lt)


def _row_copy(src_ref, src_row, dst_ref, dst_row, sem):
    return pltpu.make_async_copy(src_ref.at[pl.ds(src_row, 1), :], dst_ref.at[pl.ds(dst_row, 1), :], sem)


def _expert_kernel(be_ref, nused_ref, st_ref, nxe_ref, h_hbm, wg_hbm, wu_hbm, wd_hbm, out_ref,
                   xbuf, sem, wgf, wuf, wdf, wsem, wgu_s, wd_s, *, layer):
    i = pl.program_id(0)
    slot = i % 2
    nused = nused_ref[0]
    d = xbuf.shape[2]
    ff = EXPERT_FF

    def weight_copies(e):
        return (pltpu.make_async_copy(wg_hbm.at[layer, e], wgf, wsem),
                pltpu.make_async_copy(wu_hbm.at[layer, e], wuf, wsem),
                pltpu.make_async_copy(wd_hbm.at[layer, e], wdf, wsem))

    @pl.when(jnp.logical_and(i == 0, nused > 0))
    def _():
        for c in weight_copies(be_ref[0]):
            c.start()

    def gather(blk, s):
        def body(r, carry):
            _row_copy(h_hbm, st_ref[blk * MOE_BLOCK + r], xbuf.at[s], r, sem.at[s]).start(priority=1)
            return carry
        lax.fori_loop(0, MOE_BLOCK, body, 0, unroll=8)

    @pl.when(jnp.logical_and(i == 0, nused > 0))
    def _():
        gather(0, 0)

    @pl.when(i + 1 < nused)
    def _():
        gather(i + 1, 1 - slot)

    @pl.when(i < nused)
    def _():
        def drain(r, carry):
            _row_copy(h_hbm, 0, xbuf.at[slot], 0, sem.at[slot]).wait()
            return carry
        lax.fori_loop(0, MOE_BLOCK, drain, 0, unroll=8)

        changed = jnp.logical_or(i == 0, be_ref[i] != be_ref[jnp.maximum(i - 1, 0)])

        @pl.when(changed)
        def _():
            for c in weight_copies(be_ref[i]):
                c.wait()
            rc = _pick(d, 256)

            def body(r, carry):
                rows = pl.ds(pl.multiple_of(r * rc, rc), rc)
                wgu_s[rows, 0:ff] = wgf[rows, :].astype(BF16)
                wgu_s[rows, ff:2 * ff] = wuf[rows, :].astype(BF16)
                return carry
            lax.fori_loop(0, d // rc, body, 0)
            wd_s[...] = wdf[...].astype(BF16)

            nxt_e = nxe_ref[be_ref[i]]

            @pl.when(nxt_e >= 0)
            def _():
                for c in weight_copies(nxt_e):
                    c.start()

        gu = jnp.dot(xbuf[slot].astype(BF16), wgu_s[...], preferred_element_type=F32)
        hid = _silu(gu[:, 0:ff]) * gu[:, ff:2 * ff]
        out_ref[...] = jnp.dot(hid.astype(BF16), wd_s[...], preferred_element_type=F32)

    @pl.when(i >= nused)
    def _():
        out_ref[...] = jnp.zeros(out_ref.shape, out_ref.dtype)


def moe_experts(h, slot_tok, blk_expert, nused, next_expert, w_gate, w_up, w_down, layer):
    n, d = h.shape
    p = slot_tok.shape[0]
    nblk = p // MOE_BLOCK
    ff = EXPERT_FF
    hbm = pl.BlockSpec(memory_space=pl.ANY)
    grid_spec = pltpu.PrefetchScalarGridSpec(
        num_scalar_prefetch=4, grid=(nblk,),
        in_specs=[hbm, hbm, hbm, hbm],
        out_specs=pl.BlockSpec((MOE_BLOCK, d), lambda i, be, nu, st, nx: (i, 0)),
        scratch_shapes=[pltpu.VMEM((2, MOE_BLOCK, d), h.dtype), pltpu.SemaphoreType.DMA((2,)),
                        pltpu.VMEM((d, ff), F32), pltpu.VMEM((d, ff), F32), pltpu.VMEM((ff, d), F32),
                        pltpu.SemaphoreType.DMA(()),
                        pltpu.VMEM((d, 2 * ff), BF16), pltpu.VMEM((ff, d), BF16)])
    return pl.pallas_call(
        functools.partial(_expert_kernel, layer=layer), grid_spec=grid_spec,
        out_shape=jax.ShapeDtypeStruct((p, d), F32),
        compiler_params=_cparams(1), name="moe_experts",
    )(blk_expert, nused, slot_tok, next_expert, h, w_gate, w_up, w_down)


def _slots_kernel(dest_ref, slot_ref):
    def clear(s, carry):
        slot_ref[s] = 0
        return carry
    lax.fori_loop(0, slot_ref.shape[0], clear, 0, unroll=16)

    def body(t, carry):
        for k in range(MOE_TOPK):
            slot_ref[dest_ref[t * MOE_TOPK + k]] = t
        return carry
    lax.fori_loop(0, dest_ref.shape[0] // MOE_TOPK, body, 0, unroll=8)


def moe_slots(dest_flat, p):
    smem = pl.BlockSpec(memory_space=pltpu.SMEM)
    return pl.pallas_call(
        _slots_kernel,
        in_specs=[smem],
        out_specs=smem,
        out_shape=jax.ShapeDtypeStruct((p,), I32),
        name="moe_slots",
    )(dest_flat)


def _combine_kernel(dcur_ref, dnxt_ref, info_ref, x_ref, y_hbm, out_ref, ybuf, sem, *, tb, nsteps):
    i = pl.program_id(0)
    slot = i % 2

    def issue(dref, s):
        def body(r, carry):
            for k in range(MOE_TOPK):
                _row_copy(y_hbm, dref[0, 0, MOE_TOPK * r + k], ybuf.at[s, k], r, sem.at[s]).start(
                    priority=k % 2)
            return carry
        lax.fori_loop(0, tb, body, 0, unroll=8)

    @pl.when(i == 0)
    def _():
        issue(dcur_ref, 0)

    @pl.when(i + 1 < nsteps)
    def _():
        issue(dnxt_ref, 1 - slot)

    def drain(r, carry):
        for k in range(MOE_TOPK):
            _row_copy(y_hbm, 0, ybuf.at[slot, k], 0, sem.at[slot]).wait()
        return carry
    lax.fori_loop(0, tb, drain, 0, unroll=8)

    info = info_ref[...]
    out = x_ref[...]
    for k in range(MOE_TOPK):
        out = out + ybuf[slot, k] * info[:, INFO_GATE + k:INFO_GATE + k + 1]
    out_ref[...] = out


def moe_combine(x, y, dest, info):
    n, d = x.shape
    tb = _pick(n, COMBINE_ROWS)
    nsteps = n // tb
    kern = functools.partial(_combine_kernel, tb=tb, nsteps=nsteps)
    dest3 = dest.reshape(nsteps, 1, MOE_TOPK * tb)
    return pl.pallas_call(
        kern, grid=(nsteps,),
        in_specs=[pl.BlockSpec((1, 1, MOE_TOPK * tb), lambda i: (i, 0, 0), memory_space=pltpu.SMEM),
                  pl.BlockSpec((1, 1, MOE_TOPK * tb), lambda i: (jnp.minimum(i + 1, nsteps - 1), 0, 0),
                               memory_space=pltpu.SMEM),
                  pl.BlockSpec((tb, LANES), lambda i: (i, 0)),
                  pl.BlockSpec((tb, d), lambda i: (i, 0)),
                  pl.BlockSpec(memory_space=pl.ANY)],
        out_specs=pl.BlockSpec((tb, d), lambda i: (i, 0)),
        out_shape=jax.ShapeDtypeStruct((n, d), x.dtype),
        scratch_shapes=[pltpu.VMEM((2, MOE_TOPK, tb, d), y.dtype), pltpu.SemaphoreType.DMA((2,))],
        compiler_params=_cparams(1), name="moe_combine",
    )(dest3, dest3, info, x, y)


def hier_moe_layer(x, norm_g, w_rg, w_re, w_gate, w_up, w_down, layer):
    n, d = x.shape
    w_router = jnp.concatenate(
        [w_rg, w_re, jnp.zeros((d, LANES - MOE_GROUPS - N_EXPERTS), F32)], axis=1)
    h, info, cnt = moe_router(x, norm_g, w_router)
    counts = cnt[0, ROUTE_E0:ROUTE_E0 + N_EXPERTS].astype(I32)
    eid = info[:, INFO_EID:INFO_EID + MOE_TOPK].astype(I32)
    rank = info[:, INFO_RANK:INFO_RANK + MOE_TOPK].astype(I32)
    padded = (counts + MOE_BLOCK - 1) // MOE_BLOCK * MOE_BLOCK
    pad_end = jnp.cumsum(padded)
    pad_start = pad_end - padded
    onehot = (eid[..., None] == jnp.arange(N_EXPERTS, dtype=I32)).astype(F32)
    dest = jnp.dot(onehot, pad_start.astype(F32), precision=lax.Precision.HIGHEST).astype(I32) + rank
    a = n * MOE_TOPK
    p = (-(-a // MOE_BLOCK) + N_EXPERTS) * MOE_BLOCK
    nblk = p // MOE_BLOCK
    blk_start = jnp.arange(nblk, dtype=I32) * MOE_BLOCK
    blk_expert = jnp.minimum(jnp.sum((pad_end[None, :] <= blk_start[:, None]).astype(I32), axis=1),
                             N_EXPERTS - 1).astype(I32)
    nused = (pad_end[-1:] // MOE_BLOCK).astype(I32)
    slot_tok = moe_slots(dest.reshape(-1), p)
    ids = jnp.arange(N_EXPERTS, dtype=I32)
    owner = jnp.where(counts > 0, ids, N_EXPERTS)
    after = jnp.concatenate([lax.cummin(owner, reverse=True)[1:], jnp.full((1,), N_EXPERTS, I32)])
    next_expert = jnp.where(after < N_EXPERTS, after, -1).astype(I32)
    y = moe_experts(h, slot_tok, blk_expert, nused, next_expert, w_gate, w_up, w_down, layer)
    return moe_combine(x, y, dest, info)


def _pad_lanes(v, offset, width=LANES):
    out = jnp.zeros((1, width), F32)
    return out.at[0, offset:offset + v.shape[0]].set(v.astype(F32))


def hybrid_layer(x, pos, norm_g, w_in, q_norm, k_norm, kidx_norm, conv_w, conv_b, dt_bias, a_log,
                 d_skip, ssm_norm, w_out, batch, seq, k_sel):
    n, d = x.shape
    offs = np.cumsum([0, ATTN_INNER, ATTN_KV, ATTN_KV, IDX_INNER, IDX_HEAD_DIM, IDX_HEADS,
                      SSM_INNER, SSM_CONV_DIM, SSM_HEADS])
    seg = {name: w_in[:, offs[j]:offs[j + 1]] for j, name in
           enumerate(["q", "k", "v", "qi", "ki", "wi", "z", "xbc", "dt"])}
    tail_pad = LANES - IDX_HEAD_DIM - IDX_HEADS - SSM_HEADS
    w_perm = jnp.concatenate(
        [seg["xbc"], seg["k"], seg["v"], seg["z"], seg["q"], seg["qi"], seg["ki"], seg["wi"],
         seg["dt"], jnp.zeros((d, tail_pad + HYB_COLS - COL_TAIL - LANES), F32)], axis=1).astype(BF16)
    proj = fused_matmul([x], w_perm, g=norm_g, out_dtype=F32, tn=1024, name="hyb_in_proj")

    rot = ATTN_HEAD_DIM // ROPE_FRACTION
    rot_i = IDX_HEAD_DIM // ROPE_FRACTION
    inv16 = jnp.power(ROPE_THETA, -(jnp.arange(rot // 2, dtype=F32) * 2.0 / rot))
    inv8 = jnp.power(ROPE_THETA, -(jnp.arange(rot_i // 2, dtype=F32) * 2.0 / rot_i))
    inv_c = jnp.concatenate(
        [inv16, inv8, jnp.zeros((LANES - rot // 2 - rot_i // 2,), F32)]).reshape(1, LANES)
    q_r, k_r, v_r, qi_r, ki_r, wi = hyb_prep(
        proj, pos, inv_c, q_norm.reshape(1, LANES), k_norm.reshape(1, LANES),
        _pad_lanes(kidx_norm, TAIL_KI))
    attn = dsa_attention(q_r, qi_r, wi, k_r, v_r, ki_r, batch, seq, k_sel)

    ssm = ssd_mixer(proj, conv_w, conv_b.reshape(1, SSM_CONV_DIM), _pad_lanes(dt_bias, TAIL_DT),
                    _pad_lanes(a_log, TAIL_DT), jnp.repeat(d_skip, SSM_HEAD_DIM).reshape(1, SSM_INNER),
                    ssm_norm.reshape(1, SSM_INNER), batch, seq)
    return fused_matmul([attn, ssm], w_out.astype(BF16), res=x, out_dtype=F32, name="hyb_out_proj")


def retention_layer(x, pos, norm_g, w_in, ret_norm, w_out, batch, seq):
    proj = fused_matmul([x], w_in.astype(BF16), g=norm_g, out_dtype=BF16, tn=1024, name="ret_in_proj")
    o = retention_mixer(proj, pos, ret_norm, batch, seq)
    return fused_matmul([o], w_out.astype(BF16), res=x, out_dtype=F32, name="ret_out_proj")


def kernel(x, positions, mix_norm, ffn_norm, hyb_w_in, attn_q_norm, attn_k_norm, idx_k_norm, ssm_conv_w, ssm_conv_b, ssm_dt_bias, ssm_a_log, ssm_d, ssm_norm, hyb_w_out, ret_w_in, ret_norm, ret_w_out, moe_router_group, moe_router_expert, moe_w_gate, moe_w_up, moe_w_down):
    batch, seq, d = x.shape
    depth = mix_norm.shape[0]
    k_sel = min(TOPK_MAX, seq // 4)
    n = batch * seq
    xf = x.reshape(n, d)
    pos = positions.reshape(n, 1).astype(I32)
    for layer in range(depth):
        i = layer // 2
        if layer % 2 == 0:
            xf = hybrid_layer(xf, pos, mix_norm[layer], hyb_w_in[i], attn_q_norm[i], attn_k_norm[i],
                              idx_k_norm[i], ssm_conv_w[i], ssm_conv_b[i], ssm_dt_bias[i],
                              ssm_a_log[i], ssm_d[i], ssm_norm[i], hyb_w_out[i], batch, seq, k_sel)
        else:
            xf = retention_layer(xf, pos, mix_norm[layer], ret_w_in[i], ret_norm[i], ret_w_out[i],
                                 batch, seq)
        xf = hier_moe_layer(xf, ffn_norm[layer], moe_router_group[layer], moe_router_expert[layer],
                            moe_w_gate, moe_w_up, moe_w_down, layer)
    return xf.reshape(batch, seq, d)
```

```python
import functools

import jax
import jax.numpy as jnp
import numpy as np
from jax import lax
from jax.experimental import pallas as pl
from jax.experimental.pallas import tpu as pltpu

F32 = jnp.float32
BF16 = jnp.bfloat16
I32 = jnp.int32
I16 = jnp.int16
HALF_BITS = 16
COUNT_GROUP = 4

ATTN_HEADS = 8
ATTN_KV_HEADS = 2
ATTN_GROUP = ATTN_HEADS // ATTN_KV_HEADS
ATTN_HEAD_DIM = 128
ATTN_INNER = ATTN_HEADS * ATTN_HEAD_DIM
ATTN_KV = ATTN_KV_HEADS * ATTN_HEAD_DIM
IDX_HEADS = 8
IDX_HEAD_DIM = 64
IDX_INNER = IDX_HEADS * IDX_HEAD_DIM
TOPK_MAX = 256
Q_BLOCK = 128
ROPE_THETA = 500000.0
ROPE_FRACTION = 4

SSM_HEADS = 16
SSM_HEAD_DIM = 64
SSM_INNER = SSM_HEADS * SSM_HEAD_DIM
SSM_GROUPS = 2
SSM_STATE = 128
SSM_CONV = 4
SSM_CONV_DIM = SSM_INNER + 2 * SSM_GROUPS * SSM_STATE
SSM_CHUNK = 128
SSM_GROUP_INNER = SSM_INNER // SSM_GROUPS

RET_HEADS = 8
RET_QK_DIM = 256
RET_V_DIM = 512
RET_QK_TOTAL = RET_HEADS * RET_QK_DIM
RET_V_TOTAL = RET_HEADS * RET_V_DIM
RET_CHUNK = 128
RET_THETA = 10000.0

MOE_GROUPS = 8
MOE_EXPERTS_PER_GROUP = 8
N_EXPERTS = MOE_GROUPS * MOE_EXPERTS_PER_GROUP
MOE_TOPK = 2
EXPERT_FF = 512
MOE_BLOCK = 256
COMBINE_ROWS = 128

EPS = 1e-6

LANES = 128
SUBLANES = 8
VMEM_LIMIT = 52 * 1024 * 1024

LOG2E = 1.4426950408889634
NEG_BIG = -1e30
INT_MIN = -2147483648

COL_XBC = 0
COL_K = SSM_CONV_DIM
COL_V = COL_K + ATTN_KV
COL_Z = COL_V + ATTN_KV
COL_Q = COL_Z + SSM_INNER
COL_QI = COL_Q + ATTN_INNER
COL_TAIL = COL_QI + IDX_INNER
HYB_COLS = 5120
TAIL_KI = 0
TAIL_WI = IDX_HEAD_DIM
TAIL_DT = TAIL_WI + IDX_HEADS


def _cparams(n_axes):
    return pltpu.CompilerParams(dimension_semantics=("arbitrary",) * n_axes,
                                vmem_limit_bytes=VMEM_LIMIT)


def _pick(n, pref):
    t = min(n, pref)
    while n % t:
        t //= 2
    return t


def _mm_kernel(*refs, n_a, has_norm, has_res, row_chunk):
    a_refs = refs[:n_a]
    pos = n_a
    g_ref = None
    if has_norm:
        g_ref = refs[pos]
        pos += 1
    w_refs = refs[pos:pos + n_a]
    pos += n_a
    res_ref = None
    if has_res:
        res_ref = refs[pos]
        pos += 1
    out_ref = refs[pos]
    pos += 1
    xn_ref = refs[pos] if has_norm else None
    tm = out_ref.shape[0]

    if has_norm:
        @pl.when(pl.program_id(1) == 0)
        def _():
            def body(r, carry):
                rows = pl.ds(pl.multiple_of(r * row_chunk, row_chunk), row_chunk)
                x = a_refs[0][rows, :]
                ms = jnp.mean(x * x, axis=-1, keepdims=True)
                xn_ref[rows, :] = (x * lax.rsqrt(ms + EPS) * g_ref[...]).astype(BF16)
                return carry
            lax.fori_loop(0, tm // row_chunk, body, 0)
        acc = jnp.dot(xn_ref[...], w_refs[0][...], preferred_element_type=F32)
    else:
        acc = jnp.dot(a_refs[0][...], w_refs[0][...], preferred_element_type=F32)
        for p in range(1, n_a):
            acc = acc + jnp.dot(a_refs[p][...], w_refs[p][...], preferred_element_type=F32)
    if has_res:
        acc = acc + res_ref[...]
    out_ref[...] = acc.astype(out_ref.dtype)


def fused_matmul(a_list, w, *, g=None, res=None, out_dtype=F32, tm=1024, tn=512, name="mm"):
    n = a_list[0].shape[0]
    kp = a_list[0].shape[1]
    m = w.shape[1]
    tm = _pick(n, tm)
    tn = _pick(m, tn)
    n_a = len(a_list)
    has_norm = g is not None
    has_res = res is not None
    in_specs = [pl.BlockSpec((tm, kp), lambda i, j: (i, 0)) for _ in a_list]
    args = list(a_list)
    if has_norm:
        in_specs.append(pl.BlockSpec((1, kp), lambda i, j: (0, 0)))
        args.append(g.reshape(1, kp).astype(F32))
    for p in range(n_a):
        in_specs.append(pl.BlockSpec((kp, tn), lambda i, j, p=p: (p, j)))
        args.append(w)
    if has_res:
        in_specs.append(pl.BlockSpec((tm, tn), lambda i, j: (i, j)))
        args.append(res)
    scratch = [pltpu.VMEM((tm, kp), BF16)] if has_norm else []
    kern = functools.partial(_mm_kernel, n_a=n_a, has_norm=has_norm, has_res=has_res,
                             row_chunk=_pick(tm, 128))
    return pl.pallas_call(
        kern,
        grid=(n // tm, m // tn),
        in_specs=in_specs,
        out_specs=pl.BlockSpec((tm, tn), lambda i, j: (i, j)),
        out_shape=jax.ShapeDtypeStruct((n, m), out_dtype),
        scratch_shapes=scratch,
        compiler_params=_cparams(2),
        name=name,
    )(*args)


def _rope_lanes(x, cosf, s_neg, s_pos, half):
    width = x.shape[-1]
    return (x * cosf + pltpu.roll(x, width - half, axis=1) * s_neg
            + pltpu.roll(x, half, axis=1) * s_pos)


def _prep_kernel(q_ref, k_ref, v_ref, qi_ref, tail_ref, pos_ref, invc_ref, pcos_ref, pneg_ref, ppos_ref,
                 qn_ref, kn_ref, kin_ref,
                 qo_ref, ko_ref, vo_ref, qio_ref, kio_ref, wio_ref):
    tb = q_ref.shape[0]
    posf = pos_ref[...].astype(F32)
    lane = lax.broadcasted_iota(I32, (tb, LANES), 1)

    hi = lax.Precision.HIGHEST
    ang = posf * invc_ref[...]
    cos_c = jnp.cos(ang)
    sin_c = jnp.sin(ang)
    cos_all = jnp.dot(cos_c, pcos_ref[...], precision=hi, preferred_element_type=F32)
    sneg_all = jnp.dot(sin_c, pneg_ref[...], precision=hi, preferred_element_type=F32)
    spos_all = jnp.dot(sin_c, ppos_ref[...], precision=hi, preferred_element_type=F32)

    half = ATTN_HEAD_DIM // ROPE_FRACTION // 2
    cosf = cos_all[:, 0:LANES]
    s_neg = sneg_all[:, 0:LANES]
    s_pos = spos_all[:, 0:LANES]
    scale = ATTN_HEAD_DIM ** -0.5 * LOG2E
    for h in range(ATTN_HEADS):
        x = q_ref[:, h * LANES:(h + 1) * LANES]
        ms = jnp.mean(x * x, axis=-1, keepdims=True)
        x = x * lax.rsqrt(ms + EPS) * qn_ref[...]
        x = _rope_lanes(x, cosf, s_neg, s_pos, half)
        qo_ref[h] = (x * scale).astype(BF16)
    for h in range(ATTN_KV_HEADS):
        x = k_ref[:, h * LANES:(h + 1) * LANES]
        ms = jnp.mean(x * x, axis=-1, keepdims=True)
        x = x * lax.rsqrt(ms + EPS) * kn_ref[...]
        x = _rope_lanes(x, cosf, s_neg, s_pos, half)
        ko_ref[h] = x.T.astype(BF16)
        vo_ref[h] = v_ref[:, h * LANES:(h + 1) * LANES].astype(BF16)

    half_i = IDX_HEAD_DIM // ROPE_FRACTION // 2
    cos_i = cos_all[:, LANES:LANES + IDX_INNER]
    sn_i = sneg_all[:, LANES:LANES + IDX_INNER]
    sp_i = spos_all[:, LANES:LANES + IDX_INNER]
    qi = _rope_lanes(qi_ref[...], cos_i, sn_i, sp_i, half_i)
    for h in range(IDX_HEADS):
        slab = qi[:, (h // 2) * LANES:(h // 2 + 1) * LANES]
        if h % 2:
            slab = pltpu.roll(slab, IDX_HEAD_DIM, axis=1)
        qio_ref[h] = jnp.where(lane < IDX_HEAD_DIM, slab, 0.0).astype(BF16)

    t = tail_ref[...]
    is_ki = lane < IDX_HEAD_DIM
    ms = jnp.sum(jnp.where(is_ki, t * t, 0.0), axis=-1, keepdims=True) * (1.0 / IDX_HEAD_DIM)
    kin = t * lax.rsqrt(ms + EPS) * kin_ref[...]
    kin = _rope_lanes(kin, cos_i[:, 0:LANES], sn_i[:, 0:LANES], sp_i[:, 0:LANES], half_i)
    kio_ref[...] = jnp.where(is_ki, kin, 0.0).T.astype(BF16)
    wio_ref[...] = t * (IDX_HEADS ** -0.5 * IDX_HEAD_DIM ** -0.5)


def _rope_selectors():
    a_half = ATTN_HEAD_DIM // ROPE_FRACTION // 2
    i_half = IDX_HEAD_DIM // ROPE_FRACTION // 2
    unit = LANES - 1
    width = LANES + IDX_INNER
    pcos = np.zeros((LANES, width), np.float32)
    pneg = np.zeros((LANES, width), np.float32)
    ppos = np.zeros((LANES, width), np.float32)
    for m in range(LANES):
        if m < 2 * a_half:
            pcos[m % a_half, m] = 1.0
            (pneg if m < a_half else ppos)[m % a_half, m] = -1.0 if m < a_half else 1.0
        else:
            pcos[unit, m] = 1.0
    for m in range(IDX_INNER):
        r = m % IDX_HEAD_DIM
        if r < 2 * i_half:
            pcos[a_half + r % i_half, LANES + m] = 1.0
            (pneg if r < i_half else ppos)[a_half + r % i_half, LANES + m] = -1.0 if r < i_half else 1.0
        else:
            pcos[unit, LANES + m] = 1.0
    return jnp.asarray(pcos), jnp.asarray(pneg), jnp.asarray(ppos)


def hyb_prep(proj, pos, inv_c, q_norm, k_norm, kidx_norm_pad):
    n = proj.shape[0]
    tb = _pick(n, 256)
    pcos, pneg, ppos = _rope_selectors()
    sel_w = LANES + IDX_INNER

    def col(width, off):
        return pl.BlockSpec((tb, width), lambda i: (i, off // width))

    def full(shape):
        return pl.BlockSpec(shape, lambda i: (0,) * len(shape))

    in_specs = [col(ATTN_INNER, COL_Q), col(ATTN_KV, COL_K), col(ATTN_KV, COL_V),
                col(IDX_INNER, COL_QI), col(LANES, COL_TAIL),
                pl.BlockSpec((tb, 1), lambda i: (i, 0)),
                full((1, LANES)), full((LANES, sel_w)), full((LANES, sel_w)), full((LANES, sel_w)),
                full((1, LANES)), full((1, LANES)), full((1, LANES))]
    out_shape = [jax.ShapeDtypeStruct((ATTN_HEADS, n, LANES), BF16),
                 jax.ShapeDtypeStruct((ATTN_KV_HEADS, LANES, n), BF16),
                 jax.ShapeDtypeStruct((ATTN_KV_HEADS, n, LANES), BF16),
                 jax.ShapeDtypeStruct((IDX_HEADS, n, LANES), BF16),
                 jax.ShapeDtypeStruct((LANES, n), BF16),
                 jax.ShapeDtypeStruct((n, LANES), F32)]
    out_specs = [pl.BlockSpec((ATTN_HEADS, tb, LANES), lambda i: (0, i, 0)),
                 pl.BlockSpec((ATTN_KV_HEADS, LANES, tb), lambda i: (0, 0, i)),
                 pl.BlockSpec((ATTN_KV_HEADS, tb, LANES), lambda i: (0, i, 0)),
                 pl.BlockSpec((IDX_HEADS, tb, LANES), lambda i: (0, i, 0)),
                 pl.BlockSpec((LANES, tb), lambda i: (0, i)),
                 pl.BlockSpec((tb, LANES), lambda i: (i, 0))]
    return pl.pallas_call(
        _prep_kernel, grid=(n // tb,), in_specs=in_specs, out_specs=out_specs,
        out_shape=out_shape, compiler_params=_cparams(1), name="hyb_prep",
    )(proj, proj, proj, proj, proj, pos, inv_c, pcos, pneg, ppos, q_norm, k_norm, kidx_norm_pad)


def _sortable(x):
    b = pltpu.bitcast(x, I32)
    return jnp.where(b < 0, b ^ jnp.int32(0x7FFFFFFF), b)


def _dsa_kernel(q_ref, qi_ref, wi_ref, k_ref, v_ref, ki_ref, triu_ref, eye_ref, onec_ref, out_ref,
                keys_ref, half_ref, bias_ref, wb_ref, mx_ref, acc_ref, *, k_sel, kc):
    i = pl.program_id(1)
    tq = Q_BLOCK
    nkc = (i * tq + tq + kc - 1) // kc
    row = i * tq + lax.broadcasted_iota(I32, (tq, kc), 0)
    lane = lax.broadcasted_iota(I32, (tq, kc), 1)
    nslab = kc // LANES

    w = wi_ref[...]
    for h in range(IDX_HEADS):
        wb_ref[h] = jnp.broadcast_to(w[:, TAIL_WI + h:TAIL_WI + h + 1], (tq, LANES))

    def idx_body(c, carry):
        cols = pl.ds(pl.multiple_of(c * kc, kc), kc)
        d = jnp.dot(qi_ref[...].reshape(IDX_HEADS * tq, LANES), ki_ref[:, cols],
                    preferred_element_type=F32)
        slabs = [jnp.zeros((tq, LANES), F32)] * nslab
        for h in range(IDX_HEADS):
            wh = wb_ref[h]
            dh = d[h * tq:(h + 1) * tq, :]
            slabs = [slabs[a] + jnp.maximum(dh[:, a * LANES:(a + 1) * LANES], 0.0) * wh
                     for a in range(nslab)]
        sc = jnp.concatenate(slabs, axis=1)
        causal = (c * kc + lane) <= row
        keys_ref[:, cols] = jnp.where(causal, _sortable(sc), jnp.int32(INT_MIN))
        return carry
    lax.fori_loop(0, nkc, idx_body, 0)

    def fold(x, op):
        acc = x[:, 0:LANES]
        for j in range(1, nslab):
            acc = op(acc, x[:, j * LANES:(j + 1) * LANES])
        return acc

    i16_min = -(1 << (HALF_BITS - 1))

    def count16(cand, strict):
        c16 = cand.astype(I16)

        def hits(start, width):
            kk = half_ref[:, pl.ds(pl.multiple_of(start, kc), width)]
            hit = (kk > c16) if strict else (kk >= c16)
            x = jnp.where(hit, jnp.int16(1), jnp.int16(0))
            acc = x[:, 0:LANES]
            for j in range(1, width // LANES):
                acc = acc + x[:, j * LANES:(j + 1) * LANES]
            return acc

        ngrp = nkc // COUNT_GROUP
        acc = lax.fori_loop(0, ngrp, lambda g, a: a + hits(g * (COUNT_GROUP * kc), COUNT_GROUP * kc),
                            jnp.zeros((tq, LANES), I16))
        acc = lax.fori_loop(ngrp * COUNT_GROUP, nkc, lambda c, a: a + hits(c * kc, kc), acc)
        return jnp.sum(acc.astype(F32), axis=1, keepdims=True)

    def kth_largest16(kf):
        zero = jnp.zeros((tq, 1), I32)
        lo = jnp.where(count16(zero, False) >= kf, zero, jnp.full((tq, 1), i16_min, I32))

        def bis_body(it, lo):
            cand = lo + jnp.left_shift(jnp.int32(1), jnp.int32(HALF_BITS - 2) - it)
            return jnp.where(count16(cand, False) >= kf, cand, lo)
        return lax.fori_loop(0, HALF_BITS - 1, bis_body, lo)

    def fill_half(fn):
        def body(c, carry):
            cols = pl.ds(pl.multiple_of(c * kc, kc), kc)
            half_ref[:, cols] = fn(keys_ref[:, cols]).astype(I16)
            return carry
        lax.fori_loop(0, nkc, body, 0)

    kf = jnp.float32(k_sel)
    fill_half(lambda kk: jnp.right_shift(kk, HALF_BITS))
    thr_hi = kth_largest16(kf)
    above = count16(thr_hi, True)
    low_mask = (1 << HALF_BITS) - 1
    fill_half(lambda kk: jnp.where(jnp.right_shift(kk, HALF_BITS) == thr_hi,
                                   (kk & low_mask) + i16_min, i16_min))
    thr_lo = kth_largest16(kf - above)
    thr = jnp.left_shift(thr_hi, HALF_BITS) + (thr_lo - i16_min)
    cnt_gt_lo = count16(thr_lo, True)
    need = kf - above - cnt_gt_lo
    n_eq = count16(thr_lo, False) - cnt_gt_lo
    row_plain = jnp.logical_or(jnp.logical_and(n_eq == need, thr_lo > i16_min),
                               thr == jnp.int32(INT_MIN))
    all_plain = jnp.min(jnp.where(row_plain, 1.0, 0.0)) > 0.5

    @pl.when(all_plain)
    def _():
        def fin_body(c, carry):
            cols = pl.ds(pl.multiple_of(c * kc, kc), kc)
            causal = (c * kc + lane) <= row
            sel = (keys_ref[:, cols] >= thr) & causal
            bias_ref[:, cols] = jnp.where(sel, 0.0, NEG_BIG).astype(BF16)
            return carry
        lax.fori_loop(0, nkc, fin_body, 0)

    @pl.when(jnp.logical_not(all_plain))
    def _():
        def fin_body(c, run):
            cols = pl.ds(pl.multiple_of(c * kc, kc), kc)
            kk = keys_ref[:, cols]
            eq = kk == thr
            eqf = jnp.where(eq, 1.0, 0.0)
            incl = jnp.dot(eqf.astype(BF16), triu_ref[...], preferred_element_type=F32)
            sel = (kk > thr) | (eq & ((run + incl) <= need))
            causal = (c * kc + lane) <= row
            bias_ref[:, cols] = jnp.where(sel & causal, 0.0, NEG_BIG).astype(BF16)
            return run + jnp.sum(eqf, axis=1, keepdims=True)
        lax.fori_loop(0, nkc, fin_body, jnp.zeros((tq, 1), F32))

    gq = ATTN_GROUP * tq

    def logits(j, kx):
        qg = q_ref[j * ATTN_GROUP:(j + 1) * ATTN_GROUP].reshape(gq, LANES)
        qx = jnp.concatenate([qg, eye_ref[...]], axis=1)
        return jnp.dot(qx, kx, preferred_element_type=F32)

    mx_ref[...] = jnp.full(mx_ref.shape, NEG_BIG, F32)
    acc_ref[...] = jnp.zeros(acc_ref.shape, F32)

    def att_body(c, carry):
        cols = pl.ds(pl.multiple_of(c * kc, kc), kc)
        bt = bias_ref[:, cols]
        for j in range(ATTN_KV_HEADS):
            kx = jnp.concatenate([k_ref[j, :, cols], bt], axis=0)
            vx = jnp.concatenate([v_ref[j, cols, :], onec_ref[...]], axis=1)
            s = logits(j, kx)
            m_old = mx_ref[j]
            m_new = jnp.maximum(m_old, jnp.max(fold(s, jnp.maximum), axis=1, keepdims=True))
            alpha = jnp.exp2(m_old - m_new)
            p = jnp.concatenate(
                [jnp.exp2(s[:, a * LANES:(a + 1) * LANES] - m_new) for a in range(nslab)], axis=1)
            acc_ref[j] = (acc_ref[j] * jnp.concatenate([alpha, alpha], axis=1)
                          + jnp.dot(p.astype(BF16), vx, preferred_element_type=F32))
            mx_ref[j] = m_new
        return carry
    lax.fori_loop(0, nkc, att_body, 0)

    for h in range(ATTN_HEADS):
        a = acc_ref[h // ATTN_GROUP, (h % ATTN_GROUP) * tq:(h % ATTN_GROUP + 1) * tq, :]
        out_ref[:, h * LANES:(h + 1) * LANES] = (
            a[:, 0:LANES] / a[:, LANES:LANES + 1]).astype(out_ref.dtype)


def dsa_attention(q_r, qi_r, wi, k_r, v_r, ki_r, batch, seq, k_sel):
    n = batch * seq
    nq = seq // Q_BLOCK
    kc = _pick(seq, 1024)
    triu =jnp.asarray((np.arange(kc)[:, None] <= np.arange(kc)[None, :]).astype(np.float32), BF16)
    gq = ATTN_GROUP * Q_BLOCK
    eye = jnp.asarray(np.tile(np.eye(Q_BLOCK, dtype=np.float32), (ATTN_GROUP, 1)), BF16)
    onec = np.zeros((kc, LANES), np.float32)
    onec[:, 0] = 1.0
    onec = jnp.asarray(onec, BF16)
    kern = functools.partial(_dsa_kernel, k_sel=k_sel, kc=kc)
    in_specs = [
        pl.BlockSpec((ATTN_HEADS, Q_BLOCK, LANES), lambda b, i: (0, b * nq + i, 0)),
        pl.BlockSpec((IDX_HEADS, Q_BLOCK, LANES), lambda b, i: (0, b * nq + i, 0)),
        pl.BlockSpec((Q_BLOCK, LANES), lambda b, i: (b * nq + i, 0)),
        pl.BlockSpec((ATTN_KV_HEADS, LANES, seq), lambda b, i: (0, 0, b)),
        pl.BlockSpec((ATTN_KV_HEADS, seq, LANES), lambda b, i: (0, b, 0)),
        pl.BlockSpec((LANES, seq), lambda b, i: (0, b)),
        pl.BlockSpec((kc, kc), lambda b, i: (0, 0)),
        pl.BlockSpec((gq, Q_BLOCK), lambda b, i: (0, 0)),
        pl.BlockSpec((kc, LANES), lambda b, i: (0, 0)),
    ]
    scratch = [pltpu.VMEM((Q_BLOCK, seq), I32), pltpu.VMEM((Q_BLOCK, seq), I16),
               pltpu.VMEM((Q_BLOCK, seq), BF16),
               pltpu.VMEM((IDX_HEADS, Q_BLOCK, LANES), F32),
               pltpu.VMEM((ATTN_KV_HEADS, gq, LANES), F32),
               pltpu.VMEM((ATTN_KV_HEADS, gq, 2 * LANES), F32)]
    return pl.pallas_call(
        kern, grid=(batch, nq), in_specs=in_specs,
        out_specs=pl.BlockSpec((Q_BLOCK, ATTN_INNER), lambda b, i: (b * nq + i, 0)),
        out_shape=jax.ShapeDtypeStruct((n, ATTN_INNER), BF16),
        scratch_shapes=scratch, compiler_params=_cparams(2), name="dsa_attention",
    )(q_r, qi_r, wi, k_r, v_r, ki_r, triu, eye, onec)


def _silu(x):
    return x * (1.0 / (1.0 + jnp.exp(-x)))


def _ssd_kernel(xbc_ref, z_ref, tail_ref, cw_ref, cb_ref, dtb_ref, alog_ref, dexp_ref, ng_ref,
                tri_ref, e_ref, out_ref, state_ref, ext_ref, y_ref):
    q = SSM_CHUNK
    hi = lax.Precision.HIGHEST

    @pl.when(pl.program_id(1) == 0)
    def _():
        state_ref[...] = jnp.zeros(state_ref.shape, F32)
        ext_ref[0:SUBLANES, :] = jnp.zeros((SUBLANES, SSM_CONV_DIM), F32)

    x = xbc_ref[...]
    ext_ref[SUBLANES:SUBLANES + q, :] = x
    conv = x * cw_ref[SSM_CONV - 1:SSM_CONV, :] + cb_ref[...]
    for k in range(1, SSM_CONV):
        conv = conv + ext_ref[SUBLANES - k:SUBLANES - k + q, :] * cw_ref[SSM_CONV - 1 - k:SSM_CONV - k, :]
    ext_ref[0:SUBLANES, :] = x[q - SUBLANES:q, :]
    xbc = _silu(conv)
    xs = xbc[:, 0:SSM_INNER]
    bm = xbc[:, SSM_INNER:SSM_INNER + SSM_GROUPS * SSM_STATE]
    cm = xbc[:, SSM_INNER + SSM_GROUPS * SSM_STATE:SSM_CONV_DIM]

    lane = lax.broadcasted_iota(I32, (q, LANES), 1)
    is_dt = (lane >= TAIL_DT) & (lane < TAIL_DT + SSM_HEADS)
    raw = tail_ref[...] + dtb_ref[...]
    dt = jnp.where(is_dt, jnp.maximum(raw, 0.0) + jnp.log1p(jnp.exp(-jnp.abs(raw))), 0.0)
    a = -jnp.exp(alog_ref[...])
    da = jnp.where(is_dt, dt * a, 0.0)
    acum = jnp.dot(tri_ref[...], da, precision=hi, preferred_element_type=F32)
    acum_t = acum.T
    acum_x = jnp.dot(acum, e_ref[...], precision=hi, preferred_element_type=F32)
    dt_x = jnp.dot(dt, e_ref[...], precision=hi, preferred_element_type=F32)
    last_x = acum_x[q - 1:q, :]
    xdt = xs * dt_x
    xdec = xdt * jnp.exp(last_x - acum_x)
    ri = lax.broadcasted_iota(I32, (q, q), 0)
    ci = lax.broadcasted_iota(I32, (q, q), 1)
    tril = ri >= ci
    pairs_per_group = SSM_GROUP_INNER // LANES

    for g in range(SSM_GROUPS):
        bg = bm[:, g * SSM_STATE:(g + 1) * SSM_STATE]
        cg = cm[:, g * SSM_STATE:(g + 1) * SSM_STATE]
        cgb = cg.astype(BF16)
        gsl = slice(g * SSM_GROUP_INNER, (g + 1) * SSM_GROUP_INNER)
        cb = lax.dot_general(cgb, bg.astype(BF16), (((1,), (1,)), ((), ())),
                             preferred_element_type=F32)
        y_off = jnp.dot(cgb, state_ref[g].astype(BF16), preferred_element_type=F32)
        y_ref[:, gsl] = y_off * jnp.exp(acum_x[:, gsl])
        for mth in range(pairs_per_group):
            slab = slice((g * pairs_per_group + mth) * LANES, (g * pairs_per_group + mth + 1) * LANES)
            xp = xdt[:, slab]
            yd = None
            for side in range(2):
                hl = TAIL_DT + (g * pairs_per_group + mth) * 2 + side
                seg = acum[:, hl:hl + 1] - acum_t[hl:hl + 1, :]
                lmat = jnp.exp(jnp.where(tril, seg, NEG_BIG))
                if side == 0:
                    xh = jnp.where(lane < SSM_HEAD_DIM, xp, 0.0)
                else:
                    xh = jnp.where(lane >= SSM_HEAD_DIM, xp, 0.0)
                part = jnp.dot((cb * lmat).astype(BF16), xh.astype(BF16), preferred_element_type=F32)
                yd = part if yd is None else yd + part
            y_ref[:, slab] = y_ref[:, slab] + yd
        st = jnp.dot(bg.T.astype(BF16), xdec[:, gsl].astype(BF16), preferred_element_type=F32)
        state_ref[g] = state_ref[g] * jnp.exp(last_x[:, gsl]) + st

    y = (y_ref[...] + dexp_ref[...] * xs) * _silu(z_ref[...])
    for g in range(SSM_GROUPS):
        gsl = slice(g * SSM_GROUP_INNER, (g + 1) * SSM_GROUP_INNER)
        yg = y[:, gsl]
        ms = jnp.mean(yg * yg, axis=-1, keepdims=True)
        out_ref[:, gsl] = (yg * lax.rsqrt(ms + EPS) * ng_ref[:, gsl]).astype(out_ref.dtype)


def ssd_mixer(proj, conv_w, conv_b, dtb_pad, alog_pad, d_exp, norm_g, batch, seq):
    n = batch * seq
    q = SSM_CHUNK
    nc = seq // q
    tri = jnp.asarray((np.arange(q)[:, None] >= np.arange(q)[None, :]).astype(np.float32))
    e = np.zeros((LANES, SSM_INNER), np.float32)
    for h in range(SSM_HEADS):
        e[TAIL_DT + h, h * SSM_HEAD_DIM:(h + 1) * SSM_HEAD_DIM] = 1.0
    e = jnp.asarray(e)

    def col(width, off):
        return pl.BlockSpec((q, width), lambda b, c: (b * nc + c, off // width))

    def full(shape):
        return pl.BlockSpec(shape, lambda b, c: (0,) * len(shape))

    in_specs = [col(SSM_CONV_DIM, COL_XBC), col(SSM_INNER, COL_Z), col(LANES, COL_TAIL),
                full((SSM_CONV, SSM_CONV_DIM)), full((1, SSM_CONV_DIM)), full((1, LANES)),
                full((1, LANES)), full((1, SSM_INNER)), full((1, SSM_INNER)),
                full((q, q)), full((LANES, SSM_INNER))]
    scratch = [pltpu.VMEM((SSM_GROUPS, SSM_STATE, SSM_GROUP_INNER), F32),
               pltpu.VMEM((SUBLANES + q, SSM_CONV_DIM), F32),
               pltpu.VMEM((q, SSM_INNER), F32)]
    return pl.pallas_call(
        _ssd_kernel, grid=(batch, nc), in_specs=in_specs,
        out_specs=pl.BlockSpec((q, SSM_INNER), lambda b, c: (b * nc + c, 0)),
        out_shape=jax.ShapeDtypeStruct((n, SSM_INNER), BF16),
        scratch_shapes=scratch, compiler_params=_cparams(2), name="ssd_mixer",
    )(proj, proj, proj, conv_w, conv_b, dtb_pad, alog_pad, d_exp, norm_g, tri, e)


def _ret_kernel(q_ref, k_ref, v_ref, g_ref, pos_ref, inv_ref, dmat_ref, qdec_ref, kdec_ref,
                cdec_ref, ng_ref, out_ref, state_ref):
    half = RET_QK_DIM // 2

    @pl.when(pl.program_id(1) == 0)
    def _():
        state_ref[...] = jnp.zeros(state_ref.shape, F32)

    ang = pos_ref[...].astype(F32) * inv_ref[...]
    cosf = jnp.cos(ang)
    sinf = jnp.sin(ang)

    def rope(x):
        x1 = x[:, 0:half]
        x2 = x[:, half:2 * half]
        return jnp.concatenate([x1 * cosf - x2 * sinf, x2 * cosf + x1 * sinf], axis=-1)

    for h in range(RET_HEADS):
        qh = rope(q_ref[:, h * RET_QK_DIM:(h + 1) * RET_QK_DIM].astype(F32))
        kh = rope(k_ref[:, h * RET_QK_DIM:(h + 1) * RET_QK_DIM].astype(F32)) * (RET_QK_DIM ** -0.5)
        vh = v_ref[:, h * RET_V_DIM:(h + 1) * RET_V_DIM]
        qb = qh.astype(BF16)
        s = lax.dot_general(qb, kh.astype(BF16), (((1,), (1,)), ((), ())),
                            preferred_element_type=F32) * dmat_ref[h]
        inner = jnp.dot(s.astype(BF16), vh, preferred_element_type=F32)
        st = state_ref[h]
        cross = jnp.dot(qb, st.astype(BF16), preferred_element_type=F32) * qdec_ref[h]
        kd = (kh * kdec_ref[h]).T.astype(BF16)
        state_ref[h] = st * cdec_ref[h] + jnp.dot(kd, vh, preferred_element_type=F32)
        o = inner + cross
        ms = jnp.mean(o * o, axis=-1, keepdims=True)
        o = o * lax.rsqrt(ms + EPS) * ng_ref[:, h * RET_V_DIM:(h + 1) * RET_V_DIM]
        gate = _silu(g_ref[:, h * RET_V_DIM:(h + 1) * RET_V_DIM].astype(F32))
        out_ref[:, h * RET_V_DIM:(h + 1) * RET_V_DIM] = (gate * o).astype(out_ref.dtype)


def retention_mixer(proj, pos, norm_g, batch, seq):
    n = batch * seq
    q = RET_CHUNK
    nc = seq // q
    half = RET_QK_DIM // 2
    inv = jnp.power(RET_THETA, -(jnp.arange(half, dtype=F32) * 2.0 / RET_QK_DIM)).reshape(1, half)
    log_gamma = jnp.log(1.0 - jnp.power(2.0, -5.0 - jnp.arange(RET_HEADS, dtype=F32)))
    i = jnp.arange(q, dtype=F32)
    diff = i[:, None] - i[None, :]
    dmat = jnp.where(diff[None] >= 0,
                     jnp.exp(jnp.maximum(diff, 0.0)[None] * log_gamma[:, None, None]), 0.0)
    qdec = jnp.exp((i + 1.0)[None, :, None] * log_gamma[:, None, None])
    kdec = jnp.exp((q - 1.0 - i)[None, :, None] * log_gamma[:, None, None])
    cdec = jnp.broadcast_to(jnp.exp(q * log_gamma)[:, None, None], (RET_HEADS, 1, RET_V_DIM))

    def col(width, off):
        return pl.BlockSpec((q, width), lambda b, c: (b * nc + c, off // width))

    def full(shape):
        return pl.BlockSpec(shape, lambda b, c: (0,) * len(shape))

    in_specs = [col(RET_QK_TOTAL, 0), col(RET_QK_TOTAL, RET_QK_TOTAL),
                col(RET_V_TOTAL, 2 * RET_QK_TOTAL), col(RET_V_TOTAL, 2 * RET_QK_TOTAL + RET_V_TOTAL),
                pl.BlockSpec((q, 1), lambda b, c: (b * nc + c, 0)),
                full((1, half)), full((RET_HEADS, q, q)), full((RET_HEADS, q, 1)),
                full((RET_HEADS, q, 1)), full((RET_HEADS, 1, RET_V_DIM)), full((1, RET_V_TOTAL))]
    return pl.pallas_call(
        _ret_kernel, grid=(batch, nc), in_specs=in_specs,
        out_specs=pl.BlockSpec((q, RET_V_TOTAL), lambda b, c: (b * nc + c, 0)),
        out_shape=jax.ShapeDtypeStruct((n, RET_V_TOTAL), BF16),
        scratch_shapes=[pltpu.VMEM((RET_HEADS, RET_QK_DIM, RET_V_DIM), F32)],
        compiler_params=_cparams(2), name="retention",
    )(proj, proj, proj, proj, pos, inv, dmat, qdec, kdec, cdec, norm_g.reshape(1, RET_V_TOTAL))


ROUTE_E0 = MOE_GROUPS
INFO_EID, INFO_GATE, INFO_RANK = 0, 2, 4


def _router_kernel(x_ref, g_ref, wr_ref, lt_ref, h_ref, info_ref, cnt_ref, run_ref):
    tb = x_ref.shape[0]

    @pl.when(pl.program_id(0) == 0)
    def _():
        run_ref[...] = jnp.zeros(run_ref.shape, F32)

    x = x_ref[...]
    ms = jnp.mean(x * x, axis=-1, keepdims=True)
    hn = x * lax.rsqrt(ms + EPS) * g_ref[...]
    h_ref[...] = hn
    h_hi = hn.astype(BF16)
    h_lo = (hn - h_hi.astype(F32)).astype(BF16)
    both = jnp.dot(h_hi, wr_ref[...], preferred_element_type=F32)
    logits = (both[:, 0:LANES] + both[:, LANES:2 * LANES]
              + jnp.dot(h_lo, wr_ref[:, 0:LANES], preferred_element_type=F32))

    lane = lax.broadcasted_iota(I32, (tb, LANES), 1).astype(F32)
    far = jnp.float32(4 * LANES)
    ninf = jnp.float32(-jnp.inf)
    gl = jnp.where(lane < MOE_GROUPS, logits, ninf)
    gmax = jnp.max(gl, axis=1, keepdims=True)
    grp = jnp.min(jnp.where(gl == gmax, lane, far), axis=1, keepdims=True)
    p_grp = 1.0 / jnp.sum(jnp.exp(gl - gmax), axis=1, keepdims=True)
    lo = ROUTE_E0 + grp * MOE_EXPERTS_PER_GROUP
    el = jnp.where((lane >= lo) & (lane < lo + MOE_EXPERTS_PER_GROUP), logits, ninf)
    m1 = jnp.max(el, axis=1, keepdims=True)
    i1 = jnp.min(jnp.where(el == m1, lane, far), axis=1, keepdims=True)
    el2 = jnp.where(lane == i1, ninf, el)
    m2 = jnp.max(el2, axis=1, keepdims=True)
    i2 = jnp.min(jnp.where(el2 == m2, lane, far), axis=1, keepdims=True)
    e2 = jnp.exp(m2 - m1)
    g1 = p_grp / (1.0 + e2)
    g2 = p_grp * e2 / (1.0 + e2)
    oh1 = jnp.where(lane == i1, 1.0, 0.0)
    oh2 = jnp.where(lane == i2, 1.0, 0.0)
    cnt = oh1 + oh2
    before = jnp.dot(lt_ref[...], cnt.astype(BF16), preferred_element_type=F32) + run_ref[...]
    r1 = jnp.sum(oh1 * before, axis=1, keepdims=True)
    r2 = jnp.sum(oh2 * before, axis=1, keepdims=True)
    run_ref[...] = run_ref[...] + jnp.sum(cnt, axis=0, keepdims=True)
    cnt_ref[...] = run_ref[...]

    info = jnp.where(lane == INFO_EID, i1 - ROUTE_E0, 0.0)
    info = jnp.where(lane == INFO_EID + 1, i2 - ROUTE_E0, info)
    info = jnp.where(lane == INFO_GATE, g1, info)
    info = jnp.where(lane == INFO_GATE + 1, g2, info)
    info = jnp.where(lane == INFO_RANK, r1, info)
    info = jnp.where(lane == INFO_RANK + 1, r2, info)
    info_ref[...] = info


def moe_router(x, g, w_router_pad):
    n, d = x.shape
    tb = _pick(n, 512)
    lt = jnp.asarray((np.arange(tb)[:, None] > np.arange(tb)[None, :]).astype(np.float32), BF16)
    w_hi = w_router_pad.astype(BF16)
    w_lo = (w_router_pad - w_hi.astype(F32)).astype(BF16)
    w_split = jnp.concatenate([w_hi, w_lo], axis=1)
    return pl.pallas_call(
        _router_kernel, grid=(n // tb,),
        in_specs=[pl.BlockSpec((tb, d), lambda i: (i, 0)), pl.BlockSpec((1, d), lambda i: (0, 0)),
                  pl.BlockSpec((d, 2 * LANES), lambda i: (0, 0)), pl.BlockSpec((tb, tb), lambda i: (0, 0))],
        out_specs=[pl.BlockSpec((tb, d), lambda i: (i, 0)), pl.BlockSpec((tb, LANES), lambda i: (i, 0)),
                   pl.BlockSpec((1, LANES), lambda i: (0, 0))],
        out_shape=[jax.ShapeDtypeStruct((n, d), F32), jax.ShapeDtypeStruct((n, LANES), F32),
                   jax.ShapeDtypeStruct((1, LANES), F32)],
        scratch_shapes=[pltpu.VMEM((1, LANES), F32)],
        compiler_params=_cparams(1), name="moe_router",
    )(x, g.reshape(1, d), w_split, lt)


def _row_copy(src_ref, src_row, dst_ref, dst_row, sem):
    return pltpu.make_async_copy(src_ref.at[pl.ds(src_row, 1), :], dst_ref.at[pl.ds(dst_row, 1), :], sem)


def _expert_kernel(be_ref, nused_ref, st_ref, nxe_ref, h_hbm, wg_hbm, wu_hbm, wd_hbm, out_ref,
                   xbuf, sem, wgf, wuf, wdf, wsem, wgu_s, wd_s, *, layer):
    i = pl.program_id(0)
    slot = i % 2
    nused = nused_ref[0]
    d = xbuf.shape[2]
    ff = EXPERT_FF

    def weight_copies(e):
        return (pltpu.make_async_copy(wg_hbm.at[layer, e], wgf, wsem),
                pltpu.make_async_copy(wu_hbm.at[layer, e], wuf, wsem),
                pltpu.make_async_copy(wd_hbm.at[layer, e], wdf, wsem))

    @pl.when(jnp.logical_and(i == 0, nused > 0))
    def _():
        for c in weight_copies(be_ref[0]):
            c.start()

    def gather(blk, s):
        def body(r, carry):
            _row_copy(h_hbm, st_ref[blk * MOE_BLOCK + r], xbuf.at[s], r, sem.at[s]).start(priority=1)
            return carry
        lax.fori_loop(0, MOE_BLOCK, body, 0, unroll=8)

    @pl.when(jnp.logical_and(i == 0, nused > 0))
    def _():
        gather(0, 0)

    @pl.when(i + 1 < nused)
    def _():
        gather(i + 1, 1 - slot)

    @pl.when(i < nused)
    def _():
        def drain(r, carry):
            _row_copy(h_hbm, 0, xbuf.at[slot], 0, sem.at[slot]).wait()
            return carry
        lax.fori_loop(0, MOE_BLOCK, drain, 0, unroll=8)

        changed = jnp.logical_or(i == 0, be_ref[i] != be_ref[jnp.maximum(i - 1, 0)])

        @pl.when(changed)
        def _():
            for c in weight_copies(be_ref[i]):
                c.wait()
            rc = _pick(d, 256)

            def body(r, carry):
                rows = pl.ds(pl.multiple_of(r * rc, rc), rc)
                wgu_s[rows, 0:ff] = wgf[rows, :].astype(BF16)
                wgu_s[rows, ff:2 * ff] = wuf[rows, :].astype(BF16)
                return carry
            lax.fori_loop(0, d // rc, body, 0)
            wd_s[...] = wdf[...].astype(BF16)

            nxt_e = nxe_ref[be_ref[i]]

            @pl.when(nxt_e >= 0)
            def _():
                for c in weight_copies(nxt_e):
                    c.start()

        gu = jnp.dot(xbuf[slot].astype(BF16), wgu_s[...], preferred_element_type=F32)
        hid = _silu(gu[:, 0:ff]) * gu[:, ff:2 * ff]
        out_ref[...] = jnp.dot(hid.astype(BF16), wd_s[...], preferred_element_type=F32)

    @pl.when(i >= nused)
    def _():
        out_ref[...] = jnp.zeros(out_ref.shape, out_ref.dtype)


def moe_experts(h, slot_tok, blk_expert, nused, next_expert, w_gate, w_up, w_down, layer):
    n, d = h.shape
    p = slot_tok.shape[0]
    nblk = p // MOE_BLOCK
    ff = EXPERT_FF
    hbm = pl.BlockSpec(memory_space=pl.ANY)
    grid_spec = pltpu.PrefetchScalarGridSpec(
        num_scalar_prefetch=4, grid=(nblk,),
        in_specs=[hbm, hbm, hbm, hbm],
        out_specs=pl.BlockSpec((MOE_BLOCK, d), lambda i, be, nu, st, nx: (i, 0)),
        scratch_shapes=[pltpu.VMEM((2, MOE_BLOCK, d), h.dtype), pltpu.SemaphoreType.DMA((2,)),
                        pltpu.VMEM((d, ff), F32), pltpu.VMEM((d, ff), F32), pltpu.VMEM((ff, d), F32),
                        pltpu.SemaphoreType.DMA(()),
                        pltpu.VMEM((d, 2 * ff), BF16), pltpu.VMEM((ff, d), BF16)])
    return pl.pallas_call(
        functools.partial(_expert_kernel, layer=layer), grid_spec=grid_spec,
        out_shape=jax.ShapeDtypeStruct((p, d), F32),
        compiler_params=_cparams(1), name="moe_experts",
    )(blk_expert, nused, slot_tok, next_expert, h, w_gate, w_up, w_down)


def _slots_kernel(dest_ref, slot_ref):
    def clear(s, carry):
        slot_ref[s] = 0
        return carry
    lax.fori_loop(0, slot_ref.shape[0], clear, 0, unroll=16)

    def body(t, carry):
        for k in range(MOE_TOPK):
            slot_ref[dest_ref[t * MOE_TOPK + k]] = t
        return carry
    lax.fori_loop(0, dest_ref.shape[0] // MOE_TOPK, body, 0, unroll=8)


def moe_slots(dest_flat, p):
    smem = pl.BlockSpec(memory_space=pltpu.SMEM)
    return pl.pallas_call(
        _slots_kernel,
        in_specs=[smem],
        out_specs=smem,
        out_shape=jax.ShapeDtypeStruct((p,), I32),
        name="moe_slots",
    )(dest_flat)


def _combine_kernel(dcur_ref, dnxt_ref, info_ref, x_ref, y_hbm, out_ref, ybuf, sem, *, tb, nsteps):
    i = pl.program_id(0)
    slot = i % 2

    def issue(dref, s):
        def body(r, carry):
            for k in range(MOE_TOPK):
                _row_copy(y_hbm, dref[0, 0, MOE_TOPK * r + k], ybuf.at[s, k], r, sem.at[s]).start(
                    priority=k % 2)
            return carry
        lax.fori_loop(0, tb, body, 0, unroll=8)

    @pl.when(i == 0)
    def _():
        issue(dcur_ref, 0)

    @pl.when(i + 1 < nsteps)
    def _():
        issue(dnxt_ref, 1 - slot)

    def drain(r, carry):
        for k in range(MOE_TOPK):
            _row_copy(y_hbm, 0, ybuf.at[slot, k], 0, sem.at[slot]).wait()
        return carry
    lax.fori_loop(0, tb, drain, 0, unroll=8)

    info = info_ref[...]
    out = x_ref[...]
    for k in range(MOE_TOPK):
        out = out + ybuf[slot, k] * info[:, INFO_GATE + k:INFO_GATE + k + 1]
    out_ref[...] = out


def moe_combine(x, y, dest, info):
    n, d = x.shape
    tb = _pick(n, COMBINE_ROWS)
    nsteps = n // tb
    kern = functools.partial(_combine_kernel, tb=tb, nsteps=nsteps)
    dest3 = dest.reshape(nsteps, 1, MOE_TOPK * tb)
    return pl.pallas_call(
        kern, grid=(nsteps,),
        in_specs=[pl.BlockSpec((1, 1, MOE_TOPK * tb), lambda i: (i, 0, 0), memory_space=pltpu.SMEM),
                  pl.BlockSpec((1, 1, MOE_TOPK * tb), lambda i: (jnp.minimum(i + 1, nsteps - 1), 0, 0),
                               memory_space=pltpu.SMEM),
                  pl.BlockSpec((tb, LANES), lambda i: (i, 0)),
                  pl.BlockSpec((tb, d), lambda i: (i, 0)),
                  pl.BlockSpec(memory_space=pl.ANY)],
        out_specs=pl.BlockSpec((tb, d), lambda i: (i, 0)),
        out_shape=jax.ShapeDtypeStruct((n, d), x.dtype),
        scratch_shapes=[pltpu.VMEM((2, MOE_TOPK, tb, d), y.dtype), pltpu.SemaphoreType.DMA((2,))],
        compiler_params=_cparams(1), name="moe_combine",
    )(dest3, dest3, info, x, y)


def hier_moe_layer(x, norm_g, w_rg, w_re, w_gate, w_up, w_down, layer):
    n, d = x.shape
    w_router = jnp.concatenate(
        [w_rg, w_re, jnp.zeros((d, LANES - MOE_GROUPS - N_EXPERTS), F32)], axis=1)
    h, info, cnt = moe_router(x, norm_g, w_router)
    counts = cnt[0, ROUTE_E0:ROUTE_E0 + N_EXPERTS].astype(I32)
    eid = info[:, INFO_EID:INFO_EID + MOE_TOPK].astype(I32)
    rank = info[:, INFO_RANK:INFO_RANK + MOE_TOPK].astype(I32)
    padded = (counts + MOE_BLOCK - 1) // MOE_BLOCK * MOE_BLOCK
    pad_end = jnp.cumsum(padded)
    pad_start = pad_end - padded
    onehot = (eid[..., None] == jnp.arange(N_EXPERTS, dtype=I32)).astype(F32)
    dest = jnp.dot(onehot, pad_start.astype(F32), precision=lax.Precision.HIGHEST).astype(I32) + rank
    a = n * MOE_TOPK
    p = (-(-a // MOE_BLOCK) + N_EXPERTS) * MOE_BLOCK
    nblk = p // MOE_BLOCK
    blk_start = jnp.arange(nblk, dtype=I32) * MOE_BLOCK
    blk_expert = jnp.minimum(jnp.sum((pad_end[None, :] <= blk_start[:, None]).astype(I32), axis=1),
                             N_EXPERTS - 1).astype(I32)
    nused = (pad_end[-1:] // MOE_BLOCK).astype(I32)
    slot_tok = moe_slots(dest.reshape(-1), p)
    ids = jnp.arange(N_EXPERTS, dtype=I32)
    owner = jnp.where(counts > 0, ids, N_EXPERTS)
    after = jnp.concatenate([lax.cummin(owner, reverse=True)[1:], jnp.full((1,), N_EXPERTS, I32)])
    next_expert = jnp.where(after < N_EXPERTS, after, -1).astype(I32)
    y = moe_experts(h, slot_tok, blk_expert, nused, next_expert, w_gate, w_up, w_down, layer)
    return moe_combine(x, y, dest, info)


def _pad_lanes(v, offset, width=LANES):
    out = jnp.zeros((1, width), F32)
    return out.at[0, offset:offset + v.shape[0]].set(v.astype(F32))


def hybrid_layer(x, pos, norm_g, w_in, q_norm, k_norm, kidx_norm, conv_w, conv_b, dt_bias, a_log,
                 d_skip, ssm_norm, w_out, batch, seq, k_sel):
    n, d = x.shape
    offs = np.cumsum([0, ATTN_INNER, ATTN_KV, ATTN_KV, IDX_INNER, IDX_HEAD_DIM, IDX_HEADS,
                      SSM_INNER, SSM_CONV_DIM, SSM_HEADS])
    seg = {name: w_in[:, offs[j]:offs[j + 1]] for j, name in
           enumerate(["q", "k", "v", "qi", "ki", "wi", "z", "xbc", "dt"])}
    tail_pad = LANES - IDX_HEAD_DIM - IDX_HEADS - SSM_HEADS
    w_perm = jnp.concatenate(
        [seg["xbc"], seg["k"], seg["v"], seg["z"], seg["q"], seg["qi"], seg["ki"], seg["wi"],
         seg["dt"], jnp.zeros((d, tail_pad + HYB_COLS - COL_TAIL - LANES), F32)], axis=1).astype(BF16)
    proj = fused_matmul([x], w_perm, g=norm_g, out_dtype=F32, tn=1024, name="hyb_in_proj")

    rot = ATTN_HEAD_DIM // ROPE_FRACTION
    rot_i = IDX_HEAD_DIM // ROPE_FRACTION
    inv16 = jnp.power(ROPE_THETA, -(jnp.arange(rot // 2, dtype=F32) * 2.0 / rot))
    inv8 = jnp.power(ROPE_THETA, -(jnp.arange(rot_i // 2, dtype=F32) * 2.0 / rot_i))
    inv_c = jnp.concatenate(
        [inv16, inv8, jnp.zeros((LANES - rot // 2 - rot_i // 2,), F32)]).reshape(1, LANES)
    q_r, k_r, v_r, qi_r, ki_r, wi = hyb_prep(
        proj, pos, inv_c, q_norm.reshape(1, LANES), k_norm.reshape(1, LANES),
        _pad_lanes(kidx_norm, TAIL_KI))
    attn = dsa_attention(q_r, qi_r, wi, k_r, v_r, ki_r, batch, seq, k_sel)

    ssm = ssd_mixer(proj, conv_w, conv_b.reshape(1, SSM_CONV_DIM), _pad_lanes(dt_bias, TAIL_DT),
                    _pad_lanes(a_log, TAIL_DT), jnp.repeat(d_skip, SSM_HEAD_DIM).reshape(1, SSM_INNER),
                    ssm_norm.reshape(1, SSM_INNER), batch, seq)
    return fused_matmul([attn, ssm], w_out.astype(BF16), res=x, out_dtype=F32, name="hyb_out_proj")


def retention_layer(x, pos, norm_g, w_in, ret_norm, w_out, batch, seq):
    proj = fused_matmul([x], w_in.astype(BF16), g=norm_g, out_dtype=BF16, tn=1024, name="ret_in_proj")
    o = retention_mixer(proj, pos, ret_norm, batch, seq)
    return fused_matmul([o], w_out.astype(BF16), res=x, out_dtype=F32, name="ret_out_proj")


def kernel(x, positions, mix_norm, ffn_norm, hyb_w_in, attn_q_norm, attn_k_norm, idx_k_norm, ssm_conv_w, ssm_conv_b, ssm_dt_bias, ssm_a_log, ssm_d, ssm_norm, hyb_w_out, ret_w_in, ret_norm, ret_w_out, moe_router_group, moe_router_expert, moe_w_gate, moe_w_up, moe_w_down):
    batch, seq, d = x.shape
    depth = mix_norm.shape[0]
    k_sel = min(TOPK_MAX, seq // 4)
    n = batch * seq
    xf = x.reshape(n, d)
    pos = positions.reshape(n, 1).astype(I32)
    for layer in range(depth):
        i = layer // 2
        if layer % 2 == 0:
            xf = hybrid_layer(xf, pos, mix_norm[layer], hyb_w_in[i], attn_q_norm[i], attn_k_norm[i],
                              idx_k_norm[i], ssm_conv_w[i], ssm_conv_b[i], ssm_dt_bias[i],
                              ssm_a_log[i], ssm_d[i], ssm_norm[i], hyb_w_out[i], batch, seq, k_sel)
        else:
            xf = retention_layer(xf, pos, mix_norm[layer], ret_w_in[i], ret_norm[i], ret_w_out[i],
                                 batch, seq)
        xf = hier_moe_layer(xf, ffn_norm[layer], moe_router_group[layer], moe_router_expert[layer],
                            moe_w_gate, moe_w_up, moe_w_down, layer)
    return xf.reshape(batch, seq, d)
```

```python
import functools

import jax
import jax.numpy as jnp
import numpy as np
from jax import lax
from jax.experimental import pallas as pl
from jax.experimental.pallas import tpu as pltpu

F32 = jnp.float32
BF16 = jnp.bfloat16
I32 = jnp.int32
I16 = jnp.int16
HALF_BITS = 16
COUNT_GROUP = 4

ATTN_HEADS = 8
ATTN_KV_HEADS = 2
ATTN_GROUP = ATTN_HEADS // ATTN_KV_HEADS
ATTN_HEAD_DIM = 128
ATTN_INNER = ATTN_HEADS * ATTN_HEAD_DIM
ATTN_KV = ATTN_KV_HEADS * ATTN_HEAD_DIM
IDX_HEADS = 8
IDX_HEAD_DIM = 64
IDX_INNER = IDX_HEADS * IDX_HEAD_DIM
TOPK_MAX = 256
Q_BLOCK = 128
ROPE_THETA = 500000.0
ROPE_FRACTION = 4

SSM_HEADS = 16
SSM_HEAD_DIM = 64
SSM_INNER = SSM_HEADS * SSM_HEAD_DIM
SSM_GROUPS = 2
SSM_STATE = 128
SSM_CONV = 4
SSM_CONV_DIM = SSM_INNER + 2 * SSM_GROUPS * SSM_STATE
SSM_CHUNK = 128
SSM_GROUP_INNER = SSM_INNER // SSM_GROUPS

RET_HEADS = 8
RET_QK_DIM = 256
RET_V_DIM = 512
RET_QK_TOTAL = RET_HEADS * RET_QK_DIM
RET_V_TOTAL = RET_HEADS * RET_V_DIM
RET_CHUNK = 128
RET_THETA = 10000.0

MOE_GROUPS = 8
MOE_EXPERTS_PER_GROUP = 8
N_EXPERTS = MOE_GROUPS * MOE_EXPERTS_PER_GROUP
MOE_TOPK = 2
EXPERT_FF = 512
MOE_BLOCK = 256
COMBINE_ROWS = 128

EPS = 1e-6

LANES = 128
SUBLANES = 8
VMEM_LIMIT = 52 * 1024 * 1024

LOG2E = 1.4426950408889634
NEG_BIG = -1e30
INT_MIN = -2147483648

COL_XBC = 0
COL_K = SSM_CONV_DIM
COL_V = COL_K + ATTN_KV
COL_Z = COL_V + ATTN_KV
COL_Q = COL_Z + SSM_INNER
COL_QI = COL_Q + ATTN_INNER
COL_TAIL = COL_QI + IDX_INNER
HYB_COLS = 5120
TAIL_KI = 0
TAIL_WI = IDX_HEAD_DIM
TAIL_DT = TAIL_WI + IDX_HEADS


def _cparams(n_axes):
    return pltpu.CompilerParams(dimension_semantics=("arbitrary",) * n_axes,
                                vmem_limit_bytes=VMEM_LIMIT)


def _pick(n, pref):
    t = min(n, pref)
    while n % t:
        t //= 2
    return t


def _mm_kernel(*refs, n_a, has_norm, has_res, row_chunk):
    a_refs = refs[:n_a]
    pos = n_a
    g_ref = None
    if has_norm:
        g_ref = refs[pos]
        pos += 1
    w_refs = refs[pos:pos + n_a]
    pos += n_a
    res_ref = None
    if has_res:
        res_ref = refs[pos]
        pos += 1
    out_ref = refs[pos]
    pos += 1
    xn_ref = refs[pos] if has_norm else None
    tm = out_ref.shape[0]

    if has_norm:
        @pl.when(pl.program_id(1) == 0)
        def _():
            def body(r, carry):
                rows = pl.ds(pl.multiple_of(r * row_chunk, row_chunk), row_chunk)
                x = a_refs[0][rows, :]
                ms = jnp.mean(x * x, axis=-1, keepdims=True)
                xn_ref[rows, :] = (x * lax.rsqrt(ms + EPS) * g_ref[...]).astype(BF16)
                return carry
            lax.fori_loop(0, tm // row_chunk, body, 0)
        acc = jnp.dot(xn_ref[...], w_refs[0][...], preferred_element_type=F32)
    else:
        acc = jnp.dot(a_refs[0][...], w_refs[0][...], preferred_element_type=F32)
        for p in range(1, n_a):
            acc = acc + jnp.dot(a_refs[p][...], w_refs[p][...], preferred_element_type=F32)
    if has_res:
        acc = acc + res_ref[...]
    out_ref[...] = acc.astype(out_ref.dtype)


def fused_matmul(a_list, w, *, g=None, res=None, out_dtype=F32, tm=1024, tn=512, name="mm"):
    n = a_list[0].shape[0]
    kp = a_list[0].shape[1]
    m = w.shape[1]
    tm = _pick(n, tm)
    tn = _pick(m, tn)
    n_a = len(a_list)
    has_norm = g is not None
    has_res = res is not None
    in_specs = [pl.BlockSpec((tm, kp), lambda i, j: (i, 0)) for _ in a_list]
    args = list(a_list)
    if has_norm:
        in_specs.append(pl.BlockSpec((1, kp), lambda i, j: (0, 0)))
        args.append(g.reshape(1, kp).astype(F32))
    for p in range(n_a):
        in_specs.append(pl.BlockSpec((kp, tn), lambda i, j, p=p: (p, j)))
        args.append(w)
    if has_res:
        in_specs.append(pl.BlockSpec((tm, tn), lambda i, j: (i, j)))
        args.append(res)
    scratch = [pltpu.VMEM((tm, kp), BF16)] if has_norm else []
    kern = functools.partial(_mm_kernel, n_a=n_a, has_norm=has_norm, has_res=has_res,
                             row_chunk=_pick(tm, 128))
    return pl.pallas_call(
        kern,
        grid=(n // tm, m // tn),
        in_specs=in_specs,
        out_specs=pl.BlockSpec((tm, tn), lambda i, j: (i, j)),
        out_shape=jax.ShapeDtypeStruct((n, m), out_dtype),
        scratch_shapes=scratch,
        compiler_params=_cparams(2),
        name=name,
    )(*args)


def _rope_lanes(x, cosf, s_neg, s_pos, half):
    width = x.shape[-1]
    return (x * cosf + pltpu.roll(x, width - half, axis=1) * s_neg
            + pltpu.roll(x, half, axis=1) * s_pos)


def _prep_kernel(q_ref, k_ref, v_ref, qi_ref, tail_ref, pos_ref, invc_ref, pcos_ref, pneg_ref, ppos_ref,
                 qn_ref, kn_ref, kin_ref,
                 qo_ref, ko_ref, vo_ref, qio_ref, kio_ref, wio_ref):
    tb = q_ref.shape[0]
    posf = pos_ref[...].astype(F32)
    lane = lax.broadcasted_iota(I32, (tb, LANES), 1)

    hi = lax.Precision.HIGHEST
    ang = posf * invc_ref[...]
    cos_c = jnp.cos(ang)
    sin_c = jnp.sin(ang)
    cos_all = jnp.dot(cos_c, pcos_ref[...], precision=hi, preferred_element_type=F32)
    sneg_all = jnp.dot(sin_c, pneg_ref[...], precision=hi, preferred_element_type=F32)
    spos_all = jnp.dot(sin_c, ppos_ref[...], precision=hi, preferred_element_type=F32)

    half = ATTN_HEAD_DIM // ROPE_FRACTION // 2
    cosf = cos_all[:, 0:LANES]
    s_neg = sneg_all[:, 0:LANES]
    s_pos = spos_all[:, 0:LANES]
    scale = ATTN_HEAD_DIM ** -0.5 * LOG2E
    for h in range(ATTN_HEADS):
        x = q_ref[:, h * LANES:(h + 1) * LANES]
        ms = jnp.mean(x * x, axis=-1, keepdims=True)
        x = x * lax.rsqrt(ms + EPS) * qn_ref[...]
        x = _rope_lanes(x, cosf, s_neg, s_pos, half)
        qo_ref[h] = (x * scale).astype(BF16)
    for h in range(ATTN_KV_HEADS):
        x = k_ref[:, h * LANES:(h + 1) * LANES]
        ms = jnp.mean(x * x, axis=-1, keepdims=True)
        x = x * lax.rsqrt(ms + EPS) * kn_ref[...]
        x = _rope_lanes(x, cosf, s_neg, s_pos, half)
        ko_ref[h] = x.T.astype(BF16)
        vo_ref[h] = v_ref[:, h * LANES:(h + 1) * LANES].astype(BF16)

    half_i = IDX_HEAD_DIM // ROPE_FRACTION // 2
    cos_i = cos_all[:, LANES:LANES + IDX_INNER]
    sn_i = sneg_all[:, LANES:LANES + IDX_INNER]
    sp_i = spos_all[:, LANES:LANES + IDX_INNER]
    qi = _rope_lanes(qi_ref[...], cos_i, sn_i, sp_i, half_i)
    for h in range(IDX_HEADS):
        slab = qi[:, (h // 2) * LANES:(h // 2 + 1) * LANES]
        if h % 2:
            slab = pltpu.roll(slab, IDX_HEAD_DIM, axis=1)
        qio_ref[h] = jnp.where(lane < IDX_HEAD_DIM, slab, 0.0).astype(BF16)

    t = tail_ref[...]
    is_ki = lane < IDX_HEAD_DIM
    ms = jnp.sum(jnp.where(is_ki, t * t, 0.0), axis=-1, keepdims=True) * (1.0 / IDX_HEAD_DIM)
    kin = t * lax.rsqrt(ms + EPS) * kin_ref[...]
    kin = _rope_lanes(kin, cos_i[:, 0:LANES], sn_i[:, 0:LANES], sp_i[:, 0:LANES], half_i)
    kio_ref[...] = jnp.where(is_ki, kin, 0.0).T.astype(BF16)
    wio_ref[...] = t * (IDX_HEADS ** -0.5 * IDX_HEAD_DIM ** -0.5)


def _rope_selectors():
    a_half = ATTN_HEAD_DIM // ROPE_FRACTION // 2
    i_half = IDX_HEAD_DIM // ROPE_FRACTION // 2
    unit = LANES - 1
    width = LANES + IDX_INNER
    pcos = np.zeros((LANES, width), np.float32)
    pneg = np.zeros((LANES, width), np.float32)
    ppos = np.zeros((LANES, width), np.float32)
    for m in range(LANES):
        if m < 2 * a_half:
            pcos[m % a_half, m] = 1.0
            (pneg if m < a_half else ppos)[m % a_half, m] = -1.0 if m < a_half else 1.0
        else:
            pcos[unit, m] = 1.0
    for m in range(IDX_INNER):
        r = m % IDX_HEAD_DIM
        if r < 2 * i_half:
            pcos[a_half + r % i_half, LANES + m] = 1.0
            (pneg if r < i_half else ppos)[a_half + r % i_half, LANES + m] = -1.0 if r < i_half else 1.0
        else:
            pcos[unit, LANES + m] = 1.0
    return jnp.asarray(pcos), jnp.asarray(pneg), jnp.asarray(ppos)


def hyb_prep(proj, pos, inv_c, q_norm, k_norm, kidx_norm_pad):
    n = proj.shape[0]
    tb = _pick(n, 256)
    pcos, pneg, ppos = _rope_selectors()
    sel_w = LANES + IDX_INNER

    def col(width, off):
        return pl.BlockSpec((tb, width), lambda i: (i, off // width))

    def full(shape):
        return pl.BlockSpec(shape, lambda i: (0,) * len(shape))

    in_specs = [col(ATTN_INNER, COL_Q), col(ATTN_KV, COL_K), col(ATTN_KV, COL_V),
                col(IDX_INNER, COL_QI), col(LANES, COL_TAIL),
                pl.BlockSpec((tb, 1), lambda i: (i, 0)),
                full((1, LANES)), full((LANES, sel_w)), full((LANES, sel_w)), full((LANES, sel_w)),
                full((1, LANES)), full((1, LANES)), full((1, LANES))]
    out_shape = [jax.ShapeDtypeStruct((ATTN_HEADS, n, LANES), BF16),
                 jax.ShapeDtypeStruct((ATTN_KV_HEADS, LANES, n), BF16),
                 jax.ShapeDtypeStruct((ATTN_KV_HEADS, n, LANES), BF16),
                 jax.ShapeDtypeStruct((IDX_HEADS, n, LANES), BF16),
                 jax.ShapeDtypeStruct((LANES, n), BF16),
                 jax.ShapeDtypeStruct((n, LANES), F32)]
    out_specs = [pl.BlockSpec((ATTN_HEADS, tb, LANES), lambda i: (0, i, 0)),
                 pl.BlockSpec((ATTN_KV_HEADS, LANES, tb), lambda i: (0, 0, i)),
                 pl.BlockSpec((ATTN_KV_HEADS, tb, LANES), lambda i: (0, i, 0)),
                 pl.BlockSpec((IDX_HEADS, tb, LANES), lambda i: (0, i, 0)),
                 pl.BlockSpec((LANES, tb), lambda i: (0, i)),
                 pl.BlockSpec((tb, LANES), lambda i: (i, 0))]
    return pl.pallas_call(
        _prep_kernel, grid=(n // tb,), in_specs=in_specs, out_specs=out_specs,
        out_shape=out_shape, compiler_params=_cparams(1), name="hyb_prep",
    )(proj, proj, proj, proj, proj, pos, inv_c, pcos, pneg, ppos, q_norm, k_norm, kidx_norm_pad)


def _sortable(x):
    b = pltpu.bitcast(x, I32)
    return jnp.where(b < 0, b ^ jnp.int32(0x7FFFFFFF), b)


def _dsa_kernel(q_ref, qi_ref, wi_ref, k_ref, v_ref, ki_ref, triu_ref, eye_ref, onec_ref, out_ref,
                keys_ref, half_ref, bias_ref, wb_ref, mx_ref, acc_ref, *, k_sel, kc):
    i = pl.program_id(1)
    tq = Q_BLOCK
    nkc = (i * tq + tq + kc - 1) // kc
    row = i * tq + lax.broadcasted_iota(I32, (tq, kc), 0)
    lane = lax.broadcasted_iota(I32, (tq, kc), 1)
    nslab = kc // LANES

    w = wi_ref[...]
    for h in range(IDX_HEADS):
        wb_ref[h] = jnp.broadcast_to(w[:, TAIL_WI + h:TAIL_WI + h + 1], (tq, LANES))

    def idx_body(c, carry):
        cols = pl.ds(pl.multiple_of(c * kc, kc), kc)
        d = jnp.dot(qi_ref[...].reshape(IDX_HEADS * tq, LANES), ki_ref[:, cols],
                    preferred_element_type=F32)
        slabs = [jnp.zeros((tq, LANES), F32)] * nslab
        for h in range(IDX_HEADS):
            wh = wb_ref[h]
            dh = d[h * tq:(h + 1) * tq, :]
            slabs = [slabs[a] + jnp.maximum(dh[:, a * LANES:(a + 1) * LANES], 0.0) * wh
                     for a in range(nslab)]
        sc = jnp.concatenate(slabs, axis=1)
        causal = (c * kc + lane) <= row
        keys_ref[:, cols] = jnp.where(causal, _sortable(sc), jnp.int32(INT_MIN))
        return carry
    lax.fori_loop(0, nkc, idx_body, 0)

    def fold(x, op):
        acc = x[:, 0:LANES]
        for j in range(1, nslab):
            acc = op(acc, x[:, j * LANES:(j + 1) * LANES])
        return acc

    i16_min = -(1 << (HALF_BITS - 1))

    def count16(cand, strict):
        c16 = cand.astype(I16)

        def hits(start, width):
            kk = half_ref[:, pl.ds(pl.multiple_of(start, kc), width)]
            hit = (kk > c16) if strict else (kk >= c16)
            x = jnp.where(hit, jnp.int16(1), jnp.int16(0))
            acc = x[:, 0:LANES]
            for j in range(1, width // LANES):
                acc = acc + x[:, j * LANES:(j + 1) * LANES]
            return acc

        ngrp = nkc // COUNT_GROUP
        acc = lax.fori_loop(0, ngrp, lambda g, a: a + hits(g * (COUNT_GROUP * kc), COUNT_GROUP * kc),
                            jnp.zeros((tq, LANES), I16))
        acc = lax.fori_loop(ngrp * COUNT_GROUP, nkc, lambda c, a: a + hits(c * kc, kc), acc)
        return jnp.sum(acc.astype(F32), axis=1, keepdims=True)

    def kth_largest16(kf):
        zero = jnp.zeros((tq, 1), I32)
        lo = jnp.where(count16(zero, False) >= kf, zero, jnp.full((tq, 1), i16_min, I32))

        def bis_body(it, lo):
            cand = lo + jnp.left_shift(jnp.int32(1), jnp.int32(HALF_BITS - 2) - it)
            return jnp.where(count16(cand, False) >= kf, cand, lo)
        return lax.fori_loop(0, HALF_BITS - 1, bis_body, lo)

    def fill_half(fn):
        def body(c, carry):
            cols = pl.ds(pl.multiple_of(c * kc, kc), kc)
            half_ref[:, cols] = fn(keys_ref[:, cols]).astype(I16)
            return carry
        lax.fori_loop(0, nkc, body, 0)

    kf = jnp.float32(k_sel)
    fill_half(lambda kk: jnp.right_shift(kk, HALF_BITS))
    thr_hi = kth_largest16(kf)
    above = count16(thr_hi, True)
    low_mask = (1 << HALF_BITS) - 1
    fill_half(lambda kk: jnp.where(jnp.right_shift(kk, HALF_BITS) == thr_hi,
                                   (kk & low_mask) + i16_min, i16_min))
    thr_lo = kth_largest16(kf - above)
    thr = jnp.left_shift(thr_hi, HALF_BITS) + (thr_lo - i16_min)
    cnt_gt_lo = count16(thr_lo, True)
    need = kf - above - cnt_gt_lo
    n_eq = count16(thr_lo, False) - cnt_gt_lo
    row_plain = jnp.logical_or(jnp.logical_and(n_eq == need, thr_lo > i16_min),
                               thr == jnp.int32(INT_MIN))
    all_plain = jnp.min(jnp.where(row_plain, 1.0, 0.0)) > 0.5

    @pl.when(all_plain)
    def _():
        def fin_body(c, carry):
            cols = pl.ds(pl.multiple_of(c * kc, kc), kc)
            causal = (c * kc + lane) <= row
            sel = (keys_ref[:, cols] >= thr) & causal
            bias_ref[:, cols] = jnp.where(sel, 0.0, NEG_BIG).astype(BF16)
            return carry
        lax.fori_loop(0, nkc, fin_body, 0)

    @pl.when(jnp.logical_not(all_plain))
    def _():
        def fin_body(c, run):
            cols = pl.ds(pl.multiple_of(c * kc, kc), kc)
            kk = keys_ref[:, cols]
            eq = kk == thr
            eqf = jnp.where(eq, 1.0, 0.0)
            incl = jnp.dot(eqf.astype(BF16), triu_ref[...], preferred_element_type=F32)
            sel = (kk > thr) | (eq & ((run + incl) <= need))
            causal = (c * kc + lane) <= row
            bias_ref[:, cols] = jnp.where(sel & causal, 0.0, NEG_BIG).astype(BF16)
            return run + jnp.sum(eqf, axis=1, keepdims=True)
        lax.fori_loop(0, nkc, fin_body, jnp.zeros((tq, 1), F32))

    gq = ATTN_GROUP * tq

    def logits(j, kx):
        qg = q_ref[j * ATTN_GROUP:(j + 1) * ATTN_GROUP].reshape(gq, LANES)
        qx = jnp.concatenate([qg, eye_ref[...]], axis=1)
        return jnp.dot(qx, kx, preferred_element_type=F32)

    mx_ref[...] = jnp.full(mx_ref.shape, NEG_BIG, F32)
    acc_ref[...] = jnp.zeros(acc_ref.shape, F32)

    def chunk_logits(c):
        cols = pl.ds(pl.multiple_of(c * kc, kc), kc)
        bt = bias_ref[:, cols]
        return [logits(j, jnp.concatenate([k_ref[j, :, cols], bt], axis=0))
                for j in range(ATTN_KV_HEADS)]

    def chunk_update(c, s_list):
        cols = pl.ds(pl.multiple_of(c * kc, kc), kc)
        for j in range(ATTN_KV_HEADS):
            vx = jnp.concatenate([v_ref[j, cols, :], onec_ref[...]], axis=1)
            s = s_list[j]
            m_old = mx_ref[j]
            m_new = jnp.maximum(m_old, jnp.max(fold(s, jnp.maximum), axis=1, keepdims=True))
            alpha = jnp.exp2(m_old - m_new)
            p = jnp.concatenate(
                [jnp.exp2(s[:, a * LANES:(a + 1) * LANES] - m_new) for a in range(nslab)], axis=1)
            acc_ref[j] = (acc_ref[j] * jnp.concatenate([alpha, alpha], axis=1)
                          + jnp.dot(p.astype(BF16), vx, preferred_element_type=F32))
            mx_ref[j] = m_new

    def pair_body(g, carry):
        s0 = chunk_logits(2 * g)
        s1 = chunk_logits(2 * g + 1)
        chunk_update(2 * g, s0)
        chunk_update(2 * g + 1, s1)
        return carry
    lax.fori_loop(0, nkc // 2, pair_body, 0)

    @pl.when(nkc % 2 == 1)
    def _():
        chunk_update(nkc - 1, chunk_logits(nkc - 1))

    for h in range(ATTN_HEADS):
        a = acc_ref[h // ATTN_GROUP, (h % ATTN_GROUP) * tq:(h % ATTN_GROUP + 1) * tq, :]
        out_ref[:, h * LANES:(h + 1) * LANES] = (
            a[:, 0:LANES] / a[:, LANES:LANES + 1]).astype(out_ref.dtype)


def dsa_attention(q_r, qi_r, wi, k_r, v_r, ki_r, batch, seq, k_sel):
    n = batch * seq
    nq = seq // Q_BLOCK
    kc = _pick(seq, 1024)
    triu =jnp.asarray((np.arange(kc)[:, None] <= np.arange(kc)[None, :]).astype(np.float32), BF16)
    gq = ATTN_GROUP * Q_BLOCK
    eye = jnp.asarray(np.tile(np.eye(Q_BLOCK, dtype=np.float32), (ATTN_GROUP, 1)), BF16)
    onec = np.zeros((kc, LANES), np.float32)
    onec[:, 0] = 1.0
    onec = jnp.asarray(onec, BF16)
    kern = functools.partial(_dsa_kernel, k_sel=k_sel, kc=kc)
    in_specs = [
        pl.BlockSpec((ATTN_HEADS, Q_BLOCK, LANES), lambda b, i: (0, b * nq + i, 0)),
        pl.BlockSpec((IDX_HEADS, Q_BLOCK, LANES), lambda b, i: (0, b * nq + i, 0)),
        pl.BlockSpec((Q_BLOCK, LANES), lambda b, i: (b * nq + i, 0)),
        pl.BlockSpec((ATTN_KV_HEADS, LANES, seq), lambda b, i: (0, 0, b)),
        pl.BlockSpec((ATTN_KV_HEADS, seq, LANES), lambda b, i: (0, b, 0)),
        pl.BlockSpec((LANES, seq), lambda b, i: (0, b)),
        pl.BlockSpec((kc, kc), lambda b, i: (0, 0)),
        pl.BlockSpec((gq, Q_BLOCK), lambda b, i: (0, 0)),
        pl.BlockSpec((kc, LANES), lambda b, i: (0, 0)),
    ]
    scratch = [pltpu.VMEM((Q_BLOCK, seq), I32), pltpu.VMEM((Q_BLOCK, seq), I16),
               pltpu.VMEM((Q_BLOCK, seq), BF16),
               pltpu.VMEM((IDX_HEADS, Q_BLOCK, LANES), F32),
               pltpu.VMEM((ATTN_KV_HEADS, gq, LANES), F32),
               pltpu.VMEM((ATTN_KV_HEADS, gq, 2 * LANES), F32)]
    return pl.pallas_call(
        kern, grid=(batch, nq), in_specs=in_specs,
        out_specs=pl.BlockSpec((Q_BLOCK, ATTN_INNER), lambda b, i: (b * nq + i, 0)),
        out_shape=jax.ShapeDtypeStruct((n, ATTN_INNER), BF16),
        scratch_shapes=scratch, compiler_params=_cparams(2), name="dsa_attention",
    )(q_r, qi_r, wi, k_r, v_r, ki_r, triu, eye, onec)


def _silu(x):
    return x * (1.0 / (1.0 + jnp.exp(-x)))


def _ssd_kernel(xbc_ref, z_ref, tail_ref, cw_ref, cb_ref, dtb_ref, alog_ref, dexp_ref, ng_ref,
                tri_ref, e_ref, out_ref, state_ref, ext_ref, y_ref):
    q = SSM_CHUNK
    hi = lax.Precision.HIGHEST

    @pl.when(pl.program_id(1) == 0)
    def _():
        state_ref[...] = jnp.zeros(state_ref.shape, F32)
        ext_ref[0:SUBLANES, :] = jnp.zeros((SUBLANES, SSM_CONV_DIM), F32)

    x = xbc_ref[...]
    ext_ref[SUBLANES:SUBLANES + q, :] = x
    conv = x * cw_ref[SSM_CONV - 1:SSM_CONV, :] + cb_ref[...]
    for k in range(1, SSM_CONV):
        conv = conv + ext_ref[SUBLANES - k:SUBLANES - k + q, :] * cw_ref[SSM_CONV - 1 - k:SSM_CONV - k, :]
    ext_ref[0:SUBLANES, :] = x[q - SUBLANES:q, :]
    xbc = _silu(conv)
    xs = xbc[:, 0:SSM_INNER]
    bm = xbc[:, SSM_INNER:SSM_INNER + SSM_GROUPS * SSM_STATE]
    cm = xbc[:, SSM_INNER + SSM_GROUPS * SSM_STATE:SSM_CONV_DIM]

    lane = lax.broadcasted_iota(I32, (q, LANES), 1)
    is_dt = (lane >= TAIL_DT) & (lane < TAIL_DT + SSM_HEADS)
    raw = tail_ref[...] + dtb_ref[...]
    dt = jnp.where(is_dt, jnp.maximum(raw, 0.0) + jnp.log1p(jnp.exp(-jnp.abs(raw))), 0.0)
    a = -jnp.exp(alog_ref[...])
    da = jnp.where(is_dt, dt * a, 0.0)
    acum = jnp.dot(tri_ref[...], da, precision=hi, preferred_element_type=F32)
    acum_t = acum.T
    acum_x = jnp.dot(acum, e_ref[...], precision=hi, preferred_element_type=F32)
    dt_x = jnp.dot(dt, e_ref[...], precision=hi, preferred_element_type=F32)
    last_x = acum_x[q - 1:q, :]
    xdt = xs * dt_x
    xdec = xdt * jnp.exp(last_x - acum_x)
    ri = lax.broadcasted_iota(I32, (q, q), 0)
    ci = lax.broadcasted_iota(I32, (q, q), 1)
    tril = ri >= ci
    pairs_per_group = SSM_GROUP_INNER // LANES

    for g in range(SSM_GROUPS):
        bg = bm[:, g * SSM_STATE:(g + 1) * SSM_STATE]
        cg = cm[:, g * SSM_STATE:(g + 1) * SSM_STATE]
        cgb = cg.astype(BF16)
        gsl = slice(g * SSM_GROUP_INNER, (g + 1) * SSM_GROUP_INNER)
        cb = lax.dot_general(cgb, bg.astype(BF16), (((1,), (1,)), ((), ())),
                             preferred_element_type=F32)
        y_off = jnp.dot(cgb, state_ref[g].astype(BF16), preferred_element_type=F32)
        y_ref[:, gsl] = y_off * jnp.exp(acum_x[:, gsl])
        for mth in range(pairs_per_group):
            slab = slice((g * pairs_per_group + mth) * LANES, (g * pairs_per_group + mth + 1) * LANES)
            xp = xdt[:, slab]
            yd = None
            for side in range(2):
                hl = TAIL_DT + (g * pairs_per_group + mth) * 2 + side
                seg = acum[:, hl:hl + 1] - acum_t[hl:hl + 1, :]
                lmat = jnp.exp(jnp.where(tril, seg, NEG_BIG))
                if side == 0:
                    xh = jnp.where(lane < SSM_HEAD_DIM, xp, 0.0)
                else:
                    xh = jnp.where(lane >= SSM_HEAD_DIM, xp, 0.0)
                part = jnp.dot((cb * lmat).astype(BF16), xh.astype(BF16), preferred_element_type=F32)
                yd = part if yd is None else yd + part
            y_ref[:, slab] = y_ref[:, slab] + yd
        st = jnp.dot(bg.T.astype(BF16), xdec[:, gsl].astype(BF16), preferred_element_type=F32)
        state_ref[g] = state_ref[g] * jnp.exp(last_x[:, gsl]) + st

    y = (y_ref[...] + dexp_ref[...] * xs) * _silu(z_ref[...])
    for g in range(SSM_GROUPS):
        gsl = slice(g * SSM_GROUP_INNER, (g + 1) * SSM_GROUP_INNER)
        yg = y[:, gsl]
        ms = jnp.mean(yg * yg, axis=-1, keepdims=True)
        out_ref[:, gsl] = (yg * lax.rsqrt(ms + EPS) * ng_ref[:, gsl]).astype(out_ref.dtype)


def ssd_mixer(proj, conv_w, conv_b, dtb_pad, alog_pad, d_exp, norm_g, batch, seq):
    n = batch * seq
    q = SSM_CHUNK
    nc = seq // q
    tri = jnp.asarray((np.arange(q)[:, None] >= np.arange(q)[None, :]).astype(np.float32))
    e = np.zeros((LANES, SSM_INNER), np.float32)
    for h in range(SSM_HEADS):
        e[TAIL_DT + h, h * SSM_HEAD_DIM:(h + 1) * SSM_HEAD_DIM] = 1.0
    e = jnp.asarray(e)

    def col(width, off):
        return pl.BlockSpec((q, width), lambda b, c: (b * nc + c, off // width))

    def full(shape):
        return pl.BlockSpec(shape, lambda b, c: (0,) * len(shape))

    in_specs = [col(SSM_CONV_DIM, COL_XBC), col(SSM_INNER, COL_Z), col(LANES, COL_TAIL),
                full((SSM_CONV, SSM_CONV_DIM)), full((1, SSM_CONV_DIM)), full((1, LANES)),
                full((1, LANES)), full((1, SSM_INNER)), full((1, SSM_INNER)),
                full((q, q)), full((LANES, SSM_INNER))]
    scratch = [pltpu.VMEM((SSM_GROUPS, SSM_STATE, SSM_GROUP_INNER), F32),
               pltpu.VMEM((SUBLANES + q, SSM_CONV_DIM), F32),
               pltpu.VMEM((q, SSM_INNER), F32)]
    return pl.pallas_call(
        _ssd_kernel, grid=(batch, nc), in_specs=in_specs,
        out_specs=pl.BlockSpec((q, SSM_INNER), lambda b, c: (b * nc + c, 0)),
        out_shape=jax.ShapeDtypeStruct((n, SSM_INNER), BF16),
        scratch_shapes=scratch, compiler_params=_cparams(2), name="ssd_mixer",
    )(proj, proj, proj, conv_w, conv_b, dtb_pad, alog_pad, d_exp, norm_g, tri, e)


def _ret_kernel(q_ref, k_ref, v_ref, g_ref, pos_ref, inv_ref, dmat_ref, qdec_ref, kdec_ref,
                cdec_ref, ng_ref, out_ref, state_ref):
    half = RET_QK_DIM // 2

    @pl.when(pl.program_id(1) == 0)
    def _():
        state_ref[...] = jnp.zeros(state_ref.shape, F32)

    ang = pos_ref[...].astype(F32) * inv_ref[...]
    cosf = jnp.cos(ang)
    sinf = jnp.sin(ang)

    def rope(x):
        x1 = x[:, 0:half]
        x2 = x[:, half:2 * half]
        return jnp.concatenate([x1 * cosf - x2 * sinf, x2 * cosf + x1 * sinf], axis=-1)

    for h in range(RET_HEADS):
        qh = rope(q_ref[:, h * RET_QK_DIM:(h + 1) * RET_QK_DIM].astype(F32))
        kh = rope(k_ref[:, h * RET_QK_DIM:(h + 1) * RET_QK_DIM].astype(F32)) * (RET_QK_DIM ** -0.5)
        vh = v_ref[:, h * RET_V_DIM:(h + 1) * RET_V_DIM]
        qb = qh.astype(BF16)
        s = lax.dot_general(qb, kh.astype(BF16), (((1,), (1,)), ((), ())),
                            preferred_element_type=F32) * dmat_ref[h]
        inner = jnp.dot(s.astype(BF16), vh, preferred_element_type=F32)
        st = state_ref[h]
        cross = jnp.dot(qb, st.astype(BF16), preferred_element_type=F32) * qdec_ref[h]
        kd = (kh * kdec_ref[h]).T.astype(BF16)
        state_ref[h] = st * cdec_ref[h] + jnp.dot(kd, vh, preferred_element_type=F32)
        o = inner + cross
        ms = jnp.mean(o * o, axis=-1, keepdims=True)
        o = o * lax.rsqrt(ms + EPS) * ng_ref[:, h * RET_V_DIM:(h + 1) * RET_V_DIM]
        gate = _silu(g_ref[:, h * RET_V_DIM:(h + 1) * RET_V_DIM].astype(F32))
        out_ref[:, h * RET_V_DIM:(h + 1) * RET_V_DIM] = (gate * o).astype(out_ref.dtype)


def retention_mixer(proj, pos, norm_g, batch, seq):
    n = batch * seq
    q = RET_CHUNK
    nc = seq // q
    half = RET_QK_DIM // 2
    inv = jnp.power(RET_THETA, -(jnp.arange(half, dtype=F32) * 2.0 / RET_QK_DIM)).reshape(1, half)
    log_gamma = jnp.log(1.0 - jnp.power(2.0, -5.0 - jnp.arange(RET_HEADS, dtype=F32)))
    i = jnp.arange(q, dtype=F32)
    diff = i[:, None] - i[None, :]
    dmat = jnp.where(diff[None] >= 0,
                     jnp.exp(jnp.maximum(diff, 0.0)[None] * log_gamma[:, None, None]), 0.0)
    qdec = jnp.exp((i + 1.0)[None, :, None] * log_gamma[:, None, None])
    kdec = jnp.exp((q - 1.0 - i)[None, :, None] * log_gamma[:, None, None])
    cdec = jnp.broadcast_to(jnp.exp(q * log_gamma)[:, None, None], (RET_HEADS, 1, RET_V_DIM))

    def col(width, off):
        return pl.BlockSpec((q, width), lambda b, c: (b * nc + c, off // width))

    def full(shape):
        return pl.BlockSpec(shape, lambda b, c: (0,) * len(shape))

    in_specs = [col(RET_QK_TOTAL, 0), col(RET_QK_TOTAL, RET_QK_TOTAL),
                col(RET_V_TOTAL, 2 * RET_QK_TOTAL), col(RET_V_TOTAL, 2 * RET_QK_TOTAL + RET_V_TOTAL),
                pl.BlockSpec((q, 1), lambda b, c: (b * nc + c, 0)),
                full((1, half)), full((RET_HEADS, q, q)), full((RET_HEADS, q, 1)),
                full((RET_HEADS, q, 1)), full((RET_HEADS, 1, RET_V_DIM)), full((1, RET_V_TOTAL))]
    return pl.pallas_call(
        _ret_kernel, grid=(batch, nc), in_specs=in_specs,
        out_specs=pl.BlockSpec((q, RET_V_TOTAL), lambda b, c: (b * nc + c, 0)),
        out_shape=jax.ShapeDtypeStruct((n, RET_V_TOTAL), BF16),
        scratch_shapes=[pltpu.VMEM((RET_HEADS, RET_QK_DIM, RET_V_DIM), F32)],
        compiler_params=_cparams(2), name="retention",
    )(proj, proj, proj, proj, pos, inv, dmat, qdec, kdec, cdec, norm_g.reshape(1, RET_V_TOTAL))


ROUTE_E0 = MOE_GROUPS
INFO_EID, INFO_GATE, INFO_RANK = 0, 2, 4


def _router_kernel(x_ref, g_ref, wr_ref, lt_ref, h_ref, info_ref, cnt_ref, run_ref):
    tb = x_ref.shape[0]

    @pl.when(pl.program_id(0) == 0)
    def _():
        run_ref[...] = jnp.zeros(run_ref.shape, F32)

    x = x_ref[...]
    ms = jnp.mean(x * x, axis=-1, keepdims=True)
    hn = x * lax.rsqrt(ms + EPS) * g_ref[...]
    h_ref[...] = hn
    h_hi = hn.astype(BF16)
    h_lo = (hn - h_hi.astype(F32)).astype(BF16)
    both = jnp.dot(h_hi, wr_ref[...], preferred_element_type=F32)
    logits = (both[:, 0:LANES] + both[:, LANES:2 * LANES]
              + jnp.dot(h_lo, wr_ref[:, 0:LANES], preferred_element_type=F32))

    lane = lax.broadcasted_iota(I32, (tb, LANES), 1).astype(F32)
    far = jnp.float32(4 * LANES)
    ninf = jnp.float32(-jnp.inf)
    gl = jnp.where(lane < MOE_GROUPS, logits, ninf)
    gmax = jnp.max(gl, axis=1, keepdims=True)
    grp = jnp.min(jnp.where(gl == gmax, lane, far), axis=1, keepdims=True)
    p_grp = 1.0 / jnp.sum(jnp.exp(gl - gmax), axis=1, keepdims=True)
    lo = ROUTE_E0 + grp * MOE_EXPERTS_PER_GROUP
    el = jnp.where((lane >= lo) & (lane < lo + MOE_EXPERTS_PER_GROUP), logits, ninf)
    m1 = jnp.max(el, axis=1, keepdims=True)
    i1 = jnp.min(jnp.where(el == m1, lane, far), axis=1, keepdims=True)
    el2 = jnp.where(lane == i1, ninf, el)
    m2 = jnp.max(el2, axis=1, keepdims=True)
    i2 = jnp.min(jnp.where(el2 == m2, lane, far), axis=1, keepdims=True)
    e2 = jnp.exp(m2 - m1)
    g1 = p_grp / (1.0 + e2)
    g2 = p_grp * e2 / (1.0 + e2)
    oh1 = jnp.where(lane == i1, 1.0, 0.0)
    oh2 = jnp.where(lane == i2, 1.0, 0.0)
    cnt = oh1 + oh2
    before = jnp.dot(lt_ref[...], cnt.astype(BF16), preferred_element_type=F32) + run_ref[...]
    r1 = jnp.sum(oh1 * before, axis=1, keepdims=True)
    r2 = jnp.sum(oh2 * before, axis=1, keepdims=True)
    run_ref[...] = run_ref[...] + jnp.sum(cnt, axis=0, keepdims=True)
    cnt_ref[...] = run_ref[...]

    info = jnp.where(lane == INFO_EID, i1 - ROUTE_E0, 0.0)
    info = jnp.where(lane == INFO_EID + 1, i2 - ROUTE_E0, info)
    info = jnp.where(lane == INFO_GATE, g1, info)
    info = jnp.where(lane == INFO_GATE + 1, g2, info)
    info = jnp.where(lane == INFO_RANK, r1, info)
    info = jnp.where(lane == INFO_RANK + 1, r2, info)
    info_ref[...] = info


def moe_router(x, g, w_router_pad):
    n, d = x.shape
    tb = _pick(n, 512)
    lt = jnp.asarray((np.arange(tb)[:, None] > np.arange(tb)[None, :]).astype(np.float32), BF16)
    w_hi = w_router_pad.astype(BF16)
    w_lo = (w_router_pad - w_hi.astype(F32)).astype(BF16)
    w_split = jnp.concatenate([w_hi, w_lo], axis=1)
    return pl.pallas_call(
        _router_kernel, grid=(n // tb,),
        in_specs=[pl.BlockSpec((tb, d), lambda i: (i, 0)), pl.BlockSpec((1, d), lambda i: (0, 0)),
                  pl.BlockSpec((d, 2 * LANES), lambda i: (0, 0)), pl.BlockSpec((tb, tb), lambda i: (0, 0))],
        out_specs=[pl.BlockSpec((tb, d), lambda i: (i, 0)), pl.BlockSpec((tb, LANES), lambda i: (i, 0)),
                   pl.BlockSpec((1, LANES), lambda i: (0, 0))],
        out_shape=[jax.ShapeDtypeStruct((n, d), F32), jax.ShapeDtypeStruct((n, LANES), F32),
                   jax.ShapeDtypeStruct((1, LANES), F32)],
        scratch_shapes=[pltpu.VMEM((1, LANES), F32)],
        compiler_params=_cparams(1), name="moe_router",
    )(x, g.reshape(1, d), w_split, lt)


def _row_copy(src_ref, src_row, dst_ref, dst_row, sem):
    return pltpu.make_async_copy(src_ref.at[pl.ds(src_row, 1), :], dst_ref.at[pl.ds(dst_row, 1), :], sem)


def _expert_kernel(be_ref, nused_ref, st_ref, nxe_ref, h_hbm, wg_hbm, wu_hbm, wd_hbm, out_ref,
                   xbuf, sem, wgf, wuf, wdf, wsem, wgu_s, wd_s, *, layer):
    i = pl.program_id(0)
    slot = i % 2
    nused = nused_ref[0]
    d = xbuf.shape[2]
    ff = EXPERT_FF

    def weight_copies(e):
        return (pltpu.make_async_copy(wg_hbm.at[layer, e], wgf, wsem),
                pltpu.make_async_copy(wu_hbm.at[layer, e], wuf, wsem),
                pltpu.make_async_copy(wd_hbm.at[layer, e], wdf, wsem))

    @pl.when(jnp.logical_and(i == 0, nused > 0))
    def _():
        for c in weight_copies(be_ref[0]):
            c.start()

    def gather(blk, s):
        def body(r, carry):
            _row_copy(h_hbm, st_ref[blk * MOE_BLOCK + r], xbuf.at[s], r, sem.at[s]).start(priority=1)
            return carry
        lax.fori_loop(0, MOE_BLOCK, body, 0, unroll=8)

    @pl.when(jnp.logical_and(i == 0, nused > 0))
    def _():
        gather(0, 0)

    @pl.when(i + 1 < nused)
    def _():
        gather(i + 1, 1 - slot)

    @pl.when(i < nused)
    def _():
        def drain(r, carry):
            _row_copy(h_hbm, 0, xbuf.at[slot], 0, sem.at[slot]).wait()
            return carry
        lax.fori_loop(0, MOE_BLOCK, drain, 0, unroll=8)

        changed = jnp.logical_or(i == 0, be_ref[i] != be_ref[jnp.maximum(i - 1, 0)])

        @pl.when(changed)
        def _():
            for c in weight_copies(be_ref[i]):
                c.wait()
            rc = _pick(d, 256)

            def body(r, carry):
                rows = pl.ds(pl.multiple_of(r * rc, rc), rc)
                wgu_s[rows, 0:ff] = wgf[rows, :].astype(BF16)
                wgu_s[rows, ff:2 * ff] = wuf[rows, :].astype(BF16)
                return carry
            lax.fori_loop(0, d // rc, body, 0)
            wd_s[...] = wdf[...].astype(BF16)

            nxt_e = nxe_ref[be_ref[i]]

            @pl.when(nxt_e >= 0)
            def _():
                for c in weight_copies(nxt_e):
                    c.start()

        gu = jnp.dot(xbuf[slot].astype(BF16), wgu_s[...], preferred_element_type=F32)
        hid = _silu(gu[:, 0:ff]) * gu[:, ff:2 * ff]
        out_ref[...] = jnp.dot(hid.astype(BF16), wd_s[...], preferred_element_type=F32)

    @pl.when(i >= nused)
    def _():
        out_ref[...] = jnp.zeros(out_ref.shape, out_ref.dtype)


def moe_experts(h, slot_tok, blk_expert, nused, next_expert, w_gate, w_up, w_down, layer):
    n, d = h.shape
    p = slot_tok.shape[0]
    nblk = p // MOE_BLOCK
    ff = EXPERT_FF
    hbm = pl.BlockSpec(memory_space=pl.ANY)
    grid_spec = pltpu.PrefetchScalarGridSpec(
        num_scalar_prefetch=4, grid=(nblk,),
        in_specs=[hbm, hbm, hbm, hbm],
        out_specs=pl.BlockSpec((MOE_BLOCK, d), lambda i, be, nu, st, nx: (i, 0)),
        scratch_shapes=[pltpu.VMEM((2, MOE_BLOCK, d), h.dtype), pltpu.SemaphoreType.DMA((2,)),
                        pltpu.VMEM((d, ff), F32), pltpu.VMEM((d, ff), F32), pltpu.VMEM((ff, d), F32),
                        pltpu.SemaphoreType.DMA(()),
                        pltpu.VMEM((d, 2 * ff), BF16), pltpu.VMEM((ff, d), BF16)])
    return pl.pallas_call(
        functools.partial(_expert_kernel, layer=layer), grid_spec=grid_spec,
        out_shape=jax.ShapeDtypeStruct((p, d), F32),
        compiler_params=_cparams(1), name="moe_experts",
    )(blk_expert, nused, slot_tok, next_expert, h, w_gate, w_up, w_down)


def _slots_kernel(dest_ref, slot_ref):
    def clear(s, carry):
        slot_ref[s] = 0
        return carry
    lax.fori_loop(0, slot_ref.shape[0], clear, 0, unroll=16)

    def body(t, carry):
        for k in range(MOE_TOPK):
            slot_ref[dest_ref[t * MOE_TOPK + k]] = t
        return carry
    lax.fori_loop(0, dest_ref.shape[0] // MOE_TOPK, body, 0, unroll=8)


def moe_slots(dest_flat, p):
    smem = pl.BlockSpec(memory_space=pltpu.SMEM)
    return pl.pallas_call(
        _slots_kernel,
        in_specs=[smem],
        out_specs=smem,
        out_shape=jax.ShapeDtypeStruct((p,), I32),
        name="moe_slots",
    )(dest_flat)


def _combine_kernel(dcur_ref, dnxt_ref, info_ref, x_ref, y_hbm, out_ref, ybuf, sem, *, tb, nsteps):
    i = pl.program_id(0)
    slot = i % 2

    def issue(dref, s):
        def body(r, carry):
            for k in range(MOE_TOPK):
                _row_copy(y_hbm, dref[0, 0, MOE_TOPK * r + k], ybuf.at[s, k], r, sem.at[s]).start(
                    priority=k % 2)
            return carry
        lax.fori_loop(0, tb, body, 0, unroll=8)

    @pl.when(i == 0)
    def _():
        issue(dcur_ref, 0)

    @pl.when(i + 1 < nsteps)
    def _():
        issue(dnxt_ref, 1 - slot)

    def drain(r, carry):
        for k in range(MOE_TOPK):
            _row_copy(y_hbm, 0, ybuf.at[slot, k], 0, sem.at[slot]).wait()
        return carry
    lax.fori_loop(0, tb, drain, 0, unroll=8)

    info = info_ref[...]
    out = x_ref[...]
    for k in range(MOE_TOPK):
        out = out + ybuf[slot, k] * info[:, INFO_GATE + k:INFO_GATE + k + 1]
    out_ref[...] = out


def moe_combine(x, y, dest, info):
    n, d = x.shape
    tb = _pick(n, COMBINE_ROWS)
    nsteps = n // tb
    kern = functools.partial(_combine_kernel, tb=tb, nsteps=nsteps)
    dest3 = dest.reshape(nsteps, 1, MOE_TOPK * tb)
    return pl.pallas_call(
        kern, grid=(nsteps,),
        in_specs=[pl.BlockSpec((1, 1, MOE_TOPK * tb), lambda i: (i, 0, 0), memory_space=pltpu.SMEM),
                  pl.BlockSpec((1, 1, MOE_TOPK * tb), lambda i: (jnp.minimum(i + 1, nsteps - 1), 0, 0),
                               memory_space=pltpu.SMEM),
                  pl.BlockSpec((tb, LANES), lambda i: (i, 0)),
                  pl.BlockSpec((tb, d), lambda i: (i, 0)),
                  pl.BlockSpec(memory_space=pl.ANY)],
        out_specs=pl.BlockSpec((tb, d), lambda i: (i, 0)),
        out_shape=jax.ShapeDtypeStruct((n, d), x.dtype),
        scratch_shapes=[pltpu.VMEM((2, MOE_TOPK, tb, d), y.dtype), pltpu.SemaphoreType.DMA((2,))],
        compiler_params=_cparams(1), name="moe_combine",
    )(dest3, dest3, info, x, y)


def hier_moe_layer(x, norm_g, w_rg, w_re, w_gate, w_up, w_down, layer):
    n, d = x.shape
    w_router = jnp.concatenate(
        [w_rg, w_re, jnp.zeros((d, LANES - MOE_GROUPS - N_EXPERTS), F32)], axis=1)
    h, info, cnt = moe_router(x, norm_g, w_router)
    counts = cnt[0, ROUTE_E0:ROUTE_E0 + N_EXPERTS].astype(I32)
    eid = info[:, INFO_EID:INFO_EID + MOE_TOPK].astype(I32)
    rank = info[:, INFO_RANK:INFO_RANK + MOE_TOPK].astype(I32)
    padded = (counts + MOE_BLOCK - 1) // MOE_BLOCK * MOE_BLOCK
    pad_end = jnp.cumsum(padded)
    pad_start = pad_end - padded
    onehot = (eid[..., None] == jnp.arange(N_EXPERTS, dtype=I32)).astype(F32)
    dest = jnp.dot(onehot, pad_start.astype(F32), precision=lax.Precision.HIGHEST).astype(I32) + rank
    a = n * MOE_TOPK
    p = (-(-a // MOE_BLOCK) + N_EXPERTS) * MOE_BLOCK
    nblk = p // MOE_BLOCK
    blk_start = jnp.arange(nblk, dtype=I32) * MOE_BLOCK
    blk_expert = jnp.minimum(jnp.sum((pad_end[None, :] <= blk_start[:, None]).astype(I32), axis=1),
                             N_EXPERTS - 1).astype(I32)
    nused = (pad_end[-1:] // MOE_BLOCK).astype(I32)
    slot_tok = moe_slots(dest.reshape(-1), p)
    ids = jnp.arange(N_EXPERTS, dtype=I32)
    owner = jnp.where(counts > 0, ids, N_EXPERTS)
    after = jnp.concatenate([lax.cummin(owner, reverse=True)[1:], jnp.full((1,), N_EXPERTS, I32)])
    next_expert = jnp.where(after < N_EXPERTS, after, -1).astype(I32)
    y = moe_experts(h, slot_tok, blk_expert, nused, next_expert, w_gate, w_up, w_down, layer)
    return moe_combine(x, y, dest, info)


def _pad_lanes(v, offset, width=LANES):
    out = jnp.zeros((1, width), F32)
    return out.at[0, offset:offset + v.shape[0]].set(v.astype(F32))


def hybrid_layer(x, pos, norm_g, w_in, q_norm, k_norm, kidx_norm, conv_w, conv_b, dt_bias, a_log,
                 d_skip, ssm_norm, w_out, batch, seq, k_sel):
    n, d = x.shape
    offs = np.cumsum([0, ATTN_INNER, ATTN_KV, ATTN_KV, IDX_INNER, IDX_HEAD_DIM, IDX_HEADS,
                      SSM_INNER, SSM_CONV_DIM, SSM_HEADS])
    seg = {name: w_in[:, offs[j]:offs[j + 1]] for j, name in
           enumerate(["q", "k", "v", "qi", "ki", "wi", "z", "xbc", "dt"])}
    tail_pad = LANES - IDX_HEAD_DIM - IDX_HEADS - SSM_HEADS
    w_perm = jnp.concatenate(
        [seg["xbc"], seg["k"], seg["v"], seg["z"], seg["q"], seg["qi"], seg["ki"], seg["wi"],
         seg["dt"], jnp.zeros((d, tail_pad + HYB_COLS - COL_TAIL - LANES), F32)], axis=1).astype(BF16)
    proj = fused_matmul([x], w_perm, g=norm_g, out_dtype=F32, tn=1024, name="hyb_in_proj")

    rot = ATTN_HEAD_DIM // ROPE_FRACTION
    rot_i = IDX_HEAD_DIM // ROPE_FRACTION
    inv16 = jnp.power(ROPE_THETA, -(jnp.arange(rot // 2, dtype=F32) * 2.0 / rot))
    inv8 = jnp.power(ROPE_THETA, -(jnp.arange(rot_i // 2, dtype=F32) * 2.0 / rot_i))
    inv_c = jnp.concatenate(
        [inv16, inv8, jnp.zeros((LANES - rot // 2 - rot_i // 2,), F32)]).reshape(1, LANES)
    q_r, k_r, v_r, qi_r, ki_r, wi = hyb_prep(
        proj, pos, inv_c, q_norm.reshape(1, LANES), k_norm.reshape(1, LANES),
        _pad_lanes(kidx_norm, TAIL_KI))
    attn = dsa_attention(q_r, qi_r, wi, k_r, v_r, ki_r, batch, seq, k_sel)

    ssm = ssd_mixer(proj, conv_w, conv_b.reshape(1, SSM_CONV_DIM), _pad_lanes(dt_bias, TAIL_DT),
                    _pad_lanes(a_log, TAIL_DT), jnp.repeat(d_skip, SSM_HEAD_DIM).reshape(1, SSM_INNER),
                    ssm_norm.reshape(1, SSM_INNER), batch, seq)
    return fused_matmul([attn, ssm], w_out.astype(BF16), res=x, out_dtype=F32, name="hyb_out_proj")


def retention_layer(x, pos, norm_g, w_in, ret_norm, w_out, batch, seq):
    proj = fused_matmul([x], w_in.astype(BF16), g=norm_g, out_dtype=BF16, tn=1024, name="ret_in_proj")
    o = retention_mixer(proj, pos, ret_norm, batch, seq)
    return fused_matmul([o], w_out.astype(BF16), res=x, out_dtype=F32, name="ret_out_proj")


def kernel(x, positions, mix_norm, ffn_norm, hyb_w_in, attn_q_norm, attn_k_norm, idx_k_norm, ssm_conv_w, ssm_conv_b, ssm_dt_bias, ssm_a_log, ssm_d, ssm_norm, hyb_w_out, ret_w_in, ret_norm, ret_w_out, moe_router_group, moe_router_expert, moe_w_gate, moe_w_up, moe_w_down):
    batch, seq, d = x.shape
    depth = mix_norm.shape[0]
    k_sel = min(TOPK_MAX, seq // 4)
    n = batch * seq
    xf = x.reshape(n, d)
    pos = positions.reshape(n, 1).astype(I32)
    for layer in range(depth):
        i = layer // 2
        if layer % 2 == 0:
            xf = hybrid_layer(xf, pos, mix_norm[layer], hyb_w_in[i], attn_q_norm[i], attn_k_norm[i],
                              idx_k_norm[i], ssm_conv_w[i], ssm_conv_b[i], ssm_dt_bias[i],
                              ssm_a_log[i], ssm_d[i], ssm_norm[i], hyb_w_out[i], batch, seq, k_sel)
        else:
            xf = retention_layer(xf, pos, mix_norm[layer], ret_w_in[i], ret_norm[i], ret_w_out[i],
                                 batch, seq)
        xf = hier_moe_layer(xf, ffn_norm[layer], moe_router_group[layer], moe_router_expert[layer],
                            moe_w_gate, moe_w_up, moe_w_down, layer)
    return xf.reshape(batch, seq, d)
```

```python
import functools

import jax
import jax.numpy as jnp
import numpy as np
from jax import lax
from jax.experimental import pallas as pl
from jax.experimental.pallas import tpu as pltpu

F32 = jnp.float32
BF16 = jnp.bfloat16
I32 = jnp.int32
I16 = jnp.int16
HALF_BITS = 16
COUNT_GROUP = 4

ATTN_HEADS = 8
ATTN_KV_HEADS = 2
ATTN_GROUP = ATTN_HEADS // ATTN_KV_HEADS
ATTN_HEAD_DIM = 128
ATTN_INNER = ATTN_HEADS * ATTN_HEAD_DIM
ATTN_KV = ATTN_KV_HEADS * ATTN_HEAD_DIM
IDX_HEADS = 8
IDX_HEAD_DIM = 64
IDX_INNER = IDX_HEADS * IDX_HEAD_DIM
TOPK_MAX = 256
Q_BLOCK = 128
ROPE_THETA = 500000.0
ROPE_FRACTION = 4

SSM_HEADS = 16
SSM_HEAD_DIM = 64
SSM_INNER = SSM_HEADS * SSM_HEAD_DIM
SSM_GROUPS = 2
SSM_STATE = 128
SSM_CONV = 4
SSM_CONV_DIM = SSM_INNER + 2 * SSM_GROUPS * SSM_STATE
SSM_CHUNK = 128
SSM_GROUP_INNER = SSM_INNER // SSM_GROUPS

RET_HEADS = 8
RET_QK_DIM = 256
RET_V_DIM = 512
RET_QK_TOTAL = RET_HEADS * RET_QK_DIM
RET_V_TOTAL = RET_HEADS * RET_V_DIM
RET_CHUNK = 128
RET_THETA = 10000.0

MOE_GROUPS = 8
MOE_EXPERTS_PER_GROUP = 8
N_EXPERTS = MOE_GROUPS * MOE_EXPERTS_PER_GROUP
MOE_TOPK = 2
EXPERT_FF = 512
MOE_BLOCK = 256
COMBINE_ROWS = 128

EPS = 1e-6

LANES = 128
SUBLANES = 8
VMEM_LIMIT = 52 * 1024 * 1024

LOG2E = 1.4426950408889634
NEG_BIG = -1e30
INT_MIN = -2147483648

COL_XBC = 0
COL_K = SSM_CONV_DIM
COL_V = COL_K + ATTN_KV
COL_Z = COL_V + ATTN_KV
COL_Q = COL_Z + SSM_INNER
COL_QI = COL_Q + ATTN_INNER
COL_TAIL = COL_QI + IDX_INNER
HYB_COLS = 5120
TAIL_KI = 0
TAIL_WI = IDX_HEAD_DIM
TAIL_DT = TAIL_WI + IDX_HEADS


def _cparams(n_axes):
    return pltpu.CompilerParams(dimension_semantics=("arbitrary",) * n_axes,
                                vmem_limit_bytes=VMEM_LIMIT)


def _pick(n, pref):
    t = min(n, pref)
    while n % t:
        t //= 2
    return t


def _mm_kernel(*refs, n_a, has_norm, has_res, row_chunk):
    a_refs = refs[:n_a]
    pos = n_a
    g_ref = None
    if has_norm:
        g_ref = refs[pos]
        pos += 1
    w_refs = refs[pos:pos + n_a]
    pos += n_a
    res_ref = None
    if has_res:
        res_ref = refs[pos]
        pos += 1
    out_ref = refs[pos]
    pos += 1
    xn_ref = refs[pos] if has_norm else None
    tm = out_ref.shape[0]

    if has_norm:
        @pl.when(pl.program_id(1) == 0)
        def _():
            def body(r, carry):
                rows = pl.ds(pl.multiple_of(r * row_chunk, row_chunk), row_chunk)
                x = a_refs[0][rows, :]
                ms = jnp.mean(x * x, axis=-1, keepdims=True)
                xn_ref[rows, :] = (x * lax.rsqrt(ms + EPS) * g_ref[...]).astype(BF16)
                return carry
            lax.fori_loop(0, tm // row_chunk, body, 0)
        acc = jnp.dot(xn_ref[...], w_refs[0][...], preferred_element_type=F32)
    else:
        acc = jnp.dot(a_refs[0][...], w_refs[0][...], preferred_element_type=F32)
        for p in range(1, n_a):
            acc = acc + jnp.dot(a_refs[p][...], w_refs[p][...], preferred_element_type=F32)
    if has_res:
        acc = acc + res_ref[...]
    out_ref[...] = acc.astype(out_ref.dtype)


def fused_matmul(a_list, w, *, g=None, res=None, out_dtype=F32, tm=1024, tn=512, name="mm"):
    n = a_list[0].shape[0]
    kp = a_list[0].shape[1]
    m = w.shape[1]
    tm = _pick(n, tm)
    tn = _pick(m, tn)
    n_a = len(a_list)
    has_norm = g is not None
    has_res = res is not None
    in_specs = [pl.BlockSpec((tm, kp), lambda i, j: (i, 0)) for _ in a_list]
    args = list(a_list)
    if has_norm:
        in_specs.append(pl.BlockSpec((1, kp), lambda i, j: (0, 0)))
        args.append(g.reshape(1, kp).astype(F32))
    for p in range(n_a):
        in_specs.append(pl.BlockSpec((kp, tn), lambda i, j, p=p: (p, j)))
        args.append(w)
    if has_res:
        in_specs.append(pl.BlockSpec((tm, tn), lambda i, j: (i, j)))
        args.append(res)
    scratch = [pltpu.VMEM((tm, kp), BF16)] if has_norm else []
    kern = functools.partial(_mm_kernel, n_a=n_a, has_norm=has_norm, has_res=has_res,
                             row_chunk=_pick(tm, 128))
    return pl.pallas_call(
        kern,
        grid=(n // tm, m // tn),
        in_specs=in_specs,
        out_specs=pl.BlockSpec((tm, tn), lambda i, j: (i, j)),
        out_shape=jax.ShapeDtypeStruct((n, m), out_dtype),
        scratch_shapes=scratch,
        compiler_params=_cparams(2),
        name=name,
    )(*args)


def _rope_lanes(x, cosf, s_neg, s_pos, half):
    width = x.shape[-1]
    return (x * cosf + pltpu.roll(x, width - half, axis=1) * s_neg
            + pltpu.roll(x, half, axis=1) * s_pos)


def _prep_kernel(q_ref, k_ref, v_ref, qi_ref, tail_ref, pos_ref, invc_ref, pcos_ref, pneg_ref, ppos_ref,
                 qn_ref, kn_ref, kin_ref,
                 qo_ref, ko_ref, vo_ref, qio_ref, kio_ref, wio_ref):
    tb = q_ref.shape[0]
    posf = pos_ref[...].astype(F32)
    lane = lax.broadcasted_iota(I32, (tb, LANES), 1)

    hi = lax.Precision.HIGHEST
    ang = posf * invc_ref[...]
    cos_c = jnp.cos(ang)
    sin_c = jnp.sin(ang)
    cos_all = jnp.dot(cos_c, pcos_ref[...], precision=hi, preferred_element_type=F32)
    sneg_all = jnp.dot(sin_c, pneg_ref[...], precision=hi, preferred_element_type=F32)
    spos_all = jnp.dot(sin_c, ppos_ref[...], precision=hi, preferred_element_type=F32)

    half = ATTN_HEAD_DIM // ROPE_FRACTION // 2
    cosf = cos_all[:, 0:LANES]
    s_neg = sneg_all[:, 0:LANES]
    s_pos = spos_all[:, 0:LANES]
    scale = ATTN_HEAD_DIM ** -0.5 * LOG2E
    for h in range(ATTN_HEADS):
        x = q_ref[:, h * LANES:(h + 1) * LANES]
        ms = jnp.mean(x * x, axis=-1, keepdims=True)
        x = x * lax.rsqrt(ms + EPS) * qn_ref[...]
        x = _rope_lanes(x, cosf, s_neg, s_pos, half)
        qo_ref[h] = (x * scale).astype(BF16)
    for h in range(ATTN_KV_HEADS):
        x = k_ref[:, h * LANES:(h + 1) * LANES]
        ms = jnp.mean(x * x, axis=-1, keepdims=True)
        x = x * lax.rsqrt(ms + EPS) * kn_ref[...]
        x = _rope_lanes(x, cosf, s_neg, s_pos, half)
        ko_ref[h] = x.T.astype(BF16)
        vo_ref[h] = v_ref[:, h * LANES:(h + 1) * LANES].astype(BF16)

    half_i = IDX_HEAD_DIM // ROPE_FRACTION // 2
    cos_i = cos_all[:, LANES:LANES + IDX_INNER]
    sn_i = sneg_all[:, LANES:LANES + IDX_INNER]
    sp_i = spos_all[:, LANES:LANES + IDX_INNER]
    qi = _rope_lanes(qi_ref[...], cos_i, sn_i, sp_i, half_i)
    for h in range(IDX_HEADS):
        slab = qi[:, (h // 2) * LANES:(h // 2 + 1) * LANES]
        if h % 2:
            slab = pltpu.roll(slab, IDX_HEAD_DIM, axis=1)
        qio_ref[h] = jnp.where(lane < IDX_HEAD_DIM, slab, 0.0).astype(BF16)

    t = tail_ref[...]
    is_ki = lane < IDX_HEAD_DIM
    ms = jnp.sum(jnp.where(is_ki, t * t, 0.0), axis=-1, keepdims=True) * (1.0 / IDX_HEAD_DIM)
    kin = t * lax.rsqrt(ms + EPS) * kin_ref[...]
    kin = _rope_lanes(kin, cos_i[:, 0:LANES], sn_i[:, 0:LANES], sp_i[:, 0:LANES], half_i)
    kio_ref[...] = jnp.where(is_ki, kin, 0.0).T.astype(BF16)
    wio_ref[...] = t * (IDX_HEADS ** -0.5 * IDX_HEAD_DIM ** -0.5)


def _rope_selectors():
    a_half = ATTN_HEAD_DIM // ROPE_FRACTION // 2
    i_half = IDX_HEAD_DIM // ROPE_FRACTION // 2
    unit = LANES - 1
    width = LANES + IDX_INNER
    pcos = np.zeros((LANES, width), np.float32)
    pneg = np.zeros((LANES, width), np.float32)
    ppos = np.zeros((LANES, width), np.float32)
    for m in range(LANES):
        if m < 2 * a_half:
            pcos[m % a_half, m] = 1.0
            (pneg if m < a_half else ppos)[m % a_half, m] = -1.0 if m < a_half else 1.0
        else:
            pcos[unit, m] = 1.0
    for m in range(IDX_INNER):
        r = m % IDX_HEAD_DIM
        if r < 2 * i_half:
            pcos[a_half + r % i_half, LANES + m] = 1.0
            (pneg if r < i_half else ppos)[a_half + r % i_half, LANES + m] = -1.0 if r < i_half else 1.0
        else:
            pcos[unit, LANES + m] = 1.0
    return jnp.asarray(pcos), jnp.asarray(pneg), jnp.asarray(ppos)


def hyb_prep(proj, pos, inv_c, q_norm, k_norm, kidx_norm_pad):
    n = proj.shape[0]
    tb = _pick(n, 256)
    pcos, pneg, ppos = _rope_selectors()
    sel_w = LANES + IDX_INNER

    def col(width, off):
        return pl.BlockSpec((tb, width), lambda i: (i, off // width))

    def full(shape):
        return pl.BlockSpec(shape, lambda i: (0,) * len(shape))

    in_specs = [col(ATTN_INNER, COL_Q), col(ATTN_KV, COL_K), col(ATTN_KV, COL_V),
                col(IDX_INNER, COL_QI), col(LANES, COL_TAIL),
                pl.BlockSpec((tb, 1), lambda i: (i, 0)),
                full((1, LANES)), full((LANES, sel_w)), full((LANES, sel_w)), full((LANES, sel_w)),
                full((1, LANES)), full((1, LANES)), full((1, LANES))]
    out_shape = [jax.ShapeDtypeStruct((ATTN_HEADS, n, LANES), BF16),
                 jax.ShapeDtypeStruct((ATTN_KV_HEADS, LANES, n), BF16),
                 jax.ShapeDtypeStruct((ATTN_KV_HEADS, n, LANES), BF16),
                 jax.ShapeDtypeStruct((IDX_HEADS, n, LANES), BF16),
                 jax.ShapeDtypeStruct((LANES, n), BF16),
                 jax.ShapeDtypeStruct((n, LANES), F32)]
    out_specs = [pl.BlockSpec((ATTN_HEADS, tb, LANES), lambda i: (0, i, 0)),
                 pl.BlockSpec((ATTN_KV_HEADS, LANES, tb), lambda i: (0, 0, i)),
                 pl.BlockSpec((ATTN_KV_HEADS, tb, LANES), lambda i: (0, i, 0)),
                 pl.BlockSpec((IDX_HEADS, tb, LANES), lambda i: (0, i, 0)),
                 pl.BlockSpec((LANES, tb), lambda i: (0, i)),
                 pl.BlockSpec((tb, LANES), lambda i: (i, 0))]
    return pl.pallas_call(
        _prep_kernel, grid=(n // tb,), in_specs=in_specs, out_specs=out_specs,
        out_shape=out_shape, compiler_params=_cparams(1), name="hyb_prep",
    )(proj, proj, proj, proj, proj, pos, inv_c, pcos, pneg, ppos, q_norm, k_norm, kidx_norm_pad)


def _sortable(x):
    b = pltpu.bitcast(x, I32)
    return jnp.where(b < 0, b ^ jnp.int32(0x7FFFFFFF), b)


def _dsa_kernel(q_ref, qi_ref, wi_ref, k_ref, v_ref, ki_ref, triu_ref, eye_ref, onec_ref, out_ref,
                keys_ref, half_ref, bias_ref, wb_ref, mx_ref, acc_ref, *, k_sel, kc):
    i = pl.program_id(1)
    tq = Q_BLOCK
    nkc = (i * tq + tq + kc - 1) // kc
    row = i * tq + lax.broadcasted_iota(I32, (tq, kc), 0)
    lane = lax.broadcasted_iota(I32, (tq, kc), 1)
    nslab = kc // LANES

    w = wi_ref[...]
    for h in range(IDX_HEADS):
        wb_ref[h] = jnp.broadcast_to(w[:, TAIL_WI + h:TAIL_WI + h + 1], (tq, LANES))

    def idx_chunk(c):
        cols = pl.ds(pl.multiple_of(c * kc, kc), kc)
        d = jnp.dot(qi_ref[...].reshape(IDX_HEADS * tq, LANES), ki_ref[:, cols],
                    preferred_element_type=F32)
        slabs = [jnp.zeros((tq, LANES), F32)] * nslab
        for h in range(IDX_HEADS):
            wh = wb_ref[h]
            dh = d[h * tq:(h + 1) * tq, :]
            slabs = [slabs[a] + jnp.maximum(dh[:, a * LANES:(a + 1) * LANES], 0.0) * wh
                     for a in range(nslab)]
        sc = jnp.concatenate(slabs, axis=1)
        causal = (c * kc + lane) <= row
        keys_ref[:, cols] = jnp.where(causal, _sortable(sc), jnp.int32(INT_MIN))

    def idx_pair(g, carry):
        idx_chunk(2 * g)
        idx_chunk(2 * g + 1)
        return carry
    lax.fori_loop(0, nkc // 2, idx_pair, 0)

    @pl.when(nkc % 2 == 1)
    def _():
        idx_chunk(nkc - 1)

    def fold(x, op):
        acc = x[:, 0:LANES]
        for j in range(1, nslab):
            acc = op(acc, x[:, j * LANES:(j + 1) * LANES])
        return acc

    i16_min = -(1 << (HALF_BITS - 1))

    def count16(cand, strict):
        c16 = cand.astype(I16)

        def hits(start, width):
            kk = half_ref[:, pl.ds(pl.multiple_of(start, kc), width)]
            hit = (kk > c16) if strict else (kk >= c16)
            x = jnp.where(hit, jnp.int16(1), jnp.int16(0))
            acc = x[:, 0:LANES]
            for j in range(1, width // LANES):
                acc = acc + x[:, j * LANES:(j + 1) * LANES]
            return acc

        ngrp = nkc // COUNT_GROUP
        acc = lax.fori_loop(0, ngrp, lambda g, a: a + hits(g * (COUNT_GROUP * kc), COUNT_GROUP * kc),
                            jnp.zeros((tq, LANES), I16))
        acc = lax.fori_loop(ngrp * COUNT_GROUP, nkc, lambda c, a: a + hits(c * kc, kc), acc)
        return jnp.sum(acc.astype(F32), axis=1, keepdims=True)

    def kth_largest16(kf):
        zero = jnp.zeros((tq, 1), I32)
        lo = jnp.where(count16(zero, False) >= kf, zero, jnp.full((tq, 1), i16_min, I32))

        def bis_body(it, lo):
            cand = lo + jnp.left_shift(jnp.int32(1), jnp.int32(HALF_BITS - 2) - it)
            return jnp.where(count16(cand, False) >= kf, cand, lo)
        return lax.fori_loop(0, HALF_BITS - 1, bis_body, lo)

    def fill_half(fn):
        def body(c, carry):
            cols = pl.ds(pl.multiple_of(c * kc, kc), kc)
            half_ref[:, cols] = fn(keys_ref[:, cols]).astype(I16)
            return carry
        lax.fori_loop(0, nkc, body, 0)

    kf = jnp.float32(k_sel)
    fill_half(lambda kk: jnp.right_shift(kk, HALF_BITS))
    thr_hi = kth_largest16(kf)
    above = count16(thr_hi, True)
    low_mask = (1 << HALF_BITS) - 1
    fill_half(lambda kk: jnp.where(jnp.right_shift(kk, HALF_BITS) == thr_hi,
                                   (kk & low_mask) + i16_min, i16_min))
    thr_lo = kth_largest16(kf - above)
    thr = jnp.left_shift(thr_hi, HALF_BITS) + (thr_lo - i16_min)
    cnt_gt_lo = count16(thr_lo, True)
    need = kf - above - cnt_gt_lo
    n_eq = count16(thr_lo, False) - cnt_gt_lo
    row_plain = jnp.logical_or(jnp.logical_and(n_eq == need, thr_lo > i16_min),
                               thr == jnp.int32(INT_MIN))
    all_plain = jnp.min(jnp.where(row_plain, 1.0, 0.0)) > 0.5

    @pl.when(all_plain)
    def _():
        def fin_body(c, carry):
            cols = pl.ds(pl.multiple_of(c * kc, kc), kc)
            causal = (c * kc + lane) <= row
            sel = (keys_ref[:, cols] >= thr) & causal
            bias_ref[:, cols] = jnp.where(sel, 0.0, NEG_BIG).astype(BF16)
            return carry
        lax.fori_loop(0, nkc, fin_body, 0)

    @pl.when(jnp.logical_not(all_plain))
    def _():
        def fin_body(c, run):
            cols = pl.ds(pl.multiple_of(c * kc, kc), kc)
            kk = keys_ref[:, cols]
            eq = kk == thr
            eqf = jnp.where(eq, 1.0, 0.0)
            incl = jnp.dot(eqf.astype(BF16), triu_ref[...], preferred_element_type=F32)
            sel = (kk > thr) | (eq & ((run + incl) <= need))
            causal = (c * kc + lane) <= row
            bias_ref[:, cols] = jnp.where(sel & causal, 0.0, NEG_BIG).astype(BF16)
            return run + jnp.sum(eqf, axis=1, keepdims=True)
        lax.fori_loop(0, nkc, fin_body, jnp.zeros((tq, 1), F32))

    gq = ATTN_GROUP * tq

    def logits(j, kx):
        qg = q_ref[j * ATTN_GROUP:(j + 1) * ATTN_GROUP].reshape(gq, LANES)
        qx = jnp.concatenate([qg, eye_ref[...]], axis=1)
        return jnp.dot(qx, kx, preferred_element_type=F32)

    mx_ref[...] = jnp.full(mx_ref.shape, NEG_BIG, F32)
    acc_ref[...] = jnp.zeros(acc_ref.shape, F32)

    def chunk_logits(c):
        cols = pl.ds(pl.multiple_of(c * kc, kc), kc)
        bt = bias_ref[:, cols]
        return [logits(j, jnp.concatenate([k_ref[j, :, cols], bt], axis=0))
                for j in range(ATTN_KV_HEADS)]

    def chunk_update(c, s_list):
        cols = pl.ds(pl.multiple_of(c * kc, kc), kc)
        for j in range(ATTN_KV_HEADS):
            vx = jnp.concatenate([v_ref[j, cols, :], onec_ref[...]], axis=1)
            s = s_list[j]
            m_old = mx_ref[j]
            m_new = jnp.maximum(m_old, jnp.max(fold(s, jnp.maximum), axis=1, keepdims=True))
            alpha = jnp.exp2(m_old - m_new)
            p = jnp.concatenate(
                [jnp.exp2(s[:, a * LANES:(a + 1) * LANES] - m_new) for a in range(nslab)], axis=1)
            acc_ref[j] = (acc_ref[j] * jnp.concatenate([alpha, alpha], axis=1)
                          + jnp.dot(p.astype(BF16), vx, preferred_element_type=F32))
            mx_ref[j] = m_new

    def pair_body(g, carry):
        s0 = chunk_logits(2 * g)
        s1 = chunk_logits(2 * g + 1)
        chunk_update(2 * g, s0)
        chunk_update(2 * g + 1, s1)
        return carry
    lax.fori_loop(0, nkc // 2, pair_body, 0)

    @pl.when(nkc % 2 == 1)
    def _():
        chunk_update(nkc - 1, chunk_logits(nkc - 1))

    for h in range(ATTN_HEADS):
        a = acc_ref[h // ATTN_GROUP, (h % ATTN_GROUP) * tq:(h % ATTN_GROUP + 1) * tq, :]
        out_ref[:, h * LANES:(h + 1) * LANES] = (
            a[:, 0:LANES] / a[:, LANES:LANES + 1]).astype(out_ref.dtype)


def dsa_attention(q_r, qi_r, wi, k_r, v_r, ki_r, batch, seq, k_sel):
    n = batch * seq
    nq = seq // Q_BLOCK
    kc = _pick(seq, 1024)
    triu =jnp.asarray((np.arange(kc)[:, None] <= np.arange(kc)[None, :]).astype(np.float32), BF16)
    gq = ATTN_GROUP * Q_BLOCK
    eye = jnp.asarray(np.tile(np.eye(Q_BLOCK, dtype=np.float32), (ATTN_GROUP, 1)), BF16)
    onec = np.zeros((kc, LANES), np.float32)
    onec[:, 0] = 1.0
    onec = jnp.asarray(onec, BF16)
    kern = functools.partial(_dsa_kernel, k_sel=k_sel, kc=kc)
    in_specs = [
        pl.BlockSpec((ATTN_HEADS, Q_BLOCK, LANES), lambda b, i: (0, b * nq + i, 0)),
        pl.BlockSpec((IDX_HEADS, Q_BLOCK, LANES), lambda b, i: (0, b * nq + i, 0)),
        pl.BlockSpec((Q_BLOCK, LANES), lambda b, i: (b * nq + i, 0)),
        pl.BlockSpec((ATTN_KV_HEADS, LANES, seq), lambda b, i: (0, 0, b)),
        pl.BlockSpec((ATTN_KV_HEADS, seq, LANES), lambda b, i: (0, b, 0)),
        pl.BlockSpec((LANES, seq), lambda b, i: (0, b)),
        pl.BlockSpec((kc, kc), lambda b, i: (0, 0)),
        pl.BlockSpec((gq, Q_BLOCK), lambda b, i: (0, 0)),
        pl.BlockSpec((kc, LANES), lambda b, i: (0, 0)),
    ]
    scratch = [pltpu.VMEM((Q_BLOCK, seq), I32), pltpu.VMEM((Q_BLOCK, seq), I16),
               pltpu.VMEM((Q_BLOCK, seq), BF16),
               pltpu.VMEM((IDX_HEADS, Q_BLOCK, LANES), F32),
               pltpu.VMEM((ATTN_KV_HEADS, gq, LANES), F32),
               pltpu.VMEM((ATTN_KV_HEADS, gq, 2 * LANES), F32)]
    return pl.pallas_call(
        kern, grid=(batch, nq), in_specs=in_specs,
        out_specs=pl.BlockSpec((Q_BLOCK, ATTN_INNER), lambda b, i: (b * nq + i, 0)),
        out_shape=jax.ShapeDtypeStruct((n, ATTN_INNER), BF16),
        scratch_shapes=scratch, compiler_params=_cparams(2), name="dsa_attention",
    )(q_r, qi_r, wi, k_r, v_r, ki_r, triu, eye, onec)


def _silu(x):
    return x * (1.0 / (1.0 + jnp.exp(-x)))


def _ssd_kernel(xbc_ref, z_ref, tail_ref, cw_ref, cb_ref, dtb_ref, alog_ref, dexp_ref, ng_ref,
                tri_ref, e_ref, out_ref, state_ref, ext_ref, y_ref):
    q = SSM_CHUNK
    hi = lax.Precision.HIGHEST

    @pl.when(pl.program_id(1) == 0)
    def _():
        state_ref[...] = jnp.zeros(state_ref.shape, F32)
        ext_ref[0:SUBLANES, :] = jnp.zeros((SUBLANES, SSM_CONV_DIM), F32)

    x = xbc_ref[...]
    ext_ref[SUBLANES:SUBLANES + q, :] = x
    conv = x * cw_ref[SSM_CONV - 1:SSM_CONV, :] + cb_ref[...]
    for k in range(1, SSM_CONV):
        conv = conv + ext_ref[SUBLANES - k:SUBLANES - k + q, :] * cw_ref[SSM_CONV - 1 - k:SSM_CONV - k, :]
    ext_ref[0:SUBLANES, :] = x[q - SUBLANES:q, :]
    xbc = _silu(conv)
    xs = xbc[:, 0:SSM_INNER]
    bm = xbc[:, SSM_INNER:SSM_INNER + SSM_GROUPS * SSM_STATE]
    cm = xbc[:, SSM_INNER + SSM_GROUPS * SSM_STATE:SSM_CONV_DIM]

    lane = lax.broadcasted_iota(I32, (q, LANES), 1)
    is_dt = (lane >= TAIL_DT) & (lane < TAIL_DT + SSM_HEADS)
    raw = tail_ref[...] + dtb_ref[...]
    dt = jnp.where(is_dt, jnp.maximum(raw, 0.0) + jnp.log1p(jnp.exp(-jnp.abs(raw))), 0.0)
    a = -jnp.exp(alog_ref[...])
    da = jnp.where(is_dt, dt * a, 0.0)
    acum = jnp.dot(tri_ref[...], da, precision=hi, preferred_element_type=F32)
    acum_t = acum.T
    acum_x = jnp.dot(acum, e_ref[...], precision=hi, preferred_element_type=F32)
    dt_x = jnp.dot(dt, e_ref[...], precision=hi, preferred_element_type=F32)
    last_x = acum_x[q - 1:q, :]
    xdt = xs * dt_x
    xdec = xdt * jnp.exp(last_x - acum_x)
    ri = lax.broadcasted_iota(I32, (q, q), 0)
    ci = lax.broadcasted_iota(I32, (q, q), 1)
    tril = ri >= ci
    pairs_per_group = SSM_GROUP_INNER // LANES

    for g in range(SSM_GROUPS):
        bg = bm[:, g * SSM_STATE:(g + 1) * SSM_STATE]
        cg = cm[:, g * SSM_STATE:(g + 1) * SSM_STATE]
        cgb = cg.astype(BF16)
        gsl = slice(g * SSM_GROUP_INNER, (g + 1) * SSM_GROUP_INNER)
        cb = lax.dot_general(cgb, bg.astype(BF16), (((1,), (1,)), ((), ())),
                             preferred_element_type=F32)
        y_off = jnp.dot(cgb, state_ref[g].astype(BF16), preferred_element_type=F32)
        y_ref[:, gsl] = y_off * jnp.exp(acum_x[:, gsl])
        for mth in range(pairs_per_group):
            slab = slice((g * pairs_per_group + mth) * LANES, (g * pairs_per_group + mth + 1) * LANES)
            xp = xdt[:, slab]
            yd = None
            for side in range(2):
                hl = TAIL_DT + (g * pairs_per_group + mth) * 2 + side
                seg = acum[:, hl:hl + 1] - acum_t[hl:hl + 1, :]
                lmat = jnp.exp(jnp.where(tril, seg, NEG_BIG))
                if side == 0:
                    xh = jnp.where(lane < SSM_HEAD_DIM, xp, 0.0)
                else:
                    xh = jnp.where(lane >= SSM_HEAD_DIM, xp, 0.0)
                part = jnp.dot((cb * lmat).astype(BF16), xh.astype(BF16), preferred_element_type=F32)
                yd = part if yd is None else yd + part
            y_ref[:, slab] = y_ref[:, slab] + yd
        st = jnp.dot(bg.T.astype(BF16), xdec[:, gsl].astype(BF16), preferred_element_type=F32)
        state_ref[g] = state_ref[g] * jnp.exp(last_x[:, gsl]) + st

    y = (y_ref[...] + dexp_ref[...] * xs) * _silu(z_ref[...])
    for g in range(SSM_GROUPS):
        gsl = slice(g * SSM_GROUP_INNER, (g + 1) * SSM_GROUP_INNER)
        yg = y[:, gsl]
        ms = jnp.mean(yg * yg, axis=-1, keepdims=True)
        out_ref[:, gsl] = (yg * lax.rsqrt(ms + EPS) * ng_ref[:, gsl]).astype(out_ref.dtype)


def ssd_mixer(proj, conv_w, conv_b, dtb_pad, alog_pad, d_exp, norm_g, batch, seq):
    n = batch * seq
    q = SSM_CHUNK
    nc = seq // q
    tri = jnp.asarray((np.arange(q)[:, None] >= np.arange(q)[None, :]).astype(np.float32))
    e = np.zeros((LANES, SSM_INNER), np.float32)
    for h in range(SSM_HEADS):
        e[TAIL_DT + h, h * SSM_HEAD_DIM:(h + 1) * SSM_HEAD_DIM] = 1.0
    e = jnp.asarray(e)

    def col(width, off):
        return pl.BlockSpec((q, width), lambda b, c: (b * nc + c, off // width))

    def full(shape):
        return pl.BlockSpec(shape, lambda b, c: (0,) * len(shape))

    in_specs = [col(SSM_CONV_DIM, COL_XBC), col(SSM_INNER, COL_Z), col(LANES, COL_TAIL),
                full((SSM_CONV, SSM_CONV_DIM)), full((1, SSM_CONV_DIM)), full((1, LANES)),
                full((1, LANES)), full((1, SSM_INNER)), full((1, SSM_INNER)),
                full((q, q)), full((LANES, SSM_INNER))]
    scratch = [pltpu.VMEM((SSM_GROUPS, SSM_STATE, SSM_GROUP_INNER), F32),
               pltpu.VMEM((SUBLANES + q, SSM_CONV_DIM), F32),
               pltpu.VMEM((q, SSM_INNER), F32)]
    return pl.pallas_call(
        _ssd_kernel, grid=(batch, nc), in_specs=in_specs,
        out_specs=pl.BlockSpec((q, SSM_INNER), lambda b, c: (b * nc + c, 0)),
        out_shape=jax.ShapeDtypeStruct((n, SSM_INNER), BF16),
        scratch_shapes=scratch, compiler_params=_cparams(2), name="ssd_mixer",
    )(proj, proj, proj, conv_w, conv_b, dtb_pad, alog_pad, d_exp, norm_g, tri, e)


def _ret_kernel(q_ref, k_ref, v_ref, g_ref, pos_ref, inv_ref, dmat_ref, qdec_ref, kdec_ref,
                cdec_ref, ng_ref, out_ref, state_ref):
    half = RET_QK_DIM // 2

    @pl.when(pl.program_id(1) == 0)
    def _():
        state_ref[...] = jnp.zeros(state_ref.shape, F32)

    ang = pos_ref[...].astype(F32) * inv_ref[...]
    cosf = jnp.cos(ang)
    sinf = jnp.sin(ang)

    def rope(x):
        x1 = x[:, 0:half]
        x2 = x[:, half:2 * half]
        return jnp.concatenate([x1 * cosf - x2 * sinf, x2 * cosf + x1 * sinf], axis=-1)

    for h in range(RET_HEADS):
        qh = rope(q_ref[:, h * RET_QK_DIM:(h + 1) * RET_QK_DIM].astype(F32))
        kh = rope(k_ref[:, h * RET_QK_DIM:(h + 1) * RET_QK_DIM].astype(F32)) * (RET_QK_DIM ** -0.5)
        vh = v_ref[:, h * RET_V_DIM:(h + 1) * RET_V_DIM]
        qb = qh.astype(BF16)
        s = lax.dot_general(qb, kh.astype(BF16), (((1,), (1,)), ((), ())),
                            preferred_element_type=F32) * dmat_ref[h]
        inner = jnp.dot(s.astype(BF16), vh, preferred_element_type=F32)
        st = state_ref[h]
        cross = jnp.dot(qb, st.astype(BF16), preferred_element_type=F32) * qdec_ref[h]
        kd = (kh * kdec_ref[h]).T.astype(BF16)
        state_ref[h] = st * cdec_ref[h] + jnp.dot(kd, vh, preferred_element_type=F32)
        o = inner + cross
        ms = jnp.mean(o * o, axis=-1, keepdims=True)
        o = o * lax.rsqrt(ms + EPS) * ng_ref[:, h * RET_V_DIM:(h + 1) * RET_V_DIM]
        gate = _silu(g_ref[:, h * RET_V_DIM:(h + 1) * RET_V_DIM].astype(F32))
        out_ref[:, h * RET_V_DIM:(h + 1) * RET_V_DIM] = (gate * o).astype(out_ref.dtype)


def retention_mixer(proj, pos, norm_g, batch, seq):
    n = batch * seq
    q = RET_CHUNK
    nc = seq // q
    half = RET_QK_DIM // 2
    inv = jnp.power(RET_THETA, -(jnp.arange(half, dtype=F32) * 2.0 / RET_QK_DIM)).reshape(1, half)
    log_gamma = jnp.log(1.0 - jnp.power(2.0, -5.0 - jnp.arange(RET_HEADS, dtype=F32)))
    i = jnp.arange(q, dtype=F32)
    diff = i[:, None] - i[None, :]
    dmat = jnp.where(diff[None] >= 0,
                     jnp.exp(jnp.maximum(diff, 0.0)[None] * log_gamma[:, None, None]), 0.0)
    qdec = jnp.exp((i + 1.0)[None, :, None] * log_gamma[:, None, None])
    kdec = jnp.exp((q - 1.0 - i)[None, :, None] * log_gamma[:, None, None])
    cdec = jnp.broadcast_to(jnp.exp(q * log_gamma)[:, None, None], (RET_HEADS, 1, RET_V_DIM))

    def col(width, off):
        return pl.BlockSpec((q, width), lambda b, c: (b * nc + c, off // width))

    def full(shape):
        return pl.BlockSpec(shape, lambda b, c: (0,) * len(shape))

    in_specs = [col(RET_QK_TOTAL, 0), col(RET_QK_TOTAL, RET_QK_TOTAL),
                col(RET_V_TOTAL, 2 * RET_QK_TOTAL), col(RET_V_TOTAL, 2 * RET_QK_TOTAL + RET_V_TOTAL),
                pl.BlockSpec((q, 1), lambda b, c: (b * nc + c, 0)),
                full((1, half)), full((RET_HEADS, q, q)), full((RET_HEADS, q, 1)),
                full((RET_HEADS, q, 1)), full((RET_HEADS, 1, RET_V_DIM)), full((1, RET_V_TOTAL))]
    return pl.pallas_call(
        _ret_kernel, grid=(batch, nc), in_specs=in_specs,
        out_specs=pl.BlockSpec((q, RET_V_TOTAL), lambda b, c: (b * nc + c, 0)),
        out_shape=jax.ShapeDtypeStruct((n, RET_V_TOTAL), BF16),
        scratch_shapes=[pltpu.VMEM((RET_HEADS, RET_QK_DIM, RET_V_DIM), F32)],
        compiler_params=_cparams(2), name="retention",
    )(proj, proj, proj, proj, pos, inv, dmat, qdec, kdec, cdec, norm_g.reshape(1, RET_V_TOTAL))


ROUTE_E0 = MOE_GROUPS
INFO_EID, INFO_GATE, INFO_RANK = 0, 2, 4


def _router_kernel(x_ref, g_ref, wr_ref, lt_ref, h_ref, info_ref, cnt_ref, run_ref):
    tb = x_ref.shape[0]

    @pl.when(pl.program_id(0) == 0)
    def _():
        run_ref[...] = jnp.zeros(run_ref.shape, F32)

    x = x_ref[...]
    ms = jnp.mean(x * x, axis=-1, keepdims=True)
    hn = x * lax.rsqrt(ms + EPS) * g_ref[...]
    h_ref[...] = hn
    h_hi = hn.astype(BF16)
    h_lo = (hn - h_hi.astype(F32)).astype(BF16)
    both = jnp.dot(h_hi, wr_ref[...], preferred_element_type=F32)
    logits = (both[:, 0:LANES] + both[:, LANES:2 * LANES]
              + jnp.dot(h_lo, wr_ref[:, 0:LANES], preferred_element_type=F32))

    lane = lax.broadcasted_iota(I32, (tb, LANES), 1).astype(F32)
    far = jnp.float32(4 * LANES)
    ninf = jnp.float32(-jnp.inf)
    gl = jnp.where(lane < MOE_GROUPS, logits, ninf)
    gmax = jnp.max(gl, axis=1, keepdims=True)
    grp = jnp.min(jnp.where(gl == gmax, lane, far), axis=1, keepdims=True)
    p_grp = 1.0 / jnp.sum(jnp.exp(gl - gmax), axis=1, keepdims=True)
    lo = ROUTE_E0 + grp * MOE_EXPERTS_PER_GROUP
    el = jnp.where((lane >= lo) & (lane < lo + MOE_EXPERTS_PER_GROUP), logits, ninf)
    m1 = jnp.max(el, axis=1, keepdims=True)
    i1 = jnp.min(jnp.where(el == m1, lane, far), axis=1, keepdims=True)
    el2 = jnp.where(lane == i1, ninf, el)
    m2 = jnp.max(el2, axis=1, keepdims=True)
    i2 = jnp.min(jnp.where(el2 == m2, lane, far), axis=1, keepdims=True)
    e2 = jnp.exp(m2 - m1)
    g1 = p_grp / (1.0 + e2)
    g2 = p_grp * e2 / (1.0 + e2)
    oh1 = jnp.where(lane == i1, 1.0, 0.0)
    oh2 = jnp.where(lane == i2, 1.0, 0.0)
    cnt = oh1 + oh2
    before = jnp.dot(lt_ref[...], cnt.astype(BF16), preferred_element_type=F32) + run_ref[...]
    r1 = jnp.sum(oh1 * before, axis=1, keepdims=True)
    r2 = jnp.sum(oh2 * before, axis=1, keepdims=True)
    run_ref[...] = run_ref[...] + jnp.sum(cnt, axis=0, keepdims=True)
    cnt_ref[...] = run_ref[...]

    info = jnp.where(lane == INFO_EID, i1 - ROUTE_E0, 0.0)
    info = jnp.where(lane == INFO_EID + 1, i2 - ROUTE_E0, info)
    info = jnp.where(lane == INFO_GATE, g1, info)
    info = jnp.where(lane == INFO_GATE + 1, g2, info)
    info = jnp.where(lane == INFO_RANK, r1, info)
    info = jnp.where(lane == INFO_RANK + 1, r2, info)
    info_ref[...] = info


def moe_router(x, g, w_router_pad):
    n, d = x.shape
    tb = _pick(n, 512)
    lt = jnp.asarray((np.arange(tb)[:, None] > np.arange(tb)[None, :]).astype(np.float32), BF16)
    w_hi = w_router_pad.astype(BF16)
    w_lo = (w_router_pad - w_hi.astype(F32)).astype(BF16)
    w_split = jnp.concatenate([w_hi, w_lo], axis=1)
    return pl.pallas_call(
        _router_kernel, grid=(n // tb,),
        in_specs=[pl.BlockSpec((tb, d), lambda i: (i, 0)), pl.BlockSpec((1, d), lambda i: (0, 0)),
                  pl.BlockSpec((d, 2 * LANES), lambda i: (0, 0)), pl.BlockSpec((tb, tb), lambda i: (0, 0))],
        out_specs=[pl.BlockSpec((tb, d), lambda i: (i, 0)), pl.BlockSpec((tb, LANES), lambda i: (i, 0)),
                   pl.BlockSpec((1, LANES), lambda i: (0, 0))],
        out_shape=[jax.ShapeDtypeStruct((n, d), F32), jax.ShapeDtypeStruct((n, LANES), F32),
                   jax.ShapeDtypeStruct((1, LANES), F32)],
        scratch_shapes=[pltpu.VMEM((1, LANES), F32)],
        compiler_params=_cparams(1), name="moe_router",
    )(x, g.reshape(1, d), w_split, lt)


def _row_copy(src_ref, src_row, dst_ref, dst_row, sem):
    return pltpu.make_async_copy(src_ref.at[pl.ds(src_row, 1), :], dst_ref.at[pl.ds(dst_row, 1), :], sem)


def _expert_kernel(be_ref, nused_ref, st_ref, nxe_ref, h_hbm, wg_hbm, wu_hbm, wd_hbm, out_ref,
                   xbuf, sem, wgf, wuf, wdf, wsem, wgu_s, wd_s, *, layer):
    i = pl.program_id(0)
    slot = i % 2
    nused = nused_ref[0]
    d = xbuf.shape[2]
    ff = EXPERT_FF

    def weight_copies(e):
        return (pltpu.make_async_copy(wg_hbm.at[layer, e], wgf, wsem),
                pltpu.make_async_copy(wu_hbm.at[layer, e], wuf, wsem),
                pltpu.make_async_copy(wd_hbm.at[layer, e], wdf, wsem))

    @pl.when(jnp.logical_and(i == 0, nused > 0))
    def _():
        for c in weight_copies(be_ref[0]):
            c.start()

    def gather(blk, s):
        def body(r, carry):
            _row_copy(h_hbm, st_ref[blk * MOE_BLOCK + r], xbuf.at[s], r, sem.at[s]).start(priority=1)
            return carry
        lax.fori_loop(0, MOE_BLOCK, body, 0, unroll=8)

    @pl.when(jnp.logical_and(i == 0, nused > 0))
    def _():
        gather(0, 0)

    @pl.when(i + 1 < nused)
    def _():
        gather(i + 1, 1 - slot)

    @pl.when(i < nused)
    def _():
        def drain(r, carry):
            _row_copy(h_hbm, 0, xbuf.at[slot], 0, sem.at[slot]).wait()
            return carry
        lax.fori_loop(0, MOE_BLOCK, drain, 0, unroll=8)

        changed = jnp.logical_or(i == 0, be_ref[i] != be_ref[jnp.maximum(i - 1, 0)])

        @pl.when(changed)
        def _():
            for c in weight_copies(be_ref[i]):
                c.wait()
            rc = _pick(d, 256)

            def body(r, carry):
                rows = pl.ds(pl.multiple_of(r * rc, rc), rc)
                wgu_s[rows, 0:ff] = wgf[rows, :].astype(BF16)
                wgu_s[rows, ff:2 * ff] = wuf[rows, :].astype(BF16)
                return carry
            lax.fori_loop(0, d // rc, body, 0)
            wd_s[...] = wdf[...].astype(BF16)

            nxt_e = nxe_ref[be_ref[i]]

            @pl.when(nxt_e >= 0)
            def _():
                for c in weight_copies(nxt_e):
                    c.start()

        gu = jnp.dot(xbuf[slot].astype(BF16), wgu_s[...], preferred_element_type=F32)
        hid = _silu(gu[:, 0:ff]) * gu[:, ff:2 * ff]
        out_ref[...] = jnp.dot(hid.astype(BF16), wd_s[...], preferred_element_type=F32)

    @pl.when(i >= nused)
    def _():
        out_ref[...] = jnp.zeros(out_ref.shape, out_ref.dtype)


def moe_experts(h, slot_tok, blk_expert, nused, next_expert, w_gate, w_up, w_down, layer):
    n, d = h.shape
    p = slot_tok.shape[0]
    nblk = p // MOE_BLOCK
    ff = EXPERT_FF
    hbm = pl.BlockSpec(memory_space=pl.ANY)
    grid_spec = pltpu.PrefetchScalarGridSpec(
        num_scalar_prefetch=4, grid=(nblk,),
        in_specs=[hbm, hbm, hbm, hbm],
        out_specs=pl.BlockSpec((MOE_BLOCK, d), lambda i, be, nu, st, nx: (i, 0)),
        scratch_shapes=[pltpu.VMEM((2, MOE_BLOCK, d), h.dtype), pltpu.SemaphoreType.DMA((2,)),
                        pltpu.VMEM((d, ff), F32), pltpu.VMEM((d, ff), F32), pltpu.VMEM((ff, d), F32),
                        pltpu.SemaphoreType.DMA(()),
                        pltpu.VMEM((d, 2 * ff), BF16), pltpu.VMEM((ff, d), BF16)])
    return pl.pallas_call(
        functools.partial(_expert_kernel, layer=layer), grid_spec=grid_spec,
        out_shape=jax.ShapeDtypeStruct((p, d), F32),
        compiler_params=_cparams(1), name="moe_experts",
    )(blk_expert, nused, slot_tok, next_expert, h, w_gate, w_up, w_down)


def _slots_kernel(dest_ref, slot_ref):
    def clear(s, carry):
        slot_ref[s] = 0
        return carry
    lax.fori_loop(0, slot_ref.shape[0], clear, 0, unroll=16)

    def body(t, carry):
        for k in range(MOE_TOPK):
            slot_ref[dest_ref[t * MOE_TOPK + k]] = t
        return carry
    lax.fori_loop(0, dest_ref.shape[0] // MOE_TOPK, body, 0, unroll=8)


def moe_slots(dest_flat, p):
    smem = pl.BlockSpec(memory_space=pltpu.SMEM)
    return pl.pallas_call(
        _slots_kernel,
        in_specs=[smem],
        out_specs=smem,
        out_shape=jax.ShapeDtypeStruct((p,), I32),
        name="moe_slots",
    )(dest_flat)


def _combine_kernel(dcur_ref, dnxt_ref, info_ref, x_ref, y_hbm, out_ref, ybuf, sem, *, tb, nsteps):
    i = pl.program_id(0)
    slot = i % 2

    def issue(dref, s):
        def body(r, carry):
            for k in range(MOE_TOPK):
                _row_copy(y_hbm, dref[0, 0, MOE_TOPK * r + k], ybuf.at[s, k], r, sem.at[s]).start(
                    priority=k % 2)
            return carry
        lax.fori_loop(0, tb, body, 0, unroll=8)

    @pl.when(i == 0)
    def _():
        issue(dcur_ref, 0)

    @pl.when(i + 1 < nsteps)
    def _():
        issue(dnxt_ref, 1 - slot)

    def drain(r, carry):
        for k in range(MOE_TOPK):
            _row_copy(y_hbm, 0, ybuf.at[slot, k], 0, sem.at[slot]).wait()
        return carry
    lax.fori_loop(0, tb, drain, 0, unroll=8)

    info = info_ref[...]
    out = x_ref[...]
    for k in range(MOE_TOPK):
        out = out + ybuf[slot, k] * info[:, INFO_GATE + k:INFO_GATE + k + 1]
    out_ref[...] = out


def moe_combine(x, y, dest, info):
    n, d = x.shape
    tb = _pick(n, COMBINE_ROWS)
    nsteps = n // tb
    kern = functools.partial(_combine_kernel, tb=tb, nsteps=nsteps)
    dest3 = dest.reshape(nsteps, 1, MOE_TOPK * tb)
    return pl.pallas_call(
        kern, grid=(nsteps,),
        in_specs=[pl.BlockSpec((1, 1, MOE_TOPK * tb), lambda i: (i, 0, 0), memory_space=pltpu.SMEM),
                  pl.BlockSpec((1, 1, MOE_TOPK * tb), lambda i: (jnp.minimum(i + 1, nsteps - 1), 0, 0),
                               memory_space=pltpu.SMEM),
                  pl.BlockSpec((tb, LANES), lambda i: (i, 0)),
                  pl.BlockSpec((tb, d), lambda i: (i, 0)),
                  pl.BlockSpec(memory_space=pl.ANY)],
        out_specs=pl.BlockSpec((tb, d), lambda i: (i, 0)),
        out_shape=jax.ShapeDtypeStruct((n, d), x.dtype),
        scratch_shapes=[pltpu.VMEM((2, MOE_TOPK, tb, d), y.dtype), pltpu.SemaphoreType.DMA((2,))],
        compiler_params=_cparams(1), name="moe_combine",
    )(dest3, dest3, info, x, y)


def hier_moe_layer(x, norm_g, w_rg, w_re, w_gate, w_up, w_down, layer):
    n, d = x.shape
    w_router = jnp.concatenate(
        [w_rg, w_re, jnp.zeros((d, LANES - MOE_GROUPS - N_EXPERTS), F32)], axis=1)
    h, info, cnt = moe_router(x, norm_g, w_router)
    counts = cnt[0, ROUTE_E0:ROUTE_E0 + N_EXPERTS].astype(I32)
    eid = info[:, INFO_EID:INFO_EID + MOE_TOPK].astype(I32)
    rank = info[:, INFO_RANK:INFO_RANK + MOE_TOPK].astype(I32)
    padded = (counts + MOE_BLOCK - 1) // MOE_BLOCK * MOE_BLOCK
    pad_end = jnp.cumsum(padded)
    pad_start = pad_end - padded
    onehot = (eid[..., None] == jnp.arange(N_EXPERTS, dtype=I32)).astype(F32)
    dest = jnp.dot(onehot, pad_start.astype(F32), precision=lax.Precision.HIGHEST).astype(I32) + rank
    a = n * MOE_TOPK
    p = (-(-a // MOE_BLOCK) + N_EXPERTS) * MOE_BLOCK
    nblk = p // MOE_BLOCK
    blk_start = jnp.arange(nblk, dtype=I32) * MOE_BLOCK
    blk_expert = jnp.minimum(jnp.sum((pad_end[None, :] <= blk_start[:, None]).astype(I32), axis=1),
                             N_EXPERTS - 1).astype(I32)
    nused = (pad_end[-1:] // MOE_BLOCK).astype(I32)
    slot_tok = moe_slots(dest.reshape(-1), p)
    ids = jnp.arange(N_EXPERTS, dtype=I32)
    owner = jnp.where(counts > 0, ids, N_EXPERTS)
    after = jnp.concatenate([lax.cummin(owner, reverse=True)[1:], jnp.full((1,), N_EXPERTS, I32)])
    next_expert = jnp.where(after < N_EXPERTS, after, -1).astype(I32)
    y = moe_experts(h, slot_tok, blk_expert, nused, next_expert, w_gate, w_up, w_down, layer)
    return moe_combine(x, y, dest, info)


def _pad_lanes(v, offset, width=LANES):
    out = jnp.zeros((1, width), F32)
    return out.at[0, offset:offset + v.shape[0]].set(v.astype(F32))


def hybrid_layer(x, pos, norm_g, w_in, q_norm, k_norm, kidx_norm, conv_w, conv_b, dt_bias, a_log,
                 d_skip, ssm_norm, w_out, batch, seq, k_sel):
    n, d = x.shape
    offs = np.cumsum([0, ATTN_INNER, ATTN_KV, ATTN_KV, IDX_INNER, IDX_HEAD_DIM, IDX_HEADS,
                      SSM_INNER, SSM_CONV_DIM, SSM_HEADS])
    seg = {name: w_in[:, offs[j]:offs[j + 1]] for j, name in
           enumerate(["q", "k", "v", "qi", "ki", "wi", "z", "xbc", "dt"])}
    tail_pad = LANES - IDX_HEAD_DIM - IDX_HEADS - SSM_HEADS
    w_perm = jnp.concatenate(
        [seg["xbc"], seg["k"], seg["v"], seg["z"], seg["q"], seg["qi"], seg["ki"], seg["wi"],
         seg["dt"], jnp.zeros((d, tail_pad + HYB_COLS - COL_TAIL - LANES), F32)], axis=1).astype(BF16)
    proj = fused_matmul([x], w_perm, g=norm_g, out_dtype=F32, tn=1024, name="hyb_in_proj")

    rot = ATTN_HEAD_DIM // ROPE_FRACTION
    rot_i = IDX_HEAD_DIM // ROPE_FRACTION
    inv16 = jnp.power(ROPE_THETA, -(jnp.arange(rot // 2, dtype=F32) * 2.0 / rot))
    inv8 = jnp.power(ROPE_THETA, -(jnp.arange(rot_i // 2, dtype=F32) * 2.0 / rot_i))
    inv_c = jnp.concatenate(
        [inv16, inv8, jnp.zeros((LANES - rot // 2 - rot_i // 2,), F32)]).reshape(1, LANES)
    q_r, k_r, v_r, qi_r, ki_r, wi = hyb_prep(
        proj, pos, inv_c, q_norm.reshape(1, LANES), k_norm.reshape(1, LANES),
        _pad_lanes(kidx_norm, TAIL_KI))
    attn = dsa_attention(q_r, qi_r, wi, k_r, v_r, ki_r, batch, seq, k_sel)

    ssm = ssd_mixer(proj, conv_w, conv_b.reshape(1, SSM_CONV_DIM), _pad_lanes(dt_bias, TAIL_DT),
                    _pad_lanes(a_log, TAIL_DT), jnp.repeat(d_skip, SSM_HEAD_DIM).reshape(1, SSM_INNER),
                    ssm_norm.reshape(1, SSM_INNER), batch, seq)
    return fused_matmul([attn, ssm], w_out.astype(BF16), res=x, out_dtype=F32, name="hyb_out_proj")


def retention_layer(x, pos, norm_g, w_in, ret_norm, w_out, batch, seq):
    proj = fused_matmul([x], w_in.astype(BF16), g=norm_g, out_dtype=BF16, tn=1024, name="ret_in_proj")
    o = retention_mixer(proj, pos, ret_norm, batch, seq)
    return fused_matmul([o], w_out.astype(BF16), res=x, out_dtype=F32, name="ret_out_proj")


def kernel(x, positions, mix_norm, ffn_norm, hyb_w_in, attn_q_norm, attn_k_norm, idx_k_norm, ssm_conv_w, ssm_conv_b, ssm_dt_bias, ssm_a_log, ssm_d, ssm_norm, hyb_w_out, ret_w_in, ret_norm, ret_w_out, moe_router_group, moe_router_expert, moe_w_gate, moe_w_up, moe_w_down):
    batch, seq, d = x.shape
    depth = mix_norm.shape[0]
    k_sel = min(TOPK_MAX, seq // 4)
    n = batch * seq
    xf = x.reshape(n, d)
    pos = positions.reshape(n, 1).astype(I32)
    for layer in range(depth):
        i = layer // 2
        if layer % 2 == 0:
            xf = hybrid_layer(xf, pos, mix_norm[layer], hyb_w_in[i], attn_q_norm[i], attn_k_norm[i],
                              idx_k_norm[i], ssm_conv_w[i], ssm_conv_b[i], ssm_dt_bias[i],
                              ssm_a_log[i], ssm_d[i], ssm_norm[i], hyb_w_out[i], batch, seq, k_sel)
        else:
            xf = retention_layer(xf, pos, mix_norm[layer], ret_w_in[i], ret_norm[i], ret_w_out[i],
                                 batch, seq)
        xf = hier_moe_layer(xf, ffn_norm[layer], moe_router_group[layer], moe_router_expert[layer],
                            moe_w_gate, moe_w_up, moe_w_down, layer)
    return xf.reshape(batch, seq, d)
```

```python
import functools

import jax
import jax.numpy as jnp
import numpy as np
from jax import lax
from jax.experimental import pallas as pl
from jax.experimental.pallas import tpu as pltpu

F32 = jnp.float32
BF16 = jnp.bfloat16
I32 = jnp.int32
I16 = jnp.int16
HALF_BITS = 16
COUNT_GROUP = 4

ATTN_HEADS = 8
ATTN_KV_HEADS = 2
ATTN_GROUP = ATTN_HEADS // ATTN_KV_HEADS
ATTN_HEAD_DIM = 128
ATTN_INNER = ATTN_HEADS * ATTN_HEAD_DIM
ATTN_KV = ATTN_KV_HEADS * ATTN_HEAD_DIM
IDX_HEADS = 8
IDX_HEAD_DIM = 64
IDX_INNER = IDX_HEADS * IDX_HEAD_DIM
TOPK_MAX = 256
Q_BLOCK = 128
ROPE_THETA = 500000.0
ROPE_FRACTION = 4

SSM_HEADS = 16
SSM_HEAD_DIM = 64
SSM_INNER = SSM_HEADS * SSM_HEAD_DIM
SSM_GROUPS = 2
SSM_STATE = 128
SSM_CONV = 4
SSM_CONV_DIM = SSM_INNER + 2 * SSM_GROUPS * SSM_STATE
SSM_CHUNK = 128
SSM_GROUP_INNER = SSM_INNER // SSM_GROUPS

RET_HEADS = 8
RET_QK_DIM = 256
RET_V_DIM = 512
RET_QK_TOTAL = RET_HEADS * RET_QK_DIM
RET_V_TOTAL = RET_HEADS * RET_V_DIM
RET_CHUNK = 128
RET_STEP_CHUNKS = 2
RET_THETA = 10000.0

MOE_GROUPS = 8
MOE_EXPERTS_PER_GROUP = 8
N_EXPERTS = MOE_GROUPS * MOE_EXPERTS_PER_GROUP
MOE_TOPK = 2
EXPERT_FF = 512
MOE_BLOCK = 256
COMBINE_ROWS = 128

EPS = 1e-6

LANES = 128
SUBLANES = 8
VMEM_LIMIT = 52 * 1024 * 1024

LOG2E = 1.4426950408889634
NEG_BIG = -1e30
INT_MIN = -2147483648

COL_XBC = 0
COL_K = SSM_CONV_DIM
COL_V = COL_K + ATTN_KV
COL_Z = COL_V + ATTN_KV
COL_Q = COL_Z + SSM_INNER
COL_QI = COL_Q + ATTN_INNER
COL_TAIL = COL_QI + IDX_INNER
HYB_COLS = 5120
TAIL_KI = 0
TAIL_WI = IDX_HEAD_DIM
TAIL_DT = TAIL_WI + IDX_HEADS


def _cparams(n_axes):
    return pltpu.CompilerParams(dimension_semantics=("arbitrary",) * n_axes,
                                vmem_limit_bytes=VMEM_LIMIT)


def _pick(n, pref):
    t = min(n, pref)
    while n % t:
        t //= 2
    return t


def _mm_kernel(*refs, n_a, has_norm, has_res, row_chunk):
    a_refs = refs[:n_a]
    pos = n_a
    g_ref = None
    if has_norm:
        g_ref = refs[pos]
        pos += 1
    w_refs = refs[pos:pos + n_a]
    pos += n_a
    res_ref = None
    if has_res:
        res_ref = refs[pos]
        pos += 1
    out_ref = refs[pos]
    pos += 1
    xn_ref = refs[pos] if has_norm else None
    tm = out_ref.shape[0]

    if has_norm:
        @pl.when(pl.program_id(1) == 0)
        def _():
            def body(r, carry):
                rows = pl.ds(pl.multiple_of(r * row_chunk, row_chunk), row_chunk)
                x = a_refs[0][rows, :]
                ms = jnp.mean(x * x, axis=-1, keepdims=True)
                xn_ref[rows, :] = (x * lax.rsqrt(ms + EPS) * g_ref[...]).astype(BF16)
                return carry
            lax.fori_loop(0, tm // row_chunk, body, 0)
        acc = jnp.dot(xn_ref[...], w_refs[0][...], preferred_element_type=F32)
    else:
        acc = jnp.dot(a_refs[0][...], w_refs[0][...], preferred_element_type=F32)
        for p in range(1, n_a):
            acc = acc + jnp.dot(a_refs[p][...], w_refs[p][...], preferred_element_type=F32)
    if has_res:
        acc = acc + res_ref[...]
    out_ref[...] = acc.astype(out_ref.dtype)


def fused_matmul(a_list, w, *, g=None, res=None, out_dtype=F32, tm=1024, tn=512, name="mm"):
    n = a_list[0].shape[0]
    kp = a_list[0].shape[1]
    m = w.shape[1]
    tm = _pick(n, tm)
    tn = _pick(m, tn)
    n_a = len(a_list)
    has_norm = g is not None
    has_res = res is not None
    in_specs = [pl.BlockSpec((tm, kp), lambda i, j: (i, 0)) for _ in a_list]
    args = list(a_list)
    if has_norm:
        in_specs.append(pl.BlockSpec((1, kp), lambda i, j: (0, 0)))
        args.append(g.reshape(1, kp).astype(F32))
    for p in range(n_a):
        in_specs.append(pl.BlockSpec((kp, tn), lambda i, j, p=p: (p, j)))
        args.append(w)
    if has_res:
        in_specs.append(pl.BlockSpec((tm, tn), lambda i, j: (i, j)))
        args.append(res)
    scratch = [pltpu.VMEM((tm, kp), BF16)] if has_norm else []
    kern = functools.partial(_mm_kernel, n_a=n_a, has_norm=has_norm, has_res=has_res,
                             row_chunk=_pick(tm, 128))
    return pl.pallas_call(
        kern,
        grid=(n // tm, m // tn),
        in_specs=in_specs,
        out_specs=pl.BlockSpec((tm, tn), lambda i, j: (i, j)),
        out_shape=jax.ShapeDtypeStruct((n, m), out_dtype),
        scratch_shapes=scratch,
        compiler_params=_cparams(2),
        name=name,
    )(*args)


def _rope_lanes(x, cosf, s_neg, s_pos, half):
    width = x.shape[-1]
    return (x * cosf + pltpu.roll(x, width - half, axis=1) * s_neg
            + pltpu.roll(x, half, axis=1) * s_pos)


def _prep_kernel(q_ref, k_ref, v_ref, qi_ref, tail_ref, pos_ref, invc_ref, pcos_ref, pneg_ref, ppos_ref,
                 qn_ref, kn_ref, kin_ref,
                 qo_ref, ko_ref, vo_ref, qio_ref, kio_ref, wio_ref):
    tb = q_ref.shape[0]
    posf = pos_ref[...].astype(F32)
    lane = lax.broadcasted_iota(I32, (tb, LANES), 1)

    hi = lax.Precision.HIGHEST
    ang = posf * invc_ref[...]
    cos_c = jnp.cos(ang)
    sin_c = jnp.sin(ang)
    cos_all = jnp.dot(cos_c, pcos_ref[...], precision=hi, preferred_element_type=F32)
    sneg_all = jnp.dot(sin_c, pneg_ref[...], precision=hi, preferred_element_type=F32)
    spos_all = jnp.dot(sin_c, ppos_ref[...], precision=hi, preferred_element_type=F32)

    half = ATTN_HEAD_DIM // ROPE_FRACTION // 2
    cosf = cos_all[:, 0:LANES]
    s_neg = sneg_all[:, 0:LANES]
    s_pos = spos_all[:, 0:LANES]
    scale = ATTN_HEAD_DIM ** -0.5 * LOG2E
    for h in range(ATTN_HEADS):
        x = q_ref[:, h * LANES:(h + 1) * LANES]
        ms = jnp.mean(x * x, axis=-1, keepdims=True)
        x = x * lax.rsqrt(ms + EPS) * qn_ref[...]
        x = _rope_lanes(x, cosf, s_neg, s_pos, half)
        qo_ref[h] = (x * scale).astype(BF16)
    for h in range(ATTN_KV_HEADS):
        x = k_ref[:, h * LANES:(h + 1) * LANES]
        ms = jnp.mean(x * x, axis=-1, keepdims=True)
        x = x * lax.rsqrt(ms + EPS) * kn_ref[...]
        x = _rope_lanes(x, cosf, s_neg, s_pos, half)
        ko_ref[h] = x.T.astype(BF16)
        vo_ref[h] = v_ref[:, h * LANES:(h + 1) * LANES].astype(BF16)

    half_i = IDX_HEAD_DIM // ROPE_FRACTION // 2
    cos_i = cos_all[:, LANES:LANES + IDX_INNER]
    sn_i = sneg_all[:, LANES:LANES + IDX_INNER]
    sp_i = spos_all[:, LANES:LANES + IDX_INNER]
    qi = _rope_lanes(qi_ref[...], cos_i, sn_i, sp_i, half_i)
    for h in range(IDX_HEADS):
        slab = qi[:, (h // 2) * LANES:(h // 2 + 1) * LANES]
        if h % 2:
            slab = pltpu.roll(slab, IDX_HEAD_DIM, axis=1)
        qio_ref[h] = jnp.where(lane < IDX_HEAD_DIM, slab, 0.0).astype(BF16)

    t = tail_ref[...]
    is_ki = lane < IDX_HEAD_DIM
    ms = jnp.sum(jnp.where(is_ki, t * t, 0.0), axis=-1, keepdims=True) * (1.0 / IDX_HEAD_DIM)
    kin = t * lax.rsqrt(ms + EPS) * kin_ref[...]
    kin = _rope_lanes(kin, cos_i[:, 0:LANES], sn_i[:, 0:LANES], sp_i[:, 0:LANES], half_i)
    kio_ref[...] = jnp.where(is_ki, kin, 0.0).T.astype(BF16)
    wio_ref[...] = t * (IDX_HEADS ** -0.5 * IDX_HEAD_DIM ** -0.5)


def _rope_selectors():
    a_half = ATTN_HEAD_DIM // ROPE_FRACTION // 2
    i_half = IDX_HEAD_DIM // ROPE_FRACTION // 2
    unit = LANES - 1
    width = LANES + IDX_INNER
    pcos = np.zeros((LANES, width), np.float32)
    pneg = np.zeros((LANES, width), np.float32)
    ppos = np.zeros((LANES, width), np.float32)
    for m in range(LANES):
        if m < 2 * a_half:
            pcos[m % a_half, m] = 1.0
            (pneg if m < a_half else ppos)[m % a_half, m] = -1.0 if m < a_half else 1.0
        else:
            pcos[unit, m] = 1.0
    for m in range(IDX_INNER):
        r = m % IDX_HEAD_DIM
        if r < 2 * i_half:
            pcos[a_half + r % i_half, LANES + m] = 1.0
            (pneg if r < i_half else ppos)[a_half + r % i_half, LANES + m] = -1.0 if r < i_half else 1.0
        else:
            pcos[unit, LANES + m] = 1.0
    return jnp.asarray(pcos), jnp.asarray(pneg), jnp.asarray(ppos)


def hyb_prep(proj, pos, inv_c, q_norm, k_norm, kidx_norm_pad):
    n = proj.shape[0]
    tb = _pick(n, 256)
    pcos, pneg, ppos = _rope_selectors()
    sel_w = LANES + IDX_INNER

    def col(width, off):
        return pl.BlockSpec((tb, width), lambda i: (i, off // width))

    def full(shape):
        return pl.BlockSpec(shape, lambda i: (0,) * len(shape))

    in_specs = [col(ATTN_INNER, COL_Q), col(ATTN_KV, COL_K), col(ATTN_KV, COL_V),
                col(IDX_INNER, COL_QI), col(LANES, COL_TAIL),
                pl.BlockSpec((tb, 1), lambda i: (i, 0)),
                full((1, LANES)), full((LANES, sel_w)), full((LANES, sel_w)), full((LANES, sel_w)),
                full((1, LANES)), full((1, LANES)), full((1, LANES))]
    out_shape = [jax.ShapeDtypeStruct((ATTN_HEADS, n, LANES), BF16),
                 jax.ShapeDtypeStruct((ATTN_KV_HEADS, LANES, n), BF16),
                 jax.ShapeDtypeStruct((ATTN_KV_HEADS, n, LANES), BF16),
                 jax.ShapeDtypeStruct((IDX_HEADS, n, LANES), BF16),
                 jax.ShapeDtypeStruct((LANES, n), BF16),
                 jax.ShapeDtypeStruct((n, LANES), F32)]
    out_specs = [pl.BlockSpec((ATTN_HEADS, tb, LANES), lambda i: (0, i, 0)),
                 pl.BlockSpec((ATTN_KV_HEADS, LANES, tb), lambda i: (0, 0, i)),
                 pl.BlockSpec((ATTN_KV_HEADS, tb, LANES), lambda i: (0, i, 0)),
                 pl.BlockSpec((IDX_HEADS, tb, LANES), lambda i: (0, i, 0)),
                 pl.BlockSpec((LANES, tb), lambda i: (0, i)),
                 pl.BlockSpec((tb, LANES), lambda i: (i, 0))]
    return pl.pallas_call(
        _prep_kernel, grid=(n // tb,), in_specs=in_specs, out_specs=out_specs,
        out_shape=out_shape, compiler_params=_cparams(1), name="hyb_prep",
    )(proj, proj, proj, proj, proj, pos, inv_c, pcos, pneg, ppos, q_norm, k_norm, kidx_norm_pad)


def _sortable(x):
    b = pltpu.bitcast(x, I32)
    return jnp.where(b < 0, b ^ jnp.int32(0x7FFFFFFF), b)


def _dsa_kernel(q_ref, qi_ref, wi_ref, k_ref, v_ref, ki_ref, triu_ref, eye_ref, onec_ref, out_ref,
                keys_ref, half_ref, bias_ref, wb_ref, mx_ref, acc_ref, *, k_sel, kc):
    i = pl.program_id(1)
    tq = Q_BLOCK
    nkc = (i * tq + tq + kc - 1) // kc
    row = i * tq + lax.broadcasted_iota(I32, (tq, kc), 0)
    lane = lax.broadcasted_iota(I32, (tq, kc), 1)
    nslab = kc // LANES

    w = wi_ref[...]
    for h in range(IDX_HEADS):
        wb_ref[h] = jnp.broadcast_to(w[:, TAIL_WI + h:TAIL_WI + h + 1], (tq, LANES))

    def idx_chunk(c):
        cols = pl.ds(pl.multiple_of(c * kc, kc), kc)
        d = jnp.dot(qi_ref[...].reshape(IDX_HEADS * tq, LANES), ki_ref[:, cols],
                    preferred_element_type=F32)
        slabs = [jnp.zeros((tq, LANES), F32)] * nslab
        for h in range(IDX_HEADS):
            wh = wb_ref[h]
            dh = d[h * tq:(h + 1) * tq, :]
            slabs = [slabs[a] + jnp.maximum(dh[:, a * LANES:(a + 1) * LANES], 0.0) * wh
                     for a in range(nslab)]
        sc = jnp.concatenate(slabs, axis=1)
        causal = (c * kc + lane) <= row
        keys_ref[:, cols] = jnp.where(causal, _sortable(sc), jnp.int32(INT_MIN))

    def idx_pair(g, carry):
        idx_chunk(2 * g)
        idx_chunk(2 * g + 1)
        return carry
    lax.fori_loop(0, nkc // 2, idx_pair, 0)

    @pl.when(nkc % 2 == 1)
    def _():
        idx_chunk(nkc - 1)

    def fold(x, op):
        acc = x[:, 0:LANES]
        for j in range(1, nslab):
            acc = op(acc, x[:, j * LANES:(j + 1) * LANES])
        return acc

    i16_min = -(1 << (HALF_BITS - 1))

    def count16(cand, strict):
        c16 = cand.astype(I16)

        def hits(start, width):
            kk = half_ref[:, pl.ds(pl.multiple_of(start, kc), width)]
            hit = (kk > c16) if strict else (kk >= c16)
            x = jnp.where(hit, jnp.int16(1), jnp.int16(0))
            acc = x[:, 0:LANES]
            for j in range(1, width // LANES):
                acc = acc + x[:, j * LANES:(j + 1) * LANES]
            return acc

        ngrp = nkc // COUNT_GROUP
        acc = lax.fori_loop(0, ngrp, lambda g, a: a + hits(g * (COUNT_GROUP * kc), COUNT_GROUP * kc),
                            jnp.zeros((tq, LANES), I16))
        acc = lax.fori_loop(ngrp * COUNT_GROUP, nkc, lambda c, a: a + hits(c * kc, kc), acc)
        return jnp.sum(acc.astype(F32), axis=1, keepdims=True)

    def kth_largest16(kf):
        zero = jnp.zeros((tq, 1), I32)
        lo = jnp.where(count16(zero, False) >= kf, zero, jnp.full((tq, 1), i16_min, I32))

        def bis_body(it, lo):
            cand = lo + jnp.left_shift(jnp.int32(1), jnp.int32(HALF_BITS - 2) - it)
            return jnp.where(count16(cand, False) >= kf, cand, lo)
        return lax.fori_loop(0, HALF_BITS - 1, bis_body, lo)

    def fill_half(fn):
        def body(c, carry):
            cols = pl.ds(pl.multiple_of(c * kc, kc), kc)
            half_ref[:, cols] = fn(keys_ref[:, cols]).astype(I16)
            return carry
        lax.fori_loop(0, nkc, body, 0)

    kf = jnp.float32(k_sel)
    fill_half(lambda kk: jnp.right_shift(kk, HALF_BITS))
    thr_hi = kth_largest16(kf)
    above = count16(thr_hi, True)
    low_mask = (1 << HALF_BITS) - 1
    fill_half(lambda kk: jnp.where(jnp.right_shift(kk, HALF_BITS) == thr_hi,
                                   (kk & low_mask) + i16_min, i16_min))
    thr_lo = kth_largest16(kf - above)
    thr = jnp.left_shift(thr_hi, HALF_BITS) + (thr_lo - i16_min)
    cnt_gt_lo = count16(thr_lo, True)
    need = kf - above - cnt_gt_lo
    n_eq = count16(thr_lo, False) - cnt_gt_lo
    row_plain = jnp.logical_or(jnp.logical_and(n_eq == need, thr_lo > i16_min),
                               thr == jnp.int32(INT_MIN))
    all_plain = jnp.min(jnp.where(row_plain, 1.0, 0.0)) > 0.5

    @pl.when(all_plain)
    def _():
        def fin_body(c, carry):
            cols = pl.ds(pl.multiple_of(c * kc, kc), kc)
            causal = (c * kc + lane) <= row
            sel = (keys_ref[:, cols] >= thr) & causal
            bias_ref[:, cols] = jnp.where(sel, 0.0, NEG_BIG).astype(BF16)
            return carry
        lax.fori_loop(0, nkc, fin_body, 0)

    @pl.when(jnp.logical_not(all_plain))
    def _():
        def fin_body(c, run):
            cols = pl.ds(pl.multiple_of(c * kc, kc), kc)
            kk = keys_ref[:, cols]
            eq = kk == thr
            eqf = jnp.where(eq, 1.0, 0.0)
            incl = jnp.dot(eqf.astype(BF16), triu_ref[...], preferred_element_type=F32)
            sel = (kk > thr) | (eq & ((run + incl) <= need))
            causal = (c * kc + lane) <= row
            bias_ref[:, cols] = jnp.where(sel & causal, 0.0, NEG_BIG).astype(BF16)
            return run + jnp.sum(eqf, axis=1, keepdims=True)
        lax.fori_loop(0, nkc, fin_body, jnp.zeros((tq, 1), F32))

    gq = ATTN_GROUP * tq

    def logits(j, kx):
        qg = q_ref[j * ATTN_GROUP:(j + 1) * ATTN_GROUP].reshape(gq, LANES)
        qx = jnp.concatenate([qg, eye_ref[...]], axis=1)
        return jnp.dot(qx, kx, preferred_element_type=F32)

    mx_ref[...] = jnp.full(mx_ref.shape, NEG_BIG, F32)
    acc_ref[...] = jnp.zeros(acc_ref.shape, F32)

    def chunk_logits(c):
        cols = pl.ds(pl.multiple_of(c * kc, kc), kc)
        bt = bias_ref[:, cols]
        return [logits(j, jnp.concatenate([k_ref[j, :, cols], bt], axis=0))
                for j in range(ATTN_KV_HEADS)]

    def chunk_update(c, s_list):
        cols = pl.ds(pl.multiple_of(c * kc, kc), kc)
        for j in range(ATTN_KV_HEADS):
            vx = jnp.concatenate([v_ref[j, cols, :], onec_ref[...]], axis=1)
            s = s_list[j]
            m_old = mx_ref[j]
            m_new = jnp.maximum(m_old, jnp.max(fold(s, jnp.maximum), axis=1, keepdims=True))
            alpha = jnp.exp2(m_old - m_new)
            p = jnp.concatenate(
                [jnp.exp2(s[:, a * LANES:(a + 1) * LANES] - m_new) for a in range(nslab)], axis=1)
            acc_ref[j] = (acc_ref[j] * jnp.concatenate([alpha, alpha], axis=1)
                          + jnp.dot(p.astype(BF16), vx, preferred_element_type=F32))
            mx_ref[j] = m_new

    def pair_body(g, carry):
        s0 = chunk_logits(2 * g)
        s1 = chunk_logits(2 * g + 1)
        chunk_update(2 * g, s0)
        chunk_update(2 * g + 1, s1)
        return carry
    lax.fori_loop(0, nkc // 2, pair_body, 0)

    @pl.when(nkc % 2 == 1)
    def _():
        chunk_update(nkc - 1, chunk_logits(nkc - 1))

    for h in range(ATTN_HEADS):
        a = acc_ref[h // ATTN_GROUP, (h % ATTN_GROUP) * tq:(h % ATTN_GROUP + 1) * tq, :]
        out_ref[:, h * LANES:(h + 1) * LANES] = (
            a[:, 0:LANES] / a[:, LANES:LANES + 1]).astype(out_ref.dtype)


def dsa_attention(q_r, qi_r, wi, k_r, v_r, ki_r, batch, seq, k_sel):
    n = batch * seq
    nq = seq // Q_BLOCK
    kc = _pick(seq, 1024)
    triu =jnp.asarray((np.arange(kc)[:, None] <= np.arange(kc)[None, :]).astype(np.float32), BF16)
    gq = ATTN_GROUP * Q_BLOCK
    eye = jnp.asarray(np.tile(np.eye(Q_BLOCK, dtype=np.float32), (ATTN_GROUP, 1)), BF16)
    onec = np.zeros((kc, LANES), np.float32)
    onec[:, 0] = 1.0
    onec = jnp.asarray(onec, BF16)
    kern = functools.partial(_dsa_kernel, k_sel=k_sel, kc=kc)
    in_specs = [
        pl.BlockSpec((ATTN_HEADS, Q_BLOCK, LANES), lambda b, i: (0, b * nq + i, 0)),
        pl.BlockSpec((IDX_HEADS, Q_BLOCK, LANES), lambda b, i: (0, b * nq + i, 0)),
        pl.BlockSpec((Q_BLOCK, LANES), lambda b, i: (b * nq + i, 0)),
        pl.BlockSpec((ATTN_KV_HEADS, LANES, seq), lambda b, i: (0, 0, b)),
        pl.BlockSpec((ATTN_KV_HEADS, seq, LANES), lambda b, i: (0, b, 0)),
        pl.BlockSpec((LANES, seq), lambda b, i: (0, b)),
        pl.BlockSpec((kc, kc), lambda b, i: (0, 0)),
        pl.BlockSpec((gq, Q_BLOCK), lambda b, i: (0, 0)),
        pl.BlockSpec((kc, LANES), lambda b, i: (0, 0)),
    ]
    scratch = [pltpu.VMEM((Q_BLOCK, seq), I32), pltpu.VMEM((Q_BLOCK, seq), I16),
               pltpu.VMEM((Q_BLOCK, seq), BF16),
               pltpu.VMEM((IDX_HEADS, Q_BLOCK, LANES), F32),
               pltpu.VMEM((ATTN_KV_HEADS, gq, LANES), F32),
               pltpu.VMEM((ATTN_KV_HEADS, gq, 2 * LANES), F32)]
    return pl.pallas_call(
        kern, grid=(batch, nq), in_specs=in_specs,
        out_specs=pl.BlockSpec((Q_BLOCK, ATTN_INNER), lambda b, i: (b * nq + i, 0)),
        out_shape=jax.ShapeDtypeStruct((n, ATTN_INNER), BF16),
        scratch_shapes=scratch, compiler_params=_cparams(2), name="dsa_attention",
    )(q_r, qi_r, wi, k_r, v_r, ki_r, triu, eye, onec)


def _silu(x):
    return x * (1.0 / (1.0 + jnp.exp(-x)))


def _ssd_kernel(xbc_ref, z_ref, tail_ref, cw_ref, cb_ref, dtb_ref, alog_ref, dexp_ref, ng_ref,
                tri_ref, e_ref, out_ref, state_ref, ext_ref, y_ref):
    q = SSM_CHUNK
    hi = lax.Precision.HIGHEST

    @pl.when(pl.program_id(1) == 0)
    def _():
        state_ref[...] = jnp.zeros(state_ref.shape, F32)
        ext_ref[0:SUBLANES, :] = jnp.zeros((SUBLANES, SSM_CONV_DIM), F32)

    x = xbc_ref[...]
    ext_ref[SUBLANES:SUBLANES + q, :] = x
    conv = x * cw_ref[SSM_CONV - 1:SSM_CONV, :] + cb_ref[...]
    for k in range(1, SSM_CONV):
        conv = conv + ext_ref[SUBLANES - k:SUBLANES - k + q, :] * cw_ref[SSM_CONV - 1 - k:SSM_CONV - k, :]
    ext_ref[0:SUBLANES, :] = x[q - SUBLANES:q, :]
    xbc = _silu(conv)
    xs = xbc[:, 0:SSM_INNER]
    bm = xbc[:, SSM_INNER:SSM_INNER + SSM_GROUPS * SSM_STATE]
    cm = xbc[:, SSM_INNER + SSM_GROUPS * SSM_STATE:SSM_CONV_DIM]

    lane = lax.broadcasted_iota(I32, (q, LANES), 1)
    is_dt = (lane >= TAIL_DT) & (lane < TAIL_DT + SSM_HEADS)
    raw = tail_ref[...] + dtb_ref[...]
    dt = jnp.where(is_dt, jnp.maximum(raw, 0.0) + jnp.log1p(jnp.exp(-jnp.abs(raw))), 0.0)
    a = -jnp.exp(alog_ref[...])
    da = jnp.where(is_dt, dt * a, 0.0)
    acum = jnp.dot(tri_ref[...], da, precision=hi, preferred_element_type=F32)
    acum_t = acum.T
    acum_x = jnp.dot(acum, e_ref[...], precision=hi, preferred_element_type=F32)
    dt_x = jnp.dot(dt, e_ref[...], precision=hi, preferred_element_type=F32)
    last_x = acum_x[q - 1:q, :]
    xdt = xs * dt_x
    xdec = xdt * jnp.exp(last_x - acum_x)
    ri = lax.broadcasted_iota(I32, (q, q), 0)
    ci = lax.broadcasted_iota(I32, (q, q), 1)
    tril = ri >= ci
    pairs_per_group = SSM_GROUP_INNER // LANES

    for g in range(SSM_GROUPS):
        bg = bm[:, g * SSM_STATE:(g + 1) * SSM_STATE]
        cg = cm[:, g * SSM_STATE:(g + 1) * SSM_STATE]
        cgb = cg.astype(BF16)
        gsl = slice(g * SSM_GROUP_INNER, (g + 1) * SSM_GROUP_INNER)
        cb = lax.dot_general(cgb, bg.astype(BF16), (((1,), (1,)), ((), ())),
                             preferred_element_type=F32)
        y_off = jnp.dot(cgb, state_ref[g].astype(BF16), preferred_element_type=F32)
        y_ref[:, gsl] = y_off * jnp.exp(acum_x[:, gsl])
        for mth in range(pairs_per_group):
            slab = slice((g * pairs_per_group + mth) * LANES, (g * pairs_per_group + mth + 1) * LANES)
            xp = xdt[:, slab]
            yd = None
            for side in range(2):
                hl = TAIL_DT + (g * pairs_per_group + mth) * 2 + side
                seg = acum[:, hl:hl + 1] - acum_t[hl:hl + 1, :]
                lmat = jnp.exp(jnp.where(tril, seg, NEG_BIG))
                if side == 0:
                    xh = jnp.where(lane < SSM_HEAD_DIM, xp, 0.0)
                else:
                    xh = jnp.where(lane >= SSM_HEAD_DIM, xp, 0.0)
                part = jnp.dot((cb * lmat).astype(BF16), xh.astype(BF16), preferred_element_type=F32)
                yd = part if yd is None else yd + part
            y_ref[:, slab] = y_ref[:, slab] + yd
        st = jnp.dot(bg.T.astype(BF16), xdec[:, gsl].astype(BF16), preferred_element_type=F32)
        state_ref[g] = state_ref[g] * jnp.exp(last_x[:, gsl]) + st

    y = (y_ref[...] + dexp_ref[...] * xs) * _silu(z_ref[...])
    for g in range(SSM_GROUPS):
        gsl = slice(g * SSM_GROUP_INNER, (g + 1) * SSM_GROUP_INNER)
        yg = y[:, gsl]
        ms = jnp.mean(yg * yg, axis=-1, keepdims=True)
        out_ref[:, gsl] = (yg * lax.rsqrt(ms + EPS) * ng_ref[:, gsl]).astype(out_ref.dtype)


def ssd_mixer(proj, conv_w, conv_b, dtb_pad, alog_pad, d_exp, norm_g, batch, seq):
    n = batch * seq
    q = SSM_CHUNK
    nc = seq // q
    tri = jnp.asarray((np.arange(q)[:, None] >= np.arange(q)[None, :]).astype(np.float32))
    e = np.zeros((LANES, SSM_INNER), np.float32)
    for h in range(SSM_HEADS):
        e[TAIL_DT + h, h * SSM_HEAD_DIM:(h + 1) * SSM_HEAD_DIM] = 1.0
    e = jnp.asarray(e)

    def col(width, off):
        return pl.BlockSpec((q, width), lambda b, c: (b * nc + c, off // width))

    def full(shape):
        return pl.BlockSpec(shape, lambda b, c: (0,) * len(shape))

    in_specs = [col(SSM_CONV_DIM, COL_XBC), col(SSM_INNER, COL_Z), col(LANES, COL_TAIL),
                full((SSM_CONV, SSM_CONV_DIM)), full((1, SSM_CONV_DIM)), full((1, LANES)),
                full((1, LANES)), full((1, SSM_INNER)), full((1, SSM_INNER)),
                full((q, q)), full((LANES, SSM_INNER))]
    scratch = [pltpu.VMEM((SSM_GROUPS, SSM_STATE, SSM_GROUP_INNER), F32),
               pltpu.VMEM((SUBLANES + q, SSM_CONV_DIM), F32),
               pltpu.VMEM((q, SSM_INNER), F32)]
    return pl.pallas_call(
        _ssd_kernel, grid=(batch, nc), in_specs=in_specs,
        out_specs=pl.BlockSpec((q, SSM_INNER), lambda b, c: (b * nc + c, 0)),
        out_shape=jax.ShapeDtypeStruct((n, SSM_INNER), BF16),
        scratch_shapes=scratch, compiler_params=_cparams(2), name="ssd_mixer",
    )(proj, proj, proj, conv_w, conv_b, dtb_pad, alog_pad, d_exp, norm_g, tri, e)


def _ret_kernel(q_ref, k_ref, v_ref, g_ref, pos_ref, inv_ref, dmat_ref, qdec_ref, kdec_ref,
                cdec_ref, ng_ref, out_ref, state_ref):
    half = RET_QK_DIM // 2

    @pl.when(pl.program_id(1) == 0)
    def _():
        state_ref[...] = jnp.zeros(state_ref.shape, F32)

    ang = pos_ref[...].astype(F32) * inv_ref[...]
    cos_all = jnp.cos(ang)
    sin_all = jnp.sin(ang)
    q = RET_CHUNK
    n_sub = q_ref.shape[0] // q

    for h in range(RET_HEADS):
        qcols = slice(h * RET_QK_DIM, (h + 1) * RET_QK_DIM)
        vcols = slice(h * RET_V_DIM, (h + 1) * RET_V_DIM)
        st = state_ref[h]
        for sub in range(n_sub):
            rows = slice(sub * q, (sub + 1) * q)
            cosf = cos_all[rows]
            sinf = sin_all[rows]

            def rope(x):
                x1 = x[:, 0:half]
                x2 = x[:, half:2 * half]
                return jnp.concatenate([x1 * cosf - x2 * sinf, x2 * cosf + x1 * sinf], axis=-1)

            qh = rope(q_ref[rows, qcols].astype(F32))
            kh = rope(k_ref[rows, qcols].astype(F32)) * (RET_QK_DIM ** -0.5)
            vh = v_ref[rows, vcols]
            qb = qh.astype(BF16)
            s = lax.dot_general(qb, kh.astype(BF16), (((1,), (1,)), ((), ())),
                                preferred_element_type=F32) * dmat_ref[h]
            inner = jnp.dot(s.astype(BF16), vh, preferred_element_type=F32)
            cross = jnp.dot(qb, st.astype(BF16), preferred_element_type=F32) * qdec_ref[h]
            kd = (kh * kdec_ref[h]).T.astype(BF16)
            st = st * cdec_ref[h] + jnp.dot(kd, vh, preferred_element_type=F32)
            o = inner + cross
            ms = jnp.mean(o * o, axis=-1, keepdims=True)
            o = o * lax.rsqrt(ms + EPS) * ng_ref[:, vcols]
            gate = _silu(g_ref[rows, vcols].astype(F32))
            out_ref[rows, vcols] = (gate * o).astype(out_ref.dtype)
        state_ref[h] = st


def retention_mixer(proj, pos, norm_g, batch, seq):
    n = batch * seq
    q = RET_CHUNK
    nc = seq // q
    half = RET_QK_DIM // 2
    inv = jnp.power(RET_THETA, -(jnp.arange(half, dtype=F32) * 2.0 / RET_QK_DIM)).reshape(1, half)
    log_gamma = jnp.log(1.0 - jnp.power(2.0, -5.0 - jnp.arange(RET_HEADS, dtype=F32)))
    i = jnp.arange(q, dtype=F32)
    diff = i[:, None] - i[None, :]
    dmat = jnp.where(diff[None] >= 0,
                     jnp.exp(jnp.maximum(diff, 0.0)[None] * log_gamma[:, None, None]), 0.0)
    qdec = jnp.exp((i + 1.0)[None, :, None] * log_gamma[:, None, None])
    kdec = jnp.exp((q - 1.0 - i)[None, :, None] * log_gamma[:, None, None])
    cdec = jnp.broadcast_to(jnp.exp(q * log_gamma)[:, None, None], (RET_HEADS, 1, RET_V_DIM))

    rows = q * (RET_STEP_CHUNKS if nc % RET_STEP_CHUNKS == 0 else 1)
    ns = seq // rows

    def col(width, off):
        return pl.BlockSpec((rows, width), lambda b, c: (b * ns + c, off // width))

    def full(shape):
        return pl.BlockSpec(shape, lambda b, c: (0,) * len(shape))

    in_specs = [col(RET_QK_TOTAL, 0), col(RET_QK_TOTAL, RET_QK_TOTAL),
                col(RET_V_TOTAL, 2 * RET_QK_TOTAL), col(RET_V_TOTAL, 2 * RET_QK_TOTAL + RET_V_TOTAL),
                pl.BlockSpec((rows, 1), lambda b, c: (b * ns + c, 0)),
                full((1, half)), full((RET_HEADS, q, q)), full((RET_HEADS, q, 1)),
                full((RET_HEADS, q, 1)), full((RET_HEADS, 1, RET_V_DIM)), full((1, RET_V_TOTAL))]
    return pl.pallas_call(
        _ret_kernel, grid=(batch, ns), in_specs=in_specs,
        out_specs=pl.BlockSpec((rows, RET_V_TOTAL), lambda b, c: (b * ns + c, 0)),
        out_shape=jax.ShapeDtypeStruct((n, RET_V_TOTAL), BF16),
        scratch_shapes=[pltpu.VMEM((RET_HEADS, RET_QK_DIM, RET_V_DIM), F32)],
        compiler_params=_cparams(2), name="retention",
    )(proj, proj, proj, proj, pos, inv, dmat, qdec, kdec, cdec, norm_g.reshape(1, RET_V_TOTAL))


ROUTE_E0 = MOE_GROUPS
INFO_EID, INFO_GATE, INFO_RANK = 0, 2, 4


def _router_kernel(x_ref, g_ref, wr_ref, lt_ref, h_ref, info_ref, cnt_ref, run_ref):
    tb = x_ref.shape[0]

    @pl.when(pl.program_id(0) == 0)
    def _():
        run_ref[...] = jnp.zeros(run_ref.shape, F32)

    x = x_ref[...]
    ms = jnp.mean(x * x, axis=-1, keepdims=True)
    hn = x * lax.rsqrt(ms + EPS) * g_ref[...]
    h_ref[...] = hn
    h_hi = hn.astype(BF16)
    h_lo = (hn - h_hi.astype(F32)).astype(BF16)
    both = jnp.dot(h_hi, wr_ref[...], preferred_element_type=F32)
    logits = (both[:, 0:LANES] + both[:, LANES:2 * LANES]
              + jnp.dot(h_lo, wr_ref[:, 0:LANES], preferred_element_type=F32))

    lane = lax.broadcasted_iota(I32, (tb, LANES), 1).astype(F32)
    far = jnp.float32(4 * LANES)
    ninf = jnp.float32(-jnp.inf)
    gl = jnp.where(lane < MOE_GROUPS, logits, ninf)
    gmax = jnp.max(gl, axis=1, keepdims=True)
    grp = jnp.min(jnp.where(gl == gmax, lane, far), axis=1, keepdims=True)
    p_grp = 1.0 / jnp.sum(jnp.exp(gl - gmax), axis=1, keepdims=True)
    lo = ROUTE_E0 + grp * MOE_EXPERTS_PER_GROUP
    el = jnp.where((lane >= lo) & (lane < lo + MOE_EXPERTS_PER_GROUP), logits, ninf)
    m1 = jnp.max(el, axis=1, keepdims=True)
    i1 = jnp.min(jnp.where(el == m1, lane, far), axis=1, keepdims=True)
    el2 = jnp.where(lane == i1, ninf, el)
    m2 = jnp.max(el2, axis=1, keepdims=True)
    i2 = jnp.min(jnp.where(el2 == m2, lane, far), axis=1, keepdims=True)
    e2 = jnp.exp(m2 - m1)
    g1 = p_grp / (1.0 + e2)
    g2 = p_grp * e2 / (1.0 + e2)
    oh1 = jnp.where(lane == i1, 1.0, 0.0)
    oh2 = jnp.where(lane == i2, 1.0, 0.0)
    cnt = oh1 + oh2
    before = jnp.dot(lt_ref[...], cnt.astype(BF16), preferred_element_type=F32) + run_ref[...]
    r1 = jnp.sum(oh1 * before, axis=1, keepdims=True)
    r2 = jnp.sum(oh2 * before, axis=1, keepdims=True)
    run_ref[...] = run_ref[...] + jnp.sum(cnt, axis=0, keepdims=True)
    cnt_ref[...] = run_ref[...]

    info = jnp.where(lane == INFO_EID, i1 - ROUTE_E0, 0.0)
    info = jnp.where(lane == INFO_EID + 1, i2 - ROUTE_E0, info)
    info = jnp.where(lane == INFO_GATE, g1, info)
    info = jnp.where(lane == INFO_GATE + 1, g2, info)
    info = jnp.where(lane == INFO_RANK, r1, info)
    info = jnp.where(lane == INFO_RANK + 1, r2, info)
    info_ref[...] = info


def moe_router(x, g, w_router_pad):
    n, d = x.shape
    tb = _pick(n, 512)
    lt = jnp.asarray((np.arange(tb)[:, None] > np.arange(tb)[None, :]).astype(np.float32), BF16)
    w_hi = w_router_pad.astype(BF16)
    w_lo = (w_router_pad - w_hi.astype(F32)).astype(BF16)
    w_split = jnp.concatenate([w_hi, w_lo], axis=1)
    return pl.pallas_call(
        _router_kernel, grid=(n // tb,),
        in_specs=[pl.BlockSpec((tb, d), lambda i: (i, 0)), pl.BlockSpec((1, d), lambda i: (0, 0)),
                  pl.BlockSpec((d, 2 * LANES), lambda i: (0, 0)), pl.BlockSpec((tb, tb), lambda i: (0, 0))],
        out_specs=[pl.BlockSpec((tb, d), lambda i: (i, 0)), pl.BlockSpec((tb, LANES), lambda i: (i, 0)),
                   pl.BlockSpec((1, LANES), lambda i: (0, 0))],
        out_shape=[jax.ShapeDtypeStruct((n, d), F32), jax.ShapeDtypeStruct((n, LANES), F32),
                   jax.ShapeDtypeStruct((1, LANES), F32)],
        scratch_shapes=[pltpu.VMEM((1, LANES), F32)],
        compiler_params=_cparams(1), name="moe_router",
    )(x, g.reshape(1, d), w_split, lt)


def _row_copy(src_ref, src_row, dst_ref, dst_row, sem):
    return pltpu.make_async_copy(src_ref.at[pl.ds(src_row, 1), :], dst_ref.at[pl.ds(dst_row, 1), :], sem)


def _expert_kernel(be_ref, nused_ref, st_ref, nxe_ref, h_hbm, wg_hbm, wu_hbm, wd_hbm, out_ref,
                   xbuf, sem, wgf, wuf, wdf, wsem, wgu_s, wd_s, *, layer):
    i = pl.program_id(0)
    slot = i % 2
    nused = nused_ref[0]
    d = xbuf.shape[2]
    ff = EXPERT_FF

    def weight_copies(e):
        return (pltpu.make_async_copy(wg_hbm.at[layer, e], wgf, wsem),
                pltpu.make_async_copy(wu_hbm.at[layer, e], wuf, wsem),
                pltpu.make_async_copy(wd_hbm.at[layer, e], wdf, wsem))

    @pl.when(jnp.logical_and(i == 0, nused > 0))
    def _():
        for c in weight_copies(be_ref[0]):
            c.start()

    def gather(blk, s):
        def body(r, carry):
            _row_copy(h_hbm, st_ref[blk * MOE_BLOCK + r], xbuf.at[s], r, sem.at[s]).start(priority=1)
            return carry
        lax.fori_loop(0, MOE_BLOCK, body, 0, unroll=8)

    @pl.when(jnp.logical_and(i == 0, nused > 0))
    def _():
        gather(0, 0)

    @pl.when(i + 1 < nused)
    def _():
        gather(i + 1, 1 - slot)

    @pl.when(i < nused)
    def _():
        def drain(r, carry):
            _row_copy(h_hbm, 0, xbuf.at[slot], 0, sem.at[slot]).wait()
            return carry
        lax.fori_loop(0, MOE_BLOCK, drain, 0, unroll=8)

        changed = jnp.logical_or(i == 0, be_ref[i] != be_ref[jnp.maximum(i - 1, 0)])

        @pl.when(changed)
        def _():
            for c in weight_copies(be_ref[i]):
                c.wait()
            rc = _pick(d, 256)

            def body(r, carry):
                rows = pl.ds(pl.multiple_of(r * rc, rc), rc)
                wgu_s[rows, 0:ff] = wgf[rows, :].astype(BF16)
                wgu_s[rows, ff:2 * ff] = wuf[rows, :].astype(BF16)
                return carry
            lax.fori_loop(0, d // rc, body, 0)
            wd_s[...] = wdf[...].astype(BF16)

            nxt_e = nxe_ref[be_ref[i]]

            @pl.when(nxt_e >= 0)
            def _():
                for c in weight_copies(nxt_e):
                    c.start()

        gu = jnp.dot(xbuf[slot].astype(BF16), wgu_s[...], preferred_element_type=F32)
        hid = _silu(gu[:, 0:ff]) * gu[:, ff:2 * ff]
        out_ref[...] = jnp.dot(hid.astype(BF16), wd_s[...], preferred_element_type=F32)

    @pl.when(i >= nused)
    def _():
        out_ref[...] = jnp.zeros(out_ref.shape, out_ref.dtype)


def moe_experts(h, slot_tok, blk_expert, nused, next_expert, w_gate, w_up, w_down, layer):
    n, d = h.shape
    p = slot_tok.shape[0]
    nblk = p // MOE_BLOCK
    ff = EXPERT_FF
    hbm = pl.BlockSpec(memory_space=pl.ANY)
    grid_spec = pltpu.PrefetchScalarGridSpec(
        num_scalar_prefetch=4, grid=(nblk,),
        in_specs=[hbm, hbm, hbm, hbm],
        out_specs=pl.BlockSpec((MOE_BLOCK, d), lambda i, be, nu, st, nx: (i, 0)),
        scratch_shapes=[pltpu.VMEM((2, MOE_BLOCK, d), h.dtype), pltpu.SemaphoreType.DMA((2,)),
                        pltpu.VMEM((d, ff), F32), pltpu.VMEM((d, ff), F32), pltpu.VMEM((ff, d), F32),
                        pltpu.SemaphoreType.DMA(()),
                        pltpu.VMEM((d, 2 * ff), BF16), pltpu.VMEM((ff, d), BF16)])
    return pl.pallas_call(
        functools.partial(_expert_kernel, layer=layer), grid_spec=grid_spec,
        out_shape=jax.ShapeDtypeStruct((p, d), F32),
        compiler_params=_cparams(1), name="moe_experts",
    )(blk_expert, nused, slot_tok, next_expert, h, w_gate, w_up, w_down)


def _slots_kernel(dest_ref, slot_ref):
    def clear(s, carry):
        slot_ref[s] = 0
        return carry
    lax.fori_loop(0, slot_ref.shape[0], clear, 0, unroll=16)

    def body(t, carry):
        for k in range(MOE_TOPK):
            slot_ref[dest_ref[t * MOE_TOPK + k]] = t
        return carry
    lax.fori_loop(0, dest_ref.shape[0] // MOE_TOPK, body, 0, unroll=8)


def moe_slots(dest_flat, p):
    smem = pl.BlockSpec(memory_space=pltpu.SMEM)
    return pl.pallas_call(
        _slots_kernel,
        in_specs=[smem],
        out_specs=smem,
        out_shape=jax.ShapeDtypeStruct((p,), I32),
        name="moe_slots",
    )(dest_flat)


def _combine_kernel(dcur_ref, dnxt_ref, info_ref, x_ref, y_hbm, out_ref, ybuf, sem, *, tb, nsteps):
    i = pl.program_id(0)
    slot = i % 2

    def issue(dref, s):
        def body(r, carry):
            for k in range(MOE_TOPK):
                _row_copy(y_hbm, dref[0, 0, MOE_TOPK * r + k], ybuf.at[s, k], r, sem.at[s]).start(
                    priority=k % 2)
            return carry
        lax.fori_loop(0, tb, body, 0, unroll=8)

    @pl.when(i == 0)
    def _():
        issue(dcur_ref, 0)

    @pl.when(i + 1 < nsteps)
    def _():
        issue(dnxt_ref, 1 - slot)

    def drain(r, carry):
        for k in range(MOE_TOPK):
            _row_copy(y_hbm, 0, ybuf.at[slot, k], 0, sem.at[slot]).wait()
        return carry
    lax.fori_loop(0, tb, drain, 0, unroll=8)

    info = info_ref[...]
    out = x_ref[...]
    for k in range(MOE_TOPK):
        out = out + ybuf[slot, k] * info[:, INFO_GATE + k:INFO_GATE + k + 1]
    out_ref[...] = out


def moe_combine(x, y, dest, info):
    n, d = x.shape
    tb = _pick(n, COMBINE_ROWS)
    nsteps = n // tb
    kern = functools.partial(_combine_kernel, tb=tb, nsteps=nsteps)
    dest3 = dest.reshape(nsteps, 1, MOE_TOPK * tb)
    return pl.pallas_call(
        kern, grid=(nsteps,),
        in_specs=[pl.BlockSpec((1, 1, MOE_TOPK * tb), lambda i: (i, 0, 0), memory_space=pltpu.SMEM),
                  pl.BlockSpec((1, 1, MOE_TOPK * tb), lambda i: (jnp.minimum(i + 1, nsteps - 1), 0, 0),
                               memory_space=pltpu.SMEM),
                  pl.BlockSpec((tb, LANES), lambda i: (i, 0)),
                  pl.BlockSpec((tb, d), lambda i: (i, 0)),
                  pl.BlockSpec(memory_space=pl.ANY)],
        out_specs=pl.BlockSpec((tb, d), lambda i: (i, 0)),
        out_shape=jax.ShapeDtypeStruct((n, d), x.dtype),
        scratch_shapes=[pltpu.VMEM((2, MOE_TOPK, tb, d), y.dtype), pltpu.SemaphoreType.DMA((2,))],
        compiler_params=_cparams(1), name="moe_combine",
    )(dest3, dest3, info, x, y)


def hier_moe_layer(x, norm_g, w_rg, w_re, w_gate, w_up, w_down, layer):
    n, d = x.shape
    w_router = jnp.concatenate(
        [w_rg, w_re, jnp.zeros((d, LANES - MOE_GROUPS - N_EXPERTS), F32)], axis=1)
    h, info, cnt = moe_router(x, norm_g, w_router)
    counts = cnt[0, ROUTE_E0:ROUTE_E0 + N_EXPERTS].astype(I32)
    eid = info[:, INFO_EID:INFO_EID + MOE_TOPK].astype(I32)
    rank = info[:, INFO_RANK:INFO_RANK + MOE_TOPK].astype(I32)
    padded = (counts + MOE_BLOCK - 1) // MOE_BLOCK * MOE_BLOCK
    pad_end = jnp.cumsum(padded)
    pad_start = pad_end - padded
    onehot = (eid[..., None] == jnp.arange(N_EXPERTS, dtype=I32)).astype(F32)
    dest = jnp.dot(onehot, pad_start.astype(F32), precision=lax.Precision.HIGHEST).astype(I32) + rank
    a = n * MOE_TOPK
    p = (-(-a // MOE_BLOCK) + N_EXPERTS) * MOE_BLOCK
    nblk = p // MOE_BLOCK
    blk_start = jnp.arange(nblk, dtype=I32) * MOE_BLOCK
    blk_expert = jnp.minimum(jnp.sum((pad_end[None, :] <= blk_start[:, None]).astype(I32), axis=1),
                             N_EXPERTS - 1).astype(I32)
    nused = (pad_end[-1:] // MOE_BLOCK).astype(I32)
    slot_tok = moe_slots(dest.reshape(-1), p)
    ids = jnp.arange(N_EXPERTS, dtype=I32)
    owner = jnp.where(counts > 0, ids, N_EXPERTS)
    after = jnp.concatenate([lax.cummin(owner, reverse=True)[1:], jnp.full((1,), N_EXPERTS, I32)])
    next_expert = jnp.where(after < N_EXPERTS, after, -1).astype(I32)
    y = moe_experts(h, slot_tok, blk_expert, nused, next_expert, w_gate, w_up, w_down, layer)
    return moe_combine(x, y, dest, info)


def _pad_lanes(v, offset, width=LANES):
    out = jnp.zeros((1, width), F32)
    return out.at[0, offset:offset + v.shape[0]].set(v.astype(F32))


def hybrid_layer(x, pos, norm_g, w_in, q_norm, k_norm, kidx_norm, conv_w, conv_b, dt_bias, a_log,
                 d_skip, ssm_norm, w_out, batch, seq, k_sel):
    n, d = x.shape
    offs = np.cumsum([0, ATTN_INNER, ATTN_KV, ATTN_KV, IDX_INNER, IDX_HEAD_DIM, IDX_HEADS,
                      SSM_INNER, SSM_CONV_DIM, SSM_HEADS])
    seg = {name: w_in[:, offs[j]:offs[j + 1]] for j, name in
           enumerate(["q", "k", "v", "qi", "ki", "wi", "z", "xbc", "dt"])}
    tail_pad = LANES - IDX_HEAD_DIM - IDX_HEADS - SSM_HEADS
    w_perm = jnp.concatenate(
        [seg["xbc"], seg["k"], seg["v"], seg["z"], seg["q"], seg["qi"], seg["ki"], seg["wi"],
         seg["dt"], jnp.zeros((d, tail_pad + HYB_COLS - COL_TAIL - LANES), F32)], axis=1).astype(BF16)
    proj = fused_matmul([x], w_perm, g=norm_g, out_dtype=F32, tn=1024, name="hyb_in_proj")

    rot = ATTN_HEAD_DIM // ROPE_FRACTION
    rot_i = IDX_HEAD_DIM // ROPE_FRACTION
    inv16 = jnp.power(ROPE_THETA, -(jnp.arange(rot // 2, dtype=F32) * 2.0 / rot))
    inv8 = jnp.power(ROPE_THETA, -(jnp.arange(rot_i // 2, dtype=F32) * 2.0 / rot_i))
    inv_c = jnp.concatenate(
        [inv16, inv8, jnp.zeros((LANES - rot // 2 - rot_i // 2,), F32)]).reshape(1, LANES)
    q_r, k_r, v_r, qi_r, ki_r, wi = hyb_prep(
        proj, pos, inv_c, q_norm.reshape(1, LANES), k_norm.reshape(1, LANES),
        _pad_lanes(kidx_norm, TAIL_KI))
    attn = dsa_attention(q_r, qi_r, wi, k_r, v_r, ki_r, batch, seq, k_sel)

    ssm = ssd_mixer(proj, conv_w, conv_b.reshape(1, SSM_CONV_DIM), _pad_lanes(dt_bias, TAIL_DT),
                    _pad_lanes(a_log, TAIL_DT), jnp.repeat(d_skip, SSM_HEAD_DIM).reshape(1, SSM_INNER),
                    ssm_norm.reshape(1, SSM_INNER), batch, seq)
    return fused_matmul([attn, ssm], w_out.astype(BF16), res=x, out_dtype=F32, name="hyb_out_proj")


def retention_layer(x, pos, norm_g, w_in, ret_norm, w_out, batch, seq):
    proj = fused_matmul([x], w_in.astype(BF16), g=norm_g, out_dtype=BF16, tn=1024, name="ret_in_proj")
    o = retention_mixer(proj, pos, ret_norm, batch, seq)
    return fused_matmul([o], w_out.astype(BF16), res=x, out_dtype=F32, name="ret_out_proj")


def kernel(x, positions, mix_norm, ffn_norm, hyb_w_in, attn_q_norm, attn_k_norm, idx_k_norm, ssm_conv_w, ssm_conv_b, ssm_dt_bias, ssm_a_log, ssm_d, ssm_norm, hyb_w_out, ret_w_in, ret_norm, ret_w_out, moe_router_group, moe_router_expert, moe_w_gate, moe_w_up, moe_w_down):
    batch, seq, d = x.shape
    depth = mix_norm.shape[0]
    k_sel = min(TOPK_MAX, seq // 4)
    n = batch * seq
    xf = x.reshape(n, d)
    pos = positions.reshape(n, 1).astype(I32)
    for layer in range(depth):
        i = layer // 2
        if layer % 2 == 0:
            xf = hybrid_layer(xf, pos, mix_norm[layer], hyb_w_in[i], attn_q_norm[i], attn_k_norm[i],
                              idx_k_norm[i], ssm_conv_w[i], ssm_conv_b[i], ssm_dt_bias[i],
                              ssm_a_log[i], ssm_d[i], ssm_norm[i], hyb_w_out[i], batch, seq, k_sel)
        else:
            xf = retention_layer(xf, pos, mix_norm[layer], ret_w_in[i], ret_norm[i], ret_w_out[i],
                                 batch, seq)
        xf = hier_moe_layer(xf, ffn_norm[layer], moe_router_group[layer], moe_router_expert[layer],
                            moe_w_gate, moe_w_up, moe_w_down, layer)
    return xf.reshape(batch, seq, d)
```
